```python
import math
import jax, jax.numpy as jnp
from jax import lax
import numpy as np

D_MODEL = 2048
BATCH = 8
SEQ = 2048
DEPTH = 2

N_MIXERS = 2
N_A = (DEPTH + 1) // 2
N_B = DEPTH // 2
D_FF = 5632
PLE_DIM = 256
CHUNK = 128
GMLP_HALF = 3 * D_MODEL
GMLP_GROUPS = 16
GMLP_GROUP_DIM = GMLP_HALF // GMLP_GROUPS
SSM_WIDTH = D_MODEL
SSM_GROUP_DIM = 16
SSM_GROUPS = SSM_WIDTH // SSM_GROUP_DIM
SSM_STATE = 64
DT_MIN = 0.001
DT_MAX = 0.1
EPS = 1e-6

kernel_name = "hybrid_gmlp_s5_macaron_ple"


def rms_norm(x, g):
    xf = x.astype(jnp.float32)
    y = xf * lax.rsqrt(jnp.mean(xf * xf, axis=-1, keepdims=True) + EPS)
    return (y * g.astype(jnp.float32)).astype(x.dtype)


def swiglu(h, w_gate, w_up, w_down):
    return (jax.nn.silu(h @ w_gate) * (h @ w_up)) @ w_down


def chunked_gmlp(h, w_in, ln_g, ln_b, w_s, b_s, w_out):
    B, L, _ = h.shape
    z = jax.nn.gelu(h @ w_in)
    u, v = jnp.split(z, 2, axis=-1)
    v = v.reshape(B, L, GMLP_GROUPS, GMLP_GROUP_DIM)
    vf = v.astype(jnp.float32)
    mu = jnp.mean(vf, axis=-1, keepdims=True)
    var = jnp.mean(jnp.square(vf - mu), axis=-1, keepdims=True)
    v = ((vf - mu) * lax.rsqrt(var + EPS)).astype(h.dtype)
    v = v * ln_g.reshape(GMLP_GROUPS, GMLP_GROUP_DIM) + ln_b.reshape(GMLP_GROUPS, GMLP_GROUP_DIM)
    v = v.reshape(B, L // CHUNK, CHUNK, GMLP_GROUPS, GMLP_GROUP_DIM)
    causal = jnp.tril(jnp.ones((CHUNK, CHUNK), dtype=bool))
    ws = jnp.where(causal[None], w_s, jnp.zeros_like(w_s))
    sv = jnp.einsum('gts,bnsgd->bntgd', ws, v) + b_s.T[None, None, :, :, None]
    gated = u * sv.reshape(B, L, GMLP_HALF)
    return gated @ w_out


def _diag_scan_op(left, right):
    ar1, ai1, br1, bi1 = left
    ar2, ai2, br2, bi2 = right
    ar = ar2 * ar1 - ai2 * ai1
    ai = ar2 * ai1 + ai2 * ar1
    br = ar2 * br1 - ai2 * bi1 + br2
    bi = ar2 * bi1 + ai2 * br1 + bi2
    return (ar, ai, br, bi)


def s5_mixer(h, w_in, lam_re, lam_im, log_dt, b_re, b_im, c_re, c_im, d_skip, w_out):
    B, L, _ = h.shape
    u = (h @ w_in).reshape(B, L, SSM_GROUPS, SSM_GROUP_DIM).astype(jnp.float32)
    dt = jnp.exp(log_dt.astype(jnp.float32))[:, None]
    lr = lam_re.astype(jnp.float32)
    li = lam_im.astype(jnp.float32)
    mag = jnp.exp(lr * dt)
    ang = li * dt
    abar_r = mag * jnp.cos(ang)
    abar_i = mag * jnp.sin(ang)
    den = lr * lr + li * li
    nr = abar_r - 1.0
    ni = abar_i
    z_r = (nr * lr + ni * li) / den
    z_i = (ni * lr - nr * li) / den
    br_ = b_re.astype(jnp.float32)
    bi_ = b_im.astype(jnp.float32)
    bb_r = z_r[..., None] * br_ - z_i[..., None] * bi_
    bb_i = z_r[..., None] * bi_ + z_i[..., None] * br_
    bu_r = jnp.einsum('blgh,gph->blgp', u, bb_r)
    bu_i = jnp.einsum('blgh,gph->blgp', u, bb_i)
    a_r = jnp.broadcast_to(abar_r[None, None], (1, L, SSM_GROUPS, SSM_STATE))
    a_i = jnp.broadcast_to(abar_i[None, None], (1, L, SSM_GROUPS, SSM_STATE))
    _, _, hr, hi = lax.associative_scan(_diag_scan_op, (a_r, a_i, bu_r, bu_i), axis=1)
    y = (jnp.einsum('blgp,ghp->blgh', hr, c_re.astype(jnp.float32))
         - jnp.einsum('blgp,ghp->blgh', hi, c_im.astype(jnp.float32))
         + d_skip.astype(jnp.float32).reshape(SSM_GROUPS, SSM_GROUP_DIM) * u)
    act = jax.nn.gelu(y).reshape(B, L, SSM_WIDTH).astype(h.dtype)
    val, gate = jnp.split(act @ w_out, 2, axis=-1)
    return val * jax.nn.sigmoid(gate)


def _fwd_setup_inputs(seed: int = 0) -> dict:
    key = jax.random.key(seed)
    ks = jax.random.split(key, 32)
    f32 = jnp.float32
    nrm = lambda k, shape, scale: jax.random.normal(k, shape, f32) * scale
    x = jax.random.normal(ks[0], (BATCH, SEQ, D_MODEL), f32)
    p = jax.random.normal(ks[1], (DEPTH, BATCH, SEQ, PLE_DIM), f32)
    norm_g = 1.0 + nrm(ks[2], (DEPTH, 4, D_MODEL), 0.02)
    final_norm_g = 1.0 + nrm(ks[3], (D_MODEL,), 0.02)
    ffn_w_gate = nrm(ks[4], (DEPTH, 2, D_MODEL, D_FF), D_MODEL ** -0.5)
    ffn_w_up = nrm(ks[5], (DEPTH, 2, D_MODEL, D_FF), D_MODEL ** -0.5)
    ffn_w_down = nrm(ks[6], (DEPTH, 2, D_FF, D_MODEL), D_FF ** -0.5)
    gmlp_w_in = nrm(ks[7], (N_A, D_MODEL, 2 * GMLP_HALF), D_MODEL ** -0.5)
    gmlp_ln_g = 1.0 + nrm(ks[8], (N_A, GMLP_HALF), 0.02)
    gmlp_ln_b = nrm(ks[9], (N_A, GMLP_HALF), 0.02)
    gmlp_w_s = nrm(ks[10], (N_A, GMLP_GROUPS, CHUNK, CHUNK), CHUNK ** -0.5)
    gmlp_b_s = 1.0 + nrm(ks[11], (N_A, GMLP_GROUPS, CHUNK), 0.1)
    gmlp_w_out = nrm(ks[12], (N_A, GMLP_HALF, D_MODEL), GMLP_HALF ** -0.5)
    s5_w_in = nrm(ks[13], (N_B, D_MODEL, SSM_WIDTH), D_MODEL ** -0.5)
    s5_lam_re = -0.5 + nrm(ks[14], (N_B, SSM_GROUPS, SSM_STATE), 0.01)
    s5_lam_im = (math.pi * jnp.arange(SSM_STATE, dtype=f32))[None, None, :] + nrm(ks[15], (N_B, SSM_GROUPS, SSM_STATE), 0.01)
    s5_log_dt = jax.random.uniform(ks[16], (N_B, SSM_GROUPS), f32, math.log(DT_MIN), math.log(DT_MAX))
    s5_b_re = nrm(ks[17], (N_B, SSM_GROUPS, SSM_STATE, SSM_GROUP_DIM), (2 * SSM_GROUP_DIM) ** -0.5)
    s5_b_im = nrm(ks[18], (N_B, SSM_GROUPS, SSM_STATE, SSM_GROUP_DIM), (2 * SSM_GROUP_DIM) ** -0.5)
    s5_c_re = nrm(ks[19], (N_B, SSM_GROUPS, SSM_GROUP_DIM, SSM_STATE), (2 * SSM_STATE) ** -0.5)
    s5_c_im = nrm(ks[20], (N_B, SSM_GROUPS, SSM_GROUP_DIM, SSM_STATE), (2 * SSM_STATE) ** -0.5)
    s5_d = nrm(ks[21], (N_B, SSM_WIDTH), 1.0)
    s5_w_out = nrm(ks[22], (N_B, SSM_WIDTH, 2 * D_MODEL), SSM_WIDTH ** -0.5)
    ple_w_gate = nrm(ks[23], (DEPTH, D_MODEL, D_MODEL), D_MODEL ** -0.5)
    ple_w_proj = nrm(ks[24], (DEPTH, PLE_DIM, D_MODEL), PLE_DIM ** -0.5)
    return {"x": x, "p": p, "norm_g": norm_g, "final_norm_g": final_norm_g,
            "ffn_w_gate": ffn_w_gate, "ffn_w_up": ffn_w_up, "ffn_w_down": ffn_w_down,
            "gmlp_w_in": gmlp_w_in, "gmlp_ln_g": gmlp_ln_g, "gmlp_ln_b": gmlp_ln_b,
            "gmlp_w_s": gmlp_w_s, "gmlp_b_s": gmlp_b_s, "gmlp_w_out": gmlp_w_out,
            "s5_w_in": s5_w_in, "s5_lam_re": s5_lam_re, "s5_lam_im": s5_lam_im,
            "s5_log_dt": s5_log_dt, "s5_b_re": s5_b_re, "s5_b_im": s5_b_im,
            "s5_c_re": s5_c_re, "s5_c_im": s5_c_im, "s5_d": s5_d, "s5_w_out": s5_w_out,
            "ple_w_gate": ple_w_gate, "ple_w_proj": ple_w_proj}


def _fwd_reference(x, p, norm_g, final_norm_g, ffn_w_gate, ffn_w_up, ffn_w_down,
              gmlp_w_in, gmlp_ln_g, gmlp_ln_b, gmlp_w_s, gmlp_b_s, gmlp_w_out,
              s5_w_in, s5_lam_re, s5_lam_im, s5_log_dt, s5_b_re, s5_b_im,
              s5_c_re, s5_c_im, s5_d, s5_w_out, ple_w_gate, ple_w_proj):
    for i in range(DEPTH):
        x = x + 0.5 * swiglu(rms_norm(x, norm_g[i, 0]), ffn_w_gate[i, 0], ffn_w_up[i, 0], ffn_w_down[i, 0])
        h = rms_norm(x, norm_g[i, 1])
        j = i // N_MIXERS
        if i % N_MIXERS == 0:
            x = x + chunked_gmlp(h, gmlp_w_in[j], gmlp_ln_g[j], gmlp_ln_b[j],
                                 gmlp_w_s[j], gmlp_b_s[j], gmlp_w_out[j])
        else:
            x = x + s5_mixer(h, s5_w_in[j], s5_lam_re[j], s5_lam_im[j], s5_log_dt[j],
                             s5_b_re[j], s5_b_im[j], s5_c_re[j], s5_c_im[j], s5_d[j], s5_w_out[j])
        x = x + 0.5 * swiglu(rms_norm(x, norm_g[i, 2]), ffn_w_gate[i, 1], ffn_w_up[i, 1], ffn_w_down[i, 1])
        gate = jax.nn.sigmoid(rms_norm(x, norm_g[i, 3]) @ ple_w_gate[i])
        x = x + gate * (p[i] @ ple_w_proj[i])
    return rms_norm(x, final_norm_g)


import jax as _jax
import jax.numpy as _jnp

TWIN_FORMAT = 'train_step'
FWD_PARAMS = ['x', 'p', 'norm_g', 'final_norm_g', 'ffn_w_gate', 'ffn_w_up', 'ffn_w_down', 'gmlp_w_in', 'gmlp_ln_g', 'gmlp_ln_b', 'gmlp_w_s', 'gmlp_b_s', 'gmlp_w_out', 's5_w_in', 's5_lam_re', 's5_lam_im', 's5_log_dt', 's5_b_re', 's5_b_im', 's5_c_re', 's5_c_im', 's5_d', 's5_w_out', 'ple_w_gate', 'ple_w_proj']
TWIN_WEIGHTS = ['norm_g', 'final_norm_g', 'ffn_w_gate', 'ffn_w_up', 'ffn_w_down', 'gmlp_w_in', 'gmlp_ln_g', 'gmlp_ln_b', 'gmlp_w_s', 'gmlp_b_s', 'gmlp_w_out', 's5_w_in', 's5_lam_re', 's5_lam_im', 's5_log_dt', 's5_b_re', 's5_b_im', 's5_c_re', 's5_c_im', 's5_d', 's5_w_out', 'ple_w_gate', 'ple_w_proj']
TWIN_DIFF_INPUT = 'x'
TWIN_INPUTS = ['x', 'p', 'norm_g', 'final_norm_g', 'ffn_w_gate', 'ffn_w_up', 'ffn_w_down', 'gmlp_w_in', 'gmlp_ln_g', 'gmlp_ln_b', 'gmlp_w_s', 'gmlp_b_s', 'gmlp_w_out', 's5_w_in', 's5_lam_re', 's5_lam_im', 's5_log_dt', 's5_b_re', 's5_b_im', 's5_c_re', 's5_c_im', 's5_d', 's5_w_out', 'ple_w_gate', 'ple_w_proj', 'loss_target', 'm_norm_g', 'm_final_norm_g', 'm_ffn_w_gate', 'm_ffn_w_up', 'm_ffn_w_down', 'm_gmlp_w_in', 'm_gmlp_ln_g', 'm_gmlp_ln_b', 'm_gmlp_w_s', 'm_gmlp_b_s', 'm_gmlp_w_out', 'm_s5_w_in', 'm_s5_lam_re', 'm_s5_lam_im', 'm_s5_log_dt', 'm_s5_b_re', 'm_s5_b_im', 'm_s5_c_re', 'm_s5_c_im', 'm_s5_d', 'm_s5_w_out', 'm_ple_w_gate', 'm_ple_w_proj', 'v_norm_g', 'v_final_norm_g', 'v_ffn_w_gate', 'v_ffn_w_up', 'v_ffn_w_down', 'v_gmlp_w_in', 'v_gmlp_ln_g', 'v_gmlp_ln_b', 'v_gmlp_w_s', 'v_gmlp_b_s', 'v_gmlp_w_out', 'v_s5_w_in', 'v_s5_lam_re', 'v_s5_lam_im', 'v_s5_log_dt', 'v_s5_b_re', 'v_s5_b_im', 'v_s5_c_re', 'v_s5_c_im', 'v_s5_d', 'v_s5_w_out', 'v_ple_w_gate', 'v_ple_w_proj']
TWIN_OUTPUTS = ['loss', 'grad_x', 'grad_norm_g', 'grad_final_norm_g', 'grad_ffn_w_gate', 'grad_ffn_w_up', 'grad_ffn_w_down', 'grad_gmlp_w_in', 'grad_gmlp_ln_g', 'grad_gmlp_ln_b', 'grad_gmlp_w_s', 'grad_gmlp_b_s', 'grad_gmlp_w_out', 'grad_s5_w_in', 'grad_s5_lam_re', 'grad_s5_lam_im', 'grad_s5_log_dt', 'grad_s5_b_re', 'grad_s5_b_im', 'grad_s5_c_re', 'grad_s5_c_im', 'grad_s5_d', 'grad_s5_w_out', 'grad_ple_w_gate', 'grad_ple_w_proj', 'delta_norm_g', 'delta_final_norm_g', 'delta_ffn_w_gate', 'delta_ffn_w_up', 'delta_ffn_w_down', 'delta_gmlp_w_in', 'delta_gmlp_ln_g', 'delta_gmlp_ln_b', 'delta_gmlp_w_s', 'delta_gmlp_b_s', 'delta_gmlp_w_out', 'delta_s5_w_in', 'delta_s5_lam_re', 'delta_s5_lam_im', 'delta_s5_log_dt', 'delta_s5_b_re', 'delta_s5_b_im', 'delta_s5_c_re', 'delta_s5_c_im', 'delta_s5_d', 'delta_s5_w_out', 'delta_ple_w_gate', 'delta_ple_w_proj', 'new_m_norm_g', 'new_m_final_norm_g', 'new_m_ffn_w_gate', 'new_m_ffn_w_up', 'new_m_ffn_w_down', 'new_m_gmlp_w_in', 'new_m_gmlp_ln_g', 'new_m_gmlp_ln_b', 'new_m_gmlp_w_s', 'new_m_gmlp_b_s', 'new_m_gmlp_w_out', 'new_m_s5_w_in', 'new_m_s5_lam_re', 'new_m_s5_lam_im', 'new_m_s5_log_dt', 'new_m_s5_b_re', 'new_m_s5_b_im', 'new_m_s5_c_re', 'new_m_s5_c_im', 'new_m_s5_d', 'new_m_s5_w_out', 'new_m_ple_w_gate', 'new_m_ple_w_proj', 'new_v_norm_g', 'new_v_final_norm_g', 'new_v_ffn_w_gate', 'new_v_ffn_w_up', 'new_v_ffn_w_down', 'new_v_gmlp_w_in', 'new_v_gmlp_ln_g', 'new_v_gmlp_ln_b', 'new_v_gmlp_w_s', 'new_v_gmlp_b_s', 'new_v_gmlp_w_out', 'new_v_s5_w_in', 'new_v_s5_lam_re', 'new_v_s5_lam_im', 'new_v_s5_log_dt', 'new_v_s5_b_re', 'new_v_s5_b_im', 'new_v_s5_c_re', 'new_v_s5_c_im', 'new_v_s5_d', 'new_v_s5_w_out', 'new_v_ple_w_gate', 'new_v_ple_w_proj']
TWIN_LEAF_KINDS = {'loss': 'loss', 'grad_x': 'grad_x', 'grad_norm_g': 'grad_w', 'grad_final_norm_g': 'grad_w', 'grad_ffn_w_gate': 'grad_w', 'grad_ffn_w_up': 'grad_w', 'grad_ffn_w_down': 'grad_w', 'grad_gmlp_w_in': 'grad_w', 'grad_gmlp_ln_g': 'grad_w', 'grad_gmlp_ln_b': 'grad_w', 'grad_gmlp_w_s': 'grad_w', 'grad_gmlp_b_s': 'grad_w', 'grad_gmlp_w_out': 'grad_w', 'grad_s5_w_in': 'grad_w', 'grad_s5_lam_re': 'grad_w', 'grad_s5_lam_im': 'grad_w', 'grad_s5_log_dt': 'grad_w', 'grad_s5_b_re': 'grad_w', 'grad_s5_b_im': 'grad_w', 'grad_s5_c_re': 'grad_w', 'grad_s5_c_im': 'grad_w', 'grad_s5_d': 'grad_w', 'grad_s5_w_out': 'grad_w', 'grad_ple_w_gate': 'grad_w', 'grad_ple_w_proj': 'grad_w', 'delta_norm_g': 'delta_w', 'delta_final_norm_g': 'delta_w', 'delta_ffn_w_gate': 'delta_w', 'delta_ffn_w_up': 'delta_w', 'delta_ffn_w_down': 'delta_w', 'delta_gmlp_w_in': 'delta_w', 'delta_gmlp_ln_g': 'delta_w', 'delta_gmlp_ln_b': 'delta_w', 'delta_gmlp_w_s': 'delta_w', 'delta_gmlp_b_s': 'delta_w', 'delta_gmlp_w_out': 'delta_w', 'delta_s5_w_in': 'delta_w', 'delta_s5_lam_re': 'delta_w', 'delta_s5_lam_im': 'delta_w', 'delta_s5_log_dt': 'delta_w', 'delta_s5_b_re': 'delta_w', 'delta_s5_b_im': 'delta_w', 'delta_s5_c_re': 'delta_w', 'delta_s5_c_im': 'delta_w', 'delta_s5_d': 'delta_w', 'delta_s5_w_out': 'delta_w', 'delta_ple_w_gate': 'delta_w', 'delta_ple_w_proj': 'delta_w', 'new_m_norm_g': 'new_m', 'new_m_final_norm_g': 'new_m', 'new_m_ffn_w_gate': 'new_m', 'new_m_ffn_w_up': 'new_m', 'new_m_ffn_w_down': 'new_m', 'new_m_gmlp_w_in': 'new_m', 'new_m_gmlp_ln_g': 'new_m', 'new_m_gmlp_ln_b': 'new_m', 'new_m_gmlp_w_s': 'new_m', 'new_m_gmlp_b_s': 'new_m', 'new_m_gmlp_w_out': 'new_m', 'new_m_s5_w_in': 'new_m', 'new_m_s5_lam_re': 'new_m', 'new_m_s5_lam_im': 'new_m', 'new_m_s5_log_dt': 'new_m', 'new_m_s5_b_re': 'new_m', 'new_m_s5_b_im': 'new_m', 'new_m_s5_c_re': 'new_m', 'new_m_s5_c_im': 'new_m', 'new_m_s5_d': 'new_m', 'new_m_s5_w_out': 'new_m', 'new_m_ple_w_gate': 'new_m', 'new_m_ple_w_proj': 'new_m', 'new_v_norm_g': 'new_v', 'new_v_final_norm_g': 'new_v', 'new_v_ffn_w_gate': 'new_v', 'new_v_ffn_w_up': 'new_v', 'new_v_ffn_w_down': 'new_v', 'new_v_gmlp_w_in': 'new_v', 'new_v_gmlp_ln_g': 'new_v', 'new_v_gmlp_ln_b': 'new_v', 'new_v_gmlp_w_s': 'new_v', 'new_v_gmlp_b_s': 'new_v', 'new_v_gmlp_w_out': 'new_v', 'new_v_s5_w_in': 'new_v', 'new_v_s5_lam_re': 'new_v', 'new_v_s5_lam_im': 'new_v', 'new_v_s5_log_dt': 'new_v', 'new_v_s5_b_re': 'new_v', 'new_v_s5_b_im': 'new_v', 'new_v_s5_c_re': 'new_v', 'new_v_s5_c_im': 'new_v', 'new_v_s5_d': 'new_v', 'new_v_s5_w_out': 'new_v', 'new_v_ple_w_gate': 'new_v', 'new_v_ple_w_proj': 'new_v'}


def _forward(args):
    return _fwd_reference(*[args[k] for k in FWD_PARAMS])


def _output_shape():
    out = _jax.eval_shape(lambda: _forward(_fwd_setup_inputs(0)))
    return out.shape, out.dtype

N_MICROBATCH = 1
ADAM_LR = 0.001
ADAM_B1 = 0.9
ADAM_B2 = 0.999
ADAM_EPS = 1e-08
ADAM_WD = 0.01
ADAM_STEP = 10
PER_EXAMPLE_BATCH_AXIS = {'x': 0, 'p': 1, 'loss_target': 0}
SHARED_INPUTS = []
_WEIGHT_DTYPES = {'norm_g': _jnp.float32, 'final_norm_g': _jnp.float32, 'ffn_w_gate': _jnp.float32, 'ffn_w_up': _jnp.float32, 'ffn_w_down': _jnp.float32, 'gmlp_w_in': _jnp.float32, 'gmlp_ln_g': _jnp.float32, 'gmlp_ln_b': _jnp.float32, 'gmlp_w_s': _jnp.float32, 'gmlp_b_s': _jnp.float32, 'gmlp_w_out': _jnp.float32, 's5_w_in': _jnp.float32, 's5_lam_re': _jnp.float32, 's5_lam_im': _jnp.float32, 's5_log_dt': _jnp.float32, 's5_b_re': _jnp.float32, 's5_b_im': _jnp.float32, 's5_c_re': _jnp.float32, 's5_c_im': _jnp.float32, 's5_d': _jnp.float32, 's5_w_out': _jnp.float32, 'ple_w_gate': _jnp.float32, 'ple_w_proj': _jnp.float32}
MOMENT_SCALE = {'norm_g': 2.243670e-02, 'final_norm_g': 8.006398e+00, 'ffn_w_gate': 8.796275e-03, 'ffn_w_up': 8.521514e-03, 'ffn_w_down': 1.412291e-02, 'gmlp_w_in': 1.824553e-02, 'gmlp_ln_g': 1.198341e-02, 'gmlp_ln_b': 1.173814e-02, 'gmlp_w_s': 2.073611e-02, 'gmlp_b_s': 2.972528e-02, 'gmlp_w_out': 3.723225e-02, 's5_w_in': 1.532630e-02, 's5_lam_re': 1.022990e-03, 's5_lam_im': 8.261593e-04, 's5_log_dt': 4.734321e-01, 's5_b_re': 5.286124e-04, 's5_b_im': 5.497276e-04, 's5_c_re': 1.033901e-03, 's5_c_im': 1.051999e-03, 's5_d': 1.581882e-02, 's5_w_out': 1.112821e-02, 'ple_w_gate': 8.751237e-03, 'ple_w_proj': 2.197257e-02}


def _to_microbatches(a, axis):
    t = _jnp.moveaxis(a, axis, 0)
    t = t.reshape((N_MICROBATCH, t.shape[0] // N_MICROBATCH) + t.shape[1:])
    return _jnp.moveaxis(t, 1, axis + 1)


def setup_inputs(seed: int = 0) -> dict:
    inp = _fwd_setup_inputs(seed)
    key = _jax.random.fold_in(_jax.random.key(seed), 7919)
    shape, _ = _output_shape()
    out = dict(inp)
    out["loss_target"] = _jax.random.normal(_jax.random.fold_in(key, 0), shape, _jnp.float32)
    for i, name in enumerate(TWIN_WEIGHTS):
        w = inp[name].astype(_jnp.float32)
        if MOMENT_SCALE is None:
            s = _jnp.sqrt(_jnp.mean(_jnp.square(w)) + 1e-30)
        else:
            s = MOMENT_SCALE[name]
        km, kv = _jax.random.split(_jax.random.fold_in(key, i + 1))
        out[name] = w
        out["m_" + name] = s * _jax.random.normal(km, w.shape, _jnp.float32)
        out["v_" + name] = (s * s) * _jax.random.uniform(kv, w.shape, _jnp.float32, 0.5, 1.5)
    if N_MICROBATCH > 1:
        for name, axis in PER_EXAMPLE_BATCH_AXIS.items():
            out[name] = _to_microbatches(out[name], axis)
    return {'x': out['x'], 'p': out['p'], 'norm_g': out['norm_g'], 'final_norm_g': out['final_norm_g'], 'ffn_w_gate': out['ffn_w_gate'], 'ffn_w_up': out['ffn_w_up'], 'ffn_w_down': out['ffn_w_down'], 'gmlp_w_in': out['gmlp_w_in'], 'gmlp_ln_g': out['gmlp_ln_g'], 'gmlp_ln_b': out['gmlp_ln_b'], 'gmlp_w_s': out['gmlp_w_s'], 'gmlp_b_s': out['gmlp_b_s'], 'gmlp_w_out': out['gmlp_w_out'], 's5_w_in': out['s5_w_in'], 's5_lam_re': out['s5_lam_re'], 's5_lam_im': out['s5_lam_im'], 's5_log_dt': out['s5_log_dt'], 's5_b_re': out['s5_b_re'], 's5_b_im': out['s5_b_im'], 's5_c_re': out['s5_c_re'], 's5_c_im': out['s5_c_im'], 's5_d': out['s5_d'], 's5_w_out': out['s5_w_out'], 'ple_w_gate': out['ple_w_gate'], 'ple_w_proj': out['ple_w_proj'], 'loss_target': out['loss_target'], 'm_norm_g': out['m_norm_g'], 'm_final_norm_g': out['m_final_norm_g'], 'm_ffn_w_gate': out['m_ffn_w_gate'], 'm_ffn_w_up': out['m_ffn_w_up'], 'm_ffn_w_down': out['m_ffn_w_down'], 'm_gmlp_w_in': out['m_gmlp_w_in'], 'm_gmlp_ln_g': out['m_gmlp_ln_g'], 'm_gmlp_ln_b': out['m_gmlp_ln_b'], 'm_gmlp_w_s': out['m_gmlp_w_s'], 'm_gmlp_b_s': out['m_gmlp_b_s'], 'm_gmlp_w_out': out['m_gmlp_w_out'], 'm_s5_w_in': out['m_s5_w_in'], 'm_s5_lam_re': out['m_s5_lam_re'], 'm_s5_lam_im': out['m_s5_lam_im'], 'm_s5_log_dt': out['m_s5_log_dt'], 'm_s5_b_re': out['m_s5_b_re'], 'm_s5_b_im': out['m_s5_b_im'], 'm_s5_c_re': out['m_s5_c_re'], 'm_s5_c_im': out['m_s5_c_im'], 'm_s5_d': out['m_s5_d'], 'm_s5_w_out': out['m_s5_w_out'], 'm_ple_w_gate': out['m_ple_w_gate'], 'm_ple_w_proj': out['m_ple_w_proj'], 'v_norm_g': out['v_norm_g'], 'v_final_norm_g': out['v_final_norm_g'], 'v_ffn_w_gate': out['v_ffn_w_gate'], 'v_ffn_w_up': out['v_ffn_w_up'], 'v_ffn_w_down': out['v_ffn_w_down'], 'v_gmlp_w_in': out['v_gmlp_w_in'], 'v_gmlp_ln_g': out['v_gmlp_ln_g'], 'v_gmlp_ln_b': out['v_gmlp_ln_b'], 'v_gmlp_w_s': out['v_gmlp_w_s'], 'v_gmlp_b_s': out['v_gmlp_b_s'], 'v_gmlp_w_out': out['v_gmlp_w_out'], 'v_s5_w_in': out['v_s5_w_in'], 'v_s5_lam_re': out['v_s5_lam_re'], 'v_s5_lam_im': out['v_s5_lam_im'], 'v_s5_log_dt': out['v_s5_log_dt'], 'v_s5_b_re': out['v_s5_b_re'], 'v_s5_b_im': out['v_s5_b_im'], 'v_s5_c_re': out['v_s5_c_re'], 'v_s5_c_im': out['v_s5_c_im'], 'v_s5_d': out['v_s5_d'], 'v_s5_w_out': out['v_s5_w_out'], 'v_ple_w_gate': out['v_ple_w_gate'], 'v_ple_w_proj': out['v_ple_w_proj']}


def _loss(weights, diff, rest, loss_target):
    with _jax.named_scope("forward"):
        args = {**rest, TWIN_DIFF_INPUT: diff, **{k: w.astype(_WEIGHT_DTYPES[k]) for k, w in weights.items()}}
        y = _forward(args)
    with _jax.named_scope("loss_head"):
        err = _jnp.square(y.astype(_jnp.float32) - loss_target)
        return 0.5 * _jnp.sum(_jnp.mean(err, axis=-1)) if err.ndim else 0.5 * err


def _adamw(w, g, m, v):
    m = ADAM_B1 * m + (1.0 - ADAM_B1) * g
    v = ADAM_B2 * v + (1.0 - ADAM_B2) * _jnp.square(g)
    m_hat = m / (1.0 - ADAM_B1 ** ADAM_STEP)
    v_hat = v / (1.0 - ADAM_B2 ** ADAM_STEP)
    delta = -ADAM_LR * (m_hat / (_jnp.sqrt(v_hat) + ADAM_EPS) + ADAM_WD * w)
    return delta, m, v


def reference(x, p, norm_g, final_norm_g, ffn_w_gate, ffn_w_up, ffn_w_down, gmlp_w_in, gmlp_ln_g, gmlp_ln_b, gmlp_w_s, gmlp_b_s, gmlp_w_out, s5_w_in, s5_lam_re, s5_lam_im, s5_log_dt, s5_b_re, s5_b_im, s5_c_re, s5_c_im, s5_d, s5_w_out, ple_w_gate, ple_w_proj, loss_target, m_norm_g, m_final_norm_g, m_ffn_w_gate, m_ffn_w_up, m_ffn_w_down, m_gmlp_w_in, m_gmlp_ln_g, m_gmlp_ln_b, m_gmlp_w_s, m_gmlp_b_s, m_gmlp_w_out, m_s5_w_in, m_s5_lam_re, m_s5_lam_im, m_s5_log_dt, m_s5_b_re, m_s5_b_im, m_s5_c_re, m_s5_c_im, m_s5_d, m_s5_w_out, m_ple_w_gate, m_ple_w_proj, v_norm_g, v_final_norm_g, v_ffn_w_gate, v_ffn_w_up, v_ffn_w_down, v_gmlp_w_in, v_gmlp_ln_g, v_gmlp_ln_b, v_gmlp_w_s, v_gmlp_b_s, v_gmlp_w_out, v_s5_w_in, v_s5_lam_re, v_s5_lam_im, v_s5_log_dt, v_s5_b_re, v_s5_b_im, v_s5_c_re, v_s5_c_im, v_s5_d, v_s5_w_out, v_ple_w_gate, v_ple_w_proj):
    given = dict(x=x, p=p, norm_g=norm_g, final_norm_g=final_norm_g, ffn_w_gate=ffn_w_gate, ffn_w_up=ffn_w_up, ffn_w_down=ffn_w_down, gmlp_w_in=gmlp_w_in, gmlp_ln_g=gmlp_ln_g, gmlp_ln_b=gmlp_ln_b, gmlp_w_s=gmlp_w_s, gmlp_b_s=gmlp_b_s, gmlp_w_out=gmlp_w_out, s5_w_in=s5_w_in, s5_lam_re=s5_lam_re, s5_lam_im=s5_lam_im, s5_log_dt=s5_log_dt, s5_b_re=s5_b_re, s5_b_im=s5_b_im, s5_c_re=s5_c_re, s5_c_im=s5_c_im, s5_d=s5_d, s5_w_out=s5_w_out, ple_w_gate=ple_w_gate, ple_w_proj=ple_w_proj, loss_target=loss_target, m_norm_g=m_norm_g, m_final_norm_g=m_final_norm_g, m_ffn_w_gate=m_ffn_w_gate, m_ffn_w_up=m_ffn_w_up, m_ffn_w_down=m_ffn_w_down, m_gmlp_w_in=m_gmlp_w_in, m_gmlp_ln_g=m_gmlp_ln_g, m_gmlp_ln_b=m_gmlp_ln_b, m_gmlp_w_s=m_gmlp_w_s, m_gmlp_b_s=m_gmlp_b_s, m_gmlp_w_out=m_gmlp_w_out, m_s5_w_in=m_s5_w_in, m_s5_lam_re=m_s5_lam_re, m_s5_lam_im=m_s5_lam_im, m_s5_log_dt=m_s5_log_dt, m_s5_b_re=m_s5_b_re, m_s5_b_im=m_s5_b_im, m_s5_c_re=m_s5_c_re, m_s5_c_im=m_s5_c_im, m_s5_d=m_s5_d, m_s5_w_out=m_s5_w_out, m_ple_w_gate=m_ple_w_gate, m_ple_w_proj=m_ple_w_proj, v_norm_g=v_norm_g, v_final_norm_g=v_final_norm_g, v_ffn_w_gate=v_ffn_w_gate, v_ffn_w_up=v_ffn_w_up, v_ffn_w_down=v_ffn_w_down, v_gmlp_w_in=v_gmlp_w_in, v_gmlp_ln_g=v_gmlp_ln_g, v_gmlp_ln_b=v_gmlp_ln_b, v_gmlp_w_s=v_gmlp_w_s, v_gmlp_b_s=v_gmlp_b_s, v_gmlp_w_out=v_gmlp_w_out, v_s5_w_in=v_s5_w_in, v_s5_lam_re=v_s5_lam_re, v_s5_lam_im=v_s5_lam_im, v_s5_log_dt=v_s5_log_dt, v_s5_b_re=v_s5_b_re, v_s5_b_im=v_s5_b_im, v_s5_c_re=v_s5_c_re, v_s5_c_im=v_s5_c_im, v_s5_d=v_s5_d, v_s5_w_out=v_s5_w_out, v_ple_w_gate=v_ple_w_gate, v_ple_w_proj=v_ple_w_proj)
    weights = {n: given[n] for n in TWIN_WEIGHTS}
    shared = {n: given[n] for n in SHARED_INPUTS}
    per_example = {n: given[n] for n in ['x', 'p']}
    grad_fn = _jax.value_and_grad(_loss, argnums=(0, 1))

    def one_microbatch(ex, loss_target):
        ex = dict(ex)
        diff = ex.pop(TWIN_DIFF_INPUT)
        return grad_fn(weights, diff, {**shared, **ex}, loss_target)

    if N_MICROBATCH == 1:
        loss, (grad_w, grad_x) = one_microbatch(per_example, given["loss_target"])
    else:
        def body(carry, xs):
            loss_sum, grad_sum = carry
            l_k, (gw_k, gx_k) = one_microbatch(xs[0], xs[1])
            with _jax.named_scope("update"):
                return (loss_sum + l_k, _jax.tree.map(_jnp.add, grad_sum, gw_k)), gx_k

        init = (_jnp.zeros((), _jnp.float32), _jax.tree.map(_jnp.zeros_like, weights))
        (loss, grad_w), grad_x = _jax.lax.scan(body, init, (per_example, given["loss_target"]))
    with _jax.named_scope("update"):
        delta_w, new_m, new_v = {}, {}, {}
        for n in TWIN_WEIGHTS:
            delta_w[n], new_m[n], new_v[n] = _adamw(weights[n], grad_w[n], given["m_" + n], given["v_" + n])
    return (loss, grad_x, *[grad_w[n] for n in TWIN_WEIGHTS], *[delta_w[n] for n in TWIN_WEIGHTS],
            *[new_m[n] for n in TWIN_WEIGHTS], *[new_v[n] for n in TWIN_WEIGHTS])
```

```python
import functools
import math

import jax
import jax.numpy as jnp
from jax import lax
from jax.experimental import pallas as pl
from jax.experimental.pallas import tpu as pltpu

F32 = jnp.float32
BF16 = jnp.bfloat16
MESH_ID = pl.DeviceIdType.MESH

EPS = 1e-6
ADAM_LR = 0.001
ADAM_B1 = 0.9
ADAM_B2 = 0.999
ADAM_EPS = 1e-08
ADAM_WD = 0.01
ADAM_STEP = 10

N_SHARD = 4
V7X_VMEM_LIMIT = 52 * 2 ** 20
SSM_TILE_GROUPS = 16
SCAN_LANES = 1024
GELU_C = math.sqrt(2.0 / math.pi)


def _pcall(body, **kw):
    return pl.pallas_call(body, **kw)


def _params():
    return pltpu.CompilerParams(vmem_limit_bytes=V7X_VMEM_LIMIT)


def _pick(n, cands):
    for c in cands:
        if c <= n and n % c == 0:
            return c
    return n


def _sigmoid(x):
    return 1.0 / (1.0 + jnp.exp(-x))


def _gelu(x):
    return 0.5 * x * (1.0 + jnp.tanh(GELU_C * (x + 0.044715 * x * x * x)))


def _gelu_grad(x):
    t = jnp.tanh(GELU_C * (x + 0.044715 * x * x * x))
    return 0.5 * (1.0 + t) + 0.5 * x * (1.0 - t * t) * GELU_C * (1.0 + 3.0 * 0.044715 * x * x)


def tilemap(fn, ins, outs, *, M, N, tm, tn, name):
    n_in = len(ins)
    grid = (N // tn, M // tm)
    in_specs = []
    for arr, kind, off in ins:
        if kind == 't':
            in_specs.append(pl.BlockSpec((tm, tn), lambda j, i, off=off: (i, j + off)))
        else:
            in_specs.append(pl.BlockSpec((1, tn), lambda j, i, off=off: (0, j + off)))
    out_specs, out_shape = [], []
    for dt, kind in outs:
        if kind == 't':
            out_specs.append(pl.BlockSpec((tm, tn), lambda j, i: (i, j)))
            out_shape.append(jax.ShapeDtypeStruct((M, N), dt))
        else:
            out_specs.append(pl.BlockSpec((1, tn), lambda j, i: (0, j)))
            out_shape.append(jax.ShapeDtypeStruct((1, N), F32))

    def body(*refs):
        vals = fn(*[r[...] for r in refs[:n_in]])
        for (dt, kind), ref, v in zip(outs, refs[n_in:], vals):
            if kind == 't':
                ref[...] = v.astype(ref.dtype)
            else:
                @pl.when(pl.program_id(1) == 0)
                def _():
                    ref[...] = jnp.zeros_like(ref)
                ref[...] += v

    res = _pcall(body, name=name, grid=grid, in_specs=in_specs, out_specs=out_specs,
                 out_shape=out_shape, compiler_params=_params())(*[a for a, _, _ in ins])
    return res


def _as2d(a):
    if a.ndim >= 2 and a.shape[-1] % 128 == 0:
        return a.reshape(-1, a.shape[-1])
    if a.size % 128 == 0:
        return a.reshape(-1, 128)
    return a.reshape(-1, a.shape[-1])


def _row_tile(rows, cols, nbytes=4, budget=1 << 20):
    cands = [c for c in (2048, 1024, 512, 256, 128, 64, 32, 16, 8) if c * cols * nbytes <= budget]
    return _pick(rows, cands) if cands else _pick(rows, (8,))


def cast_bf16(w):
    w2 = _as2d(w)
    R, C = w2.shape
    tm = _row_tile(R, C)
    (o,) = tilemap(lambda v: (v,), [(w2, 't', 0)], [(BF16, 't')], M=R, N=C, tm=tm, tn=C, name="cast_bf16")
    return o.reshape(w.shape)


def sum_slots(a, out_dtype, name):
    n, R, C = a.shape
    tm = _row_tile(R, C, nbytes=4 * n // 2 + 4)

    def body(a_ref, o_ref):
        acc = a_ref[0].astype(F32)
        for k in range(1, n):
            acc = acc + a_ref[k].astype(F32)
        o_ref[...] = acc.astype(o_ref.dtype)

    return _pcall(body, name=name, grid=(R // tm,),
                  in_specs=[pl.BlockSpec((n, tm, C), lambda i: (0, i, 0))],
                  out_specs=pl.BlockSpec((tm, C), lambda i: (i, 0)),
                  out_shape=jax.ShapeDtypeStruct((R, C), out_dtype), compiler_params=_params())(a)


_DIMS = {'nn': (((1,), (0,)), ((), ())), 'nt': (((1,), (1,)), ((), ())), 'tn': (((0,), (0,)), ((), ()))}


def matmul(pairs, mode, outs, *, epilogue=None, extras=(), tm=512, tn=512, name):
    a0, b0 = pairs[0]
    if mode == 'nn':
        (M, K), N = a0.shape, b0.shape[1]
    elif mode == 'nt':
        (M, K), N = a0.shape, b0.shape[0]
    else:
        (K, M), N = a0.shape, b0.shape[1]
    tm, tn = _pick(M, (tm, 256, 128)), _pick(N, (tn, 256, 128))
    n_p, n_e = len(pairs), len(extras)
    if mode == 'tn':
        a_spec = pl.BlockSpec((K, tm), lambda i, j: (0, i))
    else:
        a_spec = pl.BlockSpec((tm, K), lambda i, j: (i, 0))
    if mode == 'nt':
        b_spec = pl.BlockSpec((tn, K), lambda i, j: (j, 0))
    else:
        b_spec = pl.BlockSpec((K, tn), lambda i, j: (0, j))
    in_specs, args = [], []
    for a, b in pairs:
        in_specs += [a_spec, b_spec]
        args += [a, b]
    for arr, kind in extras:
        if kind == 't':
            in_specs.append(pl.BlockSpec((tm, tn), lambda i, j: (i, j)))
        else:
            in_specs.append(pl.BlockSpec((1, tn), lambda i, j: (0, j)))
        args.append(arr)
    dims = _DIMS[mode]

    def body(*refs):
        accs = [lax.dot_general(refs[2 * p][...].astype(BF16), refs[2 * p + 1][...].astype(BF16), dims,
                                preferred_element_type=F32) for p in range(n_p)]
        ex = [r[...] for r in refs[2 * n_p:2 * n_p + n_e]]
        if epilogue is None:
            acc = accs[0]
            for other in accs[1:]:
                acc = acc + other
            res = (acc,)
        else:
            res = epilogue(accs, ex)
        for ref, v in zip(refs[2 * n_p + n_e:], res):
            ref[...] = v.astype(ref.dtype)

    return _pcall(body, name=name, grid=(M // tm, N // tn), in_specs=in_specs,
                  out_specs=[pl.BlockSpec((tm, tn), lambda i, j: (i, j)) for _ in outs],
                  out_shape=[jax.ShapeDtypeStruct((M, N), dt) for dt in outs],
                  compiler_params=_params())(*args)


def comm_call(name, ins, out_shapes, plan, n_local, n_remote, aliases=None):
    n_in, n_out = len(ins), len(out_shapes)

    def body(*refs):
        in_refs, out_refs = refs[:n_in], refs[n_in:n_in + n_out]
        lsem, ssem, rsem = refs[n_in + n_out:]
        me = (lax.axis_index("x"), lax.axis_index("y"), lax.axis_index("c"))
        local, remote = plan(me, in_refs, out_refs)
        assert len(local) == n_local and len(remote) == n_remote
        lcs = [pltpu.make_async_copy(s, d, lsem.at[k]) for k, (s, d) in enumerate(local)]
        rcs = [pltpu.make_async_remote_copy(src_ref=s, dst_ref=d, send_sem=ssem.at[k], recv_sem=rsem.at[k],
                                            device_id=peer, device_id_type=MESH_ID)
               for k, (s, d, peer) in enumerate(remote)]
        for cp in rcs:
            cp.start()
        for cp in lcs:
            cp.start()
        for cp in rcs:
            cp.wait()
        for cp in lcs:
            cp.wait()

    any_spec = pl.BlockSpec(memory_space=pl.ANY)
    return _pcall(body, name=name, in_specs=[any_spec] * n_in, out_specs=[any_spec] * n_out,
                  out_shape=list(out_shapes),
                  scratch_shapes=[pltpu.SemaphoreType.DMA((max(n_local, 1),)),
                                  pltpu.SemaphoreType.DMA((max(n_remote, 1),)),
                                  pltpu.SemaphoreType.DMA((max(n_remote, 1),))],
                  input_output_aliases=aliases or {},
                  compiler_params=pltpu.CompilerParams(has_side_effects=True))(*ins)


def _shard_of(me):
    return 2 * me[0] + me[1]


def _plane_peers(me):
    x, y, c = me
    return [((1 - x, y, c), 2 * (1 - x) + y), ((x, 1 - y, c), 2 * x + 1 - y),
            ((1 - x, 1 - y, c), 2 * (1 - x) + 1 - y)]


def _mats(arr):
    out = [()]
    for n in arr.shape[:-2]:
        out = [o + (k,) for o in out for k in range(n)]
    return out


ROW_ALIGN = 16
LANE_ALIGN = 128


def _win(ref, lead, rows, cols):
    idx = tuple(lead)
    for spec, align in ((rows, ROW_ALIGN), (cols, LANE_ALIGN)):
        if spec is None:
            idx += (slice(None),)
        else:
            start, size = spec
            if not isinstance(start, int):
                start = pl.multiple_of(start, align)
            idx += (pl.ds(start, size),)
    return ref.at[idx]


def gather_weights(name, shards, axes, split):
    full_shapes = []
    for a, ax in zip(shards, axes):
        shp = list(a.shape)
        shp[a.ndim - 2 + ax] *= N_SHARD
        full_shapes.append(jax.ShapeDtypeStruct(tuple(shp), a.dtype))
    n_mats = sum(len(_mats(a)) for a in shards)

    def window(a, ax, s, half):
        R, C = a.shape[-2:]
        r0 = s * R if ax == 0 else 0
        rows = (r0, R) if half is None else (r0 + half * (R // 2), R // 2)
        cols = None if ax == 0 else (s * C, C)
        return rows, cols

    def plan1(me, in_refs, out_refs):
        s, c = _shard_of(me), me[2]
        local, remote = [], []
        for t, (a, ax) in enumerate(zip(shards, axes)):
            for lead in _mats(a):
                rows, cols = window(a, ax, s, None)
                local.append((_win(in_refs[t], lead, None, None), _win(out_refs[t], lead, rows, cols)))
                if split[t]:
                    R = a.shape[-2]
                    src = _win(in_refs[t], lead, (c * (R // 2), R // 2), None)
                    rows, cols = window(a, ax, s, c)
                else:
                    src = _win(in_refs[t], lead, None, None)
                for peer, _ in _plane_peers(me):
                    remote.append((src, _win(out_refs[t], lead, rows, cols), peer))
        return local, remote

    full = comm_call(name + "_ici", shards, full_shapes, plan1, n_mats, 3 * n_mats)
    if not any(split):
        return full
    idx = [t for t in range(len(shards)) if split[t]]
    n_split = sum(len(_mats(shards[t])) for t in idx)

    def plan2(me, in_refs, out_refs):
        x, y, c = me
        remote = []
        for k, t in enumerate(idx):
            a, ax = shards[t], axes[t]
            for lead in _mats(a):
                for _, ps in _plane_peers(me):
                    rows, cols = window(a, ax, ps, c)
                    remote.append((_win(in_refs[k], lead, rows, cols), _win(out_refs[k], lead, rows, cols),
                                   (x, y, 1 - c)))
        return [], remote

    done = comm_call(name + "_d2d", [full[t] for t in idx], [full_shapes[t] for t in idx], plan2, 0,
                     3 * n_split, aliases={k: k for k in range(len(idx))})
    full = list(full)
    for k, t in enumerate(idx):
        full[t] = done[k]
    return full


def reduce_scatter_grads(name, grads, axes):
    mats = [(t, lead) for t, g in enumerate(grads) for lead in _mats(g)]
    n = len(mats)

    def half_shape(g, ax):
        R, C = g.shape[-2:]
        return (R // 2, C) if ax == 1 else (R, C // 2)

    def half_win(g, ax, h):
        R, C = g.shape[-2:]
        if ax == 1:
            return (h * (R // 2), R // 2), None
        return None, (h * (C // 2), C // 2)

    a_shapes = [jax.ShapeDtypeStruct((2,) + half_shape(grads[t], axes[t]), BF16) for t, _ in mats]

    def plan_a(me, in_refs, out_refs):
        x, y, c = me
        local, remote = [], []
        for k, (t, lead) in enumerate(mats):
            g, ax = grads[t], axes[t]
            rows, cols = half_win(g, ax, c)
            local.append((_win(in_refs[t], lead, rows, cols), out_refs[k].at[0]))
            rows, cols = half_win(g, ax, 1 - c)
            remote.append((_win(in_refs[t], lead, rows, cols), out_refs[k].at[1], (x, y, 1 - c)))
        return local, remote

    a_bufs = comm_call(name + "_pair", grads, a_shapes, plan_a, n, n)
    b_bufs = [sum_slots(a, BF16, name + "_pairsum") for a in a_bufs]

    def piece_shape(g, ax):
        R, C = half_shape(g, ax)
        return (R, C // N_SHARD) if ax == 1 else (R // N_SHARD, C)

    def piece_win(g, ax, s):
        R, C = piece_shape(g, ax)
        if ax == 1:
            return None, (s * C, C)
        return (s * R, R), None

    c_shapes = [jax.ShapeDtypeStruct((N_SHARD,) + piece_shape(grads[t], axes[t]), BF16) for t, _ in mats]

    def plan_c(me, in_refs, out_refs):
        s = _shard_of(me)
        local, remote = [], []
        for k, (t, lead) in enumerate(mats):
            g, ax = grads[t], axes[t]
            rows, cols = piece_win(g, ax, s)
            local.append((_win(in_refs[k], (), rows, cols), out_refs[k].at[s]))
            for peer, ps in _plane_peers(me):
                rows, cols = piece_win(g, ax, ps)
                remote.append((_win(in_refs[k], (), rows, cols), out_refs[k].at[s], peer))
        return local, remote

    c_bufs = comm_call(name + "_ici", b_bufs, c_shapes, plan_c, n, 3 * n)
    f_bufs = [sum_slots(cb, F32, name + "_shardsum") for cb in c_bufs]

    out_shapes = []
    for g, ax in zip(grads, axes):
        shp = list(g.shape)
        shp[g.ndim - 2 + ax] //= N_SHARD
        out_shapes.append(jax.ShapeDtypeStruct(tuple(shp), F32))

    def plan_e(me, in_refs, out_refs):
        x, y, c = me
        local, remote = [], []
        for k, (t, lead) in enumerate(mats):
            g, ax = grads[t], axes[t]
            R, C = out_shapes[t].shape[-2:]
            if ax == 1:
                rows, cols = (c * (R // 2), R // 2), None
            else:
                rows, cols = None, (c * (C // 2), C // 2)
            dst = _win(out_refs[t], lead, rows, cols)
            local.append((in_refs[k], dst))
            remote.append((in_refs[k], dst, (x, y, 1 - c)))
        return local, remote

    return comm_call(name + "_swap", f_bufs, out_shapes, plan_e, n, n)


def allreduce_small(flat):
    R, C = flat.shape

    def plan(me, in_refs, out_refs):
        x, y, c = me
        slot = 4 * x + 2 * y + c
        local = [(in_refs[0], out_refs[0].at[slot])]
        remote = []
        for fx in (0, 1):
            for fy in (0, 1):
                for fc in (0, 1):
                    if fx or fy or fc:
                        peer = (x + fx * (1 - 2 * x), y + fy * (1 - 2 * y), c + fc * (1 - 2 * c))
                        remote.append((in_refs[0], out_refs[0].at[slot], peer))
        return local, remote

    (buf,) = comm_call("small_allgather", [flat], [jax.ShapeDtypeStruct((8, R, C), F32)], plan, 1, 7)
    return sum_slots(buf, F32, "small_sum")


def rms_fwd(x, g, name):
    M, D = x.shape
    tm = _pick(M, (256, 128))

    def fn(xv, gv):
        r = lax.rsqrt(jnp.mean(xv * xv, axis=-1, keepdims=True) + EPS)
        return (xv * r * gv,)

    (h,) = tilemap(fn, [(x, 't', 0), (g, 'r', 0)], [(BF16, 't')], M=M, N=D, tm=tm, tn=D, name=name)
    return h


def rms_bwd(x, g, dh, dres, name):
    M, D = x.shape
    tm = _pick(M, (256, 128))

    def fn(xv, gv, dhv, drv):
        r = lax.rsqrt(jnp.mean(xv * xv, axis=-1, keepdims=True) + EPS)
        xh = xv * r
        dxh = dhv * gv
        m = jnp.mean(dxh * xh, axis=-1, keepdims=True)
        dx = drv + r * (dxh - xh * m)
        return dx, dx, jnp.sum(dhv * xh, axis=0, keepdims=True)

    return tilemap(fn, [(x, 't', 0), (g, 'r', 0), (dh, 't', 0), (dres, 't', 0)],
                   [(F32, 't'), (BF16, 't'), (F32, 'a')], M=M, N=D, tm=tm, tn=D, name=name)


def ffn_fwd(x, g, wg, wu, wd, tag):
    h = rms_fwd(x, g, f"ffn_norm_{tag}")

    def ep(accs, ex):
        a, b = accs
        return a, b, a * _sigmoid(a) * b

    a, b, s = matmul([(h, wg), (h, wu)], 'nn', [F32, F32, BF16], epilogue=ep, tm=512, tn=512,
                     name=f"ffn_gateup_{tag}")
    (xo,) = matmul([(s, wd)], 'nn', [F32], epilogue=lambda accs, ex: (ex[0] + 0.5 * accs[0],),
                   extras=[(x, 't')], tm=512, tn=512, name=f"ffn_down_{tag}")
    return xo, (x, h, a, b, s)


def ffn_bwd(saved, g, wg, wu, wd, dxo, dxo_bf, tag):
    x, h, a, b, s = saved

    def ep(accs, ex):
        ds = 0.5 * accs[0]
        av, bv = ex
        sg = _sigmoid(av)
        return ds * bv * (sg * (1.0 + av * (1.0 - sg))), ds * (av * sg)

    da, db = matmul([(dxo_bf, wd)], 'nt', [BF16, BF16], epilogue=ep, extras=[(a, 't'), (b, 't')],
                    tm=512, tn=512, name=f"ffn_dact_{tag}")
    (dwd,) = matmul([(s, dxo_bf)], 'tn', [BF16], epilogue=lambda accs, ex: (0.5 * accs[0],),
                    tm=512, tn=512, name=f"ffn_dwd_{tag}")
    dwg, dwu = matmul([(h, da), (h, db)], 'tn', [BF16, BF16], epilogue=lambda accs, ex: tuple(accs),
                      tm=512, tn=512, name=f"ffn_dwgu_{tag}")
    (dh,) = matmul([(da, wg), (db, wu)], 'nt', [F32], tm=512, tn=256, name=f"ffn_dh_{tag}")
    dx, dx_bf, dg = rms_bwd(x, g, dh, dxo, f"ffn_dnorm_{tag}")
    return dx, dx_bf, dg, dwg, dwu, dwd


def _tril_mask(n):
    return lax.broadcasted_iota(jnp.int32, (n, n), 0) >= lax.broadcasted_iota(jnp.int32, (n, n), 1)


def _gmlp_specs(L, half, n_grp, chunk):
    gd = half // n_grp
    specs = [pl.BlockSpec((chunk, gd), lambda g, n: (n, g)),
             pl.BlockSpec((chunk, gd), lambda g, n: (n, n_grp + g)),
             pl.BlockSpec((1, gd), lambda g, n: (0, g)),
             pl.BlockSpec((1, gd), lambda g, n: (0, g)),
             pl.BlockSpec((None, chunk, chunk), lambda g, n: (g, 0, 0)),
             pl.BlockSpec((None, chunk, 1), lambda g, n: (g, 0, 0))]
    return gd, specs


def _gmlp_gate_values(zu, zv, lg, lb, ws, bs):
    u, v = _gelu(zu), _gelu(zv)
    mu = jnp.mean(v, axis=-1, keepdims=True)
    d = v - mu
    rstd = lax.rsqrt(jnp.mean(d * d, axis=-1, keepdims=True) + EPS)
    vhat = d * rstd
    vn = vhat * lg + lb
    w = jnp.where(_tril_mask(ws.shape[0]), ws, 0.0).astype(BF16)
    sv = jnp.dot(w, vn.astype(BF16), preferred_element_type=F32) + bs
    return u, vhat, rstd, vn, w, sv


def gmlp_gate_fwd(zpre, ln_g, ln_b, w_s, b_s):
    L, half = zpre.shape[0], zpre.shape[1] // 2
    n_grp, chunk = w_s.shape[0], w_s.shape[1]
    gd, specs = _gmlp_specs(L, half, n_grp, chunk)

    def body(zu, zv, lg, lb, ws, bs, o):
        u, _, _, _, _, sv = _gmlp_gate_values(zu[...], zv[...], lg[...], lb[...], ws[...], bs[...])
        o[...] = (u * sv).astype(o.dtype)

    return _pcall(body, name="gmlp_gate", grid=(n_grp, L // chunk), in_specs=specs,
                  out_specs=pl.BlockSpec((chunk, gd), lambda g, n: (n, g)),
                  out_shape=jax.ShapeDtypeStruct((L, half), BF16), compiler_params=_params())(
        zpre, zpre, ln_g, ln_b, w_s, b_s)


def gmlp_gate_bwd(zpre, ln_g, ln_b, w_s, b_s, dgated):
    L, half = zpre.shape[0], zpre.shape[1] // 2
    n_grp, chunk = w_s.shape[0], w_s.shape[1]
    gd, specs = _gmlp_specs(L, half, n_grp, chunk)
    specs = specs + [pl.BlockSpec((chunk, gd), lambda g, n: (n, g))]

    def body(zu, zv, lg, lb, ws, bs, dg, dzu, dzv, dws, dbs, dlg, dlb):
        zuv, zvv, lgv = zu[...], zv[...], lg[...]
        u, vhat, rstd, vn, w, sv = _gmlp_gate_values(zuv, zvv, lgv, lb[...], ws[...], bs[...])
        dgv = dg[...]
        du = dgv * sv
        dsv = dgv * u
        dsv_bf = dsv.astype(BF16)
        dw = lax.dot_general(dsv_bf, vn.astype(BF16), _DIMS['nt'], preferred_element_type=F32)
        dvn = lax.dot_general(w, dsv_bf, _DIMS['tn'], preferred_element_type=F32)
        dvhat = dvn * lgv
        dv = rstd * (dvhat - jnp.mean(dvhat, axis=-1, keepdims=True)
                     - vhat * jnp.mean(dvhat * vhat, axis=-1, keepdims=True))
        dzu[...] = (du * _gelu_grad(zuv)).astype(dzu.dtype)
        dzv[...] = (dv * _gelu_grad(zvv)).astype(dzv.dtype)

        @pl.when(pl.program_id(1) == 0)
        def _():
            dws[...] = jnp.zeros_like(dws)
            dbs[...] = jnp.zeros_like(dbs)
            dlg[...] = jnp.zeros_like(dlg)
            dlb[...] = jnp.zeros_like(dlb)

        dws[...] += jnp.where(_tril_mask(chunk), dw, 0.0)
        dbs[...] += jnp.sum(dsv, axis=1, keepdims=True)
        dlg[...] += jnp.sum(dvn * vhat, axis=0, keepdims=True)
        dlb[...] += jnp.sum(dvn, axis=0, keepdims=True)

    tile = pl.BlockSpec((chunk, gd), lambda g, n: (n, g))
    vec = pl.BlockSpec((1, gd), lambda g, n: (0, g))
    return _pcall(body, name="gmlp_gate_bwd", grid=(n_grp, L // chunk), in_specs=specs,
                  out_specs=[tile, tile, pl.BlockSpec((None, chunk, chunk), lambda g, n: (g, 0, 0)),
                             pl.BlockSpec((None, chunk, 1), lambda g, n: (g, 0, 0)), vec, vec],
                  out_shape=[jax.ShapeDtypeStruct((L, half), BF16), jax.ShapeDtypeStruct((L, half), BF16),
                             jax.ShapeDtypeStruct((n_grp, chunk, chunk), F32),
                             jax.ShapeDtypeStruct((n_grp, chunk, 1), F32),
                             jax.ShapeDtypeStruct((1, half), F32), jax.ShapeDtypeStruct((1, half), F32)],
                  compiler_params=_params())(zpre, zpre, ln_g, ln_b, w_s, b_s, dgated)


def gmlp_fwd(x, g, w_in, ln_g, ln_b, w_s, b_s, w_out):
    h = rms_fwd(x, g, "gmlp_norm")
    (zpre,) = matmul([(h, w_in)], 'nn', [F32], tm=512, tn=512, name="gmlp_in")
    gated = gmlp_gate_fwd(zpre, ln_g, ln_b, w_s, b_s)
    (xo,) = matmul([(gated, w_out)], 'nn', [F32], epilogue=lambda accs, ex: (ex[0] + accs[0],),
                   extras=[(x, 't')], tm=512, tn=512, name="gmlp_out")
    return xo, (x, h, zpre, gated)


def gmlp_bwd(saved, g, w_in, ln_g, ln_b, w_s, b_s, w_out, dxo, dxo_bf):
    x, h, zpre, gated = saved
    (dgated,) = matmul([(dxo_bf, w_out)], 'nt', [F32], tm=512, tn=512, name="gmlp_dgated")
    (dw_out,) = matmul([(gated, dxo_bf)], 'tn', [BF16], tm=512, tn=512, name="gmlp_dwout")
    dzu, dzv, dws, dbs, dlg, dlb = gmlp_gate_bwd(zpre, ln_g, ln_b, w_s, b_s, dgated)
    dz = jnp.concatenate([dzu, dzv], axis=1)
    (dw_in,) = matmul([(h, dz)], 'tn', [BF16], tm=512, tn=512, name="gmlp_dwin")
    (dh,) = matmul([(dz, w_in)], 'nt', [F32], tm=256, tn=256, name="gmlp_dh")
    dx, dx_bf, dg = rms_bwd(x, g, dh, dxo, "gmlp_dnorm")
    return dx, dx_bf, dg, dw_in, dlg, dlb, dws, dbs, dw_out


def _s5_disc(lr, li, ldt, br, bi):
    dt = jnp.exp(ldt)
    mag = jnp.exp(lr * dt)
    ang = li * dt
    ar = mag * jnp.cos(ang)
    ai = mag * jnp.sin(ang)
    den = lr * lr + li * li
    nr = ar - 1.0
    zr = (nr * lr + ai * li) / den
    zi = (ai * lr - nr * li) / den
    return ar, ai, zr[None] * br - zi[None] * bi, zr[None] * bi + zi[None] * br


def s5_disc_fwd(lr, li, ldt, br, bi):
    def body(lr_r, li_r, ldt_r, br_r, bi_r, ar_o, ai_o, bbr_o, bbi_o):
        res = _s5_disc(lr_r[...], li_r[...], ldt_r[...], br_r[...], bi_r[...])
        for o, v in zip((ar_o, ai_o, bbr_o, bbi_o), res):
            o[...] = v

    shp = lambda a: jax.ShapeDtypeStruct(a.shape, F32)
    return _pcall(body, name="s5_disc", out_shape=[shp(lr), shp(lr), shp(br), shp(br)],
                  compiler_params=_params())(lr, li, ldt, br, bi)


def s5_disc_bwd(lr, li, ldt, br, bi, dar, dai, dbbr, dbbi):
    def body(lr_r, li_r, ldt_r, br_r, bi_r, dar_r, dai_r, dbbr_r, dbbi_r, o1, o2, o3, o4, o5):
        _, vjp = jax.vjp(_s5_disc, lr_r[...], li_r[...], ldt_r[...], br_r[...], bi_r[...])
        res = vjp((dar_r[...], dai_r[...], dbbr_r[...], dbbi_r[...]))
        for o, v in zip((o1, o2, o3, o4, o5), res):
            o[...] = v

    shp = lambda a: jax.ShapeDtypeStruct(a.shape, F32)
    return _pcall(body, name="s5_disc_bwd", out_shape=[shp(lr), shp(lr), shp(ldt), shp(br), shp(br)],
                  compiler_params=_params())(lr, li, ldt, br, bi, dar, dai, dbbr, dbbi)


def blockdiag_matmul(pairs, outs, *, epilogue=None, extras=(), name):
    a0, b0 = pairs[0]
    M = a0.shape[0]
    T, wa, wo = b0.shape
    tm = _pick(M, (512, 256, 128))
    n_p, n_e = len(pairs), len(extras)
    in_specs, args = [], []
    for a, b in pairs:
        in_specs += [pl.BlockSpec((tm, wa), lambda k, i: (i, k)), pl.BlockSpec((None, wa, wo), lambda k, i: (k, 0, 0))]
        args += [a, b]
    for arr, kind in extras:
        in_specs.append(pl.BlockSpec((tm, wo), lambda k, i: (i, k)) if kind == 't'
                        else pl.BlockSpec((1, wo), lambda k, i: (0, k)))
        args.append(arr)

    def body(*refs):
        accs = [jnp.dot(refs[2 * p][...].astype(BF16), refs[2 * p + 1][...], preferred_element_type=F32)
                for p in range(n_p)]
        ex = [r[...] for r in refs[2 * n_p:2 * n_p + n_e]]
        res = tuple(accs) if epilogue is None else epilogue(accs, ex)
        for ref, v in zip(refs[2 * n_p + n_e:], res):
            ref[...] = v.astype(ref.dtype)

    return _pcall(body, name=name, grid=(T, M // tm), in_specs=in_specs,
                  out_specs=[pl.BlockSpec((tm, wo), lambda k, i: (i, k)) for _ in outs],
                  out_shape=[jax.ShapeDtypeStruct((M, T * wo), dt) for dt in outs],
                  compiler_params=_params())(*args)


def blockdiag_outer(pairs, name):
    M = pairs[0][0].shape[0]
    n_p = len(pairs)
    shapes = []
    in_specs, args = [], []
    tm = _pick(M, (512, 256, 128))
    T = None
    for a, b, wa, wb in pairs:
        T = a.shape[1] // wa
        in_specs += [pl.BlockSpec((tm, wa), lambda k, i: (i, k)), pl.BlockSpec((tm, wb), lambda k, i: (i, k))]
        args += [a, b]
        shapes.append((T, wa, wb))

    def body(*refs):
        @pl.when(pl.program_id(1) == 0)
        def _():
            for o in refs[2 * n_p:]:
                o[...] = jnp.zeros_like(o)
        for p in range(n_p):
            refs[2 * n_p + p][...] += lax.dot_general(refs[2 * p][...].astype(BF16), refs[2 * p + 1][...].astype(BF16),
                                                      _DIMS['tn'], preferred_element_type=F32)

    return _pcall(body, name=name, grid=(T, M // tm), in_specs=in_specs,
                  out_specs=[pl.BlockSpec((None, s[1], s[2]), lambda k, i: (k, 0, 0)) for s in shapes],
                  out_shape=[jax.ShapeDtypeStruct(s, F32) for s in shapes], compiler_params=_params())(*args)


def s5_scan(br, bi, ar, ai, reverse, name):
    L, S = br.shape
    ln = _pick(S, (SCAN_LANES, 512, 256, 128))
    tb = _pick(L, (512, 256, 128))
    n_t = L // tb
    n_q = tb // 8

    def tmap(j, t):
        return ((n_t - 1 - t) if reverse else t, j)

    blk = pl.BlockSpec((tb, ln), tmap)
    vec = pl.BlockSpec((1, ln), lambda j, t: (0, j))

    def cmul(xr, xi, yr, yi):
        return xr * yr - xi * yi, xr * yi + xi * yr

    def body(br_r, bi_r, ar_r, ai_r, hr_o, hi_o, pr_o, pi_o, cr_s, ci_s):
        @pl.when(pl.program_id(1) == 0)
        def _():
            cr_s[...] = jnp.zeros_like(cr_s)
            ci_s[...] = jnp.zeros_like(ci_s)

        a1r, a1i = ar_r[...], ai_r[...]
        a2r, a2i = cmul(a1r, a1i, a1r, a1i)
        a4r, a4i = cmul(a2r, a2i, a2r, a2i)
        row = lax.broadcasted_iota(jnp.int32, (8, ln), 0)
        dist = (7 - row) if reverse else row
        pwr, pwi = jnp.broadcast_to(a1r, (8, ln)), jnp.broadcast_to(a1i, (8, ln))
        for bit, (er, ei) in ((1, (a1r, a1i)), (2, (a2r, a2i)), (4, (a4r, a4i))):
            nr, ni = cmul(pwr, pwi, er, ei)
            sel = (dist & bit) != 0
            pwr, pwi = jnp.where(sel, nr, pwr), jnp.where(sel, ni, pwi)
        last = 0 if reverse else 7

        def shift(v, d):
            r = pltpu.roll(v, (8 - d) if reverse else d, 0)
            return jnp.where(dist >= d, r, 0.0)

        def step(q, carry):
            cr, ci = carry
            qq = (n_q - 1 - q) if reverse else q
            rows = pl.ds(pl.multiple_of(qq * 8, 8), 8)
            xr, xi = br_r[rows, :], bi_r[rows, :]
            for d, (er, ei) in ((1, (a1r, a1i)), (2, (a2r, a2i)), (4, (a4r, a4i))):
                sr, si = shift(xr, d), shift(xi, d)
                mr, mi = cmul(sr, si, er, ei)
                xr, xi = xr + mr, xi + mi
            kr, ki = cmul(pwr, pwi, cr, ci)
            xr, xi = xr + kr, xi + ki
            hr_o[rows, :] = xr
            hi_o[rows, :] = xi
            pr_o[rows, :] = jnp.where(dist >= 1, pltpu.roll(xr, 7 if reverse else 1, 0), cr)
            pi_o[rows, :] = jnp.where(dist >= 1, pltpu.roll(xi, 7 if reverse else 1, 0), ci)
            ncr = jnp.sum(jnp.where(row == last, xr, 0.0), axis=0, keepdims=True)
            nci = jnp.sum(jnp.where(row == last, xi, 0.0), axis=0, keepdims=True)
            return ncr, nci

        cr, ci = lax.fori_loop(0, n_q, step, (cr_s[...], ci_s[...]))
        cr_s[...] = cr
        ci_s[...] = ci

    shp = jax.ShapeDtypeStruct((L, S), F32)
    return _pcall(body, name=name, grid=(S // ln, n_t), in_specs=[blk, blk, vec, vec],
                  out_specs=[blk, blk, blk, blk], out_shape=[shp, shp, shp, shp],
                  scratch_shapes=[pltpu.VMEM((1, ln), F32), pltpu.VMEM((1, ln), F32)],
                  compiler_params=_params())(br, bi, ar, ai)


def _to_blockdiag(m, tile_groups):
    G, A, B = m.shape
    T = G // tile_groups
    eye = jnp.eye(tile_groups, dtype=m.dtype)
    t = m.reshape(T, tile_groups, A, 1, B) * eye[None, :, None, :, None]
    return t.reshape(T, tile_groups * A, tile_groups * B)


def _from_blockdiag(t, tile_groups):
    T, RA, RB = t.shape
    A, B = RA // tile_groups, RB // tile_groups
    d = jnp.diagonal(t.reshape(T, tile_groups, A, tile_groups, B), axis1=1, axis2=3)
    return jnp.moveaxis(d, 3, 1).reshape(T * tile_groups, A, B)


def s5_fwd(x, g, w_in, lam_re, lam_im, log_dt, b_re, b_im, c_re, c_im, d_skip, w_out):
    G, P, H = b_re.shape
    tg = min(SSM_TILE_GROUPS, G)
    h = rms_fwd(x, g, "s5_norm")
    (u,) = matmul([(h, w_in)], 'nn', [F32], tm=512, tn=512, name="s5_in")
    br_t, bi_t = jnp.transpose(b_re, (2, 0, 1)), jnp.transpose(b_im, (2, 0, 1))
    ar, ai, bbr, bbi = s5_disc_fwd(lam_re, lam_im, log_dt, br_t, bi_t)
    bbr_g, bbi_g = jnp.transpose(bbr, (1, 0, 2)), jnp.transpose(bbi, (1, 0, 2))
    bd_br, bd_bi = _to_blockdiag(bbr_g.astype(BF16), tg), _to_blockdiag(bbi_g.astype(BF16), tg)
    bur, bui = blockdiag_matmul([(u, bd_br), (u, bd_bi)], [F32, F32], name="s5_bu")
    a_r, a_i = ar.reshape(1, G * P), ai.reshape(1, G * P)
    hr, hi, hpr, hpi = s5_scan(bur, bui, a_r, a_i, False, "s5_scan")
    c_pg_r = jnp.transpose(c_re, (0, 2, 1)).astype(BF16)
    c_pg_i = jnp.transpose(c_im, (0, 2, 1)).astype(BF16)
    bd_cr, bd_nci = _to_blockdiag(c_pg_r, tg), _to_blockdiag(-c_pg_i, tg)

    def ep(accs, ex):
        y = accs[0] + accs[1] + ex[1] * ex[0]
        return y, _gelu(y)

    y, act = blockdiag_matmul([(hr, bd_cr), (hi, bd_nci)], [F32, BF16], epilogue=ep,
                              extras=[(u, 't'), (d_skip, 'r')], name="s5_y")
    (o,) = matmul([(act, w_out)], 'nn', [F32], tm=512, tn=512, name="s5_out")
    M, D = x.shape
    tm = _pick(M, (256, 128))
    (xo,) = tilemap(lambda xv, val, gt: (xv + val * _sigmoid(gt),), [(x, 't', 0), (o, 't', 0), (o, 't', 1)],
                    [(F32, 't')], M=M, N=D, tm=tm, tn=D, name="s5_glu")
    saved = (x, h, u, hr, hi, hpr, hpi, y, act, o, a_r, a_i, bd_br, bd_bi, bd_cr, bd_nci, br_t, bi_t)
    return xo, saved


def s5_bwd(saved, g, w_in, lam_re, lam_im, log_dt, b_re, d_skip, w_out, dxo):
    x, h, u, hr, hi, hpr, hpi, y, act, o, a_r, a_i, bd_br, bd_bi, bd_cr, bd_nci, br_t, bi_t = saved
    G, P, H = b_re.shape
    tg = min(SSM_TILE_GROUPS, G)
    M, D = x.shape
    tm = _pick(M, (256, 128))

    def glu_bwd(dv, val, gt):
        sg = _sigmoid(gt)
        return dv * sg, dv * val * sg * (1.0 - sg)

    dval, dgate = tilemap(glu_bwd, [(dxo, 't', 0), (o, 't', 0), (o, 't', 1)], [(BF16, 't'), (BF16, 't')],
                          M=M, N=D, tm=tm, tn=D, name="s5_dglu")
    do = jnp.concatenate([dval, dgate], axis=1)
    (dw_out,) = matmul([(act, do)], 'tn', [BF16], tm=512, tn=512, name="s5_dwout")
    (dact,) = matmul([(do, w_out)], 'nt', [F32], tm=512, tn=512, name="s5_dact")
    dy, dd = tilemap(lambda da, yv, uv: (da * _gelu_grad(yv), jnp.sum(da * _gelu_grad(yv) * uv, axis=0, keepdims=True)),
                     [(dact, 't', 0), (y, 't', 0), (u, 't', 0)], [(F32, 't'), (F32, 'a')],
                     M=M, N=D, tm=tm, tn=D, name="s5_dy")
    bd_crT, bd_nciT = jnp.transpose(bd_cr, (0, 2, 1)), jnp.transpose(bd_nci, (0, 2, 1))
    dhr, dhi = blockdiag_matmul([(dy, bd_crT), (dy, bd_nciT)], [F32, F32], name="s5_dh")
    gr, gi, _, _ = s5_scan(dhr, dhi, a_r, -a_i, True, "s5_scan_rev")
    S = G * P
    tms = _pick(M, (128,))

    def da_fn(grv, giv, hprv, hpiv):
        return (jnp.sum(grv * hprv + giv * hpiv, axis=0, keepdims=True),
                jnp.sum(giv * hprv - grv * hpiv, axis=0, keepdims=True))

    dar, dai = tilemap(da_fn, [(gr, 't', 0), (gi, 't', 0), (hpr, 't', 0), (hpi, 't', 0)], [(F32, 'a'), (F32, 'a')],
                       M=M, N=S, tm=tms, tn=_pick(S, (2048, 1024, 512)), name="s5_dabar")
    wa, wb = tg * H, tg * P
    xc_r, xc_i, xb_r, xb_i = blockdiag_outer([(dy, hr, wa, wb), (dy, hi, wa, wb), (u, gr, wa, wb), (u, gi, wa, wb)],
                                             "s5_dcb")
    dc_re = _from_blockdiag(xc_r, tg)
    dc_im = -_from_blockdiag(xc_i, tg)
    dbb_r = jnp.transpose(_from_blockdiag(xb_r, tg), (1, 0, 2))
    dbb_i = jnp.transpose(_from_blockdiag(xb_i, tg), (1, 0, 2))
    dlr, dli, dldt, dbr_t, dbi_t = s5_disc_bwd(lam_re, lam_im, log_dt, br_t, bi_t,
                                               dar.reshape(G, P), dai.reshape(G, P), dbb_r, dbb_i)
    db_re, db_im = jnp.transpose(dbr_t, (1, 2, 0)), jnp.transpose(dbi_t, (1, 2, 0))
    bd_brT, bd_biT = jnp.transpose(bd_br, (0, 2, 1)), jnp.transpose(bd_bi, (0, 2, 1))
    (du,) = blockdiag_matmul([(gr, bd_brT), (gi, bd_biT)], [BF16],
                             epilogue=lambda accs, ex: (accs[0] + accs[1] + ex[1] * ex[0],),
                             extras=[(dy, 't'), (d_skip, 'r')], name="s5_du")
    (dw_in,) = matmul([(h, du)], 'tn', [BF16], tm=512, tn=512, name="s5_dwin")
    (dh,) = matmul([(du, w_in)], 'nt', [F32], tm=512, tn=512, name="s5_dhin")
    dx, dx_bf, dg = rms_bwd(x, g, dh, dxo, "s5_dnorm")
    return dx, dx_bf, dg, dw_in, dlr, dli, dldt, db_re, db_im, dc_re, dc_im, dd, dw_out


def ple_fwd(x, g, p_emb, w_gate, w_proj, tag):
    h = rms_fwd(x, g, f"ple_norm_{tag}")
    (q,) = matmul([(p_emb, w_proj)], 'nn', [F32], tm=512, tn=512, name=f"ple_proj_{tag}")

    def ep(accs, ex):
        gt = _sigmoid(accs[0])
        return ex[0] + gt * ex[1], gt

    xo, gate = matmul([(h, w_gate)], 'nn', [F32, F32], epilogue=ep, extras=[(x, 't'), (q, 't')],
                      tm=512, tn=512, name=f"ple_gate_{tag}")
    return xo, (x, h, q, gate)


def ple_bwd(saved, g, p_emb, w_gate, dxo, tag):
    x, h, q, gate = saved
    M, D = x.shape
    tm = _pick(M, (256, 128))
    dq, dpre = tilemap(lambda dv, qv, gv: (dv * gv, dv * qv * gv * (1.0 - gv)),
                       [(dxo, 't', 0), (q, 't', 0), (gate, 't', 0)], [(BF16, 't'), (BF16, 't')],
                       M=M, N=D, tm=tm, tn=D, name=f"ple_dgate_{tag}")
    (dw_proj,) = matmul([(p_emb, dq)], 'tn', [BF16], tm=256, tn=512, name=f"ple_dwproj_{tag}")
    (dw_gate,) = matmul([(h, dpre)], 'tn', [BF16], tm=512, tn=512, name=f"ple_dwgate_{tag}")
    (dh,) = matmul([(dpre, w_gate)], 'nt', [F32], tm=512, tn=512, name=f"ple_dh_{tag}")
    dx, dx_bf, dg = rms_bwd(x, g, dh, dxo, f"ple_dnorm_{tag}")
    return dx, dx_bf, dg, dw_gate, dw_proj


def loss_head(x, g, target):
    M, D = x.shape
    tm = _pick(M, (256, 128))

    def fn(xv, gv, tv):
        r = lax.rsqrt(jnp.mean(xv * xv, axis=-1, keepdims=True) + EPS)
        xh = xv * r
        e = xh * gv - tv
        dy = e * (1.0 / D)
        dxh = dy * gv
        m = jnp.mean(dxh * xh, axis=-1, keepdims=True)
        dx = r * (dxh - xh * m)
        return jnp.sum(e * e, axis=0, keepdims=True), dx, dx, jnp.sum(dy * xh, axis=0, keepdims=True)

    return tilemap(fn, [(x, 't', 0), (g, 'r', 0), (target, 't', 0)],
                   [(F32, 'a'), (F32, 't'), (BF16, 't'), (F32, 'a')], M=M, N=D, tm=tm, tn=D, name="loss_head")


def adamw(w, g, m, v, name):
    w2, g2, m2, v2 = _as2d(w), g.reshape(_as2d(w).shape), _as2d(m), _as2d(v)
    R, C = w2.shape
    tm = _row_tile(R, C)

    def fn(wv, gv, mv, vv):
        mn = ADAM_B1 * mv + (1.0 - ADAM_B1) * gv
        vn = ADAM_B2 * vv + (1.0 - ADAM_B2) * (gv * gv)
        m_hat = mn / (1.0 - ADAM_B1 ** ADAM_STEP)
        v_hat = vn / (1.0 - ADAM_B2 ** ADAM_STEP)
        return -ADAM_LR * (m_hat / (jnp.sqrt(v_hat) + ADAM_EPS) + ADAM_WD * wv), mn, vn

    d, mn, vn = tilemap(fn, [(w2, 't', 0), (g2, 't', 0), (m2, 't', 0), (v2, 't', 0)],
                        [(F32, 't'), (F32, 't'), (F32, 't')], M=R, N=C, tm=tm, tn=C, name=name)
    return d.reshape(w.shape), mn.reshape(w.shape), vn.reshape(w.shape)


WEIGHT_NAMES = ['norm_g', 'final_norm_g', 'ffn_w_gate', 'ffn_w_up', 'ffn_w_down', 'gmlp_w_in', 'gmlp_ln_g',
                'gmlp_ln_b', 'gmlp_w_s', 'gmlp_b_s', 'gmlp_w_out', 's5_w_in', 's5_lam_re', 's5_lam_im',
                's5_log_dt', 's5_b_re', 's5_b_im', 's5_c_re', 's5_c_im', 's5_d', 's5_w_out', 'ple_w_gate',
                'ple_w_proj']
BIG = {'ffn_w_gate': 1, 'ffn_w_up': 1, 'ffn_w_down': 0, 'gmlp_w_in': 1, 'gmlp_w_out': 0, 's5_w_in': 0,
       's5_w_out': 1, 'ple_w_gate': 0, 'ple_w_proj': 1}
GROUPS = [['ffn_w_gate', 'ffn_w_up', 'ffn_w_down'], ['gmlp_w_in', 'gmlp_w_out'],
          ['s5_w_in', 's5_w_out', 'ple_w_gate', 'ple_w_proj']]


def kernel(x, p, norm_g, final_norm_g, ffn_w_gate, ffn_w_up, ffn_w_down, gmlp_w_in, gmlp_ln_g, gmlp_ln_b, gmlp_w_s, gmlp_b_s, gmlp_w_out, s5_w_in, s5_lam_re, s5_lam_im, s5_log_dt, s5_b_re, s5_b_im, s5_c_re, s5_c_im, s5_d, s5_w_out, ple_w_gate, ple_w_proj, loss_target, m_norm_g, m_final_norm_g, m_ffn_w_gate, m_ffn_w_up, m_ffn_w_down, m_gmlp_w_in, m_gmlp_ln_g, m_gmlp_ln_b, m_gmlp_w_s, m_gmlp_b_s, m_gmlp_w_out, m_s5_w_in, m_s5_lam_re, m_s5_lam_im, m_s5_log_dt, m_s5_b_re, m_s5_b_im, m_s5_c_re, m_s5_c_im, m_s5_d, m_s5_w_out, m_ple_w_gate, m_ple_w_proj, v_norm_g, v_final_norm_g, v_ffn_w_gate, v_ffn_w_up, v_ffn_w_down, v_gmlp_w_in, v_gmlp_ln_g, v_gmlp_ln_b, v_gmlp_w_s, v_gmlp_b_s, v_gmlp_w_out, v_s5_w_in, v_s5_lam_re, v_s5_lam_im, v_s5_log_dt, v_s5_b_re, v_s5_b_im, v_s5_c_re, v_s5_c_im, v_s5_d, v_s5_w_out, v_ple_w_gate, v_ple_w_proj):
    env = dict(locals())
    W = {n: env[n] for n in WEIGHT_NAMES}
    Mo = {n: env["m_" + n] for n in WEIGHT_NAMES}
    Vo = {n: env["v_" + n] for n in WEIGHT_NAMES}
    depth = norm_g.shape[0]
    L, D = x.shape[1], x.shape[2]
    s_idx = 2 * lax.axis_index("x") + lax.axis_index("y")

    full = {}
    for gi, names in enumerate(GROUPS):
        shards = [cast_bf16(W[n]) for n in names]
        res = gather_weights(f"gather{gi}", shards, [BIG[n] for n in names], [True] * len(names))
        full.update(dict(zip(names, res)))
    ng2 = norm_g.reshape(depth * 4, norm_g.shape[-1])
    ng_full, sd_full = gather_weights("gather_small", [ng2, s5_d], [1, 1], [False, False])
    ng_full = ng_full.reshape(depth, 4, 1, D)
    gf = final_norm_g.reshape(1, D)

    G, P, H = s5_b_re.shape[1:]
    n_grp, chunk = gmlp_w_s.shape[1], gmlp_w_s.shape[2]
    lam_re, lam_im = s5_lam_re[0], s5_lam_im[0]
    log_dt = s5_log_dt.reshape(G, 1)
    b_re, b_im, c_re, c_im = s5_b_re[0], s5_b_im[0], s5_c_re[0], s5_c_im[0]
    w_s, b_s = gmlp_w_s[0], gmlp_b_s[0].reshape(n_grp, chunk, 1)
    xs = x.reshape(L, D)
    saved = []
    for i in range(depth):
        sv = {}
        xs, sv['ffn_a'] = ffn_fwd(xs, ng_full[i, 0], full['ffn_w_gate'][i, 0], full['ffn_w_up'][i, 0],
                                  full['ffn_w_down'][i, 0], f"{i}a")
        j = i // 2
        if i % 2 == 0:
            xs, sv['mix'] = gmlp_fwd(xs, ng_full[i, 1], full['gmlp_w_in'][j], gmlp_ln_g, gmlp_ln_b, w_s, b_s,
                                     full['gmlp_w_out'][j])
        else:
            xs, sv['mix'] = s5_fwd(xs, ng_full[i, 1], full['s5_w_in'][j], lam_re, lam_im, log_dt, b_re, b_im,
                                   c_re, c_im, sd_full, full['s5_w_out'][j])
        xs, sv['ffn_b'] = ffn_fwd(xs, ng_full[i, 2], full['ffn_w_gate'][i, 1], full['ffn_w_up'][i, 1],
                                  full['ffn_w_down'][i, 1], f"{i}b")
        xs, sv['ple'] = ple_fwd(xs, ng_full[i, 3], p[i, 0], full['ple_w_gate'][i], full['ple_w_proj'][i], f"{i}")
        saved.append(sv)

    sq, dx, dx_bf, dgf = loss_head(xs, gf, loss_target.reshape(L, D))
    loss = lax.psum(0.5 * jnp.sum(sq) / D, ("x", "y", "c"))
    dng = [[None] * 4 for _ in range(depth)]
    gfull = {n: [None] * (depth * 2 if n.startswith('ffn') else depth) for n in ('ffn_w_gate', 'ffn_w_up', 'ffn_w_down',
                                                                                  'ple_w_gate', 'ple_w_proj')}
    small = {}
    for i in reversed(range(depth)):
        sv = saved[i]
        dx, dx_bf, dng[i][3], gfull['ple_w_gate'][i], gfull['ple_w_proj'][i] = ple_bwd(
            sv['ple'], ng_full[i, 3], p[i, 0], full['ple_w_gate'][i], dx, f"{i}")
        dx, dx_bf, dng[i][2], gfull['ffn_w_gate'][2 * i + 1], gfull['ffn_w_up'][2 * i + 1], \
            gfull['ffn_w_down'][2 * i + 1] = ffn_bwd(sv['ffn_b'], ng_full[i, 2], full['ffn_w_gate'][i, 1],
                                                     full['ffn_w_up'][i, 1], full['ffn_w_down'][i, 1], dx, dx_bf, f"{i}b")
        j = i // 2
        if i % 2 == 0:
            dx, dx_bf, dng[i][1], dw_in, dlg, dlb, dws, dbs, dw_out = gmlp_bwd(
                sv['mix'], ng_full[i, 1], full['gmlp_w_in'][j], gmlp_ln_g, gmlp_ln_b, w_s, b_s, full['gmlp_w_out'][j],
                dx, dx_bf)
            gfull['gmlp_w_in'], gfull['gmlp_w_out'] = [dw_in], [dw_out]
            small.update(gmlp_ln_g=dlg, gmlp_ln_b=dlb, gmlp_w_s=dws, gmlp_b_s=dbs)
        else:
            dx, dx_bf, dng[i][1], dw_in, dlr, dli, dldt, db_re, db_im, dc_re, dc_im, dd, dw_out = s5_bwd(
                sv['mix'], ng_full[i, 1], full['s5_w_in'][j], lam_re, lam_im, log_dt, b_re, sd_full,
                full['s5_w_out'][j], dx)
            gfull['s5_w_in'], gfull['s5_w_out'] = [dw_in], [dw_out]
            small.update(s5_lam_re=dlr, s5_lam_im=dli, s5_log_dt=dldt, s5_b_re=db_re, s5_b_im=db_im,
                         s5_c_re=dc_re, s5_c_im=dc_im, s5_d=dd)
        dx, dx_bf, dng[i][0], gfull['ffn_w_gate'][2 * i], gfull['ffn_w_up'][2 * i], gfull['ffn_w_down'][2 * i] = \
            ffn_bwd(sv['ffn_a'], ng_full[i, 0], full['ffn_w_gate'][i, 0], full['ffn_w_up'][i, 0],
                    full['ffn_w_down'][i, 0], dx, dx_bf, f"{i}a")
    grad_x = dx.reshape(x.shape)
    small['norm_g'] = jnp.stack([jnp.stack(r) for r in dng])
    small['final_norm_g'] = dgf

    grads = {}
    for gi, names in enumerate(GROUPS):
        stacked = [jnp.stack(gfull[n]).reshape(full[n].shape) for n in names]
        res = reduce_scatter_grads(f"reduce{gi}", stacked, [BIG[n] for n in names])
        grads.update(dict(zip(names, res)))
    small_names = [n for n in WEIGHT_NAMES if n not in BIG]
    flat = jnp.concatenate([small[n].astype(F32).reshape(-1) for n in small_names])
    pad = (-flat.size) % (256 * 128)
    flat = jnp.pad(flat, (0, pad)).reshape(-1, 128)
    tot = allreduce_small(flat).reshape(-1)
    off = 0
    for n in small_names:
        sz = small[n].size
        gsum = tot[off:off + sz]
        off += sz
        if n == 'norm_g':
            gsum = lax.dynamic_slice_in_dim(gsum.reshape(depth, 4, D), s_idx * W[n].shape[-1], W[n].shape[-1], axis=2)
        elif n == 's5_d':
            gsum = lax.dynamic_slice_in_dim(gsum.reshape(1, D), s_idx * W[n].shape[-1], W[n].shape[-1], axis=1)
        grads[n] = gsum.reshape(W[n].shape)

    deltas, new_m, new_v = {}, {}, {}
    for n in WEIGHT_NAMES:
        deltas[n], new_m[n], new_v[n] = adamw(W[n], grads[n], Mo[n], Vo[n], f"adamw_{n}")
    return (loss, grad_x, *[grads[n] for n in WEIGHT_NAMES], *[deltas[n] for n in WEIGHT_NAMES],
            *[new_m[n] for n in WEIGHT_NAMES], *[new_v[n] for n in WEIGHT_NAMES])
```

```python
import functools
import math

import jax
import jax.numpy as jnp
from jax import lax
from jax.experimental import pallas as pl
from jax.experimental.pallas import tpu as pltpu

F32 = jnp.float32
BF16 = jnp.bfloat16
MESH_ID = pl.DeviceIdType.MESH

EPS = 1e-6
ADAM_LR = 0.001
ADAM_B1 = 0.9
ADAM_B2 = 0.999
ADAM_EPS = 1e-08
ADAM_WD = 0.01
ADAM_STEP = 10

N_SHARD = 4
V7X_VMEM_LIMIT = 52 * 2 ** 20
SSM_TILE_GROUPS = 16
SCAN_LANES = 1024
GELU_C = math.sqrt(2.0 / math.pi)


def _pcall(body, **kw):
    return pl.pallas_call(body, **kw)


def _params():
    return pltpu.CompilerParams(vmem_limit_bytes=V7X_VMEM_LIMIT)


def _pick(n, cands):
    for c in cands:
        if c <= n and n % c == 0:
            return c
    return n


def _sigmoid(x):
    return 1.0 / (1.0 + jnp.exp(-x))


def _gelu(x):
    return 0.5 * x * (1.0 + jnp.tanh(GELU_C * (x + 0.044715 * x * x * x)))


def _gelu_grad(x):
    t = jnp.tanh(GELU_C * (x + 0.044715 * x * x * x))
    return 0.5 * (1.0 + t) + 0.5 * x * (1.0 - t * t) * GELU_C * (1.0 + 3.0 * 0.044715 * x * x)


def tilemap(fn, ins, outs, *, M, N, tm, tn, name):
    n_in = len(ins)
    grid = (N // tn, M // tm)
    in_specs = []
    for arr, kind, off in ins:
        if kind == 't':
            in_specs.append(pl.BlockSpec((tm, tn), lambda j, i, off=off: (i, j + off)))
        else:
            in_specs.append(pl.BlockSpec((1, tn), lambda j, i, off=off: (0, j + off)))
    out_specs, out_shape = [], []
    for dt, kind in outs:
        if kind == 't':
            out_specs.append(pl.BlockSpec((tm, tn), lambda j, i: (i, j)))
            out_shape.append(jax.ShapeDtypeStruct((M, N), dt))
        else:
            out_specs.append(pl.BlockSpec((1, tn), lambda j, i: (0, j)))
            out_shape.append(jax.ShapeDtypeStruct((1, N), F32))

    def body(*refs):
        vals = fn(*[r[...] for r in refs[:n_in]])
        for (dt, kind), ref, v in zip(outs, refs[n_in:], vals):
            if kind == 't':
                ref[...] = v.astype(ref.dtype)
            else:
                @pl.when(pl.program_id(1) == 0)
                def _():
                    ref[...] = jnp.zeros_like(ref)
                ref[...] += v

    res = _pcall(body, name=name, grid=grid, in_specs=in_specs, out_specs=out_specs,
                 out_shape=out_shape, compiler_params=_params())(*[a for a, _, _ in ins])
    return res


def _as2d(a):
    if a.ndim >= 2 and a.shape[-1] % 128 == 0:
        return a.reshape(-1, a.shape[-1])
    if a.size % 128 == 0:
        return a.reshape(-1, 128)
    return a.reshape(-1, a.shape[-1])


def _row_tile(rows, cols, nbytes=4, budget=1 << 20):
    cands = [c for c in (2048, 1024, 512, 256, 128, 64, 32, 16, 8) if c * cols * nbytes <= budget]
    return _pick(rows, cands) if cands else _pick(rows, (8,))


ROW_TILES = (256, 128, 64, 32, 16)


def _sc_call(body, sc, args, *, grid, in_specs, out_specs, out_shape, name):
    gs = pltpu.PrefetchScalarGridSpec(num_scalar_prefetch=1, grid=grid, in_specs=in_specs, out_specs=out_specs)
    return _pcall(body, name=name, grid_spec=gs, out_shape=out_shape, compiler_params=_params())(sc, *args)


def cast_into_full(w, lead, ax, sc, name):
    R, C = w.shape[-2:]
    tm = _pick(R, ROW_TILES)
    nb = R // tm
    lead = tuple(lead)
    in_spec = pl.BlockSpec((None,) * len(lead) + (tm, C), lambda i, s: lead + (i, 0))
    if ax == 0:
        shape, out_map = (R * N_SHARD, C), (lambda i, s: (i + s[0] * nb, 0))
    else:
        shape, out_map = (R, C * N_SHARD), (lambda i, s: (i, s[0]))

    def body(s_ref, w_ref, o_ref):
        o_ref[...] = w_ref[...].astype(BF16)

    return _sc_call(body, sc, [w], grid=(nb,), in_specs=[in_spec], out_specs=pl.BlockSpec((tm, C), out_map),
                    out_shape=jax.ShapeDtypeStruct(shape, BF16), name=name)


def pairsum(g, a, ax, sc, name):
    hR, hC = a.shape
    tm = _pick(hR, ROW_TILES)
    nb = hR // tm
    g_map = (lambda i, s: (i + s[1] * nb, 0)) if ax == 1 else (lambda i, s: (i, s[1]))
    blk = (tm, hC)

    def body(s_ref, g_ref, a_ref, o_ref):
        o_ref[...] = (g_ref[...].astype(F32) + a_ref[...].astype(F32)).astype(BF16)

    return _sc_call(body, sc, [g, a], grid=(nb,),
                    in_specs=[pl.BlockSpec(blk, g_map), pl.BlockSpec(blk, lambda i, s: (i, 0))],
                    out_specs=pl.BlockSpec(blk, lambda i, s: (i, 0)),
                    out_shape=jax.ShapeDtypeStruct((hR, hC), BF16), name=name)


def shardsum(b, cbuf, ax, sc, name):
    hR, hC = b.shape
    _, pR, pC = cbuf.shape
    tm = _pick(pR, ROW_TILES)
    nb = pR // tm
    if ax == 1:
        b_map, o_map, shape = (lambda i, s: (i, s[0])), (lambda i, s: (i + s[1] * nb, 0)), (2 * pR, pC)
    else:
        b_map, o_map, shape = (lambda i, s: (i + s[0] * nb, 0)), (lambda i, s: (i, s[1])), (pR, 2 * pC)

    def body(s_ref, b_ref, c_ref, o_ref):
        acc = b_ref[...].astype(F32)
        for k in range(N_SHARD - 1):
            acc = acc + c_ref[k].astype(F32)
        o_ref[...] = acc

    return _sc_call(body, sc, [b, cbuf], grid=(nb,),
                    in_specs=[pl.BlockSpec((tm, pC), b_map), pl.BlockSpec((N_SHARD - 1, tm, pC), lambda i, s: (0, i, 0))],
                    out_specs=pl.BlockSpec((tm, pC), o_map), out_shape=jax.ShapeDtypeStruct(shape, F32), name=name)


_DIMS = {'nn': (((1,), (0,)), ((), ())), 'nt': (((1,), (1,)), ((), ())), 'tn': (((0,), (0,)), ((), ()))}


def matmul(pairs, mode, outs, *, epilogue=None, extras=(), tm=512, tn=512, name):
    a0, b0 = pairs[0]
    if mode == 'nn':
        (M, K), N = a0.shape, b0.shape[1]
    elif mode == 'nt':
        (M, K), N = a0.shape, b0.shape[0]
    else:
        (K, M), N = a0.shape, b0.shape[1]
    tm, tn = _pick(M, (tm, 256, 128)), _pick(N, (tn, 256, 128))
    n_p, n_e = len(pairs), len(extras)
    if mode == 'tn':
        a_spec = pl.BlockSpec((K, tm), lambda i, j: (0, i))
    else:
        a_spec = pl.BlockSpec((tm, K), lambda i, j: (i, 0))
    if mode == 'nt':
        b_spec = pl.BlockSpec((tn, K), lambda i, j: (j, 0))
    else:
        b_spec = pl.BlockSpec((K, tn), lambda i, j: (0, j))
    in_specs, args = [], []
    for a, b in pairs:
        in_specs += [a_spec, b_spec]
        args += [a, b]
    for arr, kind in extras:
        if kind == 't':
            in_specs.append(pl.BlockSpec((tm, tn), lambda i, j: (i, j)))
        else:
            in_specs.append(pl.BlockSpec((1, tn), lambda i, j: (0, j)))
        args.append(arr)
    dims = _DIMS[mode]

    def body(*refs):
        accs = [lax.dot_general(refs[2 * p][...].astype(BF16), refs[2 * p + 1][...].astype(BF16), dims,
                                preferred_element_type=F32) for p in range(n_p)]
        ex = [r[...] for r in refs[2 * n_p:2 * n_p + n_e]]
        if epilogue is None:
            acc = accs[0]
            for other in accs[1:]:
                acc = acc + other
            res = (acc,)
        else:
            res = epilogue(accs, ex)
        for ref, v in zip(refs[2 * n_p + n_e:], res):
            ref[...] = v.astype(ref.dtype)

    return _pcall(body, name=name, grid=(M // tm, N // tn), in_specs=in_specs,
                  out_specs=[pl.BlockSpec((tm, tn), lambda i, j: (i, j)) for _ in outs],
                  out_shape=[jax.ShapeDtypeStruct((M, N), dt) for dt in outs],
                  compiler_params=_params())(*args)


def comm_call(name, ins, out_shapes, plan, n_local, n_remote, aliases=None):
    n_in, n_out = len(ins), len(out_shapes)

    def body(*refs):
        in_refs, out_refs = refs[:n_in], refs[n_in:n_in + n_out]
        lsem, ssem, rsem = refs[n_in + n_out:]
        me = (lax.axis_index("x"), lax.axis_index("y"), lax.axis_index("c"))
        local, remote = plan(me, in_refs, out_refs)
        assert len(local) == n_local and len(remote) == n_remote
        lcs = [pltpu.make_async_copy(s, d, lsem.at[k]) for k, (s, d) in enumerate(local)]
        rcs = [pltpu.make_async_remote_copy(src_ref=s, dst_ref=d, send_sem=ssem.at[k], recv_sem=rsem.at[k],
                                            device_id=peer, device_id_type=MESH_ID)
               for k, (s, d, peer) in enumerate(remote)]
        for cp in rcs:
            cp.start()
        for cp in lcs:
            cp.start()
        for cp in rcs:
            cp.wait()
        for cp in lcs:
            cp.wait()

    any_spec = pl.BlockSpec(memory_space=pl.ANY)
    return _pcall(body, name=name, in_specs=[any_spec] * n_in, out_specs=[any_spec] * n_out,
                  out_shape=list(out_shapes),
                  scratch_shapes=[pltpu.SemaphoreType.DMA((max(n_local, 1),)),
                                  pltpu.SemaphoreType.DMA((max(n_remote, 1),)),
                                  pltpu.SemaphoreType.DMA((max(n_remote, 1),))],
                  input_output_aliases=aliases or {},
                  compiler_params=pltpu.CompilerParams(has_side_effects=True))(*ins)


def _shard_of(me):
    return 2 * me[0] + me[1]


def _plane_peers(me):
    x, y, c = me
    return [((1 - x, y, c), 2 * (1 - x) + y), ((x, 1 - y, c), 2 * x + 1 - y),
            ((1 - x, 1 - y, c), 2 * (1 - x) + 1 - y)]


def _mats(arr):
    out = [()]
    for n in arr.shape[:-2]:
        out = [o + (k,) for o in out for k in range(n)]
    return out


ROW_ALIGN = 16
LANE_ALIGN = 128


def _win(ref, lead, rows, cols):
    idx = tuple(lead)
    for spec, align in ((rows, ROW_ALIGN), (cols, LANE_ALIGN)):
        if spec is None:
            idx += (slice(None),)
        else:
            start, size = spec
            if not isinstance(start, int):
                start = pl.multiple_of(start, align)
            idx += (pl.ds(start, size),)
    return ref.at[idx]


def gather_small(name, shards):
    full_shapes = [jax.ShapeDtypeStruct((a.shape[0], a.shape[1] * N_SHARD), a.dtype) for a in shards]
    n = len(shards)

    def plan(me, in_refs, out_refs):
        s = _shard_of(me)
        local, remote = [], []
        for t, a in enumerate(shards):
            dst = _win(out_refs[t], (), None, (s * a.shape[1], a.shape[1]))
            local.append((in_refs[t], dst))
            for peer, _ in _plane_peers(me):
                remote.append((in_refs[t], dst, peer))
        return local, remote

    return comm_call(name, shards, full_shapes, plan, n, 3 * n)


def reduce_scatter_grads(name, grads, axes, sc):
    n = len(grads)

    def half_win(m, h):
        R, C = grads[m].shape
        if axes[m] == 1:
            return (h * (R // 2), R // 2), None
        return None, (h * (C // 2), C // 2)

    a_shapes = []
    for g, ax in zip(grads, axes):
        R, C = g.shape
        a_shapes.append(jax.ShapeDtypeStruct((R // 2, C) if ax == 1 else (R, C // 2), BF16))

    def plan_a(me, in_refs, out_refs):
        x, y, c = me
        remote = []
        for m in range(n):
            rows, cols = half_win(m, 1 - c)
            remote.append((_win(in_refs[m], (), rows, cols), out_refs[m], (x, y, 1 - c)))
        return [], remote

    a_bufs = comm_call(name + "_pair", grads, a_shapes, plan_a, 0, n)
    b_bufs = [pairsum(g, a, ax, sc, f"{name}_pairsum{m}") for m, (g, a, ax) in enumerate(zip(grads, a_bufs, axes))]

    def piece_shape(m):
        R, C = a_shapes[m].shape
        return (R, C // N_SHARD) if axes[m] == 1 else (R // N_SHARD, C)

    def piece_win(m, s):
        R, C = piece_shape(m)
        if axes[m] == 1:
            return None, (s * C, C)
        return (s * R, R), None

    c_shapes = [jax.ShapeDtypeStruct((N_SHARD - 1,) + piece_shape(m), BF16) for m in range(n)]

    def plan_c(me, in_refs, out_refs):
        remote = []
        for m in range(n):
            for j, (peer, ps) in enumerate(_plane_peers(me)):
                rows, cols = piece_win(m, ps)
                remote.append((_win(in_refs[m], (), rows, cols), out_refs[m].at[j], peer))
        return [], remote

    c_bufs = comm_call(name + "_ici", b_bufs, c_shapes, plan_c, 0, 3 * n)
    shards = [shardsum(b, cb, ax, sc, f"{name}_shardsum{m}") for m, (b, cb, ax) in enumerate(zip(b_bufs, c_bufs, axes))]

    def plan_e(me, in_refs, out_refs):
        x, y, c = me
        remote = []
        for m in range(n):
            R, C = shards[m].shape
            if axes[m] == 1:
                rows, cols = (c * (R // 2), R // 2), None
            else:
                rows, cols = None, (c * (C // 2), C // 2)
            remote.append((_win(out_refs[m], (), rows, cols), _win(out_refs[m], (), rows, cols), (x, y, 1 - c)))
        return [], remote

    return comm_call(name + "_swap", shards, [jax.ShapeDtypeStruct(a.shape, a.dtype) for a in shards], plan_e, 0, n,
                     aliases={m: m for m in range(n)})


def gather_group(name, fulls, shard_shapes, axes):
    n = len(fulls)

    def win(ref, m, s, half):
        R, C = shard_shapes[m]
        r0 = s * R if axes[m] == 0 else 0
        return _win(ref, (), (r0 + half * (R // 2), R // 2), None if axes[m] == 0 else (s * C, C))

    def body(*refs):
        outs = refs[n:2 * n]
        isend, irecv, fsend, frecv = refs[2 * n:]
        x, y, c = lax.axis_index("x"), lax.axis_index("y"), lax.axis_index("c")
        me = (x, y, c)
        s = _shard_of(me)
        peers = _plane_peers(me)

        def copy(m, shard, ssem, rsem, k, to):
            return pltpu.make_async_remote_copy(src_ref=win(outs[m], m, shard, c), dst_ref=win(outs[m], m, shard, c),
                                                send_sem=ssem.at[k], recv_sem=rsem.at[k], device_id=to,
                                                device_id_type=MESH_ID)

        ici = [copy(m, s, isend, irecv, 3 * m + j, peer) for m in range(n) for j, (peer, _) in enumerate(peers)]
        for cp in ici:
            cp.start()
        fwd = []
        for m in range(n):
            for j, (_, ps) in enumerate(peers):
                ici[3 * m + j].wait_recv()
                f = copy(m, ps, fsend, frecv, 3 * m + j, (x, y, 1 - c))
                f.start()
                fwd.append(f)
        for cp in ici:
            cp.wait_send()
        for f in fwd:
            f.wait()

    any_spec = pl.BlockSpec(memory_space=pl.ANY)
    sems = pltpu.SemaphoreType.DMA((3 * n,))
    return _pcall(body, name=name, in_specs=[any_spec] * n, out_specs=[any_spec] * n,
                  out_shape=[jax.ShapeDtypeStruct(a.shape, a.dtype) for a in fulls],
                  scratch_shapes=[sems, sems, sems, sems], input_output_aliases={m: m for m in range(n)},
                  compiler_params=pltpu.CompilerParams(has_side_effects=True))(*fulls)


def allreduce_small(flat):
    cur = flat
    R, C = flat.shape
    for axis, flip in enumerate(((0, 0, 1), (0, 1, 0), (1, 0, 0))):
        def plan(me, in_refs, out_refs, flip=flip):
            peer = tuple(v + f * (1 - 2 * v) for v, f in zip(me, flip))
            return [], [(in_refs[0], out_refs[0], peer)]

        (got,) = comm_call(f"small_swap{axis}", [cur], [jax.ShapeDtypeStruct((R, C), F32)], plan, 0, 1)
        (cur,) = tilemap(lambda a, b: (a + b,), [(cur, 't', 0), (got, 't', 0)], [(F32, 't')], M=R, N=C,
                         tm=_pick(R, (2048, 1024, 512, 256)), tn=C, name=f"small_add{axis}")
    return cur


def rms_fwd(x, g, name):
    M, D = x.shape
    tm = _pick(M, (256, 128))

    def fn(xv, gv):
        r = lax.rsqrt(jnp.mean(xv * xv, axis=-1, keepdims=True) + EPS)
        return (xv * r * gv,)

    (h,) = tilemap(fn, [(x, 't', 0), (g, 'r', 0)], [(BF16, 't')], M=M, N=D, tm=tm, tn=D, name=name)
    return h


def rms_bwd(x, g, dh, dres, name):
    M, D = x.shape
    tm = _pick(M, (256, 128))

    def fn(xv, gv, dhv, drv):
        r = lax.rsqrt(jnp.mean(xv * xv, axis=-1, keepdims=True) + EPS)
        xh = xv * r
        dxh = dhv * gv
        m = jnp.mean(dxh * xh, axis=-1, keepdims=True)
        dx = drv + r * (dxh - xh * m)
        return dx, dx, jnp.sum(dhv * xh, axis=0, keepdims=True)

    return tilemap(fn, [(x, 't', 0), (g, 'r', 0), (dh, 't', 0), (dres, 't', 0)],
                   [(F32, 't'), (BF16, 't'), (F32, 'a')], M=M, N=D, tm=tm, tn=D, name=name)


def ffn_fwd(x, g, wg, wu, wd, tag):
    h = rms_fwd(x, g, f"ffn_norm_{tag}")

    def ep(accs, ex):
        a, b = accs
        return a, b, a * _sigmoid(a) * b

    a, b, s = matmul([(h, wg), (h, wu)], 'nn', [F32, F32, BF16], epilogue=ep, tm=512, tn=512,
                     name=f"ffn_gateup_{tag}")
    (xo,) = matmul([(s, wd)], 'nn', [F32], epilogue=lambda accs, ex: (ex[0] + 0.5 * accs[0],),
                   extras=[(x, 't')], tm=512, tn=512, name=f"ffn_down_{tag}")
    return xo, (x, h, a, b, s)


def ffn_bwd(saved, g, wg, wu, wd, dxo, dxo_bf, tag):
    x, h, a, b, s = saved

    def ep(accs, ex):
        ds = 0.5 * accs[0]
        av, bv = ex
        sg = _sigmoid(av)
        return ds * bv * (sg * (1.0 + av * (1.0 - sg))), ds * (av * sg)

    da, db = matmul([(dxo_bf, wd)], 'nt', [BF16, BF16], epilogue=ep, extras=[(a, 't'), (b, 't')],
                    tm=512, tn=512, name=f"ffn_dact_{tag}")
    (dwd,) = matmul([(s, dxo_bf)], 'tn', [BF16], epilogue=lambda accs, ex: (0.5 * accs[0],),
                    tm=512, tn=512, name=f"ffn_dwd_{tag}")
    dwg, dwu = matmul([(h, da), (h, db)], 'tn', [BF16, BF16], epilogue=lambda accs, ex: tuple(accs),
                      tm=512, tn=512, name=f"ffn_dwgu_{tag}")
    (dh,) = matmul([(da, wg), (db, wu)], 'nt', [F32], tm=512, tn=256, name=f"ffn_dh_{tag}")
    dx, dx_bf, dg = rms_bwd(x, g, dh, dxo, f"ffn_dnorm_{tag}")
    return dx, dx_bf, dg, dwg, dwu, dwd


def _tril_mask(n):
    return lax.broadcasted_iota(jnp.int32, (n, n), 0) >= lax.broadcasted_iota(jnp.int32, (n, n), 1)


def _gmlp_specs(L, half, n_grp, chunk):
    gd = half // n_grp
    specs = [pl.BlockSpec((chunk, gd), lambda g, n: (n, g)),
             pl.BlockSpec((chunk, gd), lambda g, n: (n, n_grp + g)),
             pl.BlockSpec((1, gd), lambda g, n: (0, g)),
             pl.BlockSpec((1, gd), lambda g, n: (0, g)),
             pl.BlockSpec((None, chunk, chunk), lambda g, n: (g, 0, 0)),
             pl.BlockSpec((None, chunk, 1), lambda g, n: (g, 0, 0))]
    return gd, specs


def _gmlp_gate_values(zu, zv, lg, lb, ws, bs):
    u, v = _gelu(zu), _gelu(zv)
    mu = jnp.mean(v, axis=-1, keepdims=True)
    d = v - mu
    rstd = lax.rsqrt(jnp.mean(d * d, axis=-1, keepdims=True) + EPS)
    vhat = d * rstd
    vn = vhat * lg + lb
    w = jnp.where(_tril_mask(ws.shape[0]), ws, 0.0).astype(BF16)
    sv = jnp.dot(w, vn.astype(BF16), preferred_element_type=F32) + bs
    return u, vhat, rstd, vn, w, sv


def gmlp_gate_fwd(zpre, ln_g, ln_b, w_s, b_s):
    L, half = zpre.shape[0], zpre.shape[1] // 2
    n_grp, chunk = w_s.shape[0], w_s.shape[1]
    gd, specs = _gmlp_specs(L, half, n_grp, chunk)

    def body(zu, zv, lg, lb, ws, bs, o):
        u, _, _, _, _, sv = _gmlp_gate_values(zu[...], zv[...], lg[...], lb[...], ws[...], bs[...])
        o[...] = (u * sv).astype(o.dtype)

    return _pcall(body, name="gmlp_gate", grid=(n_grp, L // chunk), in_specs=specs,
                  out_specs=pl.BlockSpec((chunk, gd), lambda g, n: (n, g)),
                  out_shape=jax.ShapeDtypeStruct((L, half), BF16), compiler_params=_params())(
        zpre, zpre, ln_g, ln_b, w_s, b_s)


def gmlp_gate_bwd(zpre, ln_g, ln_b, w_s, b_s, dgated):
    L, half = zpre.shape[0], zpre.shape[1] // 2
    n_grp, chunk = w_s.shape[0], w_s.shape[1]
    gd, specs = _gmlp_specs(L, half, n_grp, chunk)
    specs = specs + [pl.BlockSpec((chunk, gd), lambda g, n: (n, g))]

    def body(zu, zv, lg, lb, ws, bs, dg, dzu, dzv, dws, dbs, dlg, dlb):
        zuv, zvv, lgv = zu[...], zv[...], lg[...]
        u, vhat, rstd, vn, w, sv = _gmlp_gate_values(zuv, zvv, lgv, lb[...], ws[...], bs[...])
        dgv = dg[...]
        du = dgv * sv
        dsv = dgv * u
        dsv_bf = dsv.astype(BF16)
        dw = lax.dot_general(dsv_bf, vn.astype(BF16), _DIMS['nt'], preferred_element_type=F32)
        dvn = lax.dot_general(w, dsv_bf, _DIMS['tn'], preferred_element_type=F32)
        dvhat = dvn * lgv
        dv = rstd * (dvhat - jnp.mean(dvhat, axis=-1, keepdims=True)
                     - vhat * jnp.mean(dvhat * vhat, axis=-1, keepdims=True))
        dzu[...] = (du * _gelu_grad(zuv)).astype(dzu.dtype)
        dzv[...] = (dv * _gelu_grad(zvv)).astype(dzv.dtype)

        @pl.when(pl.program_id(1) == 0)
        def _():
            dws[...] = jnp.zeros_like(dws)
            dbs[...] = jnp.zeros_like(dbs)
            dlg[...] = jnp.zeros_like(dlg)
            dlb[...] = jnp.zeros_like(dlb)

        dws[...] += jnp.where(_tril_mask(chunk), dw, 0.0)
        dbs[...] += jnp.sum(dsv, axis=1, keepdims=True)
        dlg[...] += jnp.sum(dvn * vhat, axis=0, keepdims=True)
        dlb[...] += jnp.sum(dvn, axis=0, keepdims=True)

    tile = pl.BlockSpec((chunk, gd), lambda g, n: (n, g))
    vec = pl.BlockSpec((1, gd), lambda g, n: (0, g))
    return _pcall(body, name="gmlp_gate_bwd", grid=(n_grp, L // chunk), in_specs=specs,
                  out_specs=[tile, tile, pl.BlockSpec((None, chunk, chunk), lambda g, n: (g, 0, 0)),
                             pl.BlockSpec((None, chunk, 1), lambda g, n: (g, 0, 0)), vec, vec],
                  out_shape=[jax.ShapeDtypeStruct((L, half), BF16), jax.ShapeDtypeStruct((L, half), BF16),
                             jax.ShapeDtypeStruct((n_grp, chunk, chunk), F32),
                             jax.ShapeDtypeStruct((n_grp, chunk, 1), F32),
                             jax.ShapeDtypeStruct((1, half), F32), jax.ShapeDtypeStruct((1, half), F32)],
                  compiler_params=_params())(zpre, zpre, ln_g, ln_b, w_s, b_s, dgated)


def gmlp_fwd(x, g, w_in, ln_g, ln_b, w_s, b_s, w_out):
    h = rms_fwd(x, g, "gmlp_norm")
    (zpre,) = matmul([(h, w_in)], 'nn', [F32], tm=512, tn=512, name="gmlp_in")
    gated = gmlp_gate_fwd(zpre, ln_g, ln_b, w_s, b_s)
    (xo,) = matmul([(gated, w_out)], 'nn', [F32], epilogue=lambda accs, ex: (ex[0] + accs[0],),
                   extras=[(x, 't')], tm=512, tn=512, name="gmlp_out")
    return xo, (x, h, zpre, gated)


def gmlp_bwd(saved, g, w_in, ln_g, ln_b, w_s, b_s, w_out, dxo, dxo_bf):
    x, h, zpre, gated = saved
    (dgated,) = matmul([(dxo_bf, w_out)], 'nt', [F32], tm=512, tn=512, name="gmlp_dgated")
    (dw_out,) = matmul([(gated, dxo_bf)], 'tn', [BF16], tm=512, tn=512, name="gmlp_dwout")
    dzu, dzv, dws, dbs, dlg, dlb = gmlp_gate_bwd(zpre, ln_g, ln_b, w_s, b_s, dgated)
    dz = jnp.concatenate([dzu, dzv], axis=1)
    (dw_in,) = matmul([(h, dz)], 'tn', [BF16], tm=512, tn=512, name="gmlp_dwin")
    (dh,) = matmul([(dz, w_in)], 'nt', [F32], tm=256, tn=256, name="gmlp_dh")
    dx, dx_bf, dg = rms_bwd(x, g, dh, dxo, "gmlp_dnorm")
    return dx, dx_bf, dg, dw_in, dlg, dlb, dws, dbs, dw_out


def _s5_disc(lr, li, ldt, br, bi):
    dt = jnp.exp(ldt)
    mag = jnp.exp(lr * dt)
    ang = li * dt
    ar = mag * jnp.cos(ang)
    ai = mag * jnp.sin(ang)
    den = lr * lr + li * li
    nr = ar - 1.0
    zr = (nr * lr + ai * li) / den
    zi = (ai * lr - nr * li) / den
    return ar, ai, zr[None] * br - zi[None] * bi, zr[None] * bi + zi[None] * br


def s5_disc_fwd(lr, li, ldt, br, bi):
    def body(lr_r, li_r, ldt_r, br_r, bi_r, ar_o, ai_o, bbr_o, bbi_o):
        res = _s5_disc(lr_r[...], li_r[...], ldt_r[...], br_r[...], bi_r[...])
        for o, v in zip((ar_o, ai_o, bbr_o, bbi_o), res):
            o[...] = v

    shp = lambda a: jax.ShapeDtypeStruct(a.shape, F32)
    return _pcall(body, name="s5_disc", out_shape=[shp(lr), shp(lr), shp(br), shp(br)],
                  compiler_params=_params())(lr, li, ldt, br, bi)


def s5_disc_bwd(lr, li, ldt, br, bi, dar, dai, dbbr, dbbi):
    def body(lr_r, li_r, ldt_r, br_r, bi_r, dar_r, dai_r, dbbr_r, dbbi_r, o1, o2, o3, o4, o5):
        _, vjp = jax.vjp(_s5_disc, lr_r[...], li_r[...], ldt_r[...], br_r[...], bi_r[...])
        res = vjp((dar_r[...], dai_r[...], dbbr_r[...], dbbi_r[...]))
        for o, v in zip((o1, o2, o3, o4, o5), res):
            o[...] = v

    shp = lambda a: jax.ShapeDtypeStruct(a.shape, F32)
    return _pcall(body, name="s5_disc_bwd", out_shape=[shp(lr), shp(lr), shp(ldt), shp(br), shp(br)],
                  compiler_params=_params())(lr, li, ldt, br, bi, dar, dai, dbbr, dbbi)


def blockdiag_matmul(pairs, outs, *, epilogue=None, extras=(), name):
    a0, b0 = pairs[0]
    M = a0.shape[0]
    T, wa, wo = b0.shape
    tm = _pick(M, (512, 256, 128))
    n_p, n_e = len(pairs), len(extras)
    in_specs, args = [], []
    for a, b in pairs:
        in_specs += [pl.BlockSpec((tm, wa), lambda k, i: (i, k)), pl.BlockSpec((None, wa, wo), lambda k, i: (k, 0, 0))]
        args += [a, b]
    for arr, kind in extras:
        in_specs.append(pl.BlockSpec((tm, wo), lambda k, i: (i, k)) if kind == 't'
                        else pl.BlockSpec((1, wo), lambda k, i: (0, k)))
        args.append(arr)

    def body(*refs):
        accs = [jnp.dot(refs[2 * p][...].astype(BF16), refs[2 * p + 1][...], preferred_element_type=F32)
                for p in range(n_p)]
        ex = [r[...] for r in refs[2 * n_p:2 * n_p + n_e]]
        res = tuple(accs) if epilogue is None else epilogue(accs, ex)
        for ref, v in zip(refs[2 * n_p + n_e:], res):
            ref[...] = v.astype(ref.dtype)

    return _pcall(body, name=name, grid=(T, M // tm), in_specs=in_specs,
                  out_specs=[pl.BlockSpec((tm, wo), lambda k, i: (i, k)) for _ in outs],
                  out_shape=[jax.ShapeDtypeStruct((M, T * wo), dt) for dt in outs],
                  compiler_params=_params())(*args)


def blockdiag_outer(pairs, name):
    M = pairs[0][0].shape[0]
    n_p = len(pairs)
    shapes = []
    in_specs, args = [], []
    tm = _pick(M, (512, 256, 128))
    T = None
    for a, b, wa, wb in pairs:
        T = a.shape[1] // wa
        in_specs += [pl.BlockSpec((tm, wa), lambda k, i: (i, k)), pl.BlockSpec((tm, wb), lambda k, i: (i, k))]
        args += [a, b]
        shapes.append((T, wa, wb))

    def body(*refs):
        @pl.when(pl.program_id(1) == 0)
        def _():
            for o in refs[2 * n_p:]:
                o[...] = jnp.zeros_like(o)
        for p in range(n_p):
            refs[2 * n_p + p][...] += lax.dot_general(refs[2 * p][...].astype(BF16), refs[2 * p + 1][...].astype(BF16),
                                                      _DIMS['tn'], preferred_element_type=F32)

    return _pcall(body, name=name, grid=(T, M // tm), in_specs=in_specs,
                  out_specs=[pl.BlockSpec((None, s[1], s[2]), lambda k, i: (k, 0, 0)) for s in shapes],
                  out_shape=[jax.ShapeDtypeStruct(s, F32) for s in shapes], compiler_params=_params())(*args)


def s5_scan(br, bi, ar, ai, reverse, name):
    L, S = br.shape
    ln = _pick(S, (SCAN_LANES, 512, 256, 128))
    tb = _pick(L, (512, 256, 128))
    n_t = L // tb
    n_q = tb // 8

    def tmap(j, t):
        return ((n_t - 1 - t) if reverse else t, j)

    blk = pl.BlockSpec((tb, ln), tmap)
    vec = pl.BlockSpec((1, ln), lambda j, t: (0, j))

    def cmul(xr, xi, yr, yi):
        return xr * yr - xi * yi, xr * yi + xi * yr

    def body(br_r, bi_r, ar_r, ai_r, hr_o, hi_o, pr_o, pi_o, cr_s, ci_s):
        @pl.when(pl.program_id(1) == 0)
        def _():
            cr_s[...] = jnp.zeros_like(cr_s)
            ci_s[...] = jnp.zeros_like(ci_s)

        a1r, a1i = ar_r[...], ai_r[...]
        a2r, a2i = cmul(a1r, a1i, a1r, a1i)
        a4r, a4i = cmul(a2r, a2i, a2r, a2i)
        row = lax.broadcasted_iota(jnp.int32, (8, ln), 0)
        dist = (7 - row) if reverse else row
        pwr, pwi = jnp.broadcast_to(a1r, (8, ln)), jnp.broadcast_to(a1i, (8, ln))
        for bit, (er, ei) in ((1, (a1r, a1i)), (2, (a2r, a2i)), (4, (a4r, a4i))):
            nr, ni = cmul(pwr, pwi, er, ei)
            sel = (dist & bit) != 0
            pwr, pwi = jnp.where(sel, nr, pwr), jnp.where(sel, ni, pwi)
        last = 0 if reverse else 7

        def shift(v, d):
            r = pltpu.roll(v, (8 - d) if reverse else d, 0)
            return jnp.where(dist >= d, r, 0.0)

        def step(q, carry):
            cr, ci = carry
            qq = (n_q - 1 - q) if reverse else q
            rows = pl.ds(pl.multiple_of(qq * 8, 8), 8)
            xr, xi = br_r[rows, :], bi_r[rows, :]
            for d, (er, ei) in ((1, (a1r, a1i)), (2, (a2r, a2i)), (4, (a4r, a4i))):
                sr, si = shift(xr, d), shift(xi, d)
                mr, mi = cmul(sr, si, er, ei)
                xr, xi = xr + mr, xi + mi
            kr, ki = cmul(pwr, pwi, cr, ci)
            xr, xi = xr + kr, xi + ki
            hr_o[rows, :] = xr
            hi_o[rows, :] = xi
            pr_o[rows, :] = jnp.where(dist >= 1, pltpu.roll(xr, 7 if reverse else 1, 0), cr)
            pi_o[rows, :] = jnp.where(dist >= 1, pltpu.roll(xi, 7 if reverse else 1, 0), ci)
            ncr = jnp.sum(jnp.where(row == last, xr, 0.0), axis=0, keepdims=True)
            nci = jnp.sum(jnp.where(row == last, xi, 0.0), axis=0, keepdims=True)
            return ncr, nci

        cr, ci = lax.fori_loop(0, n_q, step, (cr_s[...], ci_s[...]))
        cr_s[...] = cr
        ci_s[...] = ci

    shp = jax.ShapeDtypeStruct((L, S), F32)
    return _pcall(body, name=name, grid=(S // ln, n_t), in_specs=[blk, blk, vec, vec],
                  out_specs=[blk, blk, blk, blk], out_shape=[shp, shp, shp, shp],
                  scratch_shapes=[pltpu.VMEM((1, ln), F32), pltpu.VMEM((1, ln), F32)],
                  compiler_params=_params())(br, bi, ar, ai)


def _to_blockdiag(m, tile_groups):
    G, A, B = m.shape
    T = G // tile_groups
    eye = jnp.eye(tile_groups, dtype=m.dtype)
    t = m.reshape(T, tile_groups, A, 1, B) * eye[None, :, None, :, None]
    return t.reshape(T, tile_groups * A, tile_groups * B)


def _from_blockdiag(t, tile_groups):
    T, RA, RB = t.shape
    A, B = RA // tile_groups, RB // tile_groups
    d = jnp.diagonal(t.reshape(T, tile_groups, A, tile_groups, B), axis1=1, axis2=3)
    return jnp.moveaxis(d, 3, 1).reshape(T * tile_groups, A, B)


def s5_fwd(x, g, w_in, lam_re, lam_im, log_dt, b_re, b_im, c_re, c_im, d_skip, w_out):
    G, P, H = b_re.shape
    tg = min(SSM_TILE_GROUPS, G)
    h = rms_fwd(x, g, "s5_norm")
    (u,) = matmul([(h, w_in)], 'nn', [F32], tm=512, tn=512, name="s5_in")
    br_t, bi_t = jnp.transpose(b_re, (2, 0, 1)), jnp.transpose(b_im, (2, 0, 1))
    ar, ai, bbr, bbi = s5_disc_fwd(lam_re, lam_im, log_dt, br_t, bi_t)
    bbr_g, bbi_g = jnp.transpose(bbr, (1, 0, 2)), jnp.transpose(bbi, (1, 0, 2))
    bd_br, bd_bi = _to_blockdiag(bbr_g.astype(BF16), tg), _to_blockdiag(bbi_g.astype(BF16), tg)
    bur, bui = blockdiag_matmul([(u, bd_br), (u, bd_bi)], [F32, F32], name="s5_bu")
    a_r, a_i = ar.reshape(1, G * P), ai.reshape(1, G * P)
    hr, hi, hpr, hpi = s5_scan(bur, bui, a_r, a_i, False, "s5_scan")
    c_pg_r = jnp.transpose(c_re, (0, 2, 1)).astype(BF16)
    c_pg_i = jnp.transpose(c_im, (0, 2, 1)).astype(BF16)
    bd_cr, bd_nci = _to_blockdiag(c_pg_r, tg), _to_blockdiag(-c_pg_i, tg)

    def ep(accs, ex):
        y = accs[0] + accs[1] + ex[1] * ex[0]
        return y, _gelu(y)

    y, act = blockdiag_matmul([(hr, bd_cr), (hi, bd_nci)], [F32, BF16], epilogue=ep,
                              extras=[(u, 't'), (d_skip, 'r')], name="s5_y")
    (o,) = matmul([(act, w_out)], 'nn', [F32], tm=512, tn=512, name="s5_out")
    M, D = x.shape
    tm = _pick(M, (256, 128))
    (xo,) = tilemap(lambda xv, val, gt: (xv + val * _sigmoid(gt),), [(x, 't', 0), (o, 't', 0), (o, 't', 1)],
                    [(F32, 't')], M=M, N=D, tm=tm, tn=D, name="s5_glu")
    saved = (x, h, u, hr, hi, hpr, hpi, y, act, o, a_r, a_i, bd_br, bd_bi, bd_cr, bd_nci, br_t, bi_t)
    return xo, saved


def s5_bwd(saved, g, w_in, lam_re, lam_im, log_dt, b_re, d_skip, w_out, dxo):
    x, h, u, hr, hi, hpr, hpi, y, act, o, a_r, a_i, bd_br, bd_bi, bd_cr, bd_nci, br_t, bi_t = saved
    G, P, H = b_re.shape
    tg = min(SSM_TILE_GROUPS, G)
    M, D = x.shape
    tm = _pick(M, (256, 128))

    def glu_bwd(dv, val, gt):
        sg = _sigmoid(gt)
        return dv * sg, dv * val * sg * (1.0 - sg)

    dval, dgate = tilemap(glu_bwd, [(dxo, 't', 0), (o, 't', 0), (o, 't', 1)], [(BF16, 't'), (BF16, 't')],
                          M=M, N=D, tm=tm, tn=D, name="s5_dglu")
    do = jnp.concatenate([dval, dgate], axis=1)
    (dw_out,) = matmul([(act, do)], 'tn', [BF16], tm=512, tn=512, name="s5_dwout")
    (dact,) = matmul([(do, w_out)], 'nt', [F32], tm=512, tn=512, name="s5_dact")
    dy, dd = tilemap(lambda da, yv, uv: (da * _gelu_grad(yv), jnp.sum(da * _gelu_grad(yv) * uv, axis=0, keepdims=True)),
                     [(dact, 't', 0), (y, 't', 0), (u, 't', 0)], [(F32, 't'), (F32, 'a')],
                     M=M, N=D, tm=tm, tn=D, name="s5_dy")
    bd_crT, bd_nciT = jnp.transpose(bd_cr, (0, 2, 1)), jnp.transpose(bd_nci, (0, 2, 1))
    dhr, dhi = blockdiag_matmul([(dy, bd_crT), (dy, bd_nciT)], [F32, F32], name="s5_dh")
    gr, gi, _, _ = s5_scan(dhr, dhi, a_r, -a_i, True, "s5_scan_rev")
    S = G * P
    tms = _pick(M, (128,))

    def da_fn(grv, giv, hprv, hpiv):
        return (jnp.sum(grv * hprv + giv * hpiv, axis=0, keepdims=True),
                jnp.sum(giv * hprv - grv * hpiv, axis=0, keepdims=True))

    dar, dai = tilemap(da_fn, [(gr, 't', 0), (gi, 't', 0), (hpr, 't', 0), (hpi, 't', 0)], [(F32, 'a'), (F32, 'a')],
                       M=M, N=S, tm=tms, tn=_pick(S, (2048, 1024, 512)), name="s5_dabar")
    wa, wb = tg * H, tg * P
    xc_r, xc_i, xb_r, xb_i = blockdiag_outer([(dy, hr, wa, wb), (dy, hi, wa, wb), (u, gr, wa, wb), (u, gi, wa, wb)],
                                             "s5_dcb")
    dc_re = _from_blockdiag(xc_r, tg)
    dc_im = -_from_blockdiag(xc_i, tg)
    dbb_r = jnp.transpose(_from_blockdiag(xb_r, tg), (1, 0, 2))
    dbb_i = jnp.transpose(_from_blockdiag(xb_i, tg), (1, 0, 2))
    dlr, dli, dldt, dbr_t, dbi_t = s5_disc_bwd(lam_re, lam_im, log_dt, br_t, bi_t,
                                               dar.reshape(G, P), dai.reshape(G, P), dbb_r, dbb_i)
    db_re, db_im = jnp.transpose(dbr_t, (1, 2, 0)), jnp.transpose(dbi_t, (1, 2, 0))
    bd_brT, bd_biT = jnp.transpose(bd_br, (0, 2, 1)), jnp.transpose(bd_bi, (0, 2, 1))
    (du,) = blockdiag_matmul([(gr, bd_brT), (gi, bd_biT)], [BF16],
                             epilogue=lambda accs, ex: (accs[0] + accs[1] + ex[1] * ex[0],),
                             extras=[(dy, 't'), (d_skip, 'r')], name="s5_du")
    (dw_in,) = matmul([(h, du)], 'tn', [BF16], tm=512, tn=512, name="s5_dwin")
    (dh,) = matmul([(du, w_in)], 'nt', [F32], tm=512, tn=512, name="s5_dhin")
    dx, dx_bf, dg = rms_bwd(x, g, dh, dxo, "s5_dnorm")
    return dx, dx_bf, dg, dw_in, dlr, dli, dldt, db_re, db_im, dc_re, dc_im, dd, dw_out


def ple_fwd(x, g, p_emb, w_gate, w_proj, tag):
    h = rms_fwd(x, g, f"ple_norm_{tag}")
    (q,) = matmul([(p_emb, w_proj)], 'nn', [F32], tm=512, tn=512, name=f"ple_proj_{tag}")

    def ep(accs, ex):
        gt = _sigmoid(accs[0])
        return ex[0] + gt * ex[1], gt

    xo, gate = matmul([(h, w_gate)], 'nn', [F32, F32], epilogue=ep, extras=[(x, 't'), (q, 't')],
                      tm=512, tn=512, name=f"ple_gate_{tag}")
    return xo, (x, h, q, gate)


def ple_bwd(saved, g, p_emb, w_gate, dxo, tag):
    x, h, q, gate = saved
    M, D = x.shape
    tm = _pick(M, (256, 128))
    dq, dpre = tilemap(lambda dv, qv, gv: (dv * gv, dv * qv * gv * (1.0 - gv)),
                       [(dxo, 't', 0), (q, 't', 0), (gate, 't', 0)], [(BF16, 't'), (BF16, 't')],
                       M=M, N=D, tm=tm, tn=D, name=f"ple_dgate_{tag}")
    (dw_proj,) = matmul([(p_emb, dq)], 'tn', [BF16], tm=256, tn=512, name=f"ple_dwproj_{tag}")
    (dw_gate,) = matmul([(h, dpre)], 'tn', [BF16], tm=512, tn=512, name=f"ple_dwgate_{tag}")
    (dh,) = matmul([(dpre, w_gate)], 'nt', [F32], tm=512, tn=512, name=f"ple_dh_{tag}")
    dx, dx_bf, dg = rms_bwd(x, g, dh, dxo, f"ple_dnorm_{tag}")
    return dx, dx_bf, dg, dw_gate, dw_proj


def loss_head(x, g, target):
    M, D = x.shape
    tm = _pick(M, (256, 128))

    def fn(xv, gv, tv):
        r = lax.rsqrt(jnp.mean(xv * xv, axis=-1, keepdims=True) + EPS)
        xh = xv * r
        e = xh * gv - tv
        dy = e * (1.0 / D)
        dxh = dy * gv
        m = jnp.mean(dxh * xh, axis=-1, keepdims=True)
        dx = r * (dxh - xh * m)
        return jnp.sum(e * e, axis=0, keepdims=True), dx, dx, jnp.sum(dy * xh, axis=0, keepdims=True)

    return tilemap(fn, [(x, 't', 0), (g, 'r', 0), (target, 't', 0)],
                   [(F32, 'a'), (F32, 't'), (BF16, 't'), (F32, 'a')], M=M, N=D, tm=tm, tn=D, name="loss_head")


def adamw(w, g, m, v, name):
    w2, g2, m2, v2 = _as2d(w), g.reshape(_as2d(w).shape), _as2d(m), _as2d(v)
    R, C = w2.shape
    tm = _row_tile(R, C)

    def fn(wv, gv, mv, vv):
        mn = ADAM_B1 * mv + (1.0 - ADAM_B1) * gv
        vn = ADAM_B2 * vv + (1.0 - ADAM_B2) * (gv * gv)
        m_hat = mn / (1.0 - ADAM_B1 ** ADAM_STEP)
        v_hat = vn / (1.0 - ADAM_B2 ** ADAM_STEP)
        return -ADAM_LR * (m_hat / (jnp.sqrt(v_hat) + ADAM_EPS) + ADAM_WD * wv), mn, vn

    d, mn, vn = tilemap(fn, [(w2, 't', 0), (g2, 't', 0), (m2, 't', 0), (v2, 't', 0)],
                        [(F32, 't'), (F32, 't'), (F32, 't')], M=R, N=C, tm=tm, tn=C, name=name)
    return d.reshape(w.shape), mn.reshape(w.shape), vn.reshape(w.shape)


WEIGHT_NAMES = ['norm_g', 'final_norm_g', 'ffn_w_gate', 'ffn_w_up', 'ffn_w_down', 'gmlp_w_in', 'gmlp_ln_g',
                'gmlp_ln_b', 'gmlp_w_s', 'gmlp_b_s', 'gmlp_w_out', 's5_w_in', 's5_lam_re', 's5_lam_im',
                's5_log_dt', 's5_b_re', 's5_b_im', 's5_c_re', 's5_c_im', 's5_d', 's5_w_out', 'ple_w_gate',
                'ple_w_proj']
BIG = {'ffn_w_gate': 1, 'ffn_w_up': 1, 'ffn_w_down': 0, 'gmlp_w_in': 1, 'gmlp_w_out': 0, 's5_w_in': 0,
       's5_w_out': 1, 'ple_w_gate': 0, 'ple_w_proj': 1}


def _blocks(depth):
    out = []
    for i in range(depth):
        for k, half in enumerate("ab"):
            ffn = [(n, (i, k)) for n in ('ffn_w_gate', 'ffn_w_up', 'ffn_w_down')]
            if k == 1:
                out.append((f"ffn{i}b", ffn))
                out.append((f"ple{i}", [('ple_w_gate', (i,)), ('ple_w_proj', (i,))]))
            else:
                out.append((f"ffn{i}a", ffn))
                mix = 'gmlp' if i % 2 == 0 else 's5'
                out.append((f"{mix}{i}", [(f'{mix}_w_in', (i // 2,)), (f'{mix}_w_out', (i // 2,))]))
    return out


def kernel(x, p, norm_g, final_norm_g, ffn_w_gate, ffn_w_up, ffn_w_down, gmlp_w_in, gmlp_ln_g, gmlp_ln_b, gmlp_w_s, gmlp_b_s, gmlp_w_out, s5_w_in, s5_lam_re, s5_lam_im, s5_log_dt, s5_b_re, s5_b_im, s5_c_re, s5_c_im, s5_d, s5_w_out, ple_w_gate, ple_w_proj, loss_target, m_norm_g, m_final_norm_g, m_ffn_w_gate, m_ffn_w_up, m_ffn_w_down, m_gmlp_w_in, m_gmlp_ln_g, m_gmlp_ln_b, m_gmlp_w_s, m_gmlp_b_s, m_gmlp_w_out, m_s5_w_in, m_s5_lam_re, m_s5_lam_im, m_s5_log_dt, m_s5_b_re, m_s5_b_im, m_s5_c_re, m_s5_c_im, m_s5_d, m_s5_w_out, m_ple_w_gate, m_ple_w_proj, v_norm_g, v_final_norm_g, v_ffn_w_gate, v_ffn_w_up, v_ffn_w_down, v_gmlp_w_in, v_gmlp_ln_g, v_gmlp_ln_b, v_gmlp_w_s, v_gmlp_b_s, v_gmlp_w_out, v_s5_w_in, v_s5_lam_re, v_s5_lam_im, v_s5_log_dt, v_s5_b_re, v_s5_b_im, v_s5_c_re, v_s5_c_im, v_s5_d, v_s5_w_out, v_ple_w_gate, v_ple_w_proj):
    env = dict(locals())
    W = {n: env[n] for n in WEIGHT_NAMES}
    Mo = {n: env["m_" + n] for n in WEIGHT_NAMES}
    Vo = {n: env["v_" + n] for n in WEIGHT_NAMES}
    depth = norm_g.shape[0]
    L, D = x.shape[1], x.shape[2]
    s_idx = 2 * lax.axis_index("x") + lax.axis_index("y")

    sc = jnp.stack([s_idx, lax.axis_index("c")]).astype(jnp.int32)
    blocks = _blocks(depth)

    full = {}
    for bname, mats in blocks:
        casts = [cast_into_full(W[n], lead, BIG[n], sc, f"cast_{bname}_{n}") for n, lead in mats]
        res = gather_group(f"gather_{bname}", casts, [W[n].shape[-2:] for n, _ in mats], [BIG[n] for n, _ in mats])
        full.update(dict(zip(mats, res)))
    ng2 = norm_g.reshape(depth * 4, norm_g.shape[-1])
    ng_full, sd_full = gather_small("gather_small", [ng2, s5_d])
    ng_full = ng_full.reshape(depth, 4, 1, D)
    gf = final_norm_g.reshape(1, D)

    G, P, H = s5_b_re.shape[1:]
    n_grp, chunk = gmlp_w_s.shape[1], gmlp_w_s.shape[2]
    lam_re, lam_im = s5_lam_re[0], s5_lam_im[0]
    log_dt = s5_log_dt.reshape(G, 1)
    b_re, b_im, c_re, c_im = s5_b_re[0], s5_b_im[0], s5_c_re[0], s5_c_im[0]
    w_s, b_s = gmlp_w_s[0], gmlp_b_s[0].reshape(n_grp, chunk, 1)
    xs = x.reshape(L, D)
    saved = []
    def ffn_w(i, k):
        return [full[(n, (i, k))] for n in ('ffn_w_gate', 'ffn_w_up', 'ffn_w_down')]

    for i in range(depth):
        sv = {}
        xs, sv['ffn_a'] = ffn_fwd(xs, ng_full[i, 0], *ffn_w(i, 0), f"{i}a")
        j = (i // 2,)
        if i % 2 == 0:
            xs, sv['mix'] = gmlp_fwd(xs, ng_full[i, 1], full[('gmlp_w_in', j)], gmlp_ln_g, gmlp_ln_b, w_s, b_s,
                                     full[('gmlp_w_out', j)])
        else:
            xs, sv['mix'] = s5_fwd(xs, ng_full[i, 1], full[('s5_w_in', j)], lam_re, lam_im, log_dt, b_re, b_im,
                                   c_re, c_im, sd_full, full[('s5_w_out', j)])
        xs, sv['ffn_b'] = ffn_fwd(xs, ng_full[i, 2], *ffn_w(i, 1), f"{i}b")
        xs, sv['ple'] = ple_fwd(xs, ng_full[i, 3], p[i, 0], full[('ple_w_gate', (i,))], full[('ple_w_proj', (i,))], f"{i}")
        saved.append(sv)

    sq, dx, dx_bf, dgf = loss_head(xs, gf, loss_target.reshape(L, D))
    loss = lax.psum(0.5 * jnp.sum(sq) / D, ("x", "y", "c"))
    dng = [[None] * 4 for _ in range(depth)]
    gmat = {}
    small = {}
    for i in reversed(range(depth)):
        sv = saved[i]
        dx, dx_bf, dng[i][3], gmat[('ple_w_gate', (i,))], gmat[('ple_w_proj', (i,))] = ple_bwd(
            sv['ple'], ng_full[i, 3], p[i, 0], full[('ple_w_gate', (i,))], dx, f"{i}")
        dx, dx_bf, dng[i][2], gmat[('ffn_w_gate', (i, 1))], gmat[('ffn_w_up', (i, 1))], gmat[('ffn_w_down', (i, 1))] = \
            ffn_bwd(sv['ffn_b'], ng_full[i, 2], *ffn_w(i, 1), dx, dx_bf, f"{i}b")
        j = (i // 2,)
        if i % 2 == 0:
            dx, dx_bf, dng[i][1], gmat[('gmlp_w_in', j)], dlg, dlb, dws, dbs, gmat[('gmlp_w_out', j)] = gmlp_bwd(
                sv['mix'], ng_full[i, 1], full[('gmlp_w_in', j)], gmlp_ln_g, gmlp_ln_b, w_s, b_s,
                full[('gmlp_w_out', j)], dx, dx_bf)
            small.update(gmlp_ln_g=dlg, gmlp_ln_b=dlb, gmlp_w_s=dws, gmlp_b_s=dbs)
        else:
            dx, dx_bf, dng[i][1], gmat[('s5_w_in', j)], dlr, dli, dldt, db_re, db_im, dc_re, dc_im, dd, \
                gmat[('s5_w_out', j)] = s5_bwd(sv['mix'], ng_full[i, 1], full[('s5_w_in', j)], lam_re, lam_im, log_dt,
                                               b_re, sd_full, full[('s5_w_out', j)], dx)
            small.update(s5_lam_re=dlr, s5_lam_im=dli, s5_log_dt=dldt, s5_b_re=db_re, s5_b_im=db_im,
                         s5_c_re=dc_re, s5_c_im=dc_im, s5_d=dd)
        dx, dx_bf, dng[i][0], gmat[('ffn_w_gate', (i, 0))], gmat[('ffn_w_up', (i, 0))], gmat[('ffn_w_down', (i, 0))] = \
            ffn_bwd(sv['ffn_a'], ng_full[i, 0], *ffn_w(i, 0), dx, dx_bf, f"{i}a")
    grad_x = dx.reshape(x.shape)
    small['norm_g'] = jnp.stack([jnp.stack(r) for r in dng])
    small['final_norm_g'] = dgf

    gshard = {}
    for bname, mats in reversed(blocks):
        res = reduce_scatter_grads(f"reduce_{bname}", [gmat[key] for key in mats], [BIG[n] for n, _ in mats], sc)
        gshard.update(dict(zip(mats, res)))
    grads = {n: jnp.stack([gshard[(n, lead)] for lead in _mats(W[n])]).reshape(W[n].shape) for n in BIG}
    small_names = [n for n in WEIGHT_NAMES if n not in BIG]
    flat = jnp.concatenate([small[n].astype(F32).reshape(-1) for n in small_names])
    pad = (-flat.size) % (256 * 128)
    flat = jnp.pad(flat, (0, pad)).reshape(-1, 128)
    tot = allreduce_small(flat).reshape(-1)
    off = 0
    for n in small_names:
        sz = small[n].size
        gsum = tot[off:off + sz]
        off += sz
        if n == 'norm_g':
            gsum = lax.dynamic_slice_in_dim(gsum.reshape(depth, 4, D), s_idx * W[n].shape[-1], W[n].shape[-1], axis=2)
        elif n == 's5_d':
            gsum = lax.dynamic_slice_in_dim(gsum.reshape(1, D), s_idx * W[n].shape[-1], W[n].shape[-1], axis=1)
        grads[n] = gsum.reshape(W[n].shape)

    deltas, new_m, new_v = {}, {}, {}
    for n in WEIGHT_NAMES:
        deltas[n], new_m[n], new_v[n] = adamw(W[n], grads[n], Mo[n], Vo[n], f"adamw_{n}")
    return (loss, grad_x, *[grads[n] for n in WEIGHT_NAMES], *[deltas[n] for n in WEIGHT_NAMES],
            *[new_m[n] for n in WEIGHT_NAMES], *[new_v[n] for n in WEIGHT_NAMES])
```

```python
import functools
import math

import jax
import jax.numpy as jnp
from jax import lax
from jax.experimental import pallas as pl
from jax.experimental.pallas import tpu as pltpu

F32 = jnp.float32
BF16 = jnp.bfloat16
MESH_ID = pl.DeviceIdType.MESH

EPS = 1e-6
ADAM_LR = 0.001
ADAM_B1 = 0.9
ADAM_B2 = 0.999
ADAM_EPS = 1e-08
ADAM_WD = 0.01
ADAM_STEP = 10

N_SHARD = 4
V7X_VMEM_LIMIT = 52 * 2 ** 20
SSM_TILE_GROUPS = 16
SCAN_LANES = 1024
GELU_C = math.sqrt(2.0 / math.pi)


def _pcall(body, **kw):
    return pl.pallas_call(body, **kw)


def _params():
    return pltpu.CompilerParams(vmem_limit_bytes=V7X_VMEM_LIMIT)


def _pick(n, cands):
    for c in cands:
        if c <= n and n % c == 0:
            return c
    return n


def _sigmoid(x):
    return 1.0 / (1.0 + jnp.exp(-x))


def _gelu(x):
    return 0.5 * x * (1.0 + jnp.tanh(GELU_C * (x + 0.044715 * x * x * x)))


def _gelu_grad(x):
    t = jnp.tanh(GELU_C * (x + 0.044715 * x * x * x))
    return 0.5 * (1.0 + t) + 0.5 * x * (1.0 - t * t) * GELU_C * (1.0 + 3.0 * 0.044715 * x * x)


def tilemap(fn, ins, outs, *, M, N, tm, tn, name):
    n_in = len(ins)
    grid = (N // tn, M // tm)
    in_specs = []
    for arr, kind, off in ins:
        if kind == 't':
            in_specs.append(pl.BlockSpec((tm, tn), lambda j, i, off=off: (i, j + off)))
        else:
            in_specs.append(pl.BlockSpec((1, tn), lambda j, i, off=off: (0, j + off)))
    out_specs, out_shape = [], []
    for dt, kind in outs:
        if kind == 't':
            out_specs.append(pl.BlockSpec((tm, tn), lambda j, i: (i, j)))
            out_shape.append(jax.ShapeDtypeStruct((M, N), dt))
        else:
            out_specs.append(pl.BlockSpec((1, tn), lambda j, i: (0, j)))
            out_shape.append(jax.ShapeDtypeStruct((1, N), F32))

    def body(*refs):
        vals = fn(*[r[...] for r in refs[:n_in]])
        for (dt, kind), ref, v in zip(outs, refs[n_in:], vals):
            if kind == 't':
                ref[...] = v.astype(ref.dtype)
            else:
                @pl.when(pl.program_id(1) == 0)
                def _():
                    ref[...] = jnp.zeros_like(ref)
                ref[...] += v

    res = _pcall(body, name=name, grid=grid, in_specs=in_specs, out_specs=out_specs,
                 out_shape=out_shape, compiler_params=_params())(*[a for a, _, _ in ins])
    return res


def _as2d(a):
    if a.ndim >= 2 and a.shape[-1] % 128 == 0:
        return a.reshape(-1, a.shape[-1])
    if a.size % 128 == 0:
        return a.reshape(-1, 128)
    return a.reshape(-1, a.shape[-1])


def _row_tile(rows, cols, nbytes=4, budget=1 << 20):
    cands = [c for c in (2048, 1024, 512, 256, 128, 64, 32, 16, 8) if c * cols * nbytes <= budget]
    return _pick(rows, cands) if cands else _pick(rows, (8,))


ROW_TILES = (256, 128, 64, 32, 16)


def _sc_call(body, sc, args, *, grid, in_specs, out_specs, out_shape, name):
    gs = pltpu.PrefetchScalarGridSpec(num_scalar_prefetch=1, grid=grid, in_specs=in_specs, out_specs=out_specs)
    return _pcall(body, name=name, grid_spec=gs, out_shape=out_shape, compiler_params=_params())(sc, *args)


def cast_into_full(w, lead, ax, sc, name):
    R, C = w.shape[-2:]
    tm = _pick(R, ROW_TILES)
    nb = R // tm
    lead = tuple(lead)
    in_spec = pl.BlockSpec((None,) * len(lead) + (tm, C), lambda i, s: lead + (i, 0))
    if ax == 0:
        shape, out_map = (R * N_SHARD, C), (lambda i, s: (i + s[0] * nb, 0))
    else:
        shape, out_map = (R, C * N_SHARD), (lambda i, s: (i, s[0]))

    def body(s_ref, w_ref, o_ref):
        o_ref[...] = w_ref[...].astype(BF16)

    return _sc_call(body, sc, [w], grid=(nb,), in_specs=[in_spec], out_specs=pl.BlockSpec((tm, C), out_map),
                    out_shape=jax.ShapeDtypeStruct(shape, BF16), name=name)


def pairsum(g, a, ax, sc, name):
    hR, hC = a.shape
    tm = _pick(hR, ROW_TILES)
    nb = hR // tm
    g_map = (lambda i, s: (i + s[1] * nb, 0)) if ax == 1 else (lambda i, s: (i, s[1]))
    blk = (tm, hC)

    def body(s_ref, g_ref, a_ref, o_ref):
        o_ref[...] = (g_ref[...].astype(F32) + a_ref[...].astype(F32)).astype(BF16)

    return _sc_call(body, sc, [g, a], grid=(nb,),
                    in_specs=[pl.BlockSpec(blk, g_map), pl.BlockSpec(blk, lambda i, s: (i, 0))],
                    out_specs=pl.BlockSpec(blk, lambda i, s: (i, 0)),
                    out_shape=jax.ShapeDtypeStruct((hR, hC), BF16), name=name)


def shardsum(b, cbuf, ax, sc, name):
    hR, hC = b.shape
    _, pR, pC = cbuf.shape
    tm = _pick(pR, ROW_TILES)
    nb = pR // tm
    if ax == 1:
        b_map, o_map, shape = (lambda i, s: (i, s[0])), (lambda i, s: (i + s[1] * nb, 0)), (2 * pR, pC)
    else:
        b_map, o_map, shape = (lambda i, s: (i + s[0] * nb, 0)), (lambda i, s: (i, s[1])), (pR, 2 * pC)

    def body(s_ref, b_ref, c_ref, o_ref):
        acc = b_ref[...].astype(F32)
        for k in range(N_SHARD - 1):
            acc = acc + c_ref[k].astype(F32)
        o_ref[...] = acc

    return _sc_call(body, sc, [b, cbuf], grid=(nb,),
                    in_specs=[pl.BlockSpec((tm, pC), b_map), pl.BlockSpec((N_SHARD - 1, tm, pC), lambda i, s: (0, i, 0))],
                    out_specs=pl.BlockSpec((tm, pC), o_map), out_shape=jax.ShapeDtypeStruct(shape, F32), name=name)


_DIMS = {'nn': (((1,), (0,)), ((), ())), 'nt': (((1,), (1,)), ((), ())), 'tn': (((0,), (0,)), ((), ()))}


def matmul(pairs, mode, outs, *, epilogue=None, extras=(), tm=512, tn=512, name):
    a0, b0 = pairs[0]
    if mode == 'nn':
        (M, K), N = a0.shape, b0.shape[1]
    elif mode == 'nt':
        (M, K), N = a0.shape, b0.shape[0]
    else:
        (K, M), N = a0.shape, b0.shape[1]
    tm, tn = _pick(M, (tm, 256, 128)), _pick(N, (tn, 256, 128))
    n_p, n_e = len(pairs), len(extras)
    if mode == 'tn':
        a_spec = pl.BlockSpec((K, tm), lambda i, j: (0, i))
    else:
        a_spec = pl.BlockSpec((tm, K), lambda i, j: (i, 0))
    if mode == 'nt':
        b_spec = pl.BlockSpec((tn, K), lambda i, j: (j, 0))
    else:
        b_spec = pl.BlockSpec((K, tn), lambda i, j: (0, j))
    in_specs, args = [], []
    for a, b in pairs:
        in_specs += [a_spec, b_spec]
        args += [a, b]
    for arr, kind in extras:
        if kind == 't':
            in_specs.append(pl.BlockSpec((tm, tn), lambda i, j: (i, j)))
        else:
            in_specs.append(pl.BlockSpec((1, tn), lambda i, j: (0, j)))
        args.append(arr)
    dims = _DIMS[mode]

    def body(*refs):
        accs = [lax.dot_general(refs[2 * p][...].astype(BF16), refs[2 * p + 1][...].astype(BF16), dims,
                                preferred_element_type=F32) for p in range(n_p)]
        ex = [r[...] for r in refs[2 * n_p:2 * n_p + n_e]]
        if epilogue is None:
            acc = accs[0]
            for other in accs[1:]:
                acc = acc + other
            res = (acc,)
        else:
            res = epilogue(accs, ex)
        for ref, v in zip(refs[2 * n_p + n_e:], res):
            ref[...] = v.astype(ref.dtype)

    return _pcall(body, name=name, grid=(M // tm, N // tn), in_specs=in_specs,
                  out_specs=[pl.BlockSpec((tm, tn), lambda i, j: (i, j)) for _ in outs],
                  out_shape=[jax.ShapeDtypeStruct((M, N), dt) for dt in outs],
                  compiler_params=_params())(*args)


def comm_call(name, ins, out_shapes, plan, n_local, n_remote, aliases=None, after=None):
    ins = list(ins) + ([] if after is None else [after])
    n_in, n_out = len(ins), len(out_shapes)

    def body(*refs):
        in_refs, out_refs = refs[:n_in], refs[n_in:n_in + n_out]
        lsem, ssem, rsem = refs[n_in + n_out:]
        me = (lax.axis_index("x"), lax.axis_index("y"), lax.axis_index("c"))
        local, remote = plan(me, in_refs, out_refs)
        assert len(local) == n_local and len(remote) == n_remote
        lcs = [pltpu.make_async_copy(s, d, lsem.at[k]) for k, (s, d) in enumerate(local)]
        rcs = [pltpu.make_async_remote_copy(src_ref=s, dst_ref=d, send_sem=ssem.at[k], recv_sem=rsem.at[k],
                                            device_id=peer, device_id_type=MESH_ID)
               for k, (s, d, peer) in enumerate(remote)]
        for cp in rcs:
            cp.start()
        for cp in lcs:
            cp.start()
        for cp in rcs:
            cp.wait()
        for cp in lcs:
            cp.wait()

    any_spec = pl.BlockSpec(memory_space=pl.ANY)
    return _pcall(body, name=name, in_specs=[any_spec] * n_in, out_specs=[any_spec] * n_out,
                  out_shape=list(out_shapes),
                  scratch_shapes=[pltpu.SemaphoreType.DMA((max(n_local, 1),)),
                                  pltpu.SemaphoreType.DMA((max(n_remote, 1),)),
                                  pltpu.SemaphoreType.DMA((max(n_remote, 1),))],
                  input_output_aliases=aliases or {},
                  compiler_params=pltpu.CompilerParams(has_side_effects=True))(*ins)


_HBM_SPEC = pl.BlockSpec(memory_space=pltpu.HBM)
_SEM_SPEC = pl.BlockSpec(memory_space=pltpu.SEMAPHORE)
_DATAFLOW = pltpu.SideEffectType.DATAFLOW_SIDE_EFFECTING


def split_start(name, arrays, n_copies, plan, after):
    n = len(arrays)

    def body(*refs):
        ins, (ssem, rsem) = refs[:n], refs[n + 1:n + 3]
        token = refs[-1]
        me = (lax.axis_index("x"), lax.axis_index("y"), lax.axis_index("c"))
        for k, (src, dst, peer) in enumerate(plan(me, ins)):
            pltpu.make_async_remote_copy(src_ref=src, dst_ref=dst, send_sem=ssem.at[k], recv_sem=rsem.at[k],
                                         device_id=peer, device_id_type=MESH_ID).start()
        token[...] = jnp.zeros_like(token)

    sems = pltpu.SemaphoreType.DMA((n_copies,))
    res = _pcall(body, name=name,
                 out_shape=(sems, sems, *[pltpu.HBM(a.shape, a.dtype) for a in arrays], jax.ShapeDtypeStruct((8, 128), F32)),
                 in_specs=[_HBM_SPEC] * n + [pl.BlockSpec(memory_space=pl.ANY)],
                 out_specs=(_SEM_SPEC, _SEM_SPEC, *[_HBM_SPEC] * n, pl.BlockSpec(memory_space=pltpu.VMEM)),
                 input_output_aliases={m: 2 + m for m in range(n)},
                 compiler_params=pltpu.CompilerParams(has_side_effects=_DATAFLOW))(
        *[pltpu.with_memory_space_constraint(a, pltpu.HBM) for a in arrays], after)
    return res[0], res[1], list(res[2:2 + n]), res[-1]


def split_wait(name, arrays, ssem, rsem, plan, after):
    n = len(arrays)

    def body(*refs):
        ins, (ssem_r, rsem_r) = refs[:n], refs[n:n + 2]
        me = (lax.axis_index("x"), lax.axis_index("y"), lax.axis_index("c"))
        for k, (src, dst, peer) in enumerate(plan(me, ins)):
            cp = pltpu.make_async_remote_copy(src_ref=src, dst_ref=dst, send_sem=ssem_r.at[k], recv_sem=rsem_r.at[k],
                                              device_id=peer, device_id_type=MESH_ID)
            cp.wait_send()
            cp.wait_recv()

    res = _pcall(body, name=name, out_shape=[pltpu.HBM(a.shape, a.dtype) for a in arrays],
                 in_specs=[_HBM_SPEC] * n + [_SEM_SPEC, _SEM_SPEC, pl.BlockSpec(memory_space=pl.ANY)],
                 out_specs=[_HBM_SPEC] * n, input_output_aliases={m: m for m in range(n)},
                 compiler_params=pltpu.CompilerParams(has_side_effects=_DATAFLOW))(*arrays, ssem, rsem, after)
    return list(res)


def _shard_of(me):
    return 2 * me[0] + me[1]


def _plane_peers(me):
    x, y, c = me
    return [((1 - x, y, c), 2 * (1 - x) + y), ((x, 1 - y, c), 2 * x + 1 - y),
            ((1 - x, 1 - y, c), 2 * (1 - x) + 1 - y)]


def _mats(arr):
    out = [()]
    for n in arr.shape[:-2]:
        out = [o + (k,) for o in out for k in range(n)]
    return out


ROW_ALIGN = 16
LANE_ALIGN = 128


def _win(ref, lead, rows, cols):
    idx = tuple(lead)
    for spec, align in ((rows, ROW_ALIGN), (cols, LANE_ALIGN)):
        if spec is None:
            idx += (slice(None),)
        else:
            start, size = spec
            if not isinstance(start, int):
                start = pl.multiple_of(start, align)
            idx += (pl.ds(start, size),)
    return ref.at[idx]


def gather_small(name, shards, after):
    full_shapes = [jax.ShapeDtypeStruct((a.shape[0], a.shape[1] * N_SHARD), a.dtype) for a in shards]
    n = len(shards)

    def plan(me, in_refs, out_refs):
        s = _shard_of(me)
        local, remote = [], []
        for t, a in enumerate(shards):
            dst = _win(out_refs[t], (), None, (s * a.shape[1], a.shape[1]))
            local.append((in_refs[t], dst))
            for peer, _ in _plane_peers(me):
                remote.append((in_refs[t], dst, peer))
        return local, remote

    return comm_call(name, shards, full_shapes, plan, n, 3 * n, after=after)


def reduce_start(name, grads, axes, sc, after):
    n = len(grads)

    def half_win(m, h):
        R, C = grads[m].shape
        if axes[m] == 1:
            return (h * (R // 2), R // 2), None
        return None, (h * (C // 2), C // 2)

    a_shapes = []
    for g, ax in zip(grads, axes):
        R, C = g.shape
        a_shapes.append(jax.ShapeDtypeStruct((R // 2, C) if ax == 1 else (R, C // 2), BF16))

    def plan_a(me, in_refs, out_refs):
        x, y, c = me
        remote = []
        for m in range(n):
            rows, cols = half_win(m, 1 - c)
            remote.append((_win(in_refs[m], (), rows, cols), out_refs[m], (x, y, 1 - c)))
        return [], remote

    a_bufs = comm_call(name + "_pair", grads, a_shapes, plan_a, 0, n)
    b_bufs = [pairsum(g, a, ax, sc, f"{name}_pairsum{m}") for m, (g, a, ax) in enumerate(zip(grads, a_bufs, axes))]

    def piece_shape(m):
        R, C = a_shapes[m].shape
        return (R, C // N_SHARD) if axes[m] == 1 else (R // N_SHARD, C)

    def piece_win(m, s):
        R, C = piece_shape(m)
        if axes[m] == 1:
            return None, (s * C, C)
        return (s * R, R), None

    landing = [lax.empty((N_SHARD - 1,) + piece_shape(m), BF16) for m in range(n)]

    def plan_c(me, refs):
        copies = []
        for m in range(n):
            for j, (peer, ps) in enumerate(_plane_peers(me)):
                rows, cols = piece_win(m, ps)
                copies.append((_win(refs[m], (), rows, cols), refs[n + m].at[j], peer))
        return copies

    ssem, rsem, thru, token = split_start(name + "_ici_start", b_bufs + landing, 3 * n, plan_c, after)
    return (name, axes, sc, plan_c, ssem, rsem, thru), token


def reduce_finish(handle, after):
    name, axes, sc, plan_c, ssem, rsem, thru = handle
    n = len(axes)
    done = split_wait(name + "_ici_wait", thru, ssem, rsem, plan_c, after)
    b_bufs, c_bufs = done[:n], done[n:]
    shards = [shardsum(b, cb, ax, sc, f"{name}_shardsum{m}") for m, (b, cb, ax) in enumerate(zip(b_bufs, c_bufs, axes))]

    def plan_e(me, in_refs, out_refs):
        x, y, c = me
        remote = []
        for m in range(n):
            R, C = shards[m].shape
            if axes[m] == 1:
                rows, cols = (c * (R // 2), R // 2), None
            else:
                rows, cols = None, (c * (C // 2), C // 2)
            remote.append((_win(out_refs[m], (), rows, cols), _win(out_refs[m], (), rows, cols), (x, y, 1 - c)))
        return [], remote

    return comm_call(name + "_swap", shards, [jax.ShapeDtypeStruct(a.shape, a.dtype) for a in shards], plan_e, 0, n,
                     aliases={m: m for m in range(n)})


def gather_start(name, fulls, shard_shapes, axes, after):
    n = len(fulls)

    def win(ref, m, s, half):
        R, C = shard_shapes[m]
        r0 = s * R if axes[m] == 0 else 0
        return _win(ref, (), (r0 + half * (R // 2), R // 2), None if axes[m] == 0 else (s * C, C))

    def plan_ici(me, refs):
        s, c = _shard_of(me), me[2]
        return [(win(refs[m], m, s, c), win(refs[m], m, s, c), peer) for m in range(n) for peer, _ in _plane_peers(me)]

    def plan_fwd(me, in_refs, out_refs):
        x, y, c = me
        return [], [(win(out_refs[m], m, ps, c), win(out_refs[m], m, ps, c), (x, y, 1 - c))
                    for m in range(n) for _, ps in _plane_peers(me)]

    ssem, rsem, thru, token = split_start(name + "_start", fulls, 3 * n, plan_ici, after)
    return (name, plan_ici, plan_fwd, ssem, rsem, thru), token


def gather_finish(handle, after):
    name, plan_ici, plan_fwd, ssem, rsem, thru = handle
    n = len(thru)
    got = split_wait(name + "_wait", thru, ssem, rsem, plan_ici, after)
    return comm_call(name + "_fwd", got, [jax.ShapeDtypeStruct(a.shape, a.dtype) for a in got], plan_fwd, 0, 3 * n,
                     aliases={m: m for m in range(n)})


def allreduce_small(flat):
    cur = flat
    R, C = flat.shape
    for axis, flip in enumerate(((0, 0, 1), (0, 1, 0), (1, 0, 0))):
        def plan(me, in_refs, out_refs, flip=flip):
            peer = tuple(v + f * (1 - 2 * v) for v, f in zip(me, flip))
            return [], [(in_refs[0], out_refs[0], peer)]

        (got,) = comm_call(f"small_swap{axis}", [cur], [jax.ShapeDtypeStruct((R, C), F32)], plan, 0, 1)
        (cur,) = tilemap(lambda a, b: (a + b,), [(cur, 't', 0), (got, 't', 0)], [(F32, 't')], M=R, N=C,
                         tm=_pick(R, (2048, 1024, 512, 256)), tn=C, name=f"small_add{axis}")
    return cur


def rms_fwd(x, g, name):
    M, D = x.shape
    tm = _pick(M, (256, 128))

    def fn(xv, gv):
        r = lax.rsqrt(jnp.mean(xv * xv, axis=-1, keepdims=True) + EPS)
        return (xv * r * gv,)

    (h,) = tilemap(fn, [(x, 't', 0), (g, 'r', 0)], [(BF16, 't')], M=M, N=D, tm=tm, tn=D, name=name)
    return h


def rms_bwd(x, g, dh, dres, name):
    M, D = x.shape
    tm = _pick(M, (256, 128))

    def fn(xv, gv, dhv, drv):
        r = lax.rsqrt(jnp.mean(xv * xv, axis=-1, keepdims=True) + EPS)
        xh = xv * r
        dxh = dhv * gv
        m = jnp.mean(dxh * xh, axis=-1, keepdims=True)
        dx = drv + r * (dxh - xh * m)
        return dx, dx, jnp.sum(dhv * xh, axis=0, keepdims=True)

    return tilemap(fn, [(x, 't', 0), (g, 'r', 0), (dh, 't', 0), (dres, 't', 0)],
                   [(F32, 't'), (BF16, 't'), (F32, 'a')], M=M, N=D, tm=tm, tn=D, name=name)


def ffn_fwd(x, g, wg, wu, wd, tag):
    h = rms_fwd(x, g, f"ffn_norm_{tag}")

    def ep(accs, ex):
        a, b = accs
        return a, b, a * _sigmoid(a) * b

    a, b, s = matmul([(h, wg), (h, wu)], 'nn', [F32, F32, BF16], epilogue=ep, tm=512, tn=512,
                     name=f"ffn_gateup_{tag}")
    (xo,) = matmul([(s, wd)], 'nn', [F32], epilogue=lambda accs, ex: (ex[0] + 0.5 * accs[0],),
                   extras=[(x, 't')], tm=512, tn=512, name=f"ffn_down_{tag}")
    return xo, (x, h, a, b, s)


def ffn_bwd(saved, g, wg, wu, wd, dxo, dxo_bf, tag):
    x, h, a, b, s = saved

    def ep(accs, ex):
        ds = 0.5 * accs[0]
        av, bv = ex
        sg = _sigmoid(av)
        return ds * bv * (sg * (1.0 + av * (1.0 - sg))), ds * (av * sg)

    da, db = matmul([(dxo_bf, wd)], 'nt', [BF16, BF16], epilogue=ep, extras=[(a, 't'), (b, 't')],
                    tm=512, tn=512, name=f"ffn_dact_{tag}")
    (dwd,) = matmul([(s, dxo_bf)], 'tn', [BF16], epilogue=lambda accs, ex: (0.5 * accs[0],),
                    tm=512, tn=512, name=f"ffn_dwd_{tag}")
    dwg, dwu = matmul([(h, da), (h, db)], 'tn', [BF16, BF16], epilogue=lambda accs, ex: tuple(accs),
                      tm=512, tn=512, name=f"ffn_dwgu_{tag}")
    (dh,) = matmul([(da, wg), (db, wu)], 'nt', [F32], tm=512, tn=256, name=f"ffn_dh_{tag}")
    dx, dx_bf, dg = rms_bwd(x, g, dh, dxo, f"ffn_dnorm_{tag}")
    return dx, dx_bf, dg, dwg, dwu, dwd


def _tril_mask(n):
    return lax.broadcasted_iota(jnp.int32, (n, n), 0) >= lax.broadcasted_iota(jnp.int32, (n, n), 1)


def _gmlp_specs(L, half, n_grp, chunk):
    gd = half // n_grp
    specs = [pl.BlockSpec((chunk, gd), lambda g, n: (n, g)),
             pl.BlockSpec((chunk, gd), lambda g, n: (n, n_grp + g)),
             pl.BlockSpec((1, gd), lambda g, n: (0, g)),
             pl.BlockSpec((1, gd), lambda g, n: (0, g)),
             pl.BlockSpec((None, chunk, chunk), lambda g, n: (g, 0, 0)),
             pl.BlockSpec((None, chunk, 1), lambda g, n: (g, 0, 0))]
    return gd, specs


def _gmlp_gate_values(zu, zv, lg, lb, ws, bs):
    u, v = _gelu(zu), _gelu(zv)
    mu = jnp.mean(v, axis=-1, keepdims=True)
    d = v - mu
    rstd = lax.rsqrt(jnp.mean(d * d, axis=-1, keepdims=True) + EPS)
    vhat = d * rstd
    vn = vhat * lg + lb
    w = jnp.where(_tril_mask(ws.shape[0]), ws, 0.0).astype(BF16)
    sv = jnp.dot(w, vn.astype(BF16), preferred_element_type=F32) + bs
    return u, vhat, rstd, vn, w, sv


def gmlp_gate_fwd(zpre, ln_g, ln_b, w_s, b_s):
    L, half = zpre.shape[0], zpre.shape[1] // 2
    n_grp, chunk = w_s.shape[0], w_s.shape[1]
    gd, specs = _gmlp_specs(L, half, n_grp, chunk)

    def body(zu, zv, lg, lb, ws, bs, o):
        u, _, _, _, _, sv = _gmlp_gate_values(zu[...], zv[...], lg[...], lb[...], ws[...], bs[...])
        o[...] = (u * sv).astype(o.dtype)

    return _pcall(body, name="gmlp_gate", grid=(n_grp, L // chunk), in_specs=specs,
                  out_specs=pl.BlockSpec((chunk, gd), lambda g, n: (n, g)),
                  out_shape=jax.ShapeDtypeStruct((L, half), BF16), compiler_params=_params())(
        zpre, zpre, ln_g, ln_b, w_s, b_s)


def gmlp_gate_bwd(zpre, ln_g, ln_b, w_s, b_s, dgated):
    L, half = zpre.shape[0], zpre.shape[1] // 2
    n_grp, chunk = w_s.shape[0], w_s.shape[1]
    gd, specs = _gmlp_specs(L, half, n_grp, chunk)
    specs = specs + [pl.BlockSpec((chunk, gd), lambda g, n: (n, g))]

    def body(zu, zv, lg, lb, ws, bs, dg, dzu, dzv, dws, dbs, dlg, dlb):
        zuv, zvv, lgv = zu[...], zv[...], lg[...]
        u, vhat, rstd, vn, w, sv = _gmlp_gate_values(zuv, zvv, lgv, lb[...], ws[...], bs[...])
        dgv = dg[...]
        du = dgv * sv
        dsv = dgv * u
        dsv_bf = dsv.astype(BF16)
        dw = lax.dot_general(dsv_bf, vn.astype(BF16), _DIMS['nt'], preferred_element_type=F32)
        dvn = lax.dot_general(w, dsv_bf, _DIMS['tn'], preferred_element_type=F32)
        dvhat = dvn * lgv
        dv = rstd * (dvhat - jnp.mean(dvhat, axis=-1, keepdims=True)
                     - vhat * jnp.mean(dvhat * vhat, axis=-1, keepdims=True))
        dzu[...] = (du * _gelu_grad(zuv)).astype(dzu.dtype)
        dzv[...] = (dv * _gelu_grad(zvv)).astype(dzv.dtype)

        @pl.when(pl.program_id(1) == 0)
        def _():
            dws[...] = jnp.zeros_like(dws)
            dbs[...] = jnp.zeros_like(dbs)
            dlg[...] = jnp.zeros_like(dlg)
            dlb[...] = jnp.zeros_like(dlb)

        dws[...] += jnp.where(_tril_mask(chunk), dw, 0.0)
        dbs[...] += jnp.sum(dsv, axis=1, keepdims=True)
        dlg[...] += jnp.sum(dvn * vhat, axis=0, keepdims=True)
        dlb[...] += jnp.sum(dvn, axis=0, keepdims=True)

    tile = pl.BlockSpec((chunk, gd), lambda g, n: (n, g))
    vec = pl.BlockSpec((1, gd), lambda g, n: (0, g))
    return _pcall(body, name="gmlp_gate_bwd", grid=(n_grp, L // chunk), in_specs=specs,
                  out_specs=[tile, tile, pl.BlockSpec((None, chunk, chunk), lambda g, n: (g, 0, 0)),
                             pl.BlockSpec((None, chunk, 1), lambda g, n: (g, 0, 0)), vec, vec],
                  out_shape=[jax.ShapeDtypeStruct((L, half), BF16), jax.ShapeDtypeStruct((L, half), BF16),
                             jax.ShapeDtypeStruct((n_grp, chunk, chunk), F32),
                             jax.ShapeDtypeStruct((n_grp, chunk, 1), F32),
                             jax.ShapeDtypeStruct((1, half), F32), jax.ShapeDtypeStruct((1, half), F32)],
                  compiler_params=_params())(zpre, zpre, ln_g, ln_b, w_s, b_s, dgated)


def gmlp_fwd(x, g, w_in, ln_g, ln_b, w_s, b_s, w_out):
    h = rms_fwd(x, g, "gmlp_norm")
    (zpre,) = matmul([(h, w_in)], 'nn', [F32], tm=512, tn=512, name="gmlp_in")
    gated = gmlp_gate_fwd(zpre, ln_g, ln_b, w_s, b_s)
    (xo,) = matmul([(gated, w_out)], 'nn', [F32], epilogue=lambda accs, ex: (ex[0] + accs[0],),
                   extras=[(x, 't')], tm=512, tn=512, name="gmlp_out")
    return xo, (x, h, zpre, gated)


def gmlp_bwd(saved, g, w_in, ln_g, ln_b, w_s, b_s, w_out, dxo, dxo_bf):
    x, h, zpre, gated = saved
    (dgated,) = matmul([(dxo_bf, w_out)], 'nt', [F32], tm=512, tn=512, name="gmlp_dgated")
    (dw_out,) = matmul([(gated, dxo_bf)], 'tn', [BF16], tm=512, tn=512, name="gmlp_dwout")
    dzu, dzv, dws, dbs, dlg, dlb = gmlp_gate_bwd(zpre, ln_g, ln_b, w_s, b_s, dgated)
    dz = jnp.concatenate([dzu, dzv], axis=1)
    (dw_in,) = matmul([(h, dz)], 'tn', [BF16], tm=512, tn=512, name="gmlp_dwin")
    (dh,) = matmul([(dz, w_in)], 'nt', [F32], tm=256, tn=256, name="gmlp_dh")
    dx, dx_bf, dg = rms_bwd(x, g, dh, dxo, "gmlp_dnorm")
    return dx, dx_bf, dg, dw_in, dlg, dlb, dws, dbs, dw_out


def _s5_disc(lr, li, ldt, br, bi):
    dt = jnp.exp(ldt)
    mag = jnp.exp(lr * dt)
    ang = li * dt
    ar = mag * jnp.cos(ang)
    ai = mag * jnp.sin(ang)
    den = lr * lr + li * li
    nr = ar - 1.0
    zr = (nr * lr + ai * li) / den
    zi = (ai * lr - nr * li) / den
    return ar, ai, zr[None] * br - zi[None] * bi, zr[None] * bi + zi[None] * br


def s5_disc_fwd(lr, li, ldt, br, bi):
    def body(lr_r, li_r, ldt_r, br_r, bi_r, ar_o, ai_o, bbr_o, bbi_o):
        res = _s5_disc(lr_r[...], li_r[...], ldt_r[...], br_r[...], bi_r[...])
        for o, v in zip((ar_o, ai_o, bbr_o, bbi_o), res):
            o[...] = v

    shp = lambda a: jax.ShapeDtypeStruct(a.shape, F32)
    return _pcall(body, name="s5_disc", out_shape=[shp(lr), shp(lr), shp(br), shp(br)],
                  compiler_params=_params())(lr, li, ldt, br, bi)


def s5_disc_bwd(lr, li, ldt, br, bi, dar, dai, dbbr, dbbi):
    def body(lr_r, li_r, ldt_r, br_r, bi_r, dar_r, dai_r, dbbr_r, dbbi_r, o1, o2, o3, o4, o5):
        _, vjp = jax.vjp(_s5_disc, lr_r[...], li_r[...], ldt_r[...], br_r[...], bi_r[...])
        res = vjp((dar_r[...], dai_r[...], dbbr_r[...], dbbi_r[...]))
        for o, v in zip((o1, o2, o3, o4, o5), res):
            o[...] = v

    shp = lambda a: jax.ShapeDtypeStruct(a.shape, F32)
    return _pcall(body, name="s5_disc_bwd", out_shape=[shp(lr), shp(lr), shp(ldt), shp(br), shp(br)],
                  compiler_params=_params())(lr, li, ldt, br, bi, dar, dai, dbbr, dbbi)


def blockdiag_matmul(pairs, outs, *, epilogue=None, extras=(), name):
    a0, b0 = pairs[0]
    M = a0.shape[0]
    T, wa, wo = b0.shape
    tm = _pick(M, (512, 256, 128))
    n_p, n_e = len(pairs), len(extras)
    in_specs, args = [], []
    for a, b in pairs:
        in_specs += [pl.BlockSpec((tm, wa), lambda k, i: (i, k)), pl.BlockSpec((None, wa, wo), lambda k, i: (k, 0, 0))]
        args += [a, b]
    for arr, kind in extras:
        in_specs.append(pl.BlockSpec((tm, wo), lambda k, i: (i, k)) if kind == 't'
                        else pl.BlockSpec((1, wo), lambda k, i: (0, k)))
        args.append(arr)

    def body(*refs):
        accs = [jnp.dot(refs[2 * p][...].astype(BF16), refs[2 * p + 1][...], preferred_element_type=F32)
                for p in range(n_p)]
        ex = [r[...] for r in refs[2 * n_p:2 * n_p + n_e]]
        res = tuple(accs) if epilogue is None else epilogue(accs, ex)
        for ref, v in zip(refs[2 * n_p + n_e:], res):
            ref[...] = v.astype(ref.dtype)

    return _pcall(body, name=name, grid=(T, M // tm), in_specs=in_specs,
                  out_specs=[pl.BlockSpec((tm, wo), lambda k, i: (i, k)) for _ in outs],
                  out_shape=[jax.ShapeDtypeStruct((M, T * wo), dt) for dt in outs],
                  compiler_params=_params())(*args)


def blockdiag_outer(pairs, name):
    M = pairs[0][0].shape[0]
    n_p = len(pairs)
    shapes = []
    in_specs, args = [], []
    tm = _pick(M, (512, 256, 128))
    T = None
    for a, b, wa, wb in pairs:
        T = a.shape[1] // wa
        in_specs += [pl.BlockSpec((tm, wa), lambda k, i: (i, k)), pl.BlockSpec((tm, wb), lambda k, i: (i, k))]
        args += [a, b]
        shapes.append((T, wa, wb))

    def body(*refs):
        @pl.when(pl.program_id(1) == 0)
        def _():
            for o in refs[2 * n_p:]:
                o[...] = jnp.zeros_like(o)
        for p in range(n_p):
            refs[2 * n_p + p][...] += lax.dot_general(refs[2 * p][...].astype(BF16), refs[2 * p + 1][...].astype(BF16),
                                                      _DIMS['tn'], preferred_element_type=F32)

    return _pcall(body, name=name, grid=(T, M // tm), in_specs=in_specs,
                  out_specs=[pl.BlockSpec((None, s[1], s[2]), lambda k, i: (k, 0, 0)) for s in shapes],
                  out_shape=[jax.ShapeDtypeStruct(s, F32) for s in shapes], compiler_params=_params())(*args)


def s5_scan(br, bi, ar, ai, reverse, name):
    L, S = br.shape
    ln = _pick(S, (SCAN_LANES, 512, 256, 128))
    tb = _pick(L, (512, 256, 128))
    n_t = L // tb
    n_q = tb // 8

    def tmap(j, t):
        return ((n_t - 1 - t) if reverse else t, j)

    blk = pl.BlockSpec((tb, ln), tmap)
    vec = pl.BlockSpec((1, ln), lambda j, t: (0, j))

    def cmul(xr, xi, yr, yi):
        return xr * yr - xi * yi, xr * yi + xi * yr

    def body(br_r, bi_r, ar_r, ai_r, hr_o, hi_o, pr_o, pi_o, cr_s, ci_s):
        @pl.when(pl.program_id(1) == 0)
        def _():
            cr_s[...] = jnp.zeros_like(cr_s)
            ci_s[...] = jnp.zeros_like(ci_s)

        a1r, a1i = ar_r[...], ai_r[...]
        a2r, a2i = cmul(a1r, a1i, a1r, a1i)
        a4r, a4i = cmul(a2r, a2i, a2r, a2i)
        row = lax.broadcasted_iota(jnp.int32, (8, ln), 0)
        dist = (7 - row) if reverse else row
        pwr, pwi = jnp.broadcast_to(a1r, (8, ln)), jnp.broadcast_to(a1i, (8, ln))
        for bit, (er, ei) in ((1, (a1r, a1i)), (2, (a2r, a2i)), (4, (a4r, a4i))):
            nr, ni = cmul(pwr, pwi, er, ei)
            sel = (dist & bit) != 0
            pwr, pwi = jnp.where(sel, nr, pwr), jnp.where(sel, ni, pwi)
        last = 0 if reverse else 7

        def shift(v, d):
            r = pltpu.roll(v, (8 - d) if reverse else d, 0)
            return jnp.where(dist >= d, r, 0.0)

        def step(q, carry):
            cr, ci = carry
            qq = (n_q - 1 - q) if reverse else q
            rows = pl.ds(pl.multiple_of(qq * 8, 8), 8)
            xr, xi = br_r[rows, :], bi_r[rows, :]
            for d, (er, ei) in ((1, (a1r, a1i)), (2, (a2r, a2i)), (4, (a4r, a4i))):
                sr, si = shift(xr, d), shift(xi, d)
                mr, mi = cmul(sr, si, er, ei)
                xr, xi = xr + mr, xi + mi
            kr, ki = cmul(pwr, pwi, cr, ci)
            xr, xi = xr + kr, xi + ki
            hr_o[rows, :] = xr
            hi_o[rows, :] = xi
            pr_o[rows, :] = jnp.where(dist >= 1, pltpu.roll(xr, 7 if reverse else 1, 0), cr)
            pi_o[rows, :] = jnp.where(dist >= 1, pltpu.roll(xi, 7 if reverse else 1, 0), ci)
            ncr = jnp.sum(jnp.where(row == last, xr, 0.0), axis=0, keepdims=True)
            nci = jnp.sum(jnp.where(row == last, xi, 0.0), axis=0, keepdims=True)
            return ncr, nci

        cr, ci = lax.fori_loop(0, n_q, step, (cr_s[...], ci_s[...]))
        cr_s[...] = cr
        ci_s[...] = ci

    shp = jax.ShapeDtypeStruct((L, S), F32)
    return _pcall(body, name=name, grid=(S // ln, n_t), in_specs=[blk, blk, vec, vec],
                  out_specs=[blk, blk, blk, blk], out_shape=[shp, shp, shp, shp],
                  scratch_shapes=[pltpu.VMEM((1, ln), F32), pltpu.VMEM((1, ln), F32)],
                  compiler_params=_params())(br, bi, ar, ai)


def _to_blockdiag(m, tile_groups):
    G, A, B = m.shape
    T = G // tile_groups
    eye = jnp.eye(tile_groups, dtype=m.dtype)
    t = m.reshape(T, tile_groups, A, 1, B) * eye[None, :, None, :, None]
    return t.reshape(T, tile_groups * A, tile_groups * B)


def _from_blockdiag(t, tile_groups):
    T, RA, RB = t.shape
    A, B = RA // tile_groups, RB // tile_groups
    d = jnp.diagonal(t.reshape(T, tile_groups, A, tile_groups, B), axis1=1, axis2=3)
    return jnp.moveaxis(d, 3, 1).reshape(T * tile_groups, A, B)


def s5_fwd(x, g, w_in, lam_re, lam_im, log_dt, b_re, b_im, c_re, c_im, d_skip, w_out):
    G, P, H = b_re.shape
    tg = min(SSM_TILE_GROUPS, G)
    h = rms_fwd(x, g, "s5_norm")
    (u,) = matmul([(h, w_in)], 'nn', [F32], tm=512, tn=512, name="s5_in")
    br_t, bi_t = jnp.transpose(b_re, (2, 0, 1)), jnp.transpose(b_im, (2, 0, 1))
    ar, ai, bbr, bbi = s5_disc_fwd(lam_re, lam_im, log_dt, br_t, bi_t)
    bbr_g, bbi_g = jnp.transpose(bbr, (1, 0, 2)), jnp.transpose(bbi, (1, 0, 2))
    bd_br, bd_bi = _to_blockdiag(bbr_g.astype(BF16), tg), _to_blockdiag(bbi_g.astype(BF16), tg)
    bur, bui = blockdiag_matmul([(u, bd_br), (u, bd_bi)], [F32, F32], name="s5_bu")
    a_r, a_i = ar.reshape(1, G * P), ai.reshape(1, G * P)
    hr, hi, hpr, hpi = s5_scan(bur, bui, a_r, a_i, False, "s5_scan")
    c_pg_r = jnp.transpose(c_re, (0, 2, 1)).astype(BF16)
    c_pg_i = jnp.transpose(c_im, (0, 2, 1)).astype(BF16)
    bd_cr, bd_nci = _to_blockdiag(c_pg_r, tg), _to_blockdiag(-c_pg_i, tg)

    def ep(accs, ex):
        y = accs[0] + accs[1] + ex[1] * ex[0]
        return y, _gelu(y)

    y, act = blockdiag_matmul([(hr, bd_cr), (hi, bd_nci)], [F32, BF16], epilogue=ep,
                              extras=[(u, 't'), (d_skip, 'r')], name="s5_y")
    (o,) = matmul([(act, w_out)], 'nn', [F32], tm=512, tn=512, name="s5_out")
    M, D = x.shape
    tm = _pick(M, (256, 128))
    (xo,) = tilemap(lambda xv, val, gt: (xv + val * _sigmoid(gt),), [(x, 't', 0), (o, 't', 0), (o, 't', 1)],
                    [(F32, 't')], M=M, N=D, tm=tm, tn=D, name="s5_glu")
    saved = (x, h, u, hr, hi, hpr, hpi, y, act, o, a_r, a_i, bd_br, bd_bi, bd_cr, bd_nci, br_t, bi_t)
    return xo, saved


def s5_bwd(saved, g, w_in, lam_re, lam_im, log_dt, b_re, d_skip, w_out, dxo):
    x, h, u, hr, hi, hpr, hpi, y, act, o, a_r, a_i, bd_br, bd_bi, bd_cr, bd_nci, br_t, bi_t = saved
    G, P, H = b_re.shape
    tg = min(SSM_TILE_GROUPS, G)
    M, D = x.shape
    tm = _pick(M, (256, 128))

    def glu_bwd(dv, val, gt):
        sg = _sigmoid(gt)
        return dv * sg, dv * val * sg * (1.0 - sg)

    dval, dgate = tilemap(glu_bwd, [(dxo, 't', 0), (o, 't', 0), (o, 't', 1)], [(BF16, 't'), (BF16, 't')],
                          M=M, N=D, tm=tm, tn=D, name="s5_dglu")
    do = jnp.concatenate([dval, dgate], axis=1)
    (dw_out,) = matmul([(act, do)], 'tn', [BF16], tm=512, tn=512, name="s5_dwout")
    (dact,) = matmul([(do, w_out)], 'nt', [F32], tm=512, tn=512, name="s5_dact")
    dy, dd = tilemap(lambda da, yv, uv: (da * _gelu_grad(yv), jnp.sum(da * _gelu_grad(yv) * uv, axis=0, keepdims=True)),
                     [(dact, 't', 0), (y, 't', 0), (u, 't', 0)], [(F32, 't'), (F32, 'a')],
                     M=M, N=D, tm=tm, tn=D, name="s5_dy")
    bd_crT, bd_nciT = jnp.transpose(bd_cr, (0, 2, 1)), jnp.transpose(bd_nci, (0, 2, 1))
    dhr, dhi = blockdiag_matmul([(dy, bd_crT), (dy, bd_nciT)], [F32, F32], name="s5_dh")
    gr, gi, _, _ = s5_scan(dhr, dhi, a_r, -a_i, True, "s5_scan_rev")
    S = G * P
    tms = _pick(M, (128,))

    def da_fn(grv, giv, hprv, hpiv):
        return (jnp.sum(grv * hprv + giv * hpiv, axis=0, keepdims=True),
                jnp.sum(giv * hprv - grv * hpiv, axis=0, keepdims=True))

    dar, dai = tilemap(da_fn, [(gr, 't', 0), (gi, 't', 0), (hpr, 't', 0), (hpi, 't', 0)], [(F32, 'a'), (F32, 'a')],
                       M=M, N=S, tm=tms, tn=_pick(S, (2048, 1024, 512)), name="s5_dabar")
    wa, wb = tg * H, tg * P
    xc_r, xc_i, xb_r, xb_i = blockdiag_outer([(dy, hr, wa, wb), (dy, hi, wa, wb), (u, gr, wa, wb), (u, gi, wa, wb)],
                                             "s5_dcb")
    dc_re = _from_blockdiag(xc_r, tg)
    dc_im = -_from_blockdiag(xc_i, tg)
    dbb_r = jnp.transpose(_from_blockdiag(xb_r, tg), (1, 0, 2))
    dbb_i = jnp.transpose(_from_blockdiag(xb_i, tg), (1, 0, 2))
    dlr, dli, dldt, dbr_t, dbi_t = s5_disc_bwd(lam_re, lam_im, log_dt, br_t, bi_t,
                                               dar.reshape(G, P), dai.reshape(G, P), dbb_r, dbb_i)
    db_re, db_im = jnp.transpose(dbr_t, (1, 2, 0)), jnp.transpose(dbi_t, (1, 2, 0))
    bd_brT, bd_biT = jnp.transpose(bd_br, (0, 2, 1)), jnp.transpose(bd_bi, (0, 2, 1))
    (du,) = blockdiag_matmul([(gr, bd_brT), (gi, bd_biT)], [BF16],
                             epilogue=lambda accs, ex: (accs[0] + accs[1] + ex[1] * ex[0],),
                             extras=[(dy, 't'), (d_skip, 'r')], name="s5_du")
    (dw_in,) = matmul([(h, du)], 'tn', [BF16], tm=512, tn=512, name="s5_dwin")
    (dh,) = matmul([(du, w_in)], 'nt', [F32], tm=512, tn=512, name="s5_dhin")
    dx, dx_bf, dg = rms_bwd(x, g, dh, dxo, "s5_dnorm")
    return dx, dx_bf, dg, dw_in, dlr, dli, dldt, db_re, db_im, dc_re, dc_im, dd, dw_out


def ple_fwd(x, g, p_emb, w_gate, w_proj, tag):
    h = rms_fwd(x, g, f"ple_norm_{tag}")
    (q,) = matmul([(p_emb, w_proj)], 'nn', [F32], tm=512, tn=512, name=f"ple_proj_{tag}")

    def ep(accs, ex):
        gt = _sigmoid(accs[0])
        return ex[0] + gt * ex[1], gt

    xo, gate = matmul([(h, w_gate)], 'nn', [F32, F32], epilogue=ep, extras=[(x, 't'), (q, 't')],
                      tm=512, tn=512, name=f"ple_gate_{tag}")
    return xo, (x, h, q, gate)


def ple_bwd(saved, g, p_emb, w_gate, dxo, tag):
    x, h, q, gate = saved
    M, D = x.shape
    tm = _pick(M, (256, 128))
    dq, dpre = tilemap(lambda dv, qv, gv: (dv * gv, dv * qv * gv * (1.0 - gv)),
                       [(dxo, 't', 0), (q, 't', 0), (gate, 't', 0)], [(BF16, 't'), (BF16, 't')],
                       M=M, N=D, tm=tm, tn=D, name=f"ple_dgate_{tag}")
    (dw_proj,) = matmul([(p_emb, dq)], 'tn', [BF16], tm=256, tn=512, name=f"ple_dwproj_{tag}")
    (dw_gate,) = matmul([(h, dpre)], 'tn', [BF16], tm=512, tn=512, name=f"ple_dwgate_{tag}")
    (dh,) = matmul([(dpre, w_gate)], 'nt', [F32], tm=512, tn=512, name=f"ple_dh_{tag}")
    dx, dx_bf, dg = rms_bwd(x, g, dh, dxo, f"ple_dnorm_{tag}")
    return dx, dx_bf, dg, dw_gate, dw_proj


def loss_head(x, g, target):
    M, D = x.shape
    tm = _pick(M, (256, 128))

    def fn(xv, gv, tv):
        r = lax.rsqrt(jnp.mean(xv * xv, axis=-1, keepdims=True) + EPS)
        xh = xv * r
        e = xh * gv - tv
        dy = e * (1.0 / D)
        dxh = dy * gv
        m = jnp.mean(dxh * xh, axis=-1, keepdims=True)
        dx = r * (dxh - xh * m)
        return jnp.sum(e * e, axis=0, keepdims=True), dx, dx, jnp.sum(dy * xh, axis=0, keepdims=True)

    return tilemap(fn, [(x, 't', 0), (g, 'r', 0), (target, 't', 0)],
                   [(F32, 'a'), (F32, 't'), (BF16, 't'), (F32, 'a')], M=M, N=D, tm=tm, tn=D, name="loss_head")


def adamw(w, g, m, v, name):
    w2, g2, m2, v2 = _as2d(w), g.reshape(_as2d(w).shape), _as2d(m), _as2d(v)
    R, C = w2.shape
    tm = _row_tile(R, C)

    def fn(wv, gv, mv, vv):
        mn = ADAM_B1 * mv + (1.0 - ADAM_B1) * gv
        vn = ADAM_B2 * vv + (1.0 - ADAM_B2) * (gv * gv)
        m_hat = mn / (1.0 - ADAM_B1 ** ADAM_STEP)
        v_hat = vn / (1.0 - ADAM_B2 ** ADAM_STEP)
        return -ADAM_LR * (m_hat / (jnp.sqrt(v_hat) + ADAM_EPS) + ADAM_WD * wv), mn, vn

    d, mn, vn = tilemap(fn, [(w2, 't', 0), (g2, 't', 0), (m2, 't', 0), (v2, 't', 0)],
                        [(F32, 't'), (F32, 't'), (F32, 't')], M=R, N=C, tm=tm, tn=C, name=name)
    return d.reshape(w.shape), mn.reshape(w.shape), vn.reshape(w.shape)


WEIGHT_NAMES = ['norm_g', 'final_norm_g', 'ffn_w_gate', 'ffn_w_up', 'ffn_w_down', 'gmlp_w_in', 'gmlp_ln_g',
                'gmlp_ln_b', 'gmlp_w_s', 'gmlp_b_s', 'gmlp_w_out', 's5_w_in', 's5_lam_re', 's5_lam_im',
                's5_log_dt', 's5_b_re', 's5_b_im', 's5_c_re', 's5_c_im', 's5_d', 's5_w_out', 'ple_w_gate',
                'ple_w_proj']
BIG = {'ffn_w_gate': 1, 'ffn_w_up': 1, 'ffn_w_down': 0, 'gmlp_w_in': 1, 'gmlp_w_out': 0, 's5_w_in': 0,
       's5_w_out': 1, 'ple_w_gate': 0, 'ple_w_proj': 1}


def _blocks(depth):
    out = []
    for i in range(depth):
        for k, half in enumerate("ab"):
            ffn = [(n, (i, k)) for n in ('ffn_w_gate', 'ffn_w_up', 'ffn_w_down')]
            if k == 1:
                out.append((f"ffn{i}b", ffn))
                out.append((f"ple{i}", [('ple_w_gate', (i,)), ('ple_w_proj', (i,))]))
            else:
                out.append((f"ffn{i}a", ffn))
                mix = 'gmlp' if i % 2 == 0 else 's5'
                out.append((f"{mix}{i}", [(f'{mix}_w_in', (i // 2,)), (f'{mix}_w_out', (i // 2,))]))
    return out


def kernel(x, p, norm_g, final_norm_g, ffn_w_gate, ffn_w_up, ffn_w_down, gmlp_w_in, gmlp_ln_g, gmlp_ln_b, gmlp_w_s, gmlp_b_s, gmlp_w_out, s5_w_in, s5_lam_re, s5_lam_im, s5_log_dt, s5_b_re, s5_b_im, s5_c_re, s5_c_im, s5_d, s5_w_out, ple_w_gate, ple_w_proj, loss_target, m_norm_g, m_final_norm_g, m_ffn_w_gate, m_ffn_w_up, m_ffn_w_down, m_gmlp_w_in, m_gmlp_ln_g, m_gmlp_ln_b, m_gmlp_w_s, m_gmlp_b_s, m_gmlp_w_out, m_s5_w_in, m_s5_lam_re, m_s5_lam_im, m_s5_log_dt, m_s5_b_re, m_s5_b_im, m_s5_c_re, m_s5_c_im, m_s5_d, m_s5_w_out, m_ple_w_gate, m_ple_w_proj, v_norm_g, v_final_norm_g, v_ffn_w_gate, v_ffn_w_up, v_ffn_w_down, v_gmlp_w_in, v_gmlp_ln_g, v_gmlp_ln_b, v_gmlp_w_s, v_gmlp_b_s, v_gmlp_w_out, v_s5_w_in, v_s5_lam_re, v_s5_lam_im, v_s5_log_dt, v_s5_b_re, v_s5_b_im, v_s5_c_re, v_s5_c_im, v_s5_d, v_s5_w_out, v_ple_w_gate, v_ple_w_proj):
    env = dict(locals())
    W = {n: env[n] for n in WEIGHT_NAMES}
    Mo = {n: env["m_" + n] for n in WEIGHT_NAMES}
    Vo = {n: env["v_" + n] for n in WEIGHT_NAMES}
    depth = norm_g.shape[0]
    L, D = x.shape[1], x.shape[2]
    s_idx = 2 * lax.axis_index("x") + lax.axis_index("y")

    sc = jnp.stack([s_idx, lax.axis_index("c")]).astype(jnp.int32)
    blocks = _blocks(depth)

    full = {}
    gathers = []
    token = sc
    for bname, mats in blocks:
        casts = [cast_into_full(W[n], lead, BIG[n], sc, f"cast_{bname}_{n}") for n, lead in mats]
        handle, token = gather_start(f"gather_{bname}", casts, [W[n].shape[-2:] for n, _ in mats],
                                     [BIG[n] for n, _ in mats], token)
        gathers.append(handle)
    ng2 = norm_g.reshape(depth * 4, norm_g.shape[-1])
    ng_full, sd_full = gather_small("gather_small", [ng2, s5_d], token)
    ng_full = ng_full.reshape(depth, 4, 1, D)

    def fetch(bi, after):
        full.update(dict(zip(blocks[bi][1], gather_finish(gathers[bi], after))))
    gf = final_norm_g.reshape(1, D)

    G, P, H = s5_b_re.shape[1:]
    n_grp, chunk = gmlp_w_s.shape[1], gmlp_w_s.shape[2]
    lam_re, lam_im = s5_lam_re[0], s5_lam_im[0]
    log_dt = s5_log_dt.reshape(G, 1)
    b_re, b_im, c_re, c_im = s5_b_re[0], s5_b_im[0], s5_c_re[0], s5_c_im[0]
    w_s, b_s = gmlp_w_s[0], gmlp_b_s[0].reshape(n_grp, chunk, 1)
    xs = x.reshape(L, D)
    saved = []
    def ffn_w(i, k):
        return [full[(n, (i, k))] for n in ('ffn_w_gate', 'ffn_w_up', 'ffn_w_down')]

    for i in range(depth):
        sv = {}
        fetch(4 * i, ng_full if i == 0 else xs)
        xs, sv['ffn_a'] = ffn_fwd(xs, ng_full[i, 0], *ffn_w(i, 0), f"{i}a")
        j = (i // 2,)
        fetch(4 * i + 1, xs)
        if i % 2 == 0:
            xs, sv['mix'] = gmlp_fwd(xs, ng_full[i, 1], full[('gmlp_w_in', j)], gmlp_ln_g, gmlp_ln_b, w_s, b_s,
                                     full[('gmlp_w_out', j)])
        else:
            xs, sv['mix'] = s5_fwd(xs, ng_full[i, 1], full[('s5_w_in', j)], lam_re, lam_im, log_dt, b_re, b_im,
                                   c_re, c_im, sd_full, full[('s5_w_out', j)])
        fetch(4 * i + 2, xs)
        xs, sv['ffn_b'] = ffn_fwd(xs, ng_full[i, 2], *ffn_w(i, 1), f"{i}b")
        fetch(4 * i + 3, xs)
        xs, sv['ple'] = ple_fwd(xs, ng_full[i, 3], p[i, 0], full[('ple_w_gate', (i,))], full[('ple_w_proj', (i,))], f"{i}")
        saved.append(sv)

    sq, dx, dx_bf, dgf = loss_head(xs, gf, loss_target.reshape(L, D))
    loss = lax.psum(0.5 * jnp.sum(sq) / D, ("x", "y", "c"))
    dng = [[None] * 4 for _ in range(depth)]
    gmat = {}
    small = {}
    gshard = {}
    reducing = []

    def reduce_block(bi, token):
        bname, mats = blocks[bi]
        handle, token = reduce_start(f"reduce_{bname}", [gmat[key] for key in mats], [BIG[n] for n, _ in mats], sc, token)
        if reducing:
            prev_handle, prev_mats = reducing.pop()
            gshard.update(dict(zip(prev_mats, reduce_finish(prev_handle, token))))
        reducing.append((handle, mats))
        return token

    rtoken = sc
    for i in reversed(range(depth)):
        sv = saved[i]
        dx, dx_bf, dng[i][3], gmat[('ple_w_gate', (i,))], gmat[('ple_w_proj', (i,))] = ple_bwd(
            sv['ple'], ng_full[i, 3], p[i, 0], full[('ple_w_gate', (i,))], dx, f"{i}")
        rtoken = reduce_block(4 * i + 3, rtoken)
        dx, dx_bf, dng[i][2], gmat[('ffn_w_gate', (i, 1))], gmat[('ffn_w_up', (i, 1))], gmat[('ffn_w_down', (i, 1))] = \
            ffn_bwd(sv['ffn_b'], ng_full[i, 2], *ffn_w(i, 1), dx, dx_bf, f"{i}b")
        rtoken = reduce_block(4 * i + 2, rtoken)
        j = (i // 2,)
        if i % 2 == 0:
            dx, dx_bf, dng[i][1], gmat[('gmlp_w_in', j)], dlg, dlb, dws, dbs, gmat[('gmlp_w_out', j)] = gmlp_bwd(
                sv['mix'], ng_full[i, 1], full[('gmlp_w_in', j)], gmlp_ln_g, gmlp_ln_b, w_s, b_s,
                full[('gmlp_w_out', j)], dx, dx_bf)
            small.update(gmlp_ln_g=dlg, gmlp_ln_b=dlb, gmlp_w_s=dws, gmlp_b_s=dbs)
        else:
            dx, dx_bf, dng[i][1], gmat[('s5_w_in', j)], dlr, dli, dldt, db_re, db_im, dc_re, dc_im, dd, \
                gmat[('s5_w_out', j)] = s5_bwd(sv['mix'], ng_full[i, 1], full[('s5_w_in', j)], lam_re, lam_im, log_dt,
                                               b_re, sd_full, full[('s5_w_out', j)], dx)
            small.update(s5_lam_re=dlr, s5_lam_im=dli, s5_log_dt=dldt, s5_b_re=db_re, s5_b_im=db_im,
                         s5_c_re=dc_re, s5_c_im=dc_im, s5_d=dd)
        rtoken = reduce_block(4 * i + 1, rtoken)
        dx, dx_bf, dng[i][0], gmat[('ffn_w_gate', (i, 0))], gmat[('ffn_w_up', (i, 0))], gmat[('ffn_w_down', (i, 0))] = \
            ffn_bwd(sv['ffn_a'], ng_full[i, 0], *ffn_w(i, 0), dx, dx_bf, f"{i}a")
        rtoken = reduce_block(4 * i, rtoken)
    grad_x = dx.reshape(x.shape)
    small['norm_g'] = jnp.stack([jnp.stack(r) for r in dng])
    small['final_norm_g'] = dgf

    grads, deltas, new_m, new_v = {}, {}, {}, {}
    small_names = [n for n in WEIGHT_NAMES if n not in BIG]
    flat = jnp.concatenate([small[n].astype(F32).reshape(-1) for n in small_names])
    pad = (-flat.size) % (256 * 128)
    flat = jnp.pad(flat, (0, pad)).reshape(-1, 128)
    tot = allreduce_small(flat).reshape(-1)
    off = 0
    for n in small_names:
        sz = small[n].size
        gsum = tot[off:off + sz]
        off += sz
        if n == 'norm_g':
            gsum = lax.dynamic_slice_in_dim(gsum.reshape(depth, 4, D), s_idx * W[n].shape[-1], W[n].shape[-1], axis=2)
        elif n == 's5_d':
            gsum = lax.dynamic_slice_in_dim(gsum.reshape(1, D), s_idx * W[n].shape[-1], W[n].shape[-1], axis=1)
        grads[n] = gsum.reshape(W[n].shape)

    def update(n):
        if n in BIG:
            grads[n] = jnp.stack([gshard[(n, lead)] for lead in _mats(W[n])]).reshape(W[n].shape)
        deltas[n], new_m[n], new_v[n] = adamw(W[n], grads[n], Mo[n], Vo[n], f"adamw_{n}")

    early = [n for n in WEIGHT_NAMES if not n.startswith('ffn')]
    for n in early:
        update(n)

    last_handle, last_mats = reducing.pop()
    gshard.update(dict(zip(last_mats, reduce_finish(last_handle, deltas[early[-1]]))))
    for n in WEIGHT_NAMES:
        if n.startswith('ffn'):
            update(n)
    return (loss, grad_x, *[grads[n] for n in WEIGHT_NAMES], *[deltas[n] for n in WEIGHT_NAMES],
            *[new_m[n] for n in WEIGHT_NAMES], *[new_v[n] for n in WEIGHT_NAMES])
```

```python
import functools
import math

import jax
import jax.numpy as jnp
from jax import lax
from jax.experimental import pallas as pl
from jax.experimental.pallas import tpu as pltpu

F32 = jnp.float32
BF16 = jnp.bfloat16
MESH_ID = pl.DeviceIdType.MESH

EPS = 1e-6
ADAM_LR = 0.001
ADAM_B1 = 0.9
ADAM_B2 = 0.999
ADAM_EPS = 1e-08
ADAM_WD = 0.01
ADAM_STEP = 10

N_SHARD = 4
V7X_VMEM_LIMIT = 52 * 2 ** 20
SSM_TILE_GROUPS = 16
SCAN_LANES = 1024
GELU_C = math.sqrt(2.0 / math.pi)


def _pcall(body, **kw):
    return pl.pallas_call(body, **kw)


def _params():
    return pltpu.CompilerParams(vmem_limit_bytes=V7X_VMEM_LIMIT)


def _pick(n, cands):
    for c in cands:
        if c <= n and n % c == 0:
            return c
    return n


def _sigmoid(x):
    return 1.0 / (1.0 + jnp.exp(-x))


def _gelu(x):
    return 0.5 * x * (1.0 + jnp.tanh(GELU_C * (x + 0.044715 * x * x * x)))


def _gelu_grad(x):
    t = jnp.tanh(GELU_C * (x + 0.044715 * x * x * x))
    return 0.5 * (1.0 + t) + 0.5 * x * (1.0 - t * t) * GELU_C * (1.0 + 3.0 * 0.044715 * x * x)


def tilemap(fn, ins, outs, *, M, N, tm, tn, name, after=None):
    n_in = len(ins)
    n_dep = 0 if after is None else 1
    grid = (N // tn, M // tm)
    in_specs = []
    for arr, kind, off in ins:
        if kind == 't':
            in_specs.append(pl.BlockSpec((tm, tn), lambda j, i, off=off: (i, j + off)))
        else:
            in_specs.append(pl.BlockSpec((1, tn), lambda j, i, off=off: (0, j + off)))
    out_specs, out_shape = [], []
    for dt, kind in outs:
        if kind == 't':
            out_specs.append(pl.BlockSpec((tm, tn), lambda j, i: (i, j)))
            out_shape.append(jax.ShapeDtypeStruct((M, N), dt))
        else:
            out_specs.append(pl.BlockSpec((1, tn), lambda j, i: (0, j)))
            out_shape.append(jax.ShapeDtypeStruct((1, N), F32))
    in_specs += [pl.BlockSpec(memory_space=pl.ANY)] * n_dep

    def body(*refs):
        vals = fn(*[r[...] for r in refs[:n_in]])
        for (dt, kind), ref, v in zip(outs, refs[n_in + n_dep:], vals):
            if kind == 't':
                ref[...] = v.astype(ref.dtype)
            else:
                @pl.when(pl.program_id(1) == 0)
                def _():
                    ref[...] = jnp.zeros_like(ref)
                ref[...] += v

    res = _pcall(body, name=name, grid=grid, in_specs=in_specs, out_specs=out_specs,
                 out_shape=out_shape, compiler_params=_params())(*[a for a, _, _ in ins], *([after] * n_dep))
    return res


def _as2d(a):
    if a.ndim >= 2 and a.shape[-1] % 128 == 0:
        return a.reshape(-1, a.shape[-1])
    if a.size % 128 == 0:
        return a.reshape(-1, 128)
    return a.reshape(-1, a.shape[-1])


def _row_tile(rows, cols, nbytes=4, budget=1 << 20):
    cands = [c for c in (2048, 1024, 512, 256, 128, 64, 32, 16, 8) if c * cols * nbytes <= budget]
    return _pick(rows, cands) if cands else _pick(rows, (8,))


ROW_TILES = (256, 128, 64, 32, 16)


def _sc_call(body, sc, args, *, grid, in_specs, out_specs, out_shape, name):
    gs = pltpu.PrefetchScalarGridSpec(num_scalar_prefetch=1, grid=grid, in_specs=in_specs, out_specs=out_specs)
    return _pcall(body, name=name, grid_spec=gs, out_shape=out_shape, compiler_params=_params())(sc, *args)


def cast_into_full(w, lead, ax, sc, name):
    R, C = w.shape[-2:]
    tm = _pick(R, ROW_TILES)
    nb = R // tm
    lead = tuple(lead)
    in_spec = pl.BlockSpec((None,) * len(lead) + (tm, C), lambda i, s: lead + (i, 0))
    if ax == 0:
        shape, out_map = (R * N_SHARD, C), (lambda i, s: (i + s[0] * nb, 0))
    else:
        shape, out_map = (R, C * N_SHARD), (lambda i, s: (i, s[0]))

    def body(s_ref, w_ref, o_ref):
        o_ref[...] = w_ref[...].astype(BF16)

    return _sc_call(body, sc, [w], grid=(nb,), in_specs=[in_spec], out_specs=pl.BlockSpec((tm, C), out_map),
                    out_shape=jax.ShapeDtypeStruct(shape, BF16), name=name)


def pairsum(g, a, ax, sc, name):
    hR, hC = a.shape
    tm = _pick(hR, ROW_TILES)
    nb = hR // tm
    g_map = (lambda i, s: (i + s[1] * nb, 0)) if ax == 1 else (lambda i, s: (i, s[1]))
    blk = (tm, hC)

    def body(s_ref, g_ref, a_ref, o_ref):
        o_ref[...] = (g_ref[...].astype(F32) + a_ref[...].astype(F32)).astype(BF16)

    return _sc_call(body, sc, [g, a], grid=(nb,),
                    in_specs=[pl.BlockSpec(blk, g_map), pl.BlockSpec(blk, lambda i, s: (i, 0))],
                    out_specs=pl.BlockSpec(blk, lambda i, s: (i, 0)),
                    out_shape=jax.ShapeDtypeStruct((hR, hC), BF16), name=name)


def shardsum(b, cbuf, ax, sc, name):
    hR, hC = b.shape
    _, pR, pC = cbuf.shape
    tm = _pick(pR, ROW_TILES)
    nb = pR // tm
    if ax == 1:
        b_map, o_map, shape = (lambda i, s: (i, s[0])), (lambda i, s: (i + s[1] * nb, 0)), (2 * pR, pC)
    else:
        b_map, o_map, shape = (lambda i, s: (i + s[0] * nb, 0)), (lambda i, s: (i, s[1])), (pR, 2 * pC)

    def body(s_ref, b_ref, c_ref, o_ref):
        acc = b_ref[...].astype(F32)
        for k in range(N_SHARD - 1):
            acc = acc + c_ref[k].astype(F32)
        o_ref[...] = acc

    return _sc_call(body, sc, [b, cbuf], grid=(nb,),
                    in_specs=[pl.BlockSpec((tm, pC), b_map), pl.BlockSpec((N_SHARD - 1, tm, pC), lambda i, s: (0, i, 0))],
                    out_specs=pl.BlockSpec((tm, pC), o_map), out_shape=jax.ShapeDtypeStruct(shape, F32), name=name)


_DIMS = {'nn': (((1,), (0,)), ((), ())), 'nt': (((1,), (1,)), ((), ())), 'tn': (((0,), (0,)), ((), ()))}


def matmul(pairs, mode, outs, *, epilogue=None, extras=(), tm=512, tn=512, name, after=None):
    a0, b0 = pairs[0]
    if mode == 'nn':
        (M, K), N = a0.shape, b0.shape[1]
    elif mode == 'nt':
        (M, K), N = a0.shape, b0.shape[0]
    else:
        (K, M), N = a0.shape, b0.shape[1]
    tm, tn = _pick(M, (tm, 256, 128)), _pick(N, (tn, 256, 128))
    n_p, n_e = len(pairs), len(extras)
    if mode == 'tn':
        a_spec = pl.BlockSpec((K, tm), lambda i, j: (0, i))
    else:
        a_spec = pl.BlockSpec((tm, K), lambda i, j: (i, 0))
    if mode == 'nt':
        b_spec = pl.BlockSpec((tn, K), lambda i, j: (j, 0))
    else:
        b_spec = pl.BlockSpec((K, tn), lambda i, j: (0, j))
    in_specs, args = [], []
    for a, b in pairs:
        in_specs += [a_spec, b_spec]
        args += [a, b]
    for arr, kind in extras:
        if kind == 't':
            in_specs.append(pl.BlockSpec((tm, tn), lambda i, j: (i, j)))
        else:
            in_specs.append(pl.BlockSpec((1, tn), lambda i, j: (0, j)))
        args.append(arr)
    n_dep = 0 if after is None else 1
    in_specs += [pl.BlockSpec(memory_space=pl.ANY)] * n_dep
    args += [after] * n_dep
    dims = _DIMS[mode]

    def body(*refs):
        accs = [lax.dot_general(refs[2 * p][...].astype(BF16), refs[2 * p + 1][...].astype(BF16), dims,
                                preferred_element_type=F32) for p in range(n_p)]
        ex = [r[...] for r in refs[2 * n_p:2 * n_p + n_e]]
        if epilogue is None:
            acc = accs[0]
            for other in accs[1:]:
                acc = acc + other
            res = (acc,)
        else:
            res = epilogue(accs, ex)
        for ref, v in zip(refs[2 * n_p + n_e + n_dep:], res):
            ref[...] = v.astype(ref.dtype)

    return _pcall(body, name=name, grid=(M // tm, N // tn), in_specs=in_specs,
                  out_specs=[pl.BlockSpec((tm, tn), lambda i, j: (i, j)) for _ in outs],
                  out_shape=[jax.ShapeDtypeStruct((M, N), dt) for dt in outs],
                  compiler_params=_params())(*args)


def comm_call(name, ins, out_shapes, plan, n_local, n_remote, aliases=None, after=None):
    ins = list(ins) + ([] if after is None else [after])
    n_in, n_out = len(ins), len(out_shapes)

    def body(*refs):
        in_refs, out_refs = refs[:n_in], refs[n_in:n_in + n_out]
        lsem, ssem, rsem = refs[n_in + n_out:]
        me = (lax.axis_index("x"), lax.axis_index("y"), lax.axis_index("c"))
        local, remote = plan(me, in_refs, out_refs)
        assert len(local) == n_local and len(remote) == n_remote
        lcs = [pltpu.make_async_copy(s, d, lsem.at[k]) for k, (s, d) in enumerate(local)]
        rcs = [pltpu.make_async_remote_copy(src_ref=s, dst_ref=d, send_sem=ssem.at[k], recv_sem=rsem.at[k],
                                            device_id=peer, device_id_type=MESH_ID)
               for k, (s, d, peer) in enumerate(remote)]
        for cp in rcs:
            cp.start()
        for cp in lcs:
            cp.start()
        for cp in rcs:
            cp.wait()
        for cp in lcs:
            cp.wait()

    any_spec = pl.BlockSpec(memory_space=pl.ANY)
    return _pcall(body, name=name, in_specs=[any_spec] * n_in, out_specs=[any_spec] * n_out,
                  out_shape=list(out_shapes),
                  scratch_shapes=[pltpu.SemaphoreType.DMA((max(n_local, 1),)),
                                  pltpu.SemaphoreType.DMA((max(n_remote, 1),)),
                                  pltpu.SemaphoreType.DMA((max(n_remote, 1),))],
                  input_output_aliases=aliases or {},
                  compiler_params=pltpu.CompilerParams(has_side_effects=True))(*ins)


_HBM_SPEC = pl.BlockSpec(memory_space=pltpu.HBM)
_SEM_SPEC = pl.BlockSpec(memory_space=pltpu.SEMAPHORE)
_DATAFLOW = pltpu.SideEffectType.DATAFLOW_SIDE_EFFECTING


def split_start(name, arrays, n_copies, plan, after):
    n = len(arrays)

    def body(*refs):
        ins, (ssem, rsem) = refs[:n], refs[n + 1:n + 3]
        token = refs[-1]
        me = (lax.axis_index("x"), lax.axis_index("y"), lax.axis_index("c"))
        for k, (src, dst, peer) in enumerate(plan(me, ins)):
            pltpu.make_async_remote_copy(src_ref=src, dst_ref=dst, send_sem=ssem.at[k], recv_sem=rsem.at[k],
                                         device_id=peer, device_id_type=MESH_ID).start()
        token[...] = jnp.zeros_like(token)

    sems = pltpu.SemaphoreType.DMA((n_copies,))
    res = _pcall(body, name=name,
                 out_shape=(sems, sems, *[pltpu.HBM(a.shape, a.dtype) for a in arrays], jax.ShapeDtypeStruct((8, 128), F32)),
                 in_specs=[_HBM_SPEC] * n + [pl.BlockSpec(memory_space=pl.ANY)],
                 out_specs=(_SEM_SPEC, _SEM_SPEC, *[_HBM_SPEC] * n, pl.BlockSpec(memory_space=pltpu.VMEM)),
                 input_output_aliases={m: 2 + m for m in range(n)},
                 compiler_params=pltpu.CompilerParams(has_side_effects=_DATAFLOW))(
        *[pltpu.with_memory_space_constraint(a, pltpu.HBM) for a in arrays], after)
    return res[0], res[1], list(res[2:2 + n]), res[-1]


def split_wait(name, arrays, ssem, rsem, plan, after):
    n = len(arrays)

    def body(*refs):
        ins, (ssem_r, rsem_r) = refs[:n], refs[n:n + 2]
        me = (lax.axis_index("x"), lax.axis_index("y"), lax.axis_index("c"))
        for k, (src, dst, peer) in enumerate(plan(me, ins)):
            cp = pltpu.make_async_remote_copy(src_ref=src, dst_ref=dst, send_sem=ssem_r.at[k], recv_sem=rsem_r.at[k],
                                              device_id=peer, device_id_type=MESH_ID)
            cp.wait_send()
            cp.wait_recv()

    res = _pcall(body, name=name, out_shape=[pltpu.HBM(a.shape, a.dtype) for a in arrays],
                 in_specs=[_HBM_SPEC] * n + [_SEM_SPEC, _SEM_SPEC, pl.BlockSpec(memory_space=pl.ANY)],
                 out_specs=[_HBM_SPEC] * n, input_output_aliases={m: m for m in range(n)},
                 compiler_params=pltpu.CompilerParams(has_side_effects=_DATAFLOW))(*arrays, ssem, rsem, after)
    return list(res)


def _shard_of(me):
    return 2 * me[0] + me[1]


def _plane_peers(me):
    x, y, c = me
    return [((1 - x, y, c), 2 * (1 - x) + y), ((x, 1 - y, c), 2 * x + 1 - y),
            ((1 - x, 1 - y, c), 2 * (1 - x) + 1 - y)]


def _mats(arr):
    out = [()]
    for n in arr.shape[:-2]:
        out = [o + (k,) for o in out for k in range(n)]
    return out


ROW_ALIGN = 16
LANE_ALIGN = 128


def _win(ref, lead, rows, cols):
    idx = tuple(lead)
    for spec, align in ((rows, ROW_ALIGN), (cols, LANE_ALIGN)):
        if spec is None:
            idx += (slice(None),)
        else:
            start, size = spec
            if not isinstance(start, int):
                start = pl.multiple_of(start, align)
            idx += (pl.ds(start, size),)
    return ref.at[idx]


def gather_small(name, shards, after):
    full_shapes = [jax.ShapeDtypeStruct((a.shape[0], a.shape[1] * N_SHARD), a.dtype) for a in shards]
    n = len(shards)

    def plan(me, in_refs, out_refs):
        s = _shard_of(me)
        local, remote = [], []
        for t, a in enumerate(shards):
            dst = _win(out_refs[t], (), None, (s * a.shape[1], a.shape[1]))
            local.append((in_refs[t], dst))
            for peer, _ in _plane_peers(me):
                remote.append((in_refs[t], dst, peer))
        return local, remote

    return comm_call(name, shards, full_shapes, plan, n, 3 * n, after=after)


def reduce_start(name, grads, axes, sc, after):
    n = len(grads)

    def half_win(m, h):
        R, C = grads[m].shape
        if axes[m] == 1:
            return (h * (R // 2), R // 2), None
        return None, (h * (C // 2), C // 2)

    a_shapes = []
    for g, ax in zip(grads, axes):
        R, C = g.shape
        a_shapes.append(jax.ShapeDtypeStruct((R // 2, C) if ax == 1 else (R, C // 2), BF16))

    def plan_a(me, in_refs, out_refs):
        x, y, c = me
        remote = []
        for m in range(n):
            rows, cols = half_win(m, 1 - c)
            remote.append((_win(in_refs[m], (), rows, cols), out_refs[m], (x, y, 1 - c)))
        return [], remote

    a_bufs = comm_call(name + "_pair", grads, a_shapes, plan_a, 0, n)
    b_bufs = [pairsum(g, a, ax, sc, f"{name}_pairsum{m}") for m, (g, a, ax) in enumerate(zip(grads, a_bufs, axes))]

    def piece_shape(m):
        R, C = a_shapes[m].shape
        return (R, C // N_SHARD) if axes[m] == 1 else (R // N_SHARD, C)

    def piece_win(m, s):
        R, C = piece_shape(m)
        if axes[m] == 1:
            return None, (s * C, C)
        return (s * R, R), None

    landing = [lax.empty((N_SHARD - 1,) + piece_shape(m), BF16) for m in range(n)]

    def plan_c(me, refs):
        copies = []
        for m in range(n):
            for j, (peer, ps) in enumerate(_plane_peers(me)):
                rows, cols = piece_win(m, ps)
                copies.append((_win(refs[m], (), rows, cols), refs[n + m].at[j], peer))
        return copies

    ssem, rsem, thru, token = split_start(name + "_ici_start", b_bufs + landing, 3 * n, plan_c, after)
    return (name, axes, sc, plan_c, ssem, rsem, thru), token


def reduce_finish(handle, after):
    name, axes, sc, plan_c, ssem, rsem, thru = handle
    n = len(axes)
    done = split_wait(name + "_ici_wait", thru, ssem, rsem, plan_c, after)
    b_bufs, c_bufs = done[:n], done[n:]
    shards = [shardsum(b, cb, ax, sc, f"{name}_shardsum{m}") for m, (b, cb, ax) in enumerate(zip(b_bufs, c_bufs, axes))]

    def plan_e(me, in_refs, out_refs):
        x, y, c = me
        remote = []
        for m in range(n):
            R, C = shards[m].shape
            if axes[m] == 1:
                rows, cols = (c * (R // 2), R // 2), None
            else:
                rows, cols = None, (c * (C // 2), C // 2)
            remote.append((_win(out_refs[m], (), rows, cols), _win(out_refs[m], (), rows, cols), (x, y, 1 - c)))
        return [], remote

    return comm_call(name + "_swap", shards, [jax.ShapeDtypeStruct(a.shape, a.dtype) for a in shards], plan_e, 0, n,
                     aliases={m: m for m in range(n)})


def gather_start(name, fulls, shard_shapes, axes, after):
    n = len(fulls)

    def win(ref, m, s, half):
        R, C = shard_shapes[m]
        r0 = s * R if axes[m] == 0 else 0
        return _win(ref, (), (r0 + half * (R // 2), R // 2), None if axes[m] == 0 else (s * C, C))

    def plan_ici(me, refs):
        s, c = _shard_of(me), me[2]
        return [(win(refs[m], m, s, c), win(refs[m], m, s, c), peer) for m in range(n) for peer, _ in _plane_peers(me)]

    def plan_fwd(me, in_refs, out_refs):
        x, y, c = me
        return [], [(win(out_refs[m], m, ps, c), win(out_refs[m], m, ps, c), (x, y, 1 - c))
                    for m in range(n) for _, ps in _plane_peers(me)]

    ssem, rsem, thru, token = split_start(name + "_start", fulls, 3 * n, plan_ici, after)
    return (name, plan_ici, plan_fwd, ssem, rsem, thru), token


def gather_finish(handle, after):
    name, plan_ici, plan_fwd, ssem, rsem, thru = handle
    n = len(thru)
    got = split_wait(name + "_wait", thru, ssem, rsem, plan_ici, after)
    return comm_call(name + "_fwd", got, [jax.ShapeDtypeStruct(a.shape, a.dtype) for a in got], plan_fwd, 0, 3 * n,
                     aliases={m: m for m in range(n)})


def allreduce_small(flat, after):
    cur = flat
    R, C = flat.shape
    for axis, flip in enumerate(((0, 0, 1), (0, 1, 0), (1, 0, 0))):
        def plan(me, in_refs, out_refs, flip=flip):
            peer = tuple(v + f * (1 - 2 * v) for v, f in zip(me, flip))
            return [], [(in_refs[0], out_refs[0], peer)]

        (got,) = comm_call(f"small_swap{axis}", [cur], [jax.ShapeDtypeStruct((R, C), F32)], plan, 0, 1,
                           after=after if axis == 0 else None)
        (cur,) = tilemap(lambda a, b: (a + b,), [(cur, 't', 0), (got, 't', 0)], [(F32, 't')], M=R, N=C,
                         tm=_pick(R, (2048, 1024, 512, 256)), tn=C, name=f"small_add{axis}")
    return cur


def rms_fwd(x, g, name):
    M, D = x.shape
    tm = _pick(M, (256, 128))

    def fn(xv, gv):
        r = lax.rsqrt(jnp.mean(xv * xv, axis=-1, keepdims=True) + EPS)
        return (xv * r * gv,)

    (h,) = tilemap(fn, [(x, 't', 0), (g, 'r', 0)], [(BF16, 't')], M=M, N=D, tm=tm, tn=D, name=name)
    return h


def rms_bwd(x, g, dh, dres, name):
    M, D = x.shape
    tm = _pick(M, (256, 128))

    def fn(xv, gv, dhv, drv):
        r = lax.rsqrt(jnp.mean(xv * xv, axis=-1, keepdims=True) + EPS)
        xh = xv * r
        dxh = dhv * gv
        m = jnp.mean(dxh * xh, axis=-1, keepdims=True)
        dx = drv + r * (dxh - xh * m)
        return dx, dx, jnp.sum(dhv * xh, axis=0, keepdims=True)

    return tilemap(fn, [(x, 't', 0), (g, 'r', 0), (dh, 't', 0), (dres, 't', 0)],
                   [(F32, 't'), (BF16, 't'), (F32, 'a')], M=M, N=D, tm=tm, tn=D, name=name)


def ffn_fwd(x, g, wg, wu, wd, tag):
    h = rms_fwd(x, g, f"ffn_norm_{tag}")

    def ep(accs, ex):
        a, b = accs
        return a, b, a * _sigmoid(a) * b

    a, b, s = matmul([(h, wg), (h, wu)], 'nn', [F32, F32, BF16], epilogue=ep, tm=512, tn=512,
                     name=f"ffn_gateup_{tag}")
    (xo,) = matmul([(s, wd)], 'nn', [F32], epilogue=lambda accs, ex: (ex[0] + 0.5 * accs[0],),
                   extras=[(x, 't')], tm=512, tn=512, name=f"ffn_down_{tag}")
    return xo, (x, h, a, b, s)


def ffn_bwd(saved, g, wg, wu, wd, dxo, dxo_bf, tag, after):
    x, h, a, b, s = saved

    def ep(accs, ex):
        ds = 0.5 * accs[0]
        av, bv = ex
        sg = _sigmoid(av)
        return ds * bv * (sg * (1.0 + av * (1.0 - sg))), ds * (av * sg)

    da, db = matmul([(dxo_bf, wd)], 'nt', [BF16, BF16], epilogue=ep, extras=[(a, 't'), (b, 't')],
                    tm=512, tn=512, name=f"ffn_dact_{tag}", after=after)
    (dwd,) = matmul([(s, dxo_bf)], 'tn', [BF16], epilogue=lambda accs, ex: (0.5 * accs[0],),
                    tm=512, tn=512, name=f"ffn_dwd_{tag}")
    dwg, dwu = matmul([(h, da), (h, db)], 'tn', [BF16, BF16], epilogue=lambda accs, ex: tuple(accs),
                      tm=512, tn=512, name=f"ffn_dwgu_{tag}")
    (dh,) = matmul([(da, wg), (db, wu)], 'nt', [F32], tm=512, tn=256, name=f"ffn_dh_{tag}")
    dx, dx_bf, dg = rms_bwd(x, g, dh, dxo, f"ffn_dnorm_{tag}")
    return dx, dx_bf, dg, dwg, dwu, dwd


def _tril_mask(n):
    return lax.broadcasted_iota(jnp.int32, (n, n), 0) >= lax.broadcasted_iota(jnp.int32, (n, n), 1)


def _gmlp_specs(L, half, n_grp, chunk):
    gd = half // n_grp
    specs = [pl.BlockSpec((chunk, gd), lambda g, n: (n, g)),
             pl.BlockSpec((chunk, gd), lambda g, n: (n, n_grp + g)),
             pl.BlockSpec((1, gd), lambda g, n: (0, g)),
             pl.BlockSpec((1, gd), lambda g, n: (0, g)),
             pl.BlockSpec((None, chunk, chunk), lambda g, n: (g, 0, 0)),
             pl.BlockSpec((None, chunk, 1), lambda g, n: (g, 0, 0))]
    return gd, specs


def _gmlp_gate_values(zu, zv, lg, lb, ws, bs):
    u, v = _gelu(zu), _gelu(zv)
    mu = jnp.mean(v, axis=-1, keepdims=True)
    d = v - mu
    rstd = lax.rsqrt(jnp.mean(d * d, axis=-1, keepdims=True) + EPS)
    vhat = d * rstd
    vn = vhat * lg + lb
    w = jnp.where(_tril_mask(ws.shape[0]), ws, 0.0).astype(BF16)
    sv = jnp.dot(w, vn.astype(BF16), preferred_element_type=F32) + bs
    return u, vhat, rstd, vn, w, sv


def gmlp_gate_fwd(zpre, ln_g, ln_b, w_s, b_s):
    L, half = zpre.shape[0], zpre.shape[1] // 2
    n_grp, chunk = w_s.shape[0], w_s.shape[1]
    gd, specs = _gmlp_specs(L, half, n_grp, chunk)

    def body(zu, zv, lg, lb, ws, bs, o):
        u, _, _, _, _, sv = _gmlp_gate_values(zu[...], zv[...], lg[...], lb[...], ws[...], bs[...])
        o[...] = (u * sv).astype(o.dtype)

    return _pcall(body, name="gmlp_gate", grid=(n_grp, L // chunk), in_specs=specs,
                  out_specs=pl.BlockSpec((chunk, gd), lambda g, n: (n, g)),
                  out_shape=jax.ShapeDtypeStruct((L, half), BF16), compiler_params=_params())(
        zpre, zpre, ln_g, ln_b, w_s, b_s)


def gmlp_gate_bwd(zpre, ln_g, ln_b, w_s, b_s, dgated):
    L, half = zpre.shape[0], zpre.shape[1] // 2
    n_grp, chunk = w_s.shape[0], w_s.shape[1]
    gd, specs = _gmlp_specs(L, half, n_grp, chunk)
    specs = specs + [pl.BlockSpec((chunk, gd), lambda g, n: (n, g))]

    def body(zu, zv, lg, lb, ws, bs, dg, dzu, dzv, dws, dbs, dlg, dlb):
        zuv, zvv, lgv = zu[...], zv[...], lg[...]
        u, vhat, rstd, vn, w, sv = _gmlp_gate_values(zuv, zvv, lgv, lb[...], ws[...], bs[...])
        dgv = dg[...]
        du = dgv * sv
        dsv = dgv * u
        dsv_bf = dsv.astype(BF16)
        dw = lax.dot_general(dsv_bf, vn.astype(BF16), _DIMS['nt'], preferred_element_type=F32)
        dvn = lax.dot_general(w, dsv_bf, _DIMS['tn'], preferred_element_type=F32)
        dvhat = dvn * lgv
        dv = rstd * (dvhat - jnp.mean(dvhat, axis=-1, keepdims=True)
                     - vhat * jnp.mean(dvhat * vhat, axis=-1, keepdims=True))
        dzu[...] = (du * _gelu_grad(zuv)).astype(dzu.dtype)
        dzv[...] = (dv * _gelu_grad(zvv)).astype(dzv.dtype)

        @pl.when(pl.program_id(1) == 0)
        def _():
            dws[...] = jnp.zeros_like(dws)
            dbs[...] = jnp.zeros_like(dbs)
            dlg[...] = jnp.zeros_like(dlg)
            dlb[...] = jnp.zeros_like(dlb)

        dws[...] += jnp.where(_tril_mask(chunk), dw, 0.0)
        dbs[...] += jnp.sum(dsv, axis=1, keepdims=True)
        dlg[...] += jnp.sum(dvn * vhat, axis=0, keepdims=True)
        dlb[...] += jnp.sum(dvn, axis=0, keepdims=True)

    tile = pl.BlockSpec((chunk, gd), lambda g, n: (n, g))
    vec = pl.BlockSpec((1, gd), lambda g, n: (0, g))
    return _pcall(body, name="gmlp_gate_bwd", grid=(n_grp, L // chunk), in_specs=specs,
                  out_specs=[tile, tile, pl.BlockSpec((None, chunk, chunk), lambda g, n: (g, 0, 0)),
                             pl.BlockSpec((None, chunk, 1), lambda g, n: (g, 0, 0)), vec, vec],
                  out_shape=[jax.ShapeDtypeStruct((L, half), BF16), jax.ShapeDtypeStruct((L, half), BF16),
                             jax.ShapeDtypeStruct((n_grp, chunk, chunk), F32),
                             jax.ShapeDtypeStruct((n_grp, chunk, 1), F32),
                             jax.ShapeDtypeStruct((1, half), F32), jax.ShapeDtypeStruct((1, half), F32)],
                  compiler_params=_params())(zpre, zpre, ln_g, ln_b, w_s, b_s, dgated)


def gmlp_fwd(x, g, w_in, ln_g, ln_b, w_s, b_s, w_out):
    h = rms_fwd(x, g, "gmlp_norm")
    (zpre,) = matmul([(h, w_in)], 'nn', [F32], tm=512, tn=512, name="gmlp_in")
    gated = gmlp_gate_fwd(zpre, ln_g, ln_b, w_s, b_s)
    (xo,) = matmul([(gated, w_out)], 'nn', [F32], epilogue=lambda accs, ex: (ex[0] + accs[0],),
                   extras=[(x, 't')], tm=512, tn=512, name="gmlp_out")
    return xo, (x, h, zpre, gated)


def gmlp_bwd(saved, g, w_in, ln_g, ln_b, w_s, b_s, w_out, dxo, dxo_bf, after):
    x, h, zpre, gated = saved
    (dgated,) = matmul([(dxo_bf, w_out)], 'nt', [F32], tm=512, tn=512, name="gmlp_dgated", after=after)
    (dw_out,) = matmul([(gated, dxo_bf)], 'tn', [BF16], tm=512, tn=512, name="gmlp_dwout")
    dzu, dzv, dws, dbs, dlg, dlb = gmlp_gate_bwd(zpre, ln_g, ln_b, w_s, b_s, dgated)
    dz = jnp.concatenate([dzu, dzv], axis=1)
    (dw_in,) = matmul([(h, dz)], 'tn', [BF16], tm=512, tn=512, name="gmlp_dwin")
    (dh,) = matmul([(dz, w_in)], 'nt', [F32], tm=256, tn=256, name="gmlp_dh")
    dx, dx_bf, dg = rms_bwd(x, g, dh, dxo, "gmlp_dnorm")
    return dx, dx_bf, dg, dw_in, dlg, dlb, dws, dbs, dw_out


def _s5_disc(lr, li, ldt, br, bi):
    dt = jnp.exp(ldt)
    mag = jnp.exp(lr * dt)
    ang = li * dt
    ar = mag * jnp.cos(ang)
    ai = mag * jnp.sin(ang)
    den = lr * lr + li * li
    nr = ar - 1.0
    zr = (nr * lr + ai * li) / den
    zi = (ai * lr - nr * li) / den
    return ar, ai, zr[None] * br - zi[None] * bi, zr[None] * bi + zi[None] * br


def s5_disc_fwd(lr, li, ldt, br, bi):
    def body(lr_r, li_r, ldt_r, br_r, bi_r, ar_o, ai_o, bbr_o, bbi_o):
        res = _s5_disc(lr_r[...], li_r[...], ldt_r[...], br_r[...], bi_r[...])
        for o, v in zip((ar_o, ai_o, bbr_o, bbi_o), res):
            o[...] = v

    shp = lambda a: jax.ShapeDtypeStruct(a.shape, F32)
    return _pcall(body, name="s5_disc", out_shape=[shp(lr), shp(lr), shp(br), shp(br)],
                  compiler_params=_params())(lr, li, ldt, br, bi)


def s5_disc_bwd(lr, li, ldt, br, bi, dar, dai, dbbr, dbbi):
    def body(lr_r, li_r, ldt_r, br_r, bi_r, dar_r, dai_r, dbbr_r, dbbi_r, o1, o2, o3, o4, o5):
        _, vjp = jax.vjp(_s5_disc, lr_r[...], li_r[...], ldt_r[...], br_r[...], bi_r[...])
        res = vjp((dar_r[...], dai_r[...], dbbr_r[...], dbbi_r[...]))
        for o, v in zip((o1, o2, o3, o4, o5), res):
            o[...] = v

    shp = lambda a: jax.ShapeDtypeStruct(a.shape, F32)
    return _pcall(body, name="s5_disc_bwd", out_shape=[shp(lr), shp(lr), shp(ldt), shp(br), shp(br)],
                  compiler_params=_params())(lr, li, ldt, br, bi, dar, dai, dbbr, dbbi)


def blockdiag_matmul(pairs, outs, *, epilogue=None, extras=(), name):
    a0, b0 = pairs[0]
    M = a0.shape[0]
    T, wa, wo = b0.shape
    tm = _pick(M, (512, 256, 128))
    n_p, n_e = len(pairs), len(extras)
    in_specs, args = [], []
    for a, b in pairs:
        in_specs += [pl.BlockSpec((tm, wa), lambda k, i: (i, k)), pl.BlockSpec((None, wa, wo), lambda k, i: (k, 0, 0))]
        args += [a, b]
    for arr, kind in extras:
        in_specs.append(pl.BlockSpec((tm, wo), lambda k, i: (i, k)) if kind == 't'
                        else pl.BlockSpec((1, wo), lambda k, i: (0, k)))
        args.append(arr)

    def body(*refs):
        accs = [jnp.dot(refs[2 * p][...].astype(BF16), refs[2 * p + 1][...], preferred_element_type=F32)
                for p in range(n_p)]
        ex = [r[...] for r in refs[2 * n_p:2 * n_p + n_e]]
        res = tuple(accs) if epilogue is None else epilogue(accs, ex)
        for ref, v in zip(refs[2 * n_p + n_e:], res):
            ref[...] = v.astype(ref.dtype)

    return _pcall(body, name=name, grid=(T, M // tm), in_specs=in_specs,
                  out_specs=[pl.BlockSpec((tm, wo), lambda k, i: (i, k)) for _ in outs],
                  out_shape=[jax.ShapeDtypeStruct((M, T * wo), dt) for dt in outs],
                  compiler_params=_params())(*args)


def blockdiag_outer(pairs, name):
    M = pairs[0][0].shape[0]
    n_p = len(pairs)
    shapes = []
    in_specs, args = [], []
    tm = _pick(M, (512, 256, 128))
    T = None
    for a, b, wa, wb in pairs:
        T = a.shape[1] // wa
        in_specs += [pl.BlockSpec((tm, wa), lambda k, i: (i, k)), pl.BlockSpec((tm, wb), lambda k, i: (i, k))]
        args += [a, b]
        shapes.append((T, wa, wb))

    def body(*refs):
        @pl.when(pl.program_id(1) == 0)
        def _():
            for o in refs[2 * n_p:]:
                o[...] = jnp.zeros_like(o)
        for p in range(n_p):
            refs[2 * n_p + p][...] += lax.dot_general(refs[2 * p][...].astype(BF16), refs[2 * p + 1][...].astype(BF16),
                                                      _DIMS['tn'], preferred_element_type=F32)

    return _pcall(body, name=name, grid=(T, M // tm), in_specs=in_specs,
                  out_specs=[pl.BlockSpec((None, s[1], s[2]), lambda k, i: (k, 0, 0)) for s in shapes],
                  out_shape=[jax.ShapeDtypeStruct(s, F32) for s in shapes], compiler_params=_params())(*args)


def s5_scan(br, bi, ar, ai, reverse, name):
    L, S = br.shape
    ln = _pick(S, (SCAN_LANES, 512, 256, 128))
    tb = _pick(L, (512, 256, 128))
    n_t = L // tb
    n_q = tb // 8

    def tmap(j, t):
        return ((n_t - 1 - t) if reverse else t, j)

    blk = pl.BlockSpec((tb, ln), tmap)
    vec = pl.BlockSpec((1, ln), lambda j, t: (0, j))

    def cmul(xr, xi, yr, yi):
        return xr * yr - xi * yi, xr * yi + xi * yr

    def body(br_r, bi_r, ar_r, ai_r, hr_o, hi_o, pr_o, pi_o, cr_s, ci_s):
        @pl.when(pl.program_id(1) == 0)
        def _():
            cr_s[...] = jnp.zeros_like(cr_s)
            ci_s[...] = jnp.zeros_like(ci_s)

        a1r, a1i = ar_r[...], ai_r[...]
        a2r, a2i = cmul(a1r, a1i, a1r, a1i)
        a4r, a4i = cmul(a2r, a2i, a2r, a2i)
        row = lax.broadcasted_iota(jnp.int32, (8, ln), 0)
        dist = (7 - row) if reverse else row
        pwr, pwi = jnp.broadcast_to(a1r, (8, ln)), jnp.broadcast_to(a1i, (8, ln))
        for bit, (er, ei) in ((1, (a1r, a1i)), (2, (a2r, a2i)), (4, (a4r, a4i))):
            nr, ni = cmul(pwr, pwi, er, ei)
            sel = (dist & bit) != 0
            pwr, pwi = jnp.where(sel, nr, pwr), jnp.where(sel, ni, pwi)
        last = 0 if reverse else 7

        def shift(v, d):
            r = pltpu.roll(v, (8 - d) if reverse else d, 0)
            return jnp.where(dist >= d, r, 0.0)

        def step(q, carry):
            cr, ci = carry
            qq = (n_q - 1 - q) if reverse else q
            rows = pl.ds(pl.multiple_of(qq * 8, 8), 8)
            xr, xi = br_r[rows, :], bi_r[rows, :]
            for d, (er, ei) in ((1, (a1r, a1i)), (2, (a2r, a2i)), (4, (a4r, a4i))):
                sr, si = shift(xr, d), shift(xi, d)
                mr, mi = cmul(sr, si, er, ei)
                xr, xi = xr + mr, xi + mi
            kr, ki = cmul(pwr, pwi, cr, ci)
            xr, xi = xr + kr, xi + ki
            hr_o[rows, :] = xr
            hi_o[rows, :] = xi
            pr_o[rows, :] = jnp.where(dist >= 1, pltpu.roll(xr, 7 if reverse else 1, 0), cr)
            pi_o[rows, :] = jnp.where(dist >= 1, pltpu.roll(xi, 7 if reverse else 1, 0), ci)
            ncr = jnp.sum(jnp.where(row == last, xr, 0.0), axis=0, keepdims=True)
            nci = jnp.sum(jnp.where(row == last, xi, 0.0), axis=0, keepdims=True)
            return ncr, nci

        cr, ci = lax.fori_loop(0, n_q, step, (cr_s[...], ci_s[...]))
        cr_s[...] = cr
        ci_s[...] = ci

    shp = jax.ShapeDtypeStruct((L, S), F32)
    return _pcall(body, name=name, grid=(S // ln, n_t), in_specs=[blk, blk, vec, vec],
                  out_specs=[blk, blk, blk, blk], out_shape=[shp, shp, shp, shp],
                  scratch_shapes=[pltpu.VMEM((1, ln), F32), pltpu.VMEM((1, ln), F32)],
                  compiler_params=_params())(br, bi, ar, ai)


def _to_blockdiag(m, tile_groups):
    G, A, B = m.shape
    T = G // tile_groups
    eye = jnp.eye(tile_groups, dtype=m.dtype)
    t = m.reshape(T, tile_groups, A, 1, B) * eye[None, :, None, :, None]
    return t.reshape(T, tile_groups * A, tile_groups * B)


def _from_blockdiag(t, tile_groups):
    T, RA, RB = t.shape
    A, B = RA // tile_groups, RB // tile_groups
    d = jnp.diagonal(t.reshape(T, tile_groups, A, tile_groups, B), axis1=1, axis2=3)
    return jnp.moveaxis(d, 3, 1).reshape(T * tile_groups, A, B)


def s5_fwd(x, g, w_in, lam_re, lam_im, log_dt, b_re, b_im, c_re, c_im, d_skip, w_out):
    G, P, H = b_re.shape
    tg = min(SSM_TILE_GROUPS, G)
    h = rms_fwd(x, g, "s5_norm")
    (u,) = matmul([(h, w_in)], 'nn', [F32], tm=512, tn=512, name="s5_in")
    br_t, bi_t = jnp.transpose(b_re, (2, 0, 1)), jnp.transpose(b_im, (2, 0, 1))
    ar, ai, bbr, bbi = s5_disc_fwd(lam_re, lam_im, log_dt, br_t, bi_t)
    bbr_g, bbi_g = jnp.transpose(bbr, (1, 0, 2)), jnp.transpose(bbi, (1, 0, 2))
    bd_br, bd_bi = _to_blockdiag(bbr_g.astype(BF16), tg), _to_blockdiag(bbi_g.astype(BF16), tg)
    bur, bui = blockdiag_matmul([(u, bd_br), (u, bd_bi)], [F32, F32], name="s5_bu")
    a_r, a_i = ar.reshape(1, G * P), ai.reshape(1, G * P)
    hr, hi, hpr, hpi = s5_scan(bur, bui, a_r, a_i, False, "s5_scan")
    c_pg_r = jnp.transpose(c_re, (0, 2, 1)).astype(BF16)
    c_pg_i = jnp.transpose(c_im, (0, 2, 1)).astype(BF16)
    bd_cr, bd_nci = _to_blockdiag(c_pg_r, tg), _to_blockdiag(-c_pg_i, tg)

    def ep(accs, ex):
        y = accs[0] + accs[1] + ex[1] * ex[0]
        return y, _gelu(y)

    y, act = blockdiag_matmul([(hr, bd_cr), (hi, bd_nci)], [F32, BF16], epilogue=ep,
                              extras=[(u, 't'), (d_skip, 'r')], name="s5_y")
    (o,) = matmul([(act, w_out)], 'nn', [F32], tm=512, tn=512, name="s5_out")
    M, D = x.shape
    tm = _pick(M, (256, 128))
    (xo,) = tilemap(lambda xv, val, gt: (xv + val * _sigmoid(gt),), [(x, 't', 0), (o, 't', 0), (o, 't', 1)],
                    [(F32, 't')], M=M, N=D, tm=tm, tn=D, name="s5_glu")
    saved = (x, h, u, hr, hi, hpr, hpi, y, act, o, a_r, a_i, bd_br, bd_bi, bd_cr, bd_nci, br_t, bi_t)
    return xo, saved


def s5_bwd(saved, g, w_in, lam_re, lam_im, log_dt, b_re, d_skip, w_out, dxo, after):
    x, h, u, hr, hi, hpr, hpi, y, act, o, a_r, a_i, bd_br, bd_bi, bd_cr, bd_nci, br_t, bi_t = saved
    G, P, H = b_re.shape
    tg = min(SSM_TILE_GROUPS, G)
    M, D = x.shape
    tm = _pick(M, (256, 128))

    def glu_bwd(dv, val, gt):
        sg = _sigmoid(gt)
        return dv * sg, dv * val * sg * (1.0 - sg)

    dval, dgate = tilemap(glu_bwd, [(dxo, 't', 0), (o, 't', 0), (o, 't', 1)], [(BF16, 't'), (BF16, 't')],
                          M=M, N=D, tm=tm, tn=D, name="s5_dglu", after=after)
    do = jnp.concatenate([dval, dgate], axis=1)
    (dw_out,) = matmul([(act, do)], 'tn', [BF16], tm=512, tn=512, name="s5_dwout")
    (dact,) = matmul([(do, w_out)], 'nt', [F32], tm=512, tn=512, name="s5_dact")
    dy, dd = tilemap(lambda da, yv, uv: (da * _gelu_grad(yv), jnp.sum(da * _gelu_grad(yv) * uv, axis=0, keepdims=True)),
                     [(dact, 't', 0), (y, 't', 0), (u, 't', 0)], [(F32, 't'), (F32, 'a')],
                     M=M, N=D, tm=tm, tn=D, name="s5_dy")
    bd_crT, bd_nciT = jnp.transpose(bd_cr, (0, 2, 1)), jnp.transpose(bd_nci, (0, 2, 1))
    dhr, dhi = blockdiag_matmul([(dy, bd_crT), (dy, bd_nciT)], [F32, F32], name="s5_dh")
    gr, gi, _, _ = s5_scan(dhr, dhi, a_r, -a_i, True, "s5_scan_rev")
    S = G * P
    tms = _pick(M, (128,))

    def da_fn(grv, giv, hprv, hpiv):
        return (jnp.sum(grv * hprv + giv * hpiv, axis=0, keepdims=True),
                jnp.sum(giv * hprv - grv * hpiv, axis=0, keepdims=True))

    dar, dai = tilemap(da_fn, [(gr, 't', 0), (gi, 't', 0), (hpr, 't', 0), (hpi, 't', 0)], [(F32, 'a'), (F32, 'a')],
                       M=M, N=S, tm=tms, tn=_pick(S, (2048, 1024, 512)), name="s5_dabar")
    wa, wb = tg * H, tg * P
    xc_r, xc_i, xb_r, xb_i = blockdiag_outer([(dy, hr, wa, wb), (dy, hi, wa, wb), (u, gr, wa, wb), (u, gi, wa, wb)],
                                             "s5_dcb")
    dc_re = _from_blockdiag(xc_r, tg)
    dc_im = -_from_blockdiag(xc_i, tg)
    dbb_r = jnp.transpose(_from_blockdiag(xb_r, tg), (1, 0, 2))
    dbb_i = jnp.transpose(_from_blockdiag(xb_i, tg), (1, 0, 2))
    dlr, dli, dldt, dbr_t, dbi_t = s5_disc_bwd(lam_re, lam_im, log_dt, br_t, bi_t,
                                               dar.reshape(G, P), dai.reshape(G, P), dbb_r, dbb_i)
    db_re, db_im = jnp.transpose(dbr_t, (1, 2, 0)), jnp.transpose(dbi_t, (1, 2, 0))
    bd_brT, bd_biT = jnp.transpose(bd_br, (0, 2, 1)), jnp.transpose(bd_bi, (0, 2, 1))
    (du,) = blockdiag_matmul([(gr, bd_brT), (gi, bd_biT)], [BF16],
                             epilogue=lambda accs, ex: (accs[0] + accs[1] + ex[1] * ex[0],),
                             extras=[(dy, 't'), (d_skip, 'r')], name="s5_du")
    (dw_in,) = matmul([(h, du)], 'tn', [BF16], tm=512, tn=512, name="s5_dwin")
    (dh,) = matmul([(du, w_in)], 'nt', [F32], tm=512, tn=512, name="s5_dhin")
    dx, dx_bf, dg = rms_bwd(x, g, dh, dxo, "s5_dnorm")
    return dx, dx_bf, dg, dw_in, dlr, dli, dldt, db_re, db_im, dc_re, dc_im, dd, dw_out


def ple_fwd(x, g, p_emb, w_gate, w_proj, tag):
    h = rms_fwd(x, g, f"ple_norm_{tag}")
    (q,) = matmul([(p_emb, w_proj)], 'nn', [F32], tm=512, tn=512, name=f"ple_proj_{tag}")

    def ep(accs, ex):
        gt = _sigmoid(accs[0])
        return ex[0] + gt * ex[1], gt

    xo, gate = matmul([(h, w_gate)], 'nn', [F32, F32], epilogue=ep, extras=[(x, 't'), (q, 't')],
                      tm=512, tn=512, name=f"ple_gate_{tag}")
    return xo, (x, h, q, gate)


def ple_bwd(saved, g, p_emb, w_gate, dxo, tag, after):
    x, h, q, gate = saved
    M, D = x.shape
    tm = _pick(M, (256, 128))
    dq, dpre = tilemap(lambda dv, qv, gv: (dv * gv, dv * qv * gv * (1.0 - gv)),
                       [(dxo, 't', 0), (q, 't', 0), (gate, 't', 0)], [(BF16, 't'), (BF16, 't')],
                       M=M, N=D, tm=tm, tn=D, name=f"ple_dgate_{tag}", after=after)
    (dw_proj,) = matmul([(p_emb, dq)], 'tn', [BF16], tm=256, tn=512, name=f"ple_dwproj_{tag}")
    (dw_gate,) = matmul([(h, dpre)], 'tn', [BF16], tm=512, tn=512, name=f"ple_dwgate_{tag}")
    (dh,) = matmul([(dpre, w_gate)], 'nt', [F32], tm=512, tn=512, name=f"ple_dh_{tag}")
    dx, dx_bf, dg = rms_bwd(x, g, dh, dxo, f"ple_dnorm_{tag}")
    return dx, dx_bf, dg, dw_gate, dw_proj


def loss_head(x, g, target):
    M, D = x.shape
    tm = _pick(M, (256, 128))

    def fn(xv, gv, tv):
        r = lax.rsqrt(jnp.mean(xv * xv, axis=-1, keepdims=True) + EPS)
        xh = xv * r
        e = xh * gv - tv
        dy = e * (1.0 / D)
        dxh = dy * gv
        m = jnp.mean(dxh * xh, axis=-1, keepdims=True)
        dx = r * (dxh - xh * m)
        return jnp.sum(e * e, axis=0, keepdims=True), dx, dx, jnp.sum(dy * xh, axis=0, keepdims=True)

    return tilemap(fn, [(x, 't', 0), (g, 'r', 0), (target, 't', 0)],
                   [(F32, 'a'), (F32, 't'), (BF16, 't'), (F32, 'a')], M=M, N=D, tm=tm, tn=D, name="loss_head")


def _adamw_math(wv, gv, mv, vv):
    mn = ADAM_B1 * mv + (1.0 - ADAM_B1) * gv
    vn = ADAM_B2 * vv + (1.0 - ADAM_B2) * (gv * gv)
    m_hat = mn / (1.0 - ADAM_B1 ** ADAM_STEP)
    v_hat = vn / (1.0 - ADAM_B2 ** ADAM_STEP)
    return -ADAM_LR * (m_hat / (jnp.sqrt(v_hat) + ADAM_EPS) + ADAM_WD * wv), mn, vn


def adamw_into(w, lead, g, m, v, carry, name):
    R, C = w.shape[-2:]
    lead = tuple(lead)
    tm = _row_tile(R, C)
    blk = pl.BlockSpec((None,) * len(lead) + (tm, C), lambda i: lead + (i, 0))
    n_carry = 0 if carry is None else 4

    def body(*refs):
        w_r, g_r, m_r, v_r = refs[:4]
        g_o, d_o, m_o, v_o = refs[4 + n_carry:]
        gv = g_r[...]
        d, mn, vn = _adamw_math(w_r[...], gv, m_r[...], v_r[...])
        g_o[...] = gv
        d_o[...] = d
        m_o[...] = mn
        v_o[...] = vn

    shp = jax.ShapeDtypeStruct(w.shape, F32)
    return _pcall(body, name=name, grid=(R // tm,),
                  in_specs=[blk, pl.BlockSpec((tm, C), lambda i: (i, 0)), blk, blk] + [pl.BlockSpec(memory_space=pl.ANY)] * n_carry,
                  out_specs=[blk] * 4, out_shape=[shp] * 4,
                  input_output_aliases={4 + k: k for k in range(n_carry)},
                  compiler_params=_params())(w, g, m, v, *(carry or ()))


def adamw(w, g, m, v, name):
    w2, g2, m2, v2 = _as2d(w), g.reshape(_as2d(w).shape), _as2d(m), _as2d(v)
    R, C = w2.shape
    tm = _row_tile(R, C)

    d, mn, vn = tilemap(_adamw_math, [(w2, 't', 0), (g2, 't', 0), (m2, 't', 0), (v2, 't', 0)],
                        [(F32, 't'), (F32, 't'), (F32, 't')], M=R, N=C, tm=tm, tn=C, name=name)
    return d.reshape(w.shape), mn.reshape(w.shape), vn.reshape(w.shape)


WEIGHT_NAMES = ['norm_g', 'final_norm_g', 'ffn_w_gate', 'ffn_w_up', 'ffn_w_down', 'gmlp_w_in', 'gmlp_ln_g',
                'gmlp_ln_b', 'gmlp_w_s', 'gmlp_b_s', 'gmlp_w_out', 's5_w_in', 's5_lam_re', 's5_lam_im',
                's5_log_dt', 's5_b_re', 's5_b_im', 's5_c_re', 's5_c_im', 's5_d', 's5_w_out', 'ple_w_gate',
                'ple_w_proj']
BIG = {'ffn_w_gate': 1, 'ffn_w_up': 1, 'ffn_w_down': 0, 'gmlp_w_in': 1, 'gmlp_w_out': 0, 's5_w_in': 0,
       's5_w_out': 1, 'ple_w_gate': 0, 'ple_w_proj': 1}


def _blocks(depth):
    out = []
    for i in range(depth):
        for k, half in enumerate("ab"):
            ffn = [(n, (i, k)) for n in ('ffn_w_gate', 'ffn_w_up', 'ffn_w_down')]
            if k == 1:
                out.append((f"ffn{i}b", ffn))
                out.append((f"ple{i}", [('ple_w_gate', (i,)), ('ple_w_proj', (i,))]))
            else:
                out.append((f"ffn{i}a", ffn))
                mix = 'gmlp' if i % 2 == 0 else 's5'
                out.append((f"{mix}{i}", [(f'{mix}_w_in', (i // 2,)), (f'{mix}_w_out', (i // 2,))]))
    return out


def kernel(x, p, norm_g, final_norm_g, ffn_w_gate, ffn_w_up, ffn_w_down, gmlp_w_in, gmlp_ln_g, gmlp_ln_b, gmlp_w_s, gmlp_b_s, gmlp_w_out, s5_w_in, s5_lam_re, s5_lam_im, s5_log_dt, s5_b_re, s5_b_im, s5_c_re, s5_c_im, s5_d, s5_w_out, ple_w_gate, ple_w_proj, loss_target, m_norm_g, m_final_norm_g, m_ffn_w_gate, m_ffn_w_up, m_ffn_w_down, m_gmlp_w_in, m_gmlp_ln_g, m_gmlp_ln_b, m_gmlp_w_s, m_gmlp_b_s, m_gmlp_w_out, m_s5_w_in, m_s5_lam_re, m_s5_lam_im, m_s5_log_dt, m_s5_b_re, m_s5_b_im, m_s5_c_re, m_s5_c_im, m_s5_d, m_s5_w_out, m_ple_w_gate, m_ple_w_proj, v_norm_g, v_final_norm_g, v_ffn_w_gate, v_ffn_w_up, v_ffn_w_down, v_gmlp_w_in, v_gmlp_ln_g, v_gmlp_ln_b, v_gmlp_w_s, v_gmlp_b_s, v_gmlp_w_out, v_s5_w_in, v_s5_lam_re, v_s5_lam_im, v_s5_log_dt, v_s5_b_re, v_s5_b_im, v_s5_c_re, v_s5_c_im, v_s5_d, v_s5_w_out, v_ple_w_gate, v_ple_w_proj):
    env = dict(locals())
    W = {n: env[n] for n in WEIGHT_NAMES}
    Mo = {n: env["m_" + n] for n in WEIGHT_NAMES}
    Vo = {n: env["v_" + n] for n in WEIGHT_NAMES}
    depth = norm_g.shape[0]
    L, D = x.shape[1], x.shape[2]
    s_idx = 2 * lax.axis_index("x") + lax.axis_index("y")

    sc = jnp.stack([s_idx, lax.axis_index("c")]).astype(jnp.int32)
    blocks = _blocks(depth)

    ng2 = norm_g.reshape(depth * 4, norm_g.shape[-1])
    ng_full, sd_full = gather_small("gather_small", [ng2, s5_d], sc)
    full = {}
    gathers = []
    token = sd_full
    for bname, mats in blocks:
        casts = [cast_into_full(W[n], lead, BIG[n], sc, f"cast_{bname}_{n}") for n, lead in mats]
        handle, token = gather_start(f"gather_{bname}", casts, [W[n].shape[-2:] for n, _ in mats],
                                     [BIG[n] for n, _ in mats], token)
        gathers.append(handle)
    all_started = token
    ng_full = ng_full.reshape(depth, 4, 1, D)

    def fetch(bi, after):
        full.update(dict(zip(blocks[bi][1], gather_finish(gathers[bi], after))))
    gf = final_norm_g.reshape(1, D)

    G, P, H = s5_b_re.shape[1:]
    n_grp, chunk = gmlp_w_s.shape[1], gmlp_w_s.shape[2]
    lam_re, lam_im = s5_lam_re[0], s5_lam_im[0]
    log_dt = s5_log_dt.reshape(G, 1)
    b_re, b_im, c_re, c_im = s5_b_re[0], s5_b_im[0], s5_c_re[0], s5_c_im[0]
    w_s, b_s = gmlp_w_s[0], gmlp_b_s[0].reshape(n_grp, chunk, 1)
    xs = x.reshape(L, D)
    saved = []
    def ffn_w(i, k):
        return [full[(n, (i, k))] for n in ('ffn_w_gate', 'ffn_w_up', 'ffn_w_down')]

    for i in range(depth):
        sv = {}
        fetch(4 * i, all_started if i == 0 else xs)
        xs, sv['ffn_a'] = ffn_fwd(xs, ng_full[i, 0], *ffn_w(i, 0), f"{i}a")
        j = (i // 2,)
        fetch(4 * i + 1, xs)
        if i % 2 == 0:
            xs, sv['mix'] = gmlp_fwd(xs, ng_full[i, 1], full[('gmlp_w_in', j)], gmlp_ln_g, gmlp_ln_b, w_s, b_s,
                                     full[('gmlp_w_out', j)])
        else:
            xs, sv['mix'] = s5_fwd(xs, ng_full[i, 1], full[('s5_w_in', j)], lam_re, lam_im, log_dt, b_re, b_im,
                                   c_re, c_im, sd_full, full[('s5_w_out', j)])
        fetch(4 * i + 2, xs)
        xs, sv['ffn_b'] = ffn_fwd(xs, ng_full[i, 2], *ffn_w(i, 1), f"{i}b")
        fetch(4 * i + 3, xs)
        xs, sv['ple'] = ple_fwd(xs, ng_full[i, 3], p[i, 0], full[('ple_w_gate', (i,))], full[('ple_w_proj', (i,))], f"{i}")
        saved.append(sv)

    sq, dx, dx_bf, dgf = loss_head(xs, gf, loss_target.reshape(L, D))
    loss = lax.psum(0.5 * jnp.sum(sq) / D, ("x", "y", "c"))
    dng = [[None] * 4 for _ in range(depth)]
    gmat = {}
    small = {}
    gshard = {}
    reducing = []

    def reduce_block(bi, fence):
        bname, mats = blocks[bi]
        handle, fence = reduce_start(f"reduce_{bname}", [gmat[key] for key in mats], [BIG[n] for n, _ in mats], sc, fence)
        if reducing:
            prev_handle, prev_mats = reducing.pop()
            res = reduce_finish(prev_handle, fence)
            gshard.update(dict(zip(prev_mats, res)))
            fence = res[-1]
        reducing.append((handle, mats))
        return fence

    fence = sc
    for i in reversed(range(depth)):
        sv = saved[i]
        dx, dx_bf, dng[i][3], gmat[('ple_w_gate', (i,))], gmat[('ple_w_proj', (i,))] = ple_bwd(
            sv['ple'], ng_full[i, 3], p[i, 0], full[('ple_w_gate', (i,))], dx, f"{i}", fence)
        fence = reduce_block(4 * i + 3, fence)
        dx, dx_bf, dng[i][2], gmat[('ffn_w_gate', (i, 1))], gmat[('ffn_w_up', (i, 1))], gmat[('ffn_w_down', (i, 1))] = \
            ffn_bwd(sv['ffn_b'], ng_full[i, 2], *ffn_w(i, 1), dx, dx_bf, f"{i}b", fence)
        fence = reduce_block(4 * i + 2, fence)
        j = (i // 2,)
        if i % 2 == 0:
            dx, dx_bf, dng[i][1], gmat[('gmlp_w_in', j)], dlg, dlb, dws, dbs, gmat[('gmlp_w_out', j)] = gmlp_bwd(
                sv['mix'], ng_full[i, 1], full[('gmlp_w_in', j)], gmlp_ln_g, gmlp_ln_b, w_s, b_s,
                full[('gmlp_w_out', j)], dx, dx_bf, fence)
            small.update(gmlp_ln_g=dlg, gmlp_ln_b=dlb, gmlp_w_s=dws, gmlp_b_s=dbs)
        else:
            dx, dx_bf, dng[i][1], gmat[('s5_w_in', j)], dlr, dli, dldt, db_re, db_im, dc_re, dc_im, dd, \
                gmat[('s5_w_out', j)] = s5_bwd(sv['mix'], ng_full[i, 1], full[('s5_w_in', j)], lam_re, lam_im, log_dt,
                                               b_re, sd_full, full[('s5_w_out', j)], dx, fence)
            small.update(s5_lam_re=dlr, s5_lam_im=dli, s5_log_dt=dldt, s5_b_re=db_re, s5_b_im=db_im,
                         s5_c_re=dc_re, s5_c_im=dc_im, s5_d=dd)
        fence = reduce_block(4 * i + 1, fence)
        dx, dx_bf, dng[i][0], gmat[('ffn_w_gate', (i, 0))], gmat[('ffn_w_up', (i, 0))], gmat[('ffn_w_down', (i, 0))] = \
            ffn_bwd(sv['ffn_a'], ng_full[i, 0], *ffn_w(i, 0), dx, dx_bf, f"{i}a", fence)
        fence = reduce_block(4 * i, fence)
    grad_x = dx.reshape(x.shape)
    small['norm_g'] = jnp.stack([jnp.stack(r) for r in dng])
    small['final_norm_g'] = dgf

    grads, deltas, new_m, new_v = {}, {}, {}, {}
    carry = {}
    last_mats = reducing[0][1]

    def update_big(key):
        n, lead = key
        carry[n] = adamw_into(W[n], lead, gshard[key], Mo[n], Vo[n], carry.get(n), f"adamw_{n}_{'_'.join(map(str, lead))}")
        return carry[n][1]

    for _, mats in reversed(blocks):
        for key in mats:
            if key not in last_mats:
                fence = update_big(key)
    small_names = [n for n in WEIGHT_NAMES if n not in BIG]
    flat = jnp.concatenate([small[n].astype(F32).reshape(-1) for n in small_names])
    pad = (-flat.size) % (256 * 128)
    flat = jnp.pad(flat, (0, pad)).reshape(-1, 128)
    tot = allreduce_small(flat, fence).reshape(-1)
    off = 0
    for n in small_names:
        sz = small[n].size
        gsum = tot[off:off + sz]
        off += sz
        if n == 'norm_g':
            gsum = lax.dynamic_slice_in_dim(gsum.reshape(depth, 4, D), s_idx * W[n].shape[-1], W[n].shape[-1], axis=2)
        elif n == 's5_d':
            gsum = lax.dynamic_slice_in_dim(gsum.reshape(1, D), s_idx * W[n].shape[-1], W[n].shape[-1], axis=1)
        grads[n] = gsum.reshape(W[n].shape)

        deltas[n], new_m[n], new_v[n] = adamw(W[n], grads[n], Mo[n], Vo[n], f"adamw_{n}")
        fence = deltas[n]

    last_handle, last_mats = reducing.pop()
    gshard.update(dict(zip(last_mats, reduce_finish(last_handle, fence))))
    for key in last_mats:
        update_big(key)
    for n in BIG:
        grads[n], deltas[n], new_m[n], new_v[n] = carry[n]
    return (loss, grad_x, *[grads[n] for n in WEIGHT_NAMES], *[deltas[n] for n in WEIGHT_NAMES],
            *[new_m[n] for n in WEIGHT_NAMES], *[new_v[n] for n in WEIGHT_NAMES])
```

```python
import functools
import math

import jax
import jax.numpy as jnp
from jax import lax
from jax.experimental import pallas as pl
from jax.experimental.pallas import tpu as pltpu

F32 = jnp.float32
BF16 = jnp.bfloat16
MESH_ID = pl.DeviceIdType.MESH

EPS = 1e-6
ADAM_LR = 0.001
ADAM_B1 = 0.9
ADAM_B2 = 0.999
ADAM_EPS = 1e-08
ADAM_WD = 0.01
ADAM_STEP = 10

N_SHARD = 4
V7X_VMEM_LIMIT = 52 * 2 ** 20
SSM_TILE_GROUPS = 16
SCAN_LANES = 1024
GELU_C = math.sqrt(2.0 / math.pi)


def _pcall(body, **kw):
    return pl.pallas_call(body, **kw)


def _params():
    return pltpu.CompilerParams(vmem_limit_bytes=V7X_VMEM_LIMIT)


def _pick(n, cands):
    for c in cands:
        if c <= n and n % c == 0:
            return c
    return n


def _sigmoid(x):
    return 1.0 / (1.0 + jnp.exp(-x))


def _gelu(x):
    return 0.5 * x * (1.0 + jnp.tanh(GELU_C * (x + 0.044715 * x * x * x)))


def _gelu_grad(x):
    t = jnp.tanh(GELU_C * (x + 0.044715 * x * x * x))
    return 0.5 * (1.0 + t) + 0.5 * x * (1.0 - t * t) * GELU_C * (1.0 + 3.0 * 0.044715 * x * x)


def tilemap(fn, ins, outs, *, M, N, tm, tn, name, after=None):
    n_in = len(ins)
    n_dep = 0 if after is None else 1
    grid = (N // tn, M // tm)
    in_specs = []
    for arr, kind, off in ins:
        if kind == 't':
            in_specs.append(pl.BlockSpec((tm, tn), lambda j, i, off=off: (i, j + off)))
        else:
            in_specs.append(pl.BlockSpec((1, tn), lambda j, i, off=off: (0, j + off)))
    out_specs, out_shape = [], []
    for dt, kind in outs:
        if kind == 't':
            out_specs.append(pl.BlockSpec((tm, tn), lambda j, i: (i, j)))
            out_shape.append(jax.ShapeDtypeStruct((M, N), dt))
        else:
            out_specs.append(pl.BlockSpec((1, tn), lambda j, i: (0, j)))
            out_shape.append(jax.ShapeDtypeStruct((1, N), F32))
    in_specs += [pl.BlockSpec(memory_space=pl.ANY)] * n_dep

    def body(*refs):
        vals = fn(*[r[...] for r in refs[:n_in]])
        for (dt, kind), ref, v in zip(outs, refs[n_in + n_dep:], vals):
            if kind == 't':
                ref[...] = v.astype(ref.dtype)
            else:
                @pl.when(pl.program_id(1) == 0)
                def _():
                    ref[...] = jnp.zeros_like(ref)
                ref[...] += v

    res = _pcall(body, name=name, grid=grid, in_specs=in_specs, out_specs=out_specs,
                 out_shape=out_shape, compiler_params=_params())(*[a for a, _, _ in ins], *([after] * n_dep))
    return res


def _as2d(a):
    if a.ndim >= 2 and a.shape[-1] % 128 == 0:
        return a.reshape(-1, a.shape[-1])
    if a.size % 128 == 0:
        return a.reshape(-1, 128)
    return a.reshape(-1, a.shape[-1])


def _row_tile(rows, cols, nbytes=4, budget=1 << 20):
    cands = [c for c in (2048, 1024, 512, 256, 128, 64, 32, 16, 8) if c * cols * nbytes <= budget]
    return _pick(rows, cands) if cands else _pick(rows, (8,))


ROW_TILES = (256, 128, 64, 32, 16)


def _sc_call(body, sc, args, *, grid, in_specs, out_specs, out_shape, name):
    gs = pltpu.PrefetchScalarGridSpec(num_scalar_prefetch=1, grid=grid, in_specs=in_specs, out_specs=out_specs)
    return _pcall(body, name=name, grid_spec=gs, out_shape=out_shape, compiler_params=_params())(sc, *args)


def cast_into_full(w, lead, ax, sc, name, after):
    R, C = w.shape[-2:]
    tm = _pick(R, ROW_TILES)
    nb = R // tm
    lead = tuple(lead)
    in_spec = pl.BlockSpec((None,) * len(lead) + (tm, C), lambda i, s: lead + (i, 0))
    if ax == 0:
        shape, out_map = (R * N_SHARD, C), (lambda i, s: (i + s[0] * nb, 0))
    else:
        shape, out_map = (R, C * N_SHARD), (lambda i, s: (i, s[0]))

    def body(s_ref, w_ref, after_ref, o_ref):
        o_ref[...] = w_ref[...].astype(BF16)

    return _sc_call(body, sc, [w, after], grid=(nb,), in_specs=[in_spec, pl.BlockSpec(memory_space=pl.ANY)],
                    out_specs=pl.BlockSpec((tm, C), out_map), out_shape=jax.ShapeDtypeStruct(shape, BF16), name=name)


def pairsum(g, a, ax, sc, name):
    hR, hC = a.shape
    tm = _pick(hR, ROW_TILES)
    nb = hR // tm
    g_map = (lambda i, s: (i + s[1] * nb, 0)) if ax == 1 else (lambda i, s: (i, s[1]))
    blk = (tm, hC)

    def body(s_ref, g_ref, a_ref, o_ref):
        o_ref[...] = (g_ref[...].astype(F32) + a_ref[...].astype(F32)).astype(BF16)

    return _sc_call(body, sc, [g, a], grid=(nb,),
                    in_specs=[pl.BlockSpec(blk, g_map), pl.BlockSpec(blk, lambda i, s: (i, 0))],
                    out_specs=pl.BlockSpec(blk, lambda i, s: (i, 0)),
                    out_shape=jax.ShapeDtypeStruct((hR, hC), BF16), name=name)


def shardsum(b, cbuf, ax, sc, name):
    hR, hC = b.shape
    _, pR, pC = cbuf.shape
    tm = _pick(pR, ROW_TILES)
    nb = pR // tm
    if ax == 1:
        b_map, o_map, shape = (lambda i, s: (i, s[0])), (lambda i, s: (i + s[1] * nb, 0)), (2 * pR, pC)
    else:
        b_map, o_map, shape = (lambda i, s: (i + s[0] * nb, 0)), (lambda i, s: (i, s[1])), (pR, 2 * pC)

    def body(s_ref, b_ref, c_ref, o_ref):
        acc = b_ref[...].astype(F32)
        for k in range(N_SHARD - 1):
            acc = acc + c_ref[k].astype(F32)
        o_ref[...] = acc

    return _sc_call(body, sc, [b, cbuf], grid=(nb,),
                    in_specs=[pl.BlockSpec((tm, pC), b_map), pl.BlockSpec((N_SHARD - 1, tm, pC), lambda i, s: (0, i, 0))],
                    out_specs=pl.BlockSpec((tm, pC), o_map), out_shape=jax.ShapeDtypeStruct(shape, F32), name=name)


_DIMS = {'nn': (((1,), (0,)), ((), ())), 'nt': (((1,), (1,)), ((), ())), 'tn': (((0,), (0,)), ((), ()))}


def matmul(pairs, mode, outs, *, epilogue=None, extras=(), tm=512, tn=512, name, after=None):
    a0, b0 = pairs[0]
    if mode == 'nn':
        (M, K), N = a0.shape, b0.shape[1]
    elif mode == 'nt':
        (M, K), N = a0.shape, b0.shape[0]
    else:
        (K, M), N = a0.shape, b0.shape[1]
    tm, tn = _pick(M, (tm, 256, 128)), _pick(N, (tn, 256, 128))
    n_p, n_e = len(pairs), len(extras)
    if mode == 'tn':
        a_spec = pl.BlockSpec((K, tm), lambda i, j: (0, i))
    else:
        a_spec = pl.BlockSpec((tm, K), lambda i, j: (i, 0))
    if mode == 'nt':
        b_spec = pl.BlockSpec((tn, K), lambda i, j: (j, 0))
    else:
        b_spec = pl.BlockSpec((K, tn), lambda i, j: (0, j))
    in_specs, args = [], []
    for a, b in pairs:
        in_specs += [a_spec, b_spec]
        args += [a, b]
    for arr, kind in extras:
        if kind == 't':
            in_specs.append(pl.BlockSpec((tm, tn), lambda i, j: (i, j)))
        else:
            in_specs.append(pl.BlockSpec((1, tn), lambda i, j: (0, j)))
        args.append(arr)
    n_dep = 0 if after is None else 1
    in_specs += [pl.BlockSpec(memory_space=pl.ANY)] * n_dep
    args += [after] * n_dep
    dims = _DIMS[mode]

    def body(*refs):
        accs = [lax.dot_general(refs[2 * p][...].astype(BF16), refs[2 * p + 1][...].astype(BF16), dims,
                                preferred_element_type=F32) for p in range(n_p)]
        ex = [r[...] for r in refs[2 * n_p:2 * n_p + n_e]]
        if epilogue is None:
            acc = accs[0]
            for other in accs[1:]:
                acc = acc + other
            res = (acc,)
        else:
            res = epilogue(accs, ex)
        for ref, v in zip(refs[2 * n_p + n_e + n_dep:], res):
            ref[...] = v.astype(ref.dtype)

    return _pcall(body, name=name, grid=(M // tm, N // tn), in_specs=in_specs,
                  out_specs=[pl.BlockSpec((tm, tn), lambda i, j: (i, j)) for _ in outs],
                  out_shape=[jax.ShapeDtypeStruct((M, N), dt) for dt in outs],
                  compiler_params=_params())(*args)


def comm_call(name, ins, out_shapes, plan, n_local, n_remote, aliases=None, after=None):
    ins = list(ins) + ([] if after is None else [after])
    n_in, n_out = len(ins), len(out_shapes)

    def body(*refs):
        in_refs, out_refs = refs[:n_in], refs[n_in:n_in + n_out]
        lsem, ssem, rsem = refs[n_in + n_out:]
        me = (lax.axis_index("x"), lax.axis_index("y"), lax.axis_index("c"))
        local, remote = plan(me, in_refs, out_refs)
        assert len(local) == n_local and len(remote) == n_remote
        lcs = [pltpu.make_async_copy(s, d, lsem.at[k]) for k, (s, d) in enumerate(local)]
        rcs = [pltpu.make_async_remote_copy(src_ref=s, dst_ref=d, send_sem=ssem.at[k], recv_sem=rsem.at[k],
                                            device_id=peer, device_id_type=MESH_ID)
               for k, (s, d, peer) in enumerate(remote)]
        for cp in rcs:
            cp.start()
        for cp in lcs:
            cp.start()
        for cp in rcs:
            cp.wait()
        for cp in lcs:
            cp.wait()

    any_spec = pl.BlockSpec(memory_space=pl.ANY)
    return _pcall(body, name=name, in_specs=[any_spec] * n_in, out_specs=[any_spec] * n_out,
                  out_shape=list(out_shapes),
                  scratch_shapes=[pltpu.SemaphoreType.DMA((max(n_local, 1),)),
                                  pltpu.SemaphoreType.DMA((max(n_remote, 1),)),
                                  pltpu.SemaphoreType.DMA((max(n_remote, 1),))],
                  input_output_aliases=aliases or {},
                  compiler_params=pltpu.CompilerParams(has_side_effects=True))(*ins)


_HBM_SPEC = pl.BlockSpec(memory_space=pltpu.HBM)
_SEM_SPEC = pl.BlockSpec(memory_space=pltpu.SEMAPHORE)
_DATAFLOW = pltpu.SideEffectType.DATAFLOW_SIDE_EFFECTING


def split_start(name, arrays, n_copies, plan, after):
    n = len(arrays)

    def body(*refs):
        ins, (ssem, rsem) = refs[:n], refs[n + 1:n + 3]
        token = refs[-1]
        me = (lax.axis_index("x"), lax.axis_index("y"), lax.axis_index("c"))
        for k, (src, dst, peer) in enumerate(plan(me, ins)):
            pltpu.make_async_remote_copy(src_ref=src, dst_ref=dst, send_sem=ssem.at[k], recv_sem=rsem.at[k],
                                         device_id=peer, device_id_type=MESH_ID).start()
        token[...] = jnp.zeros_like(token)

    sems = pltpu.SemaphoreType.DMA((n_copies,))
    res = _pcall(body, name=name,
                 out_shape=(sems, sems, *[pltpu.HBM(a.shape, a.dtype) for a in arrays], jax.ShapeDtypeStruct((8, 128), F32)),
                 in_specs=[_HBM_SPEC] * n + [pl.BlockSpec(memory_space=pl.ANY)],
                 out_specs=(_SEM_SPEC, _SEM_SPEC, *[_HBM_SPEC] * n, pl.BlockSpec(memory_space=pltpu.VMEM)),
                 input_output_aliases={m: 2 + m for m in range(n)},
                 compiler_params=pltpu.CompilerParams(has_side_effects=_DATAFLOW))(
        *[pltpu.with_memory_space_constraint(a, pltpu.HBM) for a in arrays], after)
    return res[0], res[1], list(res[2:2 + n]), res[-1]


def split_wait(name, arrays, ssem, rsem, plan, after):
    n = len(arrays)

    def body(*refs):
        ins, (ssem_r, rsem_r) = refs[:n], refs[n:n + 2]
        me = (lax.axis_index("x"), lax.axis_index("y"), lax.axis_index("c"))
        for k, (src, dst, peer) in enumerate(plan(me, ins)):
            cp = pltpu.make_async_remote_copy(src_ref=src, dst_ref=dst, send_sem=ssem_r.at[k], recv_sem=rsem_r.at[k],
                                              device_id=peer, device_id_type=MESH_ID)
            cp.wait_send()
            cp.wait_recv()

    res = _pcall(body, name=name, out_shape=[pltpu.HBM(a.shape, a.dtype) for a in arrays],
                 in_specs=[_HBM_SPEC] * n + [_SEM_SPEC, _SEM_SPEC, pl.BlockSpec(memory_space=pl.ANY)],
                 out_specs=[_HBM_SPEC] * n, input_output_aliases={m: m for m in range(n)},
                 compiler_params=pltpu.CompilerParams(has_side_effects=_DATAFLOW))(*arrays, ssem, rsem, after)
    return list(res)


def _shard_of(me):
    return 2 * me[0] + me[1]


def _plane_peers(me):
    x, y, c = me
    return [((1 - x, y, c), 2 * (1 - x) + y), ((x, 1 - y, c), 2 * x + 1 - y),
            ((1 - x, 1 - y, c), 2 * (1 - x) + 1 - y)]


def _mats(arr):
    out = [()]
    for n in arr.shape[:-2]:
        out = [o + (k,) for o in out for k in range(n)]
    return out


ROW_ALIGN = 16
LANE_ALIGN = 128


def _win(ref, lead, rows, cols):
    idx = tuple(lead)
    for spec, align in ((rows, ROW_ALIGN), (cols, LANE_ALIGN)):
        if spec is None:
            idx += (slice(None),)
        else:
            start, size = spec
            if not isinstance(start, int):
                start = pl.multiple_of(start, align)
            idx += (pl.ds(start, size),)
    return ref.at[idx]


def gather_small(name, shards, after):
    full_shapes = [jax.ShapeDtypeStruct((a.shape[0], a.shape[1] * N_SHARD), a.dtype) for a in shards]
    n = len(shards)

    def plan(me, in_refs, out_refs):
        s = _shard_of(me)
        local, remote = [], []
        for t, a in enumerate(shards):
            dst = _win(out_refs[t], (), None, (s * a.shape[1], a.shape[1]))
            local.append((in_refs[t], dst))
            for peer, _ in _plane_peers(me):
                remote.append((in_refs[t], dst, peer))
        return local, remote

    return comm_call(name, shards, full_shapes, plan, n, 3 * n, after=after)


def _half_shape(g, ax):
    R, C = g.shape
    return (R // 2, C) if ax == 1 else (R, C // 2)


def reduce_pair_start(name, grads, axes, after):
    n = len(grads)
    landing = [lax.empty(_half_shape(g, ax), BF16) for g, ax in zip(grads, axes)]

    def plan_a(me, refs):
        x, y, c = me
        copies = []
        for m in range(n):
            R, C = grads[m].shape
            if axes[m] == 1:
                rows, cols = ((1 - c) * (R // 2), R // 2), None
            else:
                rows, cols = None, ((1 - c) * (C // 2), C // 2)
            copies.append((_win(refs[m], (), rows, cols), refs[n + m], (x, y, 1 - c)))
        return copies

    ssem, rsem, thru, token = split_start(name + "_pair_start", list(grads) + landing, n, plan_a, after)
    return (name, axes, plan_a, ssem, rsem, thru), token


def reduce_ici_start(handle, sc, after):
    name, axes, plan_a, ssem, rsem, thru = handle
    n = len(axes)
    done = split_wait(name + "_pair_wait", thru, ssem, rsem, plan_a, after)
    grads, a_bufs = done[:n], done[n:]
    b_bufs = [pairsum(g, a, ax, sc, f"{name}_pairsum{m}") for m, (g, a, ax) in enumerate(zip(grads, a_bufs, axes))]

    def piece_shape(m):
        R, C = a_bufs[m].shape
        return (R, C // N_SHARD) if axes[m] == 1 else (R // N_SHARD, C)

    def piece_win(m, s):
        R, C = piece_shape(m)
        if axes[m] == 1:
            return None, (s * C, C)
        return (s * R, R), None

    landing = [lax.empty((N_SHARD - 1,) + piece_shape(m), BF16) for m in range(n)]

    def plan_c(me, refs):
        copies = []
        for m in range(n):
            for j, (peer, ps) in enumerate(_plane_peers(me)):
                rows, cols = piece_win(m, ps)
                copies.append((_win(refs[m], (), rows, cols), refs[n + m].at[j], peer))
        return copies

    ssem, rsem, thru, token = split_start(name + "_ici_start", b_bufs + landing, 3 * n, plan_c, b_bufs[-1])
    return (name, axes, sc, plan_c, ssem, rsem, thru), token


def reduce_finish(handle, after):
    name, axes, sc, plan_c, ssem, rsem, thru = handle
    n = len(axes)
    done = split_wait(name + "_ici_wait", thru, ssem, rsem, plan_c, after)
    b_bufs, c_bufs = done[:n], done[n:]
    shards = [shardsum(b, cb, ax, sc, f"{name}_shardsum{m}") for m, (b, cb, ax) in enumerate(zip(b_bufs, c_bufs, axes))]

    def plan_e(me, refs):
        x, y, c = me
        copies = []
        for m in range(n):
            R, C = shards[m].shape
            if axes[m] == 1:
                rows, cols = (c * (R // 2), R // 2), None
            else:
                rows, cols = None, (c * (C // 2), C // 2)
            copies.append((_win(refs[m], (), rows, cols), _win(refs[m], (), rows, cols), (x, y, 1 - c)))
        return copies

    ssem, rsem, thru, token = split_start(name + "_swap_start", shards, n, plan_e, shards[-1])
    return (name, plan_e, ssem, rsem, thru), token


def reduce_swap_wait(handle, after):
    name, plan_e, ssem, rsem, thru = handle
    return split_wait(name + "_swap_wait", thru, ssem, rsem, plan_e, after)


def gather_start(name, fulls, shard_shapes, axes, after):
    n = len(fulls)

    def win(ref, m, s, half):
        R, C = shard_shapes[m]
        r0 = s * R if axes[m] == 0 else 0
        return _win(ref, (), (r0 + half * (R // 2), R // 2), None if axes[m] == 0 else (s * C, C))

    def plan_ici(me, refs):
        s, c = _shard_of(me), me[2]
        return [(win(refs[m], m, s, c), win(refs[m], m, s, c), peer) for m in range(n) for peer, _ in _plane_peers(me)]

    def plan_fwd(me, in_refs, out_refs):
        x, y, c = me
        return [], [(win(out_refs[m], m, ps, c), win(out_refs[m], m, ps, c), (x, y, 1 - c))
                    for m in range(n) for _, ps in _plane_peers(me)]

    ssem, rsem, thru, token = split_start(name + "_start", fulls, 3 * n, plan_ici, after)
    return (name, plan_ici, plan_fwd, ssem, rsem, thru), token


def gather_finish(handle, after):
    name, plan_ici, plan_fwd, ssem, rsem, thru = handle
    n = len(thru)
    got = split_wait(name + "_wait", thru, ssem, rsem, plan_ici, after)
    return comm_call(name + "_fwd", got, [jax.ShapeDtypeStruct(a.shape, a.dtype) for a in got], plan_fwd, 0, 3 * n,
                     aliases={m: m for m in range(n)})


def allreduce_small(flat, after):
    cur = flat
    R, C = flat.shape
    for axis, flip in enumerate(((0, 0, 1), (0, 1, 0), (1, 0, 0))):
        def plan(me, in_refs, out_refs, flip=flip):
            peer = tuple(v + f * (1 - 2 * v) for v, f in zip(me, flip))
            return [], [(in_refs[0], out_refs[0], peer)]

        (got,) = comm_call(f"small_swap{axis}", [cur], [jax.ShapeDtypeStruct((R, C), F32)], plan, 0, 1,
                           after=after if axis == 0 else None)
        (cur,) = tilemap(lambda a, b: (a + b,), [(cur, 't', 0), (got, 't', 0)], [(F32, 't')], M=R, N=C,
                         tm=_pick(R, (2048, 1024, 512, 256)), tn=C, name=f"small_add{axis}")
    return cur


def rms_fwd(x, g, name):
    M, D = x.shape
    tm = _pick(M, (256, 128))

    def fn(xv, gv):
        r = lax.rsqrt(jnp.mean(xv * xv, axis=-1, keepdims=True) + EPS)
        return (xv * r * gv,)

    (h,) = tilemap(fn, [(x, 't', 0), (g, 'r', 0)], [(BF16, 't')], M=M, N=D, tm=tm, tn=D, name=name)
    return h


def rms_bwd(x, g, dh, dres, name):
    M, D = x.shape
    tm = _pick(M, (256, 128))

    def fn(xv, gv, dhv, drv):
        r = lax.rsqrt(jnp.mean(xv * xv, axis=-1, keepdims=True) + EPS)
        xh = xv * r
        dxh = dhv * gv
        m = jnp.mean(dxh * xh, axis=-1, keepdims=True)
        dx = drv + r * (dxh - xh * m)
        return dx, dx, jnp.sum(dhv * xh, axis=0, keepdims=True)

    return tilemap(fn, [(x, 't', 0), (g, 'r', 0), (dh, 't', 0), (dres, 't', 0)],
                   [(F32, 't'), (BF16, 't'), (F32, 'a')], M=M, N=D, tm=tm, tn=D, name=name)


def ffn_fwd(x, g, wg, wu, wd, tag):
    h = rms_fwd(x, g, f"ffn_norm_{tag}")

    def ep(accs, ex):
        a, b = accs
        return a, b, a * _sigmoid(a) * b

    a, b, s = matmul([(h, wg), (h, wu)], 'nn', [F32, F32, BF16], epilogue=ep, tm=1024, tn=512,
                     name=f"ffn_gateup_{tag}")
    (xo,) = matmul([(s, wd)], 'nn', [F32], epilogue=lambda accs, ex: (ex[0] + 0.5 * accs[0],),
                   extras=[(x, 't')], tm=512, tn=512, name=f"ffn_down_{tag}")
    return xo, (x, h, a, b, s)


def ffn_bwd(saved, g, wg, wu, wd, dxo, dxo_bf, tag, after, on_grads):
    x, h, a, b, s = saved

    def ep(accs, ex):
        ds = 0.5 * accs[0]
        av, bv = ex
        sg = _sigmoid(av)
        return ds * bv * (sg * (1.0 + av * (1.0 - sg))), ds * (av * sg)

    da, db = matmul([(dxo_bf, wd)], 'nt', [BF16, BF16], epilogue=ep, extras=[(a, 't'), (b, 't')],
                    tm=1024, tn=512, name=f"ffn_dact_{tag}", after=after)
    (dwd,) = matmul([(s, dxo_bf)], 'tn', [BF16], epilogue=lambda accs, ex: (0.5 * accs[0],),
                    tm=512, tn=512, name=f"ffn_dwd_{tag}")
    dwg, dwu = matmul([(h, da), (h, db)], 'tn', [BF16, BF16], epilogue=lambda accs, ex: tuple(accs),
                      tm=1024, tn=512, name=f"ffn_dwgu_{tag}")
    started = on_grads([dwg, dwu, dwd])
    (dh,) = matmul([(da, wg), (db, wu)], 'nt', [F32], tm=512, tn=256, name=f"ffn_dh_{tag}", after=started)
    dx, dx_bf, dg = rms_bwd(x, g, dh, dxo, f"ffn_dnorm_{tag}")
    return dx, dx_bf, dg


def _tril_mask(n):
    return lax.broadcasted_iota(jnp.int32, (n, n), 0) >= lax.broadcasted_iota(jnp.int32, (n, n), 1)


def _gmlp_specs(L, half, n_grp, chunk):
    gd = half // n_grp
    specs = [pl.BlockSpec((chunk, gd), lambda g, n: (n, g)),
             pl.BlockSpec((chunk, gd), lambda g, n: (n, n_grp + g)),
             pl.BlockSpec((1, gd), lambda g, n: (0, g)),
             pl.BlockSpec((1, gd), lambda g, n: (0, g)),
             pl.BlockSpec((None, chunk, chunk), lambda g, n: (g, 0, 0)),
             pl.BlockSpec((None, chunk, 1), lambda g, n: (g, 0, 0))]
    return gd, specs


def _gmlp_gate_values(zu, zv, lg, lb, ws, bs):
    u, v = _gelu(zu), _gelu(zv)
    mu = jnp.mean(v, axis=-1, keepdims=True)
    d = v - mu
    rstd = lax.rsqrt(jnp.mean(d * d, axis=-1, keepdims=True) + EPS)
    vhat = d * rstd
    vn = vhat * lg + lb
    w = jnp.where(_tril_mask(ws.shape[0]), ws, 0.0).astype(BF16)
    sv = jnp.dot(w, vn.astype(BF16), preferred_element_type=F32) + bs
    return u, vhat, rstd, vn, w, sv


def gmlp_gate_fwd(zpre, ln_g, ln_b, w_s, b_s):
    L, half = zpre.shape[0], zpre.shape[1] // 2
    n_grp, chunk = w_s.shape[0], w_s.shape[1]
    gd, specs = _gmlp_specs(L, half, n_grp, chunk)

    def body(zu, zv, lg, lb, ws, bs, o):
        u, _, _, _, _, sv = _gmlp_gate_values(zu[...], zv[...], lg[...], lb[...], ws[...], bs[...])
        o[...] = (u * sv).astype(o.dtype)

    return _pcall(body, name="gmlp_gate", grid=(n_grp, L // chunk), in_specs=specs,
                  out_specs=pl.BlockSpec((chunk, gd), lambda g, n: (n, g)),
                  out_shape=jax.ShapeDtypeStruct((L, half), BF16), compiler_params=_params())(
        zpre, zpre, ln_g, ln_b, w_s, b_s)


def gmlp_gate_bwd(zpre, ln_g, ln_b, w_s, b_s, dgated):
    L, half = zpre.shape[0], zpre.shape[1] // 2
    n_grp, chunk = w_s.shape[0], w_s.shape[1]
    gd, specs = _gmlp_specs(L, half, n_grp, chunk)
    specs = specs + [pl.BlockSpec((chunk, gd), lambda g, n: (n, g))]

    def body(zu, zv, lg, lb, ws, bs, dg, dzu, dzv, dws, dbs, dlg, dlb):
        zuv, zvv, lgv = zu[...], zv[...], lg[...]
        u, vhat, rstd, vn, w, sv = _gmlp_gate_values(zuv, zvv, lgv, lb[...], ws[...], bs[...])
        dgv = dg[...]
        du = dgv * sv
        dsv = dgv * u
        dsv_bf = dsv.astype(BF16)
        dw = lax.dot_general(dsv_bf, vn.astype(BF16), _DIMS['nt'], preferred_element_type=F32)
        dvn = lax.dot_general(w, dsv_bf, _DIMS['tn'], preferred_element_type=F32)
        dvhat = dvn * lgv
        dv = rstd * (dvhat - jnp.mean(dvhat, axis=-1, keepdims=True)
                     - vhat * jnp.mean(dvhat * vhat, axis=-1, keepdims=True))
        dzu[...] = (du * _gelu_grad(zuv)).astype(dzu.dtype)
        dzv[...] = (dv * _gelu_grad(zvv)).astype(dzv.dtype)

        @pl.when(pl.program_id(1) == 0)
        def _():
            dws[...] = jnp.zeros_like(dws)
            dbs[...] = jnp.zeros_like(dbs)
            dlg[...] = jnp.zeros_like(dlg)
            dlb[...] = jnp.zeros_like(dlb)

        dws[...] += jnp.where(_tril_mask(chunk), dw, 0.0)
        dbs[...] += jnp.sum(dsv, axis=1, keepdims=True)
        dlg[...] += jnp.sum(dvn * vhat, axis=0, keepdims=True)
        dlb[...] += jnp.sum(dvn, axis=0, keepdims=True)

    tile = pl.BlockSpec((chunk, gd), lambda g, n: (n, g))
    vec = pl.BlockSpec((1, gd), lambda g, n: (0, g))
    return _pcall(body, name="gmlp_gate_bwd", grid=(n_grp, L // chunk), in_specs=specs,
                  out_specs=[tile, tile, pl.BlockSpec((None, chunk, chunk), lambda g, n: (g, 0, 0)),
                             pl.BlockSpec((None, chunk, 1), lambda g, n: (g, 0, 0)), vec, vec],
                  out_shape=[jax.ShapeDtypeStruct((L, half), BF16), jax.ShapeDtypeStruct((L, half), BF16),
                             jax.ShapeDtypeStruct((n_grp, chunk, chunk), F32),
                             jax.ShapeDtypeStruct((n_grp, chunk, 1), F32),
                             jax.ShapeDtypeStruct((1, half), F32), jax.ShapeDtypeStruct((1, half), F32)],
                  compiler_params=_params())(zpre, zpre, ln_g, ln_b, w_s, b_s, dgated)


def gmlp_fwd(x, g, w_in, ln_g, ln_b, w_s, b_s, w_out):
    h = rms_fwd(x, g, "gmlp_norm")
    (zpre,) = matmul([(h, w_in)], 'nn', [F32], tm=1024, tn=512, name="gmlp_in")
    gated = gmlp_gate_fwd(zpre, ln_g, ln_b, w_s, b_s)
    (xo,) = matmul([(gated, w_out)], 'nn', [F32], epilogue=lambda accs, ex: (ex[0] + accs[0],),
                   extras=[(x, 't')], tm=512, tn=512, name="gmlp_out")
    return xo, (x, h, zpre, gated)


def gmlp_bwd(saved, g, w_in, ln_g, ln_b, w_s, b_s, w_out, dxo, dxo_bf, after, on_grads):
    x, h, zpre, gated = saved
    (dgated,) = matmul([(dxo_bf, w_out)], 'nt', [F32], tm=512, tn=512, name="gmlp_dgated", after=after)
    (dw_out,) = matmul([(gated, dxo_bf)], 'tn', [BF16], tm=512, tn=512, name="gmlp_dwout")
    dzu, dzv, dws, dbs, dlg, dlb = gmlp_gate_bwd(zpre, ln_g, ln_b, w_s, b_s, dgated)
    dz = jnp.concatenate([dzu, dzv], axis=1)
    (dw_in,) = matmul([(h, dz)], 'tn', [BF16], tm=1024, tn=512, name="gmlp_dwin")
    started = on_grads([dw_in, dw_out])
    (dh,) = matmul([(dz, w_in)], 'nt', [F32], tm=256, tn=256, name="gmlp_dh", after=started)
    dx, dx_bf, dg = rms_bwd(x, g, dh, dxo, "gmlp_dnorm")
    return dx, dx_bf, dg, dlg, dlb, dws, dbs


def _s5_disc(lr, li, ldt, br, bi):
    dt = jnp.exp(ldt)
    mag = jnp.exp(lr * dt)
    ang = li * dt
    ar = mag * jnp.cos(ang)
    ai = mag * jnp.sin(ang)
    den = lr * lr + li * li
    nr = ar - 1.0
    zr = (nr * lr + ai * li) / den
    zi = (ai * lr - nr * li) / den
    return ar, ai, zr[None] * br - zi[None] * bi, zr[None] * bi + zi[None] * br


def s5_disc_fwd(lr, li, ldt, br, bi):
    def body(lr_r, li_r, ldt_r, br_r, bi_r, ar_o, ai_o, bbr_o, bbi_o):
        res = _s5_disc(lr_r[...], li_r[...], ldt_r[...], br_r[...], bi_r[...])
        for o, v in zip((ar_o, ai_o, bbr_o, bbi_o), res):
            o[...] = v

    shp = lambda a: jax.ShapeDtypeStruct(a.shape, F32)
    return _pcall(body, name="s5_disc", out_shape=[shp(lr), shp(lr), shp(br), shp(br)],
                  compiler_params=_params())(lr, li, ldt, br, bi)


def s5_disc_bwd(lr, li, ldt, br, bi, dar, dai, dbbr, dbbi):
    def body(lr_r, li_r, ldt_r, br_r, bi_r, dar_r, dai_r, dbbr_r, dbbi_r, o1, o2, o3, o4, o5):
        _, vjp = jax.vjp(_s5_disc, lr_r[...], li_r[...], ldt_r[...], br_r[...], bi_r[...])
        res = vjp((dar_r[...], dai_r[...], dbbr_r[...], dbbi_r[...]))
        for o, v in zip((o1, o2, o3, o4, o5), res):
            o[...] = v

    shp = lambda a: jax.ShapeDtypeStruct(a.shape, F32)
    return _pcall(body, name="s5_disc_bwd", out_shape=[shp(lr), shp(lr), shp(ldt), shp(br), shp(br)],
                  compiler_params=_params())(lr, li, ldt, br, bi, dar, dai, dbbr, dbbi)


def blockdiag_matmul(pairs, outs, *, epilogue=None, extras=(), name):
    a0, b0 = pairs[0]
    M = a0.shape[0]
    T, wa, wo = b0.shape
    tm = _pick(M, (512, 256, 128))
    n_p, n_e = len(pairs), len(extras)
    in_specs, args = [], []
    for a, b in pairs:
        in_specs += [pl.BlockSpec((tm, wa), lambda k, i: (i, k)), pl.BlockSpec((None, wa, wo), lambda k, i: (k, 0, 0))]
        args += [a, b]
    for arr, kind in extras:
        in_specs.append(pl.BlockSpec((tm, wo), lambda k, i: (i, k)) if kind == 't'
                        else pl.BlockSpec((1, wo), lambda k, i: (0, k)))
        args.append(arr)

    def body(*refs):
        accs = [jnp.dot(refs[2 * p][...].astype(BF16), refs[2 * p + 1][...], preferred_element_type=F32)
                for p in range(n_p)]
        ex = [r[...] for r in refs[2 * n_p:2 * n_p + n_e]]
        res = tuple(accs) if epilogue is None else epilogue(accs, ex)
        for ref, v in zip(refs[2 * n_p + n_e:], res):
            ref[...] = v.astype(ref.dtype)

    return _pcall(body, name=name, grid=(T, M // tm), in_specs=in_specs,
                  out_specs=[pl.BlockSpec((tm, wo), lambda k, i: (i, k)) for _ in outs],
                  out_shape=[jax.ShapeDtypeStruct((M, T * wo), dt) for dt in outs],
                  compiler_params=_params())(*args)


def blockdiag_outer(pairs, name):
    M = pairs[0][0].shape[0]
    n_p = len(pairs)
    shapes = []
    in_specs, args = [], []
    tm = _pick(M, (512, 256, 128))
    T = None
    for a, b, wa, wb in pairs:
        T = a.shape[1] // wa
        in_specs += [pl.BlockSpec((tm, wa), lambda k, i: (i, k)), pl.BlockSpec((tm, wb), lambda k, i: (i, k))]
        args += [a, b]
        shapes.append((T, wa, wb))

    def body(*refs):
        @pl.when(pl.program_id(1) == 0)
        def _():
            for o in refs[2 * n_p:]:
                o[...] = jnp.zeros_like(o)
        for p in range(n_p):
            refs[2 * n_p + p][...] += lax.dot_general(refs[2 * p][...].astype(BF16), refs[2 * p + 1][...].astype(BF16),
                                                      _DIMS['tn'], preferred_element_type=F32)

    return _pcall(body, name=name, grid=(T, M // tm), in_specs=in_specs,
                  out_specs=[pl.BlockSpec((None, s[1], s[2]), lambda k, i: (k, 0, 0)) for s in shapes],
                  out_shape=[jax.ShapeDtypeStruct(s, F32) for s in shapes], compiler_params=_params())(*args)


def s5_scan(br, bi, ar, ai, reverse, want_prev, name):
    L, S = br.shape
    ln = _pick(S, (SCAN_LANES, 512, 256, 128))
    tb = _pick(L, (512, 256, 128))
    n_t = L // tb
    n_q = tb // 8

    def tmap(j, t):
        return ((n_t - 1 - t) if reverse else t, j)

    blk = pl.BlockSpec((tb, ln), tmap)
    vec = pl.BlockSpec((1, ln), lambda j, t: (0, j))

    def cmul(xr, xi, yr, yi):
        return xr * yr - xi * yi, xr * yi + xi * yr

    n_out = 4 if want_prev else 2

    def body(br_r, bi_r, ar_r, ai_r, *rest):
        outs, (cr_s, ci_s) = rest[:n_out], rest[n_out:]

        @pl.when(pl.program_id(1) == 0)
        def _():
            cr_s[...] = jnp.zeros_like(cr_s)
            ci_s[...] = jnp.zeros_like(ci_s)

        a1r, a1i = ar_r[...], ai_r[...]
        a2r, a2i = cmul(a1r, a1i, a1r, a1i)
        a4r, a4i = cmul(a2r, a2i, a2r, a2i)
        a8r, a8i = cmul(a4r, a4i, a4r, a4i)
        row = lax.broadcasted_iota(jnp.int32, (8, ln), 0)
        dist = (7 - row) if reverse else row
        pwr, pwi = jnp.broadcast_to(a1r, (8, ln)), jnp.broadcast_to(a1i, (8, ln))
        for bit, (er, ei) in ((1, (a1r, a1i)), (2, (a2r, a2i)), (4, (a4r, a4i))):
            nr, ni = cmul(pwr, pwi, er, ei)
            sel = (dist & bit) != 0
            pwr, pwi = jnp.where(sel, nr, pwr), jnp.where(sel, ni, pwi)
        last = 0 if reverse else 7
        steps = [(d, jnp.where(dist >= d, er, 0.0), jnp.where(dist >= d, ei, 0.0))
                 for d, (er, ei) in ((1, (a1r, a1i)), (2, (a2r, a2i)), (4, (a4r, a4i)))]

        def step(q, carry):
            cr, ci = carry
            qq = (n_q - 1 - q) if reverse else q
            rows = pl.ds(pl.multiple_of(qq * 8, 8), 8)
            xr, xi = br_r[rows, :], bi_r[rows, :]
            for d, er, ei in steps:
                sr = pltpu.roll(xr, (8 - d) if reverse else d, 0)
                si = pltpu.roll(xi, (8 - d) if reverse else d, 0)
                mr, mi = cmul(sr, si, er, ei)
                xr, xi = xr + mr, xi + mi
            lr = jnp.sum(jnp.where(row == last, xr, 0.0), axis=0, keepdims=True)
            li = jnp.sum(jnp.where(row == last, xi, 0.0), axis=0, keepdims=True)
            kr, ki = cmul(pwr, pwi, cr, ci)
            xr, xi = xr + kr, xi + ki
            outs[0][rows, :] = xr
            outs[1][rows, :] = xi
            if want_prev:
                outs[2][rows, :] = jnp.where(dist >= 1, pltpu.roll(xr, 7 if reverse else 1, 0), cr)
                outs[3][rows, :] = jnp.where(dist >= 1, pltpu.roll(xi, 7 if reverse else 1, 0), ci)
            nr, ni = cmul(a8r, a8i, cr, ci)
            return lr + nr, li + ni

        cr, ci = lax.fori_loop(0, n_q, step, (cr_s[...], ci_s[...]), unroll=2)
        cr_s[...] = cr
        ci_s[...] = ci

    shp = jax.ShapeDtypeStruct((L, S), F32)
    return _pcall(body, name=name, grid=(S // ln, n_t), in_specs=[blk, blk, vec, vec],
                  out_specs=[blk] * n_out, out_shape=[shp] * n_out,
                  scratch_shapes=[pltpu.VMEM((1, ln), F32), pltpu.VMEM((1, ln), F32)],
                  compiler_params=_params())(br, bi, ar, ai)


def _to_blockdiag(m, tile_groups):
    G, A, B = m.shape
    T = G // tile_groups
    eye = jnp.eye(tile_groups, dtype=m.dtype)
    t = m.reshape(T, tile_groups, A, 1, B) * eye[None, :, None, :, None]
    return t.reshape(T, tile_groups * A, tile_groups * B)


def _from_blockdiag(t, tile_groups):
    T, RA, RB = t.shape
    A, B = RA // tile_groups, RB // tile_groups
    d = jnp.diagonal(t.reshape(T, tile_groups, A, tile_groups, B), axis1=1, axis2=3)
    return jnp.moveaxis(d, 3, 1).reshape(T * tile_groups, A, B)


def s5_fwd(x, g, w_in, lam_re, lam_im, log_dt, b_re, b_im, c_re, c_im, d_skip, w_out):
    G, P, H = b_re.shape
    tg = min(SSM_TILE_GROUPS, G)
    h = rms_fwd(x, g, "s5_norm")
    (u,) = matmul([(h, w_in)], 'nn', [F32], tm=512, tn=512, name="s5_in")
    br_t, bi_t = jnp.transpose(b_re, (2, 0, 1)), jnp.transpose(b_im, (2, 0, 1))
    ar, ai, bbr, bbi = s5_disc_fwd(lam_re, lam_im, log_dt, br_t, bi_t)
    bbr_g, bbi_g = jnp.transpose(bbr, (1, 0, 2)), jnp.transpose(bbi, (1, 0, 2))
    bd_br, bd_bi = _to_blockdiag(bbr_g.astype(BF16), tg), _to_blockdiag(bbi_g.astype(BF16), tg)
    bur, bui = blockdiag_matmul([(u, bd_br), (u, bd_bi)], [F32, F32], name="s5_bu")
    a_r, a_i = ar.reshape(1, G * P), ai.reshape(1, G * P)
    hr, hi, hpr, hpi = s5_scan(bur, bui, a_r, a_i, False, True, "s5_scan")
    c_pg_r = jnp.transpose(c_re, (0, 2, 1)).astype(BF16)
    c_pg_i = jnp.transpose(c_im, (0, 2, 1)).astype(BF16)
    bd_cr, bd_nci = _to_blockdiag(c_pg_r, tg), _to_blockdiag(-c_pg_i, tg)

    def ep(accs, ex):
        y = accs[0] + accs[1] + ex[1] * ex[0]
        return y, _gelu(y)

    y, act = blockdiag_matmul([(hr, bd_cr), (hi, bd_nci)], [F32, BF16], epilogue=ep,
                              extras=[(u, 't'), (d_skip, 'r')], name="s5_y")
    (o,) = matmul([(act, w_out)], 'nn', [F32], tm=512, tn=512, name="s5_out")
    M, D = x.shape
    tm = _pick(M, (256, 128))
    (xo,) = tilemap(lambda xv, val, gt: (xv + val * _sigmoid(gt),), [(x, 't', 0), (o, 't', 0), (o, 't', 1)],
                    [(F32, 't')], M=M, N=D, tm=tm, tn=D, name="s5_glu")
    saved = (x, h, u, hr, hi, hpr, hpi, y, act, o, a_r, a_i, bd_br, bd_bi, bd_cr, bd_nci, br_t, bi_t)
    return xo, saved


def s5_bwd(saved, g, w_in, lam_re, lam_im, log_dt, b_re, d_skip, w_out, dxo, after, on_grads):
    x, h, u, hr, hi, hpr, hpi, y, act, o, a_r, a_i, bd_br, bd_bi, bd_cr, bd_nci, br_t, bi_t = saved
    G, P, H = b_re.shape
    tg = min(SSM_TILE_GROUPS, G)
    M, D = x.shape
    tm = _pick(M, (256, 128))

    def glu_bwd(dv, val, gt):
        sg = _sigmoid(gt)
        return dv * sg, dv * val * sg * (1.0 - sg)

    dval, dgate = tilemap(glu_bwd, [(dxo, 't', 0), (o, 't', 0), (o, 't', 1)], [(BF16, 't'), (BF16, 't')],
                          M=M, N=D, tm=tm, tn=D, name="s5_dglu", after=after)
    do = jnp.concatenate([dval, dgate], axis=1)
    (dw_out,) = matmul([(act, do)], 'tn', [BF16], tm=512, tn=512, name="s5_dwout")
    (dact,) = matmul([(do, w_out)], 'nt', [F32], tm=512, tn=512, name="s5_dact")
    dy, dd = tilemap(lambda da, yv, uv: (da * _gelu_grad(yv), jnp.sum(da * _gelu_grad(yv) * uv, axis=0, keepdims=True)),
                     [(dact, 't', 0), (y, 't', 0), (u, 't', 0)], [(F32, 't'), (F32, 'a')],
                     M=M, N=D, tm=tm, tn=D, name="s5_dy")
    bd_crT, bd_nciT = jnp.transpose(bd_cr, (0, 2, 1)), jnp.transpose(bd_nci, (0, 2, 1))
    dhr, dhi = blockdiag_matmul([(dy, bd_crT), (dy, bd_nciT)], [F32, F32], name="s5_dh")
    gr, gi = s5_scan(dhr, dhi, a_r, -a_i, True, False, "s5_scan_rev")
    S = G * P
    tms = _pick(M, (128,))

    def da_fn(grv, giv, hprv, hpiv):
        return (jnp.sum(grv * hprv + giv * hpiv, axis=0, keepdims=True),
                jnp.sum(giv * hprv - grv * hpiv, axis=0, keepdims=True))

    dar, dai = tilemap(da_fn, [(gr, 't', 0), (gi, 't', 0), (hpr, 't', 0), (hpi, 't', 0)], [(F32, 'a'), (F32, 'a')],
                       M=M, N=S, tm=tms, tn=_pick(S, (2048, 1024, 512)), name="s5_dabar")
    wa, wb = tg * H, tg * P
    xc_r, xc_i, xb_r, xb_i = blockdiag_outer([(dy, hr, wa, wb), (dy, hi, wa, wb), (u, gr, wa, wb), (u, gi, wa, wb)],
                                             "s5_dcb")
    dc_re = _from_blockdiag(xc_r, tg)
    dc_im = -_from_blockdiag(xc_i, tg)
    dbb_r = jnp.transpose(_from_blockdiag(xb_r, tg), (1, 0, 2))
    dbb_i = jnp.transpose(_from_blockdiag(xb_i, tg), (1, 0, 2))
    dlr, dli, dldt, dbr_t, dbi_t = s5_disc_bwd(lam_re, lam_im, log_dt, br_t, bi_t,
                                               dar.reshape(G, P), dai.reshape(G, P), dbb_r, dbb_i)
    db_re, db_im = jnp.transpose(dbr_t, (1, 2, 0)), jnp.transpose(dbi_t, (1, 2, 0))
    bd_brT, bd_biT = jnp.transpose(bd_br, (0, 2, 1)), jnp.transpose(bd_bi, (0, 2, 1))
    (du,) = blockdiag_matmul([(gr, bd_brT), (gi, bd_biT)], [BF16],
                             epilogue=lambda accs, ex: (accs[0] + accs[1] + ex[1] * ex[0],),
                             extras=[(dy, 't'), (d_skip, 'r')], name="s5_du")
    (dw_in,) = matmul([(h, du)], 'tn', [BF16], tm=512, tn=512, name="s5_dwin")
    started = on_grads([dw_in, dw_out])
    (dh,) = matmul([(du, w_in)], 'nt', [F32], tm=512, tn=512, name="s5_dhin", after=started)
    dx, dx_bf, dg = rms_bwd(x, g, dh, dxo, "s5_dnorm")
    return dx, dx_bf, dg, dlr, dli, dldt, db_re, db_im, dc_re, dc_im, dd


def ple_fwd(x, g, p_emb, w_gate, w_proj, tag):
    h = rms_fwd(x, g, f"ple_norm_{tag}")
    (q,) = matmul([(p_emb, w_proj)], 'nn', [F32], tm=512, tn=512, name=f"ple_proj_{tag}")

    def ep(accs, ex):
        gt = _sigmoid(accs[0])
        return ex[0] + gt * ex[1], gt

    xo, gate = matmul([(h, w_gate)], 'nn', [F32, F32], epilogue=ep, extras=[(x, 't'), (q, 't')],
                      tm=512, tn=512, name=f"ple_gate_{tag}")
    return xo, (x, h, q, gate)


def ple_bwd(saved, g, p_emb, w_gate, dxo, tag, after, on_grads):
    x, h, q, gate = saved
    M, D = x.shape
    tm = _pick(M, (256, 128))
    dq, dpre = tilemap(lambda dv, qv, gv: (dv * gv, dv * qv * gv * (1.0 - gv)),
                       [(dxo, 't', 0), (q, 't', 0), (gate, 't', 0)], [(BF16, 't'), (BF16, 't')],
                       M=M, N=D, tm=tm, tn=D, name=f"ple_dgate_{tag}", after=after)
    (dw_proj,) = matmul([(p_emb, dq)], 'tn', [BF16], tm=256, tn=512, name=f"ple_dwproj_{tag}")
    (dw_gate,) = matmul([(h, dpre)], 'tn', [BF16], tm=512, tn=512, name=f"ple_dwgate_{tag}")
    started = on_grads([dw_gate, dw_proj])
    (dh,) = matmul([(dpre, w_gate)], 'nt', [F32], tm=512, tn=512, name=f"ple_dh_{tag}", after=started)
    dx, dx_bf, dg = rms_bwd(x, g, dh, dxo, f"ple_dnorm_{tag}")
    return dx, dx_bf, dg


def loss_head(x, g, target):
    M, D = x.shape
    tm = _pick(M, (256, 128))

    def fn(xv, gv, tv):
        r = lax.rsqrt(jnp.mean(xv * xv, axis=-1, keepdims=True) + EPS)
        xh = xv * r
        e = xh * gv - tv
        dy = e * (1.0 / D)
        dxh = dy * gv
        m = jnp.mean(dxh * xh, axis=-1, keepdims=True)
        dx = r * (dxh - xh * m)
        return jnp.sum(e * e, axis=0, keepdims=True), dx, dx, jnp.sum(dy * xh, axis=0, keepdims=True)

    return tilemap(fn, [(x, 't', 0), (g, 'r', 0), (target, 't', 0)],
                   [(F32, 'a'), (F32, 't'), (BF16, 't'), (F32, 'a')], M=M, N=D, tm=tm, tn=D, name="loss_head")


def _adamw_math(wv, gv, mv, vv):
    mn = ADAM_B1 * mv + (1.0 - ADAM_B1) * gv
    vn = ADAM_B2 * vv + (1.0 - ADAM_B2) * (gv * gv)
    m_hat = mn / (1.0 - ADAM_B1 ** ADAM_STEP)
    v_hat = vn / (1.0 - ADAM_B2 ** ADAM_STEP)
    return -ADAM_LR * (m_hat / (jnp.sqrt(v_hat) + ADAM_EPS) + ADAM_WD * wv), mn, vn


def adamw_into(w, lead, g, m, v, carry, name):
    R, C = w.shape[-2:]
    lead = tuple(lead)
    tm = _row_tile(R, C)
    blk = pl.BlockSpec((None,) * len(lead) + (tm, C), lambda i: lead + (i, 0))
    n_carry = 0 if carry is None else 4

    def body(*refs):
        w_r, g_r, m_r, v_r = refs[:4]
        g_o, d_o, m_o, v_o = refs[4 + n_carry:]
        gv = g_r[...]
        d, mn, vn = _adamw_math(w_r[...], gv, m_r[...], v_r[...])
        g_o[...] = gv
        d_o[...] = d
        m_o[...] = mn
        v_o[...] = vn

    shp = jax.ShapeDtypeStruct(w.shape, F32)
    return _pcall(body, name=name, grid=(R // tm,),
                  in_specs=[blk, pl.BlockSpec((tm, C), lambda i: (i, 0)), blk, blk] + [pl.BlockSpec(memory_space=pl.ANY)] * n_carry,
                  out_specs=[blk] * 4, out_shape=[shp] * 4,
                  input_output_aliases={4 + k: k for k in range(n_carry)},
                  compiler_params=_params())(w, g, m, v, *(carry or ()))


def adamw(w, g, m, v, name):
    w2, g2, m2, v2 = _as2d(w), g.reshape(_as2d(w).shape), _as2d(m), _as2d(v)
    R, C = w2.shape
    tm = _row_tile(R, C)

    d, mn, vn = tilemap(_adamw_math, [(w2, 't', 0), (g2, 't', 0), (m2, 't', 0), (v2, 't', 0)],
                        [(F32, 't'), (F32, 't'), (F32, 't')], M=R, N=C, tm=tm, tn=C, name=name)
    return d.reshape(w.shape), mn.reshape(w.shape), vn.reshape(w.shape)


WEIGHT_NAMES = ['norm_g', 'final_norm_g', 'ffn_w_gate', 'ffn_w_up', 'ffn_w_down', 'gmlp_w_in', 'gmlp_ln_g',
                'gmlp_ln_b', 'gmlp_w_s', 'gmlp_b_s', 'gmlp_w_out', 's5_w_in', 's5_lam_re', 's5_lam_im',
                's5_log_dt', 's5_b_re', 's5_b_im', 's5_c_re', 's5_c_im', 's5_d', 's5_w_out', 'ple_w_gate',
                'ple_w_proj']
BIG = {'ffn_w_gate': 1, 'ffn_w_up': 1, 'ffn_w_down': 0, 'gmlp_w_in': 1, 'gmlp_w_out': 0, 's5_w_in': 0,
       's5_w_out': 1, 'ple_w_gate': 0, 'ple_w_proj': 1}


def _blocks(depth):
    out = []
    for i in range(depth):
        for k, half in enumerate("ab"):
            ffn = [(n, (i, k)) for n in ('ffn_w_gate', 'ffn_w_up', 'ffn_w_down')]
            if k == 1:
                out.append((f"ffn{i}b", ffn))
                out.append((f"ple{i}", [('ple_w_gate', (i,)), ('ple_w_proj', (i,))]))
            else:
                out.append((f"ffn{i}a", ffn))
                mix = 'gmlp' if i % 2 == 0 else 's5'
                out.append((f"{mix}{i}", [(f'{mix}_w_in', (i // 2,)), (f'{mix}_w_out', (i // 2,))]))
    return out


def kernel(x, p, norm_g, final_norm_g, ffn_w_gate, ffn_w_up, ffn_w_down, gmlp_w_in, gmlp_ln_g, gmlp_ln_b, gmlp_w_s, gmlp_b_s, gmlp_w_out, s5_w_in, s5_lam_re, s5_lam_im, s5_log_dt, s5_b_re, s5_b_im, s5_c_re, s5_c_im, s5_d, s5_w_out, ple_w_gate, ple_w_proj, loss_target, m_norm_g, m_final_norm_g, m_ffn_w_gate, m_ffn_w_up, m_ffn_w_down, m_gmlp_w_in, m_gmlp_ln_g, m_gmlp_ln_b, m_gmlp_w_s, m_gmlp_b_s, m_gmlp_w_out, m_s5_w_in, m_s5_lam_re, m_s5_lam_im, m_s5_log_dt, m_s5_b_re, m_s5_b_im, m_s5_c_re, m_s5_c_im, m_s5_d, m_s5_w_out, m_ple_w_gate, m_ple_w_proj, v_norm_g, v_final_norm_g, v_ffn_w_gate, v_ffn_w_up, v_ffn_w_down, v_gmlp_w_in, v_gmlp_ln_g, v_gmlp_ln_b, v_gmlp_w_s, v_gmlp_b_s, v_gmlp_w_out, v_s5_w_in, v_s5_lam_re, v_s5_lam_im, v_s5_log_dt, v_s5_b_re, v_s5_b_im, v_s5_c_re, v_s5_c_im, v_s5_d, v_s5_w_out, v_ple_w_gate, v_ple_w_proj):
    env = dict(locals())
    W = {n: env[n] for n in WEIGHT_NAMES}
    Mo = {n: env["m_" + n] for n in WEIGHT_NAMES}
    Vo = {n: env["v_" + n] for n in WEIGHT_NAMES}
    depth = norm_g.shape[0]
    L, D = x.shape[1], x.shape[2]
    s_idx = 2 * lax.axis_index("x") + lax.axis_index("y")

    sc = jnp.stack([s_idx, lax.axis_index("c")]).astype(jnp.int32)
    blocks = _blocks(depth)

    ng2 = norm_g.reshape(depth * 4, norm_g.shape[-1])
    ng_full, sd_full = gather_small("gather_small", [ng2, s5_d], sc)
    full = {}
    gathers = []
    token = sd_full
    for bname, mats in blocks:
        casts = [cast_into_full(W[n], lead, BIG[n], sc, f"cast_{bname}_{n}", token) for n, lead in mats]
        handle, token = gather_start(f"gather_{bname}", casts, [W[n].shape[-2:] for n, _ in mats],
                                     [BIG[n] for n, _ in mats], token)
        gathers.append(handle)
    all_started = token
    ng_full = ng_full.reshape(depth, 4, 1, D)

    def fetch(bi, after):
        full.update(dict(zip(blocks[bi][1], gather_finish(gathers[bi], after))))
    gf = final_norm_g.reshape(1, D)

    G, P, H = s5_b_re.shape[1:]
    n_grp, chunk = gmlp_w_s.shape[1], gmlp_w_s.shape[2]
    lam_re, lam_im = s5_lam_re[0], s5_lam_im[0]
    log_dt = s5_log_dt.reshape(G, 1)
    b_re, b_im, c_re, c_im = s5_b_re[0], s5_b_im[0], s5_c_re[0], s5_c_im[0]
    w_s, b_s = gmlp_w_s[0], gmlp_b_s[0].reshape(n_grp, chunk, 1)
    xs = x.reshape(L, D)
    saved = []
    def ffn_w(i, k):
        return [full[(n, (i, k))] for n in ('ffn_w_gate', 'ffn_w_up', 'ffn_w_down')]

    for i in range(depth):
        sv = {}
        fetch(4 * i, all_started if i == 0 else xs)
        xs, sv['ffn_a'] = ffn_fwd(xs, ng_full[i, 0], *ffn_w(i, 0), f"{i}a")
        j = (i // 2,)
        fetch(4 * i + 1, xs)
        if i % 2 == 0:
            xs, sv['mix'] = gmlp_fwd(xs, ng_full[i, 1], full[('gmlp_w_in', j)], gmlp_ln_g, gmlp_ln_b, w_s, b_s,
                                     full[('gmlp_w_out', j)])
        else:
            xs, sv['mix'] = s5_fwd(xs, ng_full[i, 1], full[('s5_w_in', j)], lam_re, lam_im, log_dt, b_re, b_im,
                                   c_re, c_im, sd_full, full[('s5_w_out', j)])
        fetch(4 * i + 2, xs)
        xs, sv['ffn_b'] = ffn_fwd(xs, ng_full[i, 2], *ffn_w(i, 1), f"{i}b")
        fetch(4 * i + 3, xs)
        xs, sv['ple'] = ple_fwd(xs, ng_full[i, 3], p[i, 0], full[('ple_w_gate', (i,))], full[('ple_w_proj', (i,))], f"{i}")
        saved.append(sv)

    sq, dx, dx_bf, dgf = loss_head(xs, gf, loss_target.reshape(L, D))
    loss_local = 0.5 * jnp.sum(sq) / D
    dng = [[None] * 4 for _ in range(depth)]
    small = {}
    gshard = {}
    state = {'fence': sc}
    ici_inflight = []
    swaps = []

    def hook(bi):
        def on_grads(gs):
            bname, mats = blocks[bi]
            handle, token = reduce_pair_start(f"reduce_{bname}", gs, [BIG[n] for n, _ in mats], state['fence'])
            state['pair'] = (handle, mats)
            return token
        return on_grads

    def reduce_block(after):
        handle, mats = state.pop('pair')
        ici_handle, fence = reduce_ici_start(handle, sc, after)
        if ici_inflight:
            prev_handle, prev_mats = ici_inflight.pop()
            swap_handle, fence = reduce_finish(prev_handle, fence)
            swaps.append((swap_handle, prev_mats))
        ici_inflight.append((ici_handle, mats))
        state['fence'] = fence

    for i in reversed(range(depth)):
        sv = saved[i]
        dx, dx_bf, dng[i][3] = ple_bwd(sv['ple'], ng_full[i, 3], p[i, 0], full[('ple_w_gate', (i,))], dx, f"{i}",
                                       state['fence'], hook(4 * i + 3))
        reduce_block(dx_bf)
        dx, dx_bf, dng[i][2] = ffn_bwd(sv['ffn_b'], ng_full[i, 2], *ffn_w(i, 1), dx, dx_bf, f"{i}b",
                                       state['fence'], hook(4 * i + 2))
        reduce_block(dx_bf)
        j = (i // 2,)
        if i % 2 == 0:
            dx, dx_bf, dng[i][1], dlg, dlb, dws, dbs = gmlp_bwd(
                sv['mix'], ng_full[i, 1], full[('gmlp_w_in', j)], gmlp_ln_g, gmlp_ln_b, w_s, b_s,
                full[('gmlp_w_out', j)], dx, dx_bf, state['fence'], hook(4 * i + 1))
            small.update(gmlp_ln_g=dlg, gmlp_ln_b=dlb, gmlp_w_s=dws, gmlp_b_s=dbs)
        else:
            dx, dx_bf, dng[i][1], dlr, dli, dldt, db_re, db_im, dc_re, dc_im, dd = s5_bwd(
                sv['mix'], ng_full[i, 1], full[('s5_w_in', j)], lam_re, lam_im, log_dt, b_re, sd_full,
                full[('s5_w_out', j)], dx, state['fence'], hook(4 * i + 1))
            small.update(s5_lam_re=dlr, s5_lam_im=dli, s5_log_dt=dldt, s5_b_re=db_re, s5_b_im=db_im,
                         s5_c_re=dc_re, s5_c_im=dc_im, s5_d=dd)
        reduce_block(dx_bf)
        dx, dx_bf, dng[i][0] = ffn_bwd(sv['ffn_a'], ng_full[i, 0], *ffn_w(i, 0), dx, dx_bf, f"{i}a",
                                       state['fence'], hook(4 * i))
        reduce_block(dx_bf)
    grad_x = dx.reshape(x.shape)
    small['norm_g'] = jnp.stack([jnp.stack(r) for r in dng])
    small['final_norm_g'] = dgf

    grads, deltas, new_m, new_v = {}, {}, {}, {}
    carry = {}
    fence = state['fence']

    def update_big(key):
        n, lead = key
        carry[n] = adamw_into(W[n], lead, gshard[key], Mo[n], Vo[n], carry.get(n), f"adamw_{n}_{'_'.join(map(str, lead))}")
        return carry[n][1]

    for swap_handle, mats in swaps:
        gshard.update(dict(zip(mats, reduce_swap_wait(swap_handle, fence))))
        for key in mats:
            fence = update_big(key)
    small_names = [n for n in WEIGHT_NAMES if n not in BIG]
    flat = jnp.concatenate([small[n].astype(F32).reshape(-1) for n in small_names] + [loss_local.reshape(1)])
    pad = (-flat.size) % (256 * 128)
    flat = jnp.pad(flat, (0, pad)).reshape(-1, 128)
    tot = allreduce_small(flat, fence).reshape(-1)
    off = 0
    for n in small_names:
        sz = small[n].size
        gsum = tot[off:off + sz]
        off += sz
        if n == 'norm_g':
            gsum = lax.dynamic_slice_in_dim(gsum.reshape(depth, 4, D), s_idx * W[n].shape[-1], W[n].shape[-1], axis=2)
        elif n == 's5_d':
            gsum = lax.dynamic_slice_in_dim(gsum.reshape(1, D), s_idx * W[n].shape[-1], W[n].shape[-1], axis=1)
        grads[n] = gsum.reshape(W[n].shape)
        deltas[n], new_m[n], new_v[n] = adamw(W[n], grads[n], Mo[n], Vo[n], f"adamw_{n}")
        fence = deltas[n]
    loss = tot[off]

    last_handle, last_mats = ici_inflight.pop()
    swap_handle, token = reduce_finish(last_handle, fence)
    gshard.update(dict(zip(last_mats, reduce_swap_wait(swap_handle, token))))
    for key in last_mats:
        update_big(key)
    for n in BIG:
        grads[n], deltas[n], new_m[n], new_v[n] = carry[n]
    return (loss, grad_x, *[grads[n] for n in WEIGHT_NAMES], *[deltas[n] for n in WEIGHT_NAMES],
            *[new_m[n] for n in WEIGHT_NAMES], *[new_v[n] for n in WEIGHT_NAMES])
```

```python
import functools
import math

import jax
import jax.numpy as jnp
from jax import lax
from jax.experimental import pallas as pl
from jax.experimental.pallas import tpu as pltpu

F32 = jnp.float32
BF16 = jnp.bfloat16
MESH_ID = pl.DeviceIdType.MESH

EPS = 1e-6
ADAM_LR = 0.001
ADAM_B1 = 0.9
ADAM_B2 = 0.999
ADAM_EPS = 1e-08
ADAM_WD = 0.01
ADAM_STEP = 10

N_SHARD = 4
V7X_VMEM_LIMIT = 52 * 2 ** 20
SSM_TILE_GROUPS = 16
SCAN_LANES = 1024
GELU_C = math.sqrt(2.0 / math.pi)


def _pcall(body, **kw):
    return pl.pallas_call(body, **kw)


def _params():
    return pltpu.CompilerParams(vmem_limit_bytes=V7X_VMEM_LIMIT)


def _pick(n, cands):
    for c in cands:
        if c <= n and n % c == 0:
            return c
    return n


def _sigmoid(x):
    return 1.0 / (1.0 + jnp.exp(-x))


def _gelu(x):
    return 0.5 * x * (1.0 + jnp.tanh(GELU_C * (x + 0.044715 * x * x * x)))


def _gelu_grad(x):
    t = jnp.tanh(GELU_C * (x + 0.044715 * x * x * x))
    return 0.5 * (1.0 + t) + 0.5 * x * (1.0 - t * t) * GELU_C * (1.0 + 3.0 * 0.044715 * x * x)


def tilemap(fn, ins, outs, *, M, N, tm, tn, name, after=None):
    n_in = len(ins)
    n_dep = 0 if after is None else 1
    grid = (N // tn, M // tm)
    in_specs = []
    for arr, kind, off in ins:
        if kind == 't':
            in_specs.append(pl.BlockSpec((tm, tn), lambda j, i, off=off: (i, j + off)))
        else:
            in_specs.append(pl.BlockSpec((1, tn), lambda j, i, off=off: (0, j + off)))
    out_specs, out_shape = [], []
    for dt, kind in outs:
        if kind == 't':
            out_specs.append(pl.BlockSpec((tm, tn), lambda j, i: (i, j)))
            out_shape.append(jax.ShapeDtypeStruct((M, N), dt))
        else:
            out_specs.append(pl.BlockSpec((1, tn), lambda j, i: (0, j)))
            out_shape.append(jax.ShapeDtypeStruct((1, N), F32))
    in_specs += [pl.BlockSpec(memory_space=pl.ANY)] * n_dep

    def body(*refs):
        vals = fn(*[r[...] for r in refs[:n_in]])
        for (dt, kind), ref, v in zip(outs, refs[n_in + n_dep:], vals):
            if kind == 't':
                ref[...] = v.astype(ref.dtype)
            else:
                @pl.when(pl.program_id(1) == 0)
                def _():
                    ref[...] = jnp.zeros_like(ref)
                ref[...] += v

    res = _pcall(body, name=name, grid=grid, in_specs=in_specs, out_specs=out_specs,
                 out_shape=out_shape, compiler_params=_params())(*[a for a, _, _ in ins], *([after] * n_dep))
    return res


def _as2d(a):
    if a.ndim >= 2 and a.shape[-1] % 128 == 0:
        return a.reshape(-1, a.shape[-1])
    if a.size % 128 == 0:
        return a.reshape(-1, 128)
    return a.reshape(-1, a.shape[-1])


def _row_tile(rows, cols, nbytes=4, budget=1 << 20):
    cands = [c for c in (2048, 1024, 512, 256, 128, 64, 32, 16, 8) if c * cols * nbytes <= budget]
    return _pick(rows, cands) if cands else _pick(rows, (8,))


ROW_TILES = (256, 128, 64, 32, 16)


def _sc_call(body, sc, args, *, grid, in_specs, out_specs, out_shape, name):
    gs = pltpu.PrefetchScalarGridSpec(num_scalar_prefetch=1, grid=grid, in_specs=in_specs, out_specs=out_specs)
    return _pcall(body, name=name, grid_spec=gs, out_shape=out_shape, compiler_params=_params())(sc, *args)


def cast_into_full(w, lead, ax, sc, name, after):
    R, C = w.shape[-2:]
    tm = _pick(R, ROW_TILES)
    nb = R // tm
    lead = tuple(lead)
    in_spec = pl.BlockSpec((None,) * len(lead) + (tm, C), lambda i, s: lead + (i, 0))
    if ax == 0:
        shape, out_map = (R * N_SHARD, C), (lambda i, s: (i + s[0] * nb, 0))
    else:
        shape, out_map = (R, C * N_SHARD), (lambda i, s: (i, s[0]))

    def body(s_ref, w_ref, after_ref, o_ref):
        o_ref[...] = w_ref[...].astype(BF16)

    return _sc_call(body, sc, [w, after], grid=(nb,), in_specs=[in_spec, pl.BlockSpec(memory_space=pl.ANY)],
                    out_specs=pl.BlockSpec((tm, C), out_map), out_shape=jax.ShapeDtypeStruct(shape, BF16), name=name)


def pairsum(g, a, ax, sc, name):
    hR, hC = a.shape
    tm = _pick(hR, ROW_TILES)
    nb = hR // tm
    g_map = (lambda i, s: (i + s[1] * nb, 0)) if ax == 1 else (lambda i, s: (i, s[1]))
    blk = (tm, hC)

    def body(s_ref, g_ref, a_ref, o_ref):
        o_ref[...] = (g_ref[...].astype(F32) + a_ref[...].astype(F32)).astype(BF16)

    return _sc_call(body, sc, [g, a], grid=(nb,),
                    in_specs=[pl.BlockSpec(blk, g_map), pl.BlockSpec(blk, lambda i, s: (i, 0))],
                    out_specs=pl.BlockSpec(blk, lambda i, s: (i, 0)),
                    out_shape=jax.ShapeDtypeStruct((hR, hC), BF16), name=name)


def shardsum(b, cbuf, ax, sc, name):
    hR, hC = b.shape
    _, pR, pC = cbuf.shape
    tm = _pick(pR, ROW_TILES)
    nb = pR // tm
    if ax == 1:
        b_map, o_map, shape = (lambda i, s: (i, s[0])), (lambda i, s: (i + s[1] * nb, 0)), (2 * pR, pC)
    else:
        b_map, o_map, shape = (lambda i, s: (i + s[0] * nb, 0)), (lambda i, s: (i, s[1])), (pR, 2 * pC)

    def body(s_ref, b_ref, c_ref, o_ref):
        acc = b_ref[...].astype(F32)
        for k in range(N_SHARD - 1):
            acc = acc + c_ref[k].astype(F32)
        o_ref[...] = acc

    return _sc_call(body, sc, [b, cbuf], grid=(nb,),
                    in_specs=[pl.BlockSpec((tm, pC), b_map), pl.BlockSpec((N_SHARD - 1, tm, pC), lambda i, s: (0, i, 0))],
                    out_specs=pl.BlockSpec((tm, pC), o_map), out_shape=jax.ShapeDtypeStruct(shape, F32), name=name)


_DIMS = {'nn': (((1,), (0,)), ((), ())), 'nt': (((1,), (1,)), ((), ())), 'tn': (((0,), (0,)), ((), ()))}


def matmul(pairs, mode, outs, *, epilogue=None, extras=(), tm=512, tn=512, name, after=None):
    a0, b0 = pairs[0]
    if mode == 'nn':
        (M, K), N = a0.shape, b0.shape[1]
    elif mode == 'nt':
        (M, K), N = a0.shape, b0.shape[0]
    else:
        (K, M), N = a0.shape, b0.shape[1]
    tm, tn = _pick(M, (tm, 256, 128)), _pick(N, (tn, 256, 128))
    n_p, n_e = len(pairs), len(extras)
    if mode == 'tn':
        a_spec = pl.BlockSpec((K, tm), lambda i, j: (0, i))
    else:
        a_spec = pl.BlockSpec((tm, K), lambda i, j: (i, 0))
    if mode == 'nt':
        b_spec = pl.BlockSpec((tn, K), lambda i, j: (j, 0))
    else:
        b_spec = pl.BlockSpec((K, tn), lambda i, j: (0, j))
    in_specs, args = [], []
    for a, b in pairs:
        in_specs += [a_spec, b_spec]
        args += [a, b]
    for arr, kind in extras:
        if kind == 't':
            in_specs.append(pl.BlockSpec((tm, tn), lambda i, j: (i, j)))
        else:
            in_specs.append(pl.BlockSpec((1, tn), lambda i, j: (0, j)))
        args.append(arr)
    n_dep = 0 if after is None else 1
    in_specs += [pl.BlockSpec(memory_space=pl.ANY)] * n_dep
    args += [after] * n_dep
    dims = _DIMS[mode]

    def body(*refs):
        accs = [lax.dot_general(refs[2 * p][...].astype(BF16), refs[2 * p + 1][...].astype(BF16), dims,
                                preferred_element_type=F32) for p in range(n_p)]
        ex = [r[...] for r in refs[2 * n_p:2 * n_p + n_e]]
        if epilogue is None:
            acc = accs[0]
            for other in accs[1:]:
                acc = acc + other
            res = (acc,)
        else:
            res = epilogue(accs, ex)
        for ref, v in zip(refs[2 * n_p + n_e + n_dep:], res):
            ref[...] = v.astype(ref.dtype)

    return _pcall(body, name=name, grid=(M // tm, N // tn), in_specs=in_specs,
                  out_specs=[pl.BlockSpec((tm, tn), lambda i, j: (i, j)) for _ in outs],
                  out_shape=[jax.ShapeDtypeStruct((M, N), dt) for dt in outs],
                  compiler_params=_params())(*args)


def comm_call(name, ins, out_shapes, plan, n_local, n_remote, aliases=None, after=None):
    ins = list(ins) + ([] if after is None else [after])
    n_in, n_out = len(ins), len(out_shapes)

    def body(*refs):
        in_refs, out_refs = refs[:n_in], refs[n_in:n_in + n_out]
        lsem, ssem, rsem = refs[n_in + n_out:]
        me = (lax.axis_index("x"), lax.axis_index("y"), lax.axis_index("c"))
        local, remote = plan(me, in_refs, out_refs)
        assert len(local) == n_local and len(remote) == n_remote
        lcs = [pltpu.make_async_copy(s, d, lsem.at[k]) for k, (s, d) in enumerate(local)]
        rcs = [pltpu.make_async_remote_copy(src_ref=s, dst_ref=d, send_sem=ssem.at[k], recv_sem=rsem.at[k],
                                            device_id=peer, device_id_type=MESH_ID)
               for k, (s, d, peer) in enumerate(remote)]
        for cp in rcs:
            cp.start()
        for cp in lcs:
            cp.start()
        for cp in rcs:
            cp.wait()
        for cp in lcs:
            cp.wait()

    any_spec = pl.BlockSpec(memory_space=pl.ANY)
    return _pcall(body, name=name, in_specs=[any_spec] * n_in, out_specs=[any_spec] * n_out,
                  out_shape=list(out_shapes),
                  scratch_shapes=[pltpu.SemaphoreType.DMA((max(n_local, 1),)),
                                  pltpu.SemaphoreType.DMA((max(n_remote, 1),)),
                                  pltpu.SemaphoreType.DMA((max(n_remote, 1),))],
                  input_output_aliases=aliases or {},
                  compiler_params=pltpu.CompilerParams(has_side_effects=True))(*ins)


_HBM_SPEC = pl.BlockSpec(memory_space=pltpu.HBM)
_SEM_SPEC = pl.BlockSpec(memory_space=pltpu.SEMAPHORE)
_DATAFLOW = pltpu.SideEffectType.DATAFLOW_SIDE_EFFECTING


def split_start(name, arrays, n_copies, plan, after):
    n = len(arrays)

    def body(*refs):
        ins, (ssem, rsem) = refs[:n], refs[n + 1:n + 3]
        token = refs[-1]
        me = (lax.axis_index("x"), lax.axis_index("y"), lax.axis_index("c"))
        for k, (src, dst, peer) in enumerate(plan(me, ins)):
            pltpu.make_async_remote_copy(src_ref=src, dst_ref=dst, send_sem=ssem.at[k], recv_sem=rsem.at[k],
                                         device_id=peer, device_id_type=MESH_ID).start()
        token[...] = jnp.zeros_like(token)

    sems = pltpu.SemaphoreType.DMA((n_copies,))
    res = _pcall(body, name=name,
                 out_shape=(sems, sems, *[pltpu.HBM(a.shape, a.dtype) for a in arrays], jax.ShapeDtypeStruct((8, 128), F32)),
                 in_specs=[_HBM_SPEC] * n + [pl.BlockSpec(memory_space=pl.ANY)],
                 out_specs=(_SEM_SPEC, _SEM_SPEC, *[_HBM_SPEC] * n, pl.BlockSpec(memory_space=pltpu.VMEM)),
                 input_output_aliases={m: 2 + m for m in range(n)},
                 compiler_params=pltpu.CompilerParams(has_side_effects=_DATAFLOW))(
        *[pltpu.with_memory_space_constraint(a, pltpu.HBM) for a in arrays], after)
    return res[0], res[1], list(res[2:2 + n]), res[-1]


def split_wait(name, arrays, ssem, rsem, plan, after):
    n = len(arrays)

    def body(*refs):
        ins, (ssem_r, rsem_r) = refs[:n], refs[n:n + 2]
        me = (lax.axis_index("x"), lax.axis_index("y"), lax.axis_index("c"))
        for k, (src, dst, peer) in enumerate(plan(me, ins)):
            cp = pltpu.make_async_remote_copy(src_ref=src, dst_ref=dst, send_sem=ssem_r.at[k], recv_sem=rsem_r.at[k],
                                              device_id=peer, device_id_type=MESH_ID)
            cp.wait_send()
            cp.wait_recv()

    res = _pcall(body, name=name, out_shape=[pltpu.HBM(a.shape, a.dtype) for a in arrays],
                 in_specs=[_HBM_SPEC] * n + [_SEM_SPEC, _SEM_SPEC, pl.BlockSpec(memory_space=pl.ANY)],
                 out_specs=[_HBM_SPEC] * n, input_output_aliases={m: m for m in range(n)},
                 compiler_params=pltpu.CompilerParams(has_side_effects=_DATAFLOW))(*arrays, ssem, rsem, after)
    return list(res)


def _shard_of(me):
    return 2 * me[0] + me[1]


def _plane_peers(me):
    x, y, c = me
    return [((1 - x, y, c), 2 * (1 - x) + y), ((x, 1 - y, c), 2 * x + 1 - y),
            ((1 - x, 1 - y, c), 2 * (1 - x) + 1 - y)]


def _mats(arr):
    out = [()]
    for n in arr.shape[:-2]:
        out = [o + (k,) for o in out for k in range(n)]
    return out


ROW_ALIGN = 16
LANE_ALIGN = 128


def _win(ref, lead, rows, cols):
    idx = tuple(lead)
    for spec, align in ((rows, ROW_ALIGN), (cols, LANE_ALIGN)):
        if spec is None:
            idx += (slice(None),)
        else:
            start, size = spec
            if not isinstance(start, int):
                start = pl.multiple_of(start, align)
            idx += (pl.ds(start, size),)
    return ref.at[idx]


def gather_small(name, shards, after):
    full_shapes = [jax.ShapeDtypeStruct((a.shape[0], a.shape[1] * N_SHARD), a.dtype) for a in shards]
    n = len(shards)

    def plan(me, in_refs, out_refs):
        s = _shard_of(me)
        local, remote = [], []
        for t, a in enumerate(shards):
            dst = _win(out_refs[t], (), None, (s * a.shape[1], a.shape[1]))
            local.append((in_refs[t], dst))
            for peer, _ in _plane_peers(me):
                remote.append((in_refs[t], dst, peer))
        return local, remote

    return comm_call(name, shards, full_shapes, plan, n, 3 * n, after=after)


def _half_shape(g, ax):
    R, C = g.shape
    return (R // 2, C) if ax == 1 else (R, C // 2)


def reduce_pair_start(name, grads, axes, after):
    n = len(grads)
    landing = [lax.empty(_half_shape(g, ax), BF16) for g, ax in zip(grads, axes)]

    def plan_a(me, refs):
        x, y, c = me
        copies = []
        for m in range(n):
            R, C = grads[m].shape
            if axes[m] == 1:
                rows, cols = ((1 - c) * (R // 2), R // 2), None
            else:
                rows, cols = None, ((1 - c) * (C // 2), C // 2)
            copies.append((_win(refs[m], (), rows, cols), refs[n + m], (x, y, 1 - c)))
        return copies

    ssem, rsem, thru, token = split_start(name + "_pair_start", list(grads) + landing, n, plan_a, after)
    return (name, axes, plan_a, ssem, rsem, thru), token


def reduce_ici_start(handle, sc, after):
    name, axes, plan_a, ssem, rsem, thru = handle
    n = len(axes)
    done = split_wait(name + "_pair_wait", thru, ssem, rsem, plan_a, after)
    grads, a_bufs = done[:n], done[n:]
    b_bufs = [pairsum(g, a, ax, sc, f"{name}_pairsum{m}") for m, (g, a, ax) in enumerate(zip(grads, a_bufs, axes))]

    def piece_shape(m):
        R, C = a_bufs[m].shape
        return (R, C // N_SHARD) if axes[m] == 1 else (R // N_SHARD, C)

    def piece_win(m, s):
        R, C = piece_shape(m)
        if axes[m] == 1:
            return None, (s * C, C)
        return (s * R, R), None

    landing = [lax.empty((N_SHARD - 1,) + piece_shape(m), BF16) for m in range(n)]

    def plan_c(me, refs):
        copies = []
        for m in range(n):
            for j, (peer, ps) in enumerate(_plane_peers(me)):
                rows, cols = piece_win(m, ps)
                copies.append((_win(refs[m], (), rows, cols), refs[n + m].at[j], peer))
        return copies

    ssem, rsem, thru, token = split_start(name + "_ici_start", b_bufs + landing, 3 * n, plan_c, after)
    return (name, axes, sc, plan_c, ssem, rsem, thru), token


def reduce_finish(handle, after):
    name, axes, sc, plan_c, ssem, rsem, thru = handle
    n = len(axes)
    done = split_wait(name + "_ici_wait", thru, ssem, rsem, plan_c, after)
    b_bufs, c_bufs = done[:n], done[n:]
    shards = [shardsum(b, cb, ax, sc, f"{name}_shardsum{m}") for m, (b, cb, ax) in enumerate(zip(b_bufs, c_bufs, axes))]

    def plan_e(me, refs):
        x, y, c = me
        copies = []
        for m in range(n):
            R, C = shards[m].shape
            if axes[m] == 1:
                rows, cols = (c * (R // 2), R // 2), None
            else:
                rows, cols = None, (c * (C // 2), C // 2)
            copies.append((_win(refs[m], (), rows, cols), _win(refs[m], (), rows, cols), (x, y, 1 - c)))
        return copies

    ssem, rsem, thru, token = split_start(name + "_swap_start", shards, n, plan_e, after)
    return (name, plan_e, ssem, rsem, thru), token


def reduce_swap_wait(handle, after):
    name, plan_e, ssem, rsem, thru = handle
    return split_wait(name + "_swap_wait", thru, ssem, rsem, plan_e, after)


def gather_start(name, fulls, shard_shapes, axes, after):
    n = len(fulls)

    def win(ref, m, s, half):
        R, C = shard_shapes[m]
        r0 = s * R if axes[m] == 0 else 0
        return _win(ref, (), (r0 + half * (R // 2), R // 2), None if axes[m] == 0 else (s * C, C))

    def plan_ici(me, refs):
        s, c = _shard_of(me), me[2]
        return [(win(refs[m], m, s, c), win(refs[m], m, s, c), peer) for m in range(n) for peer, _ in _plane_peers(me)]

    def plan_fwd(me, refs):
        x, y, c = me
        return [(win(refs[m], m, ps, c), win(refs[m], m, ps, c), (x, y, 1 - c)) for m in range(n) for _, ps in _plane_peers(me)]

    ssem, rsem, thru, token = split_start(name + "_start", fulls, 3 * n, plan_ici, after)
    return (name, plan_ici, plan_fwd, ssem, rsem, thru), token


def gather_land(handle, after):
    name, plan_ici, plan_fwd, ssem, rsem, thru = handle
    got = split_wait(name + "_wait", thru, ssem, rsem, plan_ici, after)
    ssem, rsem, thru, token = split_start(name + "_fwd_start", got, 3 * len(got), plan_fwd, after)
    return (name, plan_fwd, ssem, rsem, thru), token


def gather_take(handle, after):
    name, plan_fwd, ssem, rsem, thru = handle
    return split_wait(name + "_fwd_wait", thru, ssem, rsem, plan_fwd, after)


def allreduce_small(flat, after):
    cur = flat
    R, C = flat.shape
    for axis, flip in enumerate(((0, 0, 1), (0, 1, 0), (1, 0, 0))):
        def plan(me, in_refs, out_refs, flip=flip):
            peer = tuple(v + f * (1 - 2 * v) for v, f in zip(me, flip))
            return [], [(in_refs[0], out_refs[0], peer)]

        (got,) = comm_call(f"small_swap{axis}", [cur], [jax.ShapeDtypeStruct((R, C), F32)], plan, 0, 1,
                           after=after if axis == 0 else None)
        (cur,) = tilemap(lambda a, b: (a + b,), [(cur, 't', 0), (got, 't', 0)], [(F32, 't')], M=R, N=C,
                         tm=_pick(R, (2048, 1024, 512, 256)), tn=C, name=f"small_add{axis}")
    return cur


def rms_fwd(x, g, name):
    M, D = x.shape
    tm = _pick(M, (256, 128))

    def fn(xv, gv):
        r = lax.rsqrt(jnp.mean(xv * xv, axis=-1, keepdims=True) + EPS)
        return (xv * r * gv,)

    (h,) = tilemap(fn, [(x, 't', 0), (g, 'r', 0)], [(BF16, 't')], M=M, N=D, tm=tm, tn=D, name=name)
    return h


def rms_bwd(x, g, dh, dres, name):
    M, D = x.shape
    tm = _pick(M, (256, 128))

    def fn(xv, gv, dhv, drv):
        r = lax.rsqrt(jnp.mean(xv * xv, axis=-1, keepdims=True) + EPS)
        xh = xv * r
        dxh = dhv * gv
        m = jnp.mean(dxh * xh, axis=-1, keepdims=True)
        dx = drv + r * (dxh - xh * m)
        return dx, dx, jnp.sum(dhv * xh, axis=0, keepdims=True)

    return tilemap(fn, [(x, 't', 0), (g, 'r', 0), (dh, 't', 0), (dres, 't', 0)],
                   [(F32, 't'), (BF16, 't'), (F32, 'a')], M=M, N=D, tm=tm, tn=D, name=name)


def ffn_fwd(x, g, wg, wu, wd, tag, prefetch):
    h = rms_fwd(x, g, f"ffn_norm_{tag}")

    def ep(accs, ex):
        a, b = accs
        return a, b, a * _sigmoid(a) * b

    a, b, s = matmul([(h, wg), (h, wu)], 'nn', [F32, F32, BF16], epilogue=ep, tm=1024, tn=512,
                     name=f"ffn_gateup_{tag}")
    (xo,) = matmul([(s, wd)], 'nn', [F32], epilogue=lambda accs, ex: (ex[0] + 0.5 * accs[0],),
                   extras=[(x, 't')], tm=512, tn=512, name=f"ffn_down_{tag}", after=prefetch(s))
    return xo, (x, h, a, b, s)


def ffn_bwd(saved, g, wg, wu, wd, dxo, dxo_bf, tag, after, on_grads):
    x, h, a, b, s = saved

    def ep(accs, ex):
        ds = 0.5 * accs[0]
        av, bv = ex
        sg = _sigmoid(av)
        return ds * bv * (sg * (1.0 + av * (1.0 - sg))), ds * (av * sg)

    da, db = matmul([(dxo_bf, wd)], 'nt', [BF16, BF16], epilogue=ep, extras=[(a, 't'), (b, 't')],
                    tm=1024, tn=512, name=f"ffn_dact_{tag}", after=after)
    (dwd,) = matmul([(s, dxo_bf)], 'tn', [BF16], epilogue=lambda accs, ex: (0.5 * accs[0],),
                    tm=512, tn=512, name=f"ffn_dwd_{tag}")
    dwg, dwu = matmul([(h, da), (h, db)], 'tn', [BF16, BF16], epilogue=lambda accs, ex: tuple(accs),
                      tm=1024, tn=512, name=f"ffn_dwgu_{tag}")
    started = on_grads([dwg, dwu, dwd])
    (dh,) = matmul([(da, wg), (db, wu)], 'nt', [F32], tm=512, tn=256, name=f"ffn_dh_{tag}", after=started)
    dx, dx_bf, dg = rms_bwd(x, g, dh, dxo, f"ffn_dnorm_{tag}")
    return dx, dx_bf, dg


def _tril_mask(n):
    return lax.broadcasted_iota(jnp.int32, (n, n), 0) >= lax.broadcasted_iota(jnp.int32, (n, n), 1)


def _gmlp_specs(L, half, n_grp, chunk):
    gd = half // n_grp
    specs = [pl.BlockSpec((chunk, gd), lambda g, n: (n, g)),
             pl.BlockSpec((chunk, gd), lambda g, n: (n, n_grp + g)),
             pl.BlockSpec((1, gd), lambda g, n: (0, g)),
             pl.BlockSpec((1, gd), lambda g, n: (0, g)),
             pl.BlockSpec((None, chunk, chunk), lambda g, n: (g, 0, 0)),
             pl.BlockSpec((None, chunk, 1), lambda g, n: (g, 0, 0))]
    return gd, specs


def _gmlp_gate_values(zu, zv, lg, lb, ws, bs):
    u, v = _gelu(zu), _gelu(zv)
    mu = jnp.mean(v, axis=-1, keepdims=True)
    d = v - mu
    rstd = lax.rsqrt(jnp.mean(d * d, axis=-1, keepdims=True) + EPS)
    vhat = d * rstd
    vn = vhat * lg + lb
    w = jnp.where(_tril_mask(ws.shape[0]), ws, 0.0).astype(BF16)
    sv = jnp.dot(w, vn.astype(BF16), preferred_element_type=F32) + bs
    return u, vhat, rstd, vn, w, sv


def gmlp_gate_fwd(zpre, ln_g, ln_b, w_s, b_s):
    L, half = zpre.shape[0], zpre.shape[1] // 2
    n_grp, chunk = w_s.shape[0], w_s.shape[1]
    gd, specs = _gmlp_specs(L, half, n_grp, chunk)

    def body(zu, zv, lg, lb, ws, bs, o):
        u, _, _, _, _, sv = _gmlp_gate_values(zu[...], zv[...], lg[...], lb[...], ws[...], bs[...])
        o[...] = (u * sv).astype(o.dtype)

    return _pcall(body, name="gmlp_gate", grid=(n_grp, L // chunk), in_specs=specs,
                  out_specs=pl.BlockSpec((chunk, gd), lambda g, n: (n, g)),
                  out_shape=jax.ShapeDtypeStruct((L, half), BF16), compiler_params=_params())(
        zpre, zpre, ln_g, ln_b, w_s, b_s)


def gmlp_gate_bwd(zpre, ln_g, ln_b, w_s, b_s, dgated):
    L, half = zpre.shape[0], zpre.shape[1] // 2
    n_grp, chunk = w_s.shape[0], w_s.shape[1]
    gd, specs = _gmlp_specs(L, half, n_grp, chunk)
    specs = specs + [pl.BlockSpec((chunk, gd), lambda g, n: (n, g))]

    def body(zu, zv, lg, lb, ws, bs, dg, dzu, dzv, dws, dbs, dlg, dlb):
        zuv, zvv, lgv = zu[...], zv[...], lg[...]
        u, vhat, rstd, vn, w, sv = _gmlp_gate_values(zuv, zvv, lgv, lb[...], ws[...], bs[...])
        dgv = dg[...]
        du = dgv * sv
        dsv = dgv * u
        dsv_bf = dsv.astype(BF16)
        dw = lax.dot_general(dsv_bf, vn.astype(BF16), _DIMS['nt'], preferred_element_type=F32)
        dvn = lax.dot_general(w, dsv_bf, _DIMS['tn'], preferred_element_type=F32)
        dvhat = dvn * lgv
        dv = rstd * (dvhat - jnp.mean(dvhat, axis=-1, keepdims=True)
                     - vhat * jnp.mean(dvhat * vhat, axis=-1, keepdims=True))
        dzu[...] = (du * _gelu_grad(zuv)).astype(dzu.dtype)
        dzv[...] = (dv * _gelu_grad(zvv)).astype(dzv.dtype)

        @pl.when(pl.program_id(1) == 0)
        def _():
            dws[...] = jnp.zeros_like(dws)
            dbs[...] = jnp.zeros_like(dbs)
            dlg[...] = jnp.zeros_like(dlg)
            dlb[...] = jnp.zeros_like(dlb)

        dws[...] += jnp.where(_tril_mask(chunk), dw, 0.0)
        dbs[...] += jnp.sum(dsv, axis=1, keepdims=True)
        dlg[...] += jnp.sum(dvn * vhat, axis=0, keepdims=True)
        dlb[...] += jnp.sum(dvn, axis=0, keepdims=True)

    tile = pl.BlockSpec((chunk, gd), lambda g, n: (n, g))
    vec = pl.BlockSpec((1, gd), lambda g, n: (0, g))
    return _pcall(body, name="gmlp_gate_bwd", grid=(n_grp, L // chunk), in_specs=specs,
                  out_specs=[tile, tile, pl.BlockSpec((None, chunk, chunk), lambda g, n: (g, 0, 0)),
                             pl.BlockSpec((None, chunk, 1), lambda g, n: (g, 0, 0)), vec, vec],
                  out_shape=[jax.ShapeDtypeStruct((L, half), BF16), jax.ShapeDtypeStruct((L, half), BF16),
                             jax.ShapeDtypeStruct((n_grp, chunk, chunk), F32),
                             jax.ShapeDtypeStruct((n_grp, chunk, 1), F32),
                             jax.ShapeDtypeStruct((1, half), F32), jax.ShapeDtypeStruct((1, half), F32)],
                  compiler_params=_params())(zpre, zpre, ln_g, ln_b, w_s, b_s, dgated)


def gmlp_fwd(x, g, w_in, ln_g, ln_b, w_s, b_s, w_out, prefetch):
    h = rms_fwd(x, g, "gmlp_norm")
    (zpre,) = matmul([(h, w_in)], 'nn', [F32], tm=1024, tn=512, name="gmlp_in")
    gated = gmlp_gate_fwd(zpre, ln_g, ln_b, w_s, b_s)
    (xo,) = matmul([(gated, w_out)], 'nn', [F32], epilogue=lambda accs, ex: (ex[0] + accs[0],),
                   extras=[(x, 't')], tm=512, tn=512, name="gmlp_out", after=prefetch(gated))
    return xo, (x, h, zpre, gated)


def gmlp_bwd(saved, g, w_in, ln_g, ln_b, w_s, b_s, w_out, dxo, dxo_bf, after, on_grads):
    x, h, zpre, gated = saved
    (dgated,) = matmul([(dxo_bf, w_out)], 'nt', [F32], tm=512, tn=512, name="gmlp_dgated", after=after)
    (dw_out,) = matmul([(gated, dxo_bf)], 'tn', [BF16], tm=512, tn=512, name="gmlp_dwout")
    dzu, dzv, dws, dbs, dlg, dlb = gmlp_gate_bwd(zpre, ln_g, ln_b, w_s, b_s, dgated)
    dz = jnp.concatenate([dzu, dzv], axis=1)
    (dw_in,) = matmul([(h, dz)], 'tn', [BF16], tm=1024, tn=512, name="gmlp_dwin")
    started = on_grads([dw_in, dw_out])
    (dh,) = matmul([(dz, w_in)], 'nt', [F32], tm=256, tn=256, name="gmlp_dh", after=started)
    dx, dx_bf, dg = rms_bwd(x, g, dh, dxo, "gmlp_dnorm")
    return dx, dx_bf, dg, dlg, dlb, dws, dbs


def _s5_disc(lr, li, ldt, br, bi):
    dt = jnp.exp(ldt)
    mag = jnp.exp(lr * dt)
    ang = li * dt
    ar = mag * jnp.cos(ang)
    ai = mag * jnp.sin(ang)
    den = lr * lr + li * li
    nr = ar - 1.0
    zr = (nr * lr + ai * li) / den
    zi = (ai * lr - nr * li) / den
    return ar, ai, zr[None] * br - zi[None] * bi, zr[None] * bi + zi[None] * br


def s5_disc_fwd(lr, li, ldt, br, bi):
    def body(lr_r, li_r, ldt_r, br_r, bi_r, ar_o, ai_o, bbr_o, bbi_o):
        res = _s5_disc(lr_r[...], li_r[...], ldt_r[...], br_r[...], bi_r[...])
        for o, v in zip((ar_o, ai_o, bbr_o, bbi_o), res):
            o[...] = v

    shp = lambda a: jax.ShapeDtypeStruct(a.shape, F32)
    return _pcall(body, name="s5_disc", out_shape=[shp(lr), shp(lr), shp(br), shp(br)],
                  compiler_params=_params())(lr, li, ldt, br, bi)


def s5_disc_bwd(lr, li, ldt, br, bi, dar, dai, dbbr, dbbi):
    def body(lr_r, li_r, ldt_r, br_r, bi_r, dar_r, dai_r, dbbr_r, dbbi_r, o1, o2, o3, o4, o5):
        _, vjp = jax.vjp(_s5_disc, lr_r[...], li_r[...], ldt_r[...], br_r[...], bi_r[...])
        res = vjp((dar_r[...], dai_r[...], dbbr_r[...], dbbi_r[...]))
        for o, v in zip((o1, o2, o3, o4, o5), res):
            o[...] = v

    shp = lambda a: jax.ShapeDtypeStruct(a.shape, F32)
    return _pcall(body, name="s5_disc_bwd", out_shape=[shp(lr), shp(lr), shp(ldt), shp(br), shp(br)],
                  compiler_params=_params())(lr, li, ldt, br, bi, dar, dai, dbbr, dbbi)


def blockdiag_matmul(pairs, outs, *, epilogue=None, extras=(), name):
    a0, b0 = pairs[0]
    M = a0.shape[0]
    T, wa, wo = b0.shape
    tm = _pick(M, (512, 256, 128))
    n_p, n_e = len(pairs), len(extras)
    in_specs, args = [], []
    for a, b in pairs:
        in_specs += [pl.BlockSpec((tm, wa), lambda k, i: (i, k)), pl.BlockSpec((None, wa, wo), lambda k, i: (k, 0, 0))]
        args += [a, b]
    for arr, kind in extras:
        in_specs.append(pl.BlockSpec((tm, wo), lambda k, i: (i, k)) if kind == 't'
                        else pl.BlockSpec((1, wo), lambda k, i: (0, k)))
        args.append(arr)

    def body(*refs):
        accs = [jnp.dot(refs[2 * p][...].astype(BF16), refs[2 * p + 1][...], preferred_element_type=F32)
                for p in range(n_p)]
        ex = [r[...] for r in refs[2 * n_p:2 * n_p + n_e]]
        res = tuple(accs) if epilogue is None else epilogue(accs, ex)
        for ref, v in zip(refs[2 * n_p + n_e:], res):
            ref[...] = v.astype(ref.dtype)

    return _pcall(body, name=name, grid=(T, M // tm), in_specs=in_specs,
                  out_specs=[pl.BlockSpec((tm, wo), lambda k, i: (i, k)) for _ in outs],
                  out_shape=[jax.ShapeDtypeStruct((M, T * wo), dt) for dt in outs],
                  compiler_params=_params())(*args)


def blockdiag_outer(pairs, name):
    M = pairs[0][0].shape[0]
    n_p = len(pairs)
    shapes = []
    in_specs, args = [], []
    tm = _pick(M, (512, 256, 128))
    T = None
    for a, b, wa, wb in pairs:
        T = a.shape[1] // wa
        in_specs += [pl.BlockSpec((tm, wa), lambda k, i: (i, k)), pl.BlockSpec((tm, wb), lambda k, i: (i, k))]
        args += [a, b]
        shapes.append((T, wa, wb))

    def body(*refs):
        @pl.when(pl.program_id(1) == 0)
        def _():
            for o in refs[2 * n_p:]:
                o[...] = jnp.zeros_like(o)
        for p in range(n_p):
            refs[2 * n_p + p][...] += lax.dot_general(refs[2 * p][...].astype(BF16), refs[2 * p + 1][...].astype(BF16),
                                                      _DIMS['tn'], preferred_element_type=F32)

    return _pcall(body, name=name, grid=(T, M // tm), in_specs=in_specs,
                  out_specs=[pl.BlockSpec((None, s[1], s[2]), lambda k, i: (k, 0, 0)) for s in shapes],
                  out_shape=[jax.ShapeDtypeStruct(s, F32) for s in shapes], compiler_params=_params())(*args)


def s5_scan(br, bi, ar, ai, reverse, want_prev, name):
    L, S = br.shape
    ln = _pick(S, (SCAN_LANES, 512, 256, 128))
    tb = _pick(L, (512, 256, 128))
    n_t = L // tb
    n_q = tb // 8

    def tmap(j, t):
        return ((n_t - 1 - t) if reverse else t, j)

    blk = pl.BlockSpec((tb, ln), tmap)
    vec = pl.BlockSpec((1, ln), lambda j, t: (0, j))

    def cmul(xr, xi, yr, yi):
        return xr * yr - xi * yi, xr * yi + xi * yr

    n_out = 4 if want_prev else 2

    def body(br_r, bi_r, ar_r, ai_r, *rest):
        outs, (cr_s, ci_s) = rest[:n_out], rest[n_out:]

        @pl.when(pl.program_id(1) == 0)
        def _():
            cr_s[...] = jnp.zeros_like(cr_s)
            ci_s[...] = jnp.zeros_like(ci_s)

        a1r, a1i = ar_r[...], ai_r[...]
        a2r, a2i = cmul(a1r, a1i, a1r, a1i)
        a4r, a4i = cmul(a2r, a2i, a2r, a2i)
        a8r, a8i = cmul(a4r, a4i, a4r, a4i)
        row = lax.broadcasted_iota(jnp.int32, (8, ln), 0)
        dist = (7 - row) if reverse else row
        pwr, pwi = jnp.broadcast_to(a1r, (8, ln)), jnp.broadcast_to(a1i, (8, ln))
        for bit, (er, ei) in ((1, (a1r, a1i)), (2, (a2r, a2i)), (4, (a4r, a4i))):
            nr, ni = cmul(pwr, pwi, er, ei)
            sel = (dist & bit) != 0
            pwr, pwi = jnp.where(sel, nr, pwr), jnp.where(sel, ni, pwi)
        last = 0 if reverse else 7
        steps = [(d, jnp.where(dist >= d, er, 0.0), jnp.where(dist >= d, ei, 0.0))
                 for d, (er, ei) in ((1, (a1r, a1i)), (2, (a2r, a2i)), (4, (a4r, a4i)))]

        def step(q, carry):
            cr, ci = carry
            qq = (n_q - 1 - q) if reverse else q
            rows = pl.ds(pl.multiple_of(qq * 8, 8), 8)
            xr, xi = br_r[rows, :], bi_r[rows, :]
            for d, er, ei in steps:
                sr = pltpu.roll(xr, (8 - d) if reverse else d, 0)
                si = pltpu.roll(xi, (8 - d) if reverse else d, 0)
                mr, mi = cmul(sr, si, er, ei)
                xr, xi = xr + mr, xi + mi
            lr, li = xr[last:last + 1, :], xi[last:last + 1, :]
            kr, ki = cmul(pwr, pwi, cr, ci)
            xr, xi = xr + kr, xi + ki
            outs[0][rows, :] = xr
            outs[1][rows, :] = xi
            if want_prev:
                outs[2][rows, :] = jnp.where(dist >= 1, pltpu.roll(xr, 7 if reverse else 1, 0), cr)
                outs[3][rows, :] = jnp.where(dist >= 1, pltpu.roll(xi, 7 if reverse else 1, 0), ci)
            nr, ni = cmul(a8r, a8i, cr, ci)
            return lr + nr, li + ni

        cr, ci = lax.fori_loop(0, n_q, step, (cr_s[...], ci_s[...]), unroll=2)
        cr_s[...] = cr
        ci_s[...] = ci

    shp = jax.ShapeDtypeStruct((L, S), F32)
    return _pcall(body, name=name, grid=(S // ln, n_t), in_specs=[blk, blk, vec, vec],
                  out_specs=[blk] * n_out, out_shape=[shp] * n_out,
                  scratch_shapes=[pltpu.VMEM((1, ln), F32), pltpu.VMEM((1, ln), F32)],
                  compiler_params=_params())(br, bi, ar, ai)


def _to_blockdiag(m, tile_groups):
    G, A, B = m.shape
    T = G // tile_groups
    eye = jnp.eye(tile_groups, dtype=m.dtype)
    t = m.reshape(T, tile_groups, A, 1, B) * eye[None, :, None, :, None]
    return t.reshape(T, tile_groups * A, tile_groups * B)


def _from_blockdiag(t, tile_groups):
    T, RA, RB = t.shape
    A, B = RA // tile_groups, RB // tile_groups
    d = jnp.diagonal(t.reshape(T, tile_groups, A, tile_groups, B), axis1=1, axis2=3)
    return jnp.moveaxis(d, 3, 1).reshape(T * tile_groups, A, B)


def s5_fwd(x, g, w_in, lam_re, lam_im, log_dt, b_re, b_im, c_re, c_im, d_skip, w_out, prefetch):
    G, P, H = b_re.shape
    tg = min(SSM_TILE_GROUPS, G)
    h = rms_fwd(x, g, "s5_norm")
    (u,) = matmul([(h, w_in)], 'nn', [F32], tm=512, tn=512, name="s5_in")
    br_t, bi_t = jnp.transpose(b_re, (2, 0, 1)), jnp.transpose(b_im, (2, 0, 1))
    ar, ai, bbr, bbi = s5_disc_fwd(lam_re, lam_im, log_dt, br_t, bi_t)
    bbr_g, bbi_g = jnp.transpose(bbr, (1, 0, 2)), jnp.transpose(bbi, (1, 0, 2))
    bd_br, bd_bi = _to_blockdiag(bbr_g.astype(BF16), tg), _to_blockdiag(bbi_g.astype(BF16), tg)
    bur, bui = blockdiag_matmul([(u, bd_br), (u, bd_bi)], [F32, F32], name="s5_bu")
    a_r, a_i = ar.reshape(1, G * P), ai.reshape(1, G * P)
    hr, hi, hpr, hpi = s5_scan(bur, bui, a_r, a_i, False, True, "s5_scan")
    c_pg_r = jnp.transpose(c_re, (0, 2, 1)).astype(BF16)
    c_pg_i = jnp.transpose(c_im, (0, 2, 1)).astype(BF16)
    bd_cr, bd_nci = _to_blockdiag(c_pg_r, tg), _to_blockdiag(-c_pg_i, tg)

    def ep(accs, ex):
        y = accs[0] + accs[1] + ex[1] * ex[0]
        return y, _gelu(y)

    y, act = blockdiag_matmul([(hr, bd_cr), (hi, bd_nci)], [F32, BF16], epilogue=ep,
                              extras=[(u, 't'), (d_skip, 'r')], name="s5_y")
    (o,) = matmul([(act, w_out)], 'nn', [F32], tm=512, tn=512, name="s5_out", after=prefetch(act))
    M, D = x.shape
    tm = _pick(M, (256, 128))
    (xo,) = tilemap(lambda xv, val, gt: (xv + val * _sigmoid(gt),), [(x, 't', 0), (o, 't', 0), (o, 't', 1)],
                    [(F32, 't')], M=M, N=D, tm=tm, tn=D, name="s5_glu")
    saved = (x, h, u, hr, hi, hpr, hpi, y, act, o, a_r, a_i, bd_br, bd_bi, bd_cr, bd_nci, br_t, bi_t)
    return xo, saved


def s5_bwd(saved, g, w_in, lam_re, lam_im, log_dt, b_re, d_skip, w_out, dxo, after, on_grads):
    x, h, u, hr, hi, hpr, hpi, y, act, o, a_r, a_i, bd_br, bd_bi, bd_cr, bd_nci, br_t, bi_t = saved
    G, P, H = b_re.shape
    tg = min(SSM_TILE_GROUPS, G)
    M, D = x.shape
    tm = _pick(M, (256, 128))

    def glu_bwd(dv, val, gt):
        sg = _sigmoid(gt)
        return dv * sg, dv * val * sg * (1.0 - sg)

    dval, dgate = tilemap(glu_bwd, [(dxo, 't', 0), (o, 't', 0), (o, 't', 1)], [(BF16, 't'), (BF16, 't')],
                          M=M, N=D, tm=tm, tn=D, name="s5_dglu", after=after)
    do = jnp.concatenate([dval, dgate], axis=1)
    (dw_out,) = matmul([(act, do)], 'tn', [BF16], tm=512, tn=512, name="s5_dwout")
    (dact,) = matmul([(do, w_out)], 'nt', [F32], tm=512, tn=512, name="s5_dact")
    dy, dd = tilemap(lambda da, yv, uv: (da * _gelu_grad(yv), jnp.sum(da * _gelu_grad(yv) * uv, axis=0, keepdims=True)),
                     [(dact, 't', 0), (y, 't', 0), (u, 't', 0)], [(F32, 't'), (F32, 'a')],
                     M=M, N=D, tm=tm, tn=D, name="s5_dy")
    bd_crT, bd_nciT = jnp.transpose(bd_cr, (0, 2, 1)), jnp.transpose(bd_nci, (0, 2, 1))
    dhr, dhi = blockdiag_matmul([(dy, bd_crT), (dy, bd_nciT)], [F32, F32], name="s5_dh")
    gr, gi = s5_scan(dhr, dhi, a_r, -a_i, True, False, "s5_scan_rev")
    S = G * P
    tms = _pick(M, (128,))

    def da_fn(grv, giv, hprv, hpiv):
        return (jnp.sum(grv * hprv + giv * hpiv, axis=0, keepdims=True),
                jnp.sum(giv * hprv - grv * hpiv, axis=0, keepdims=True))

    dar, dai = tilemap(da_fn, [(gr, 't', 0), (gi, 't', 0), (hpr, 't', 0), (hpi, 't', 0)], [(F32, 'a'), (F32, 'a')],
                       M=M, N=S, tm=tms, tn=_pick(S, (2048, 1024, 512)), name="s5_dabar")
    wa, wb = tg * H, tg * P
    xc_r, xc_i, xb_r, xb_i = blockdiag_outer([(dy, hr, wa, wb), (dy, hi, wa, wb), (u, gr, wa, wb), (u, gi, wa, wb)],
                                             "s5_dcb")
    dc_re = _from_blockdiag(xc_r, tg)
    dc_im = -_from_blockdiag(xc_i, tg)
    dbb_r = jnp.transpose(_from_blockdiag(xb_r, tg), (1, 0, 2))
    dbb_i = jnp.transpose(_from_blockdiag(xb_i, tg), (1, 0, 2))
    dlr, dli, dldt, dbr_t, dbi_t = s5_disc_bwd(lam_re, lam_im, log_dt, br_t, bi_t,
                                               dar.reshape(G, P), dai.reshape(G, P), dbb_r, dbb_i)
    db_re, db_im = jnp.transpose(dbr_t, (1, 2, 0)), jnp.transpose(dbi_t, (1, 2, 0))
    bd_brT, bd_biT = jnp.transpose(bd_br, (0, 2, 1)), jnp.transpose(bd_bi, (0, 2, 1))
    (du,) = blockdiag_matmul([(gr, bd_brT), (gi, bd_biT)], [BF16],
                             epilogue=lambda accs, ex: (accs[0] + accs[1] + ex[1] * ex[0],),
                             extras=[(dy, 't'), (d_skip, 'r')], name="s5_du")
    (dw_in,) = matmul([(h, du)], 'tn', [BF16], tm=512, tn=512, name="s5_dwin")
    started = on_grads([dw_in, dw_out])
    (dh,) = matmul([(du, w_in)], 'nt', [F32], tm=512, tn=512, name="s5_dhin", after=started)
    dx, dx_bf, dg = rms_bwd(x, g, dh, dxo, "s5_dnorm")
    return dx, dx_bf, dg, dlr, dli, dldt, db_re, db_im, dc_re, dc_im, dd


def ple_fwd(x, g, p_emb, w_gate, w_proj, tag, prefetch):
    h = rms_fwd(x, g, f"ple_norm_{tag}")
    (q,) = matmul([(p_emb, w_proj)], 'nn', [F32], tm=512, tn=512, name=f"ple_proj_{tag}")

    def ep(accs, ex):
        gt = _sigmoid(accs[0])
        return ex[0] + gt * ex[1], gt

    xo, gate = matmul([(h, w_gate)], 'nn', [F32, F32], epilogue=ep, extras=[(x, 't'), (q, 't')],
                      tm=512, tn=512, name=f"ple_gate_{tag}", after=prefetch(q))
    return xo, (x, h, q, gate)


def ple_bwd(saved, g, p_emb, w_gate, dxo, tag, after, on_grads):
    x, h, q, gate = saved
    M, D = x.shape
    tm = _pick(M, (256, 128))
    dq, dpre = tilemap(lambda dv, qv, gv: (dv * gv, dv * qv * gv * (1.0 - gv)),
                       [(dxo, 't', 0), (q, 't', 0), (gate, 't', 0)], [(BF16, 't'), (BF16, 't')],
                       M=M, N=D, tm=tm, tn=D, name=f"ple_dgate_{tag}", after=after)
    (dw_proj,) = matmul([(p_emb, dq)], 'tn', [BF16], tm=256, tn=512, name=f"ple_dwproj_{tag}")
    (dw_gate,) = matmul([(h, dpre)], 'tn', [BF16], tm=512, tn=512, name=f"ple_dwgate_{tag}")
    started = on_grads([dw_gate, dw_proj])
    (dh,) = matmul([(dpre, w_gate)], 'nt', [F32], tm=512, tn=512, name=f"ple_dh_{tag}", after=started)
    dx, dx_bf, dg = rms_bwd(x, g, dh, dxo, f"ple_dnorm_{tag}")
    return dx, dx_bf, dg


def loss_head(x, g, target):
    M, D = x.shape
    tm = _pick(M, (256, 128))

    def fn(xv, gv, tv):
        r = lax.rsqrt(jnp.mean(xv * xv, axis=-1, keepdims=True) + EPS)
        xh = xv * r
        e = xh * gv - tv
        dy = e * (1.0 / D)
        dxh = dy * gv
        m = jnp.mean(dxh * xh, axis=-1, keepdims=True)
        dx = r * (dxh - xh * m)
        return jnp.sum(e * e, axis=0, keepdims=True), dx, dx, jnp.sum(dy * xh, axis=0, keepdims=True)

    return tilemap(fn, [(x, 't', 0), (g, 'r', 0), (target, 't', 0)],
                   [(F32, 'a'), (F32, 't'), (BF16, 't'), (F32, 'a')], M=M, N=D, tm=tm, tn=D, name="loss_head")


def _adamw_math(wv, gv, mv, vv):
    mn = ADAM_B1 * mv + (1.0 - ADAM_B1) * gv
    vn = ADAM_B2 * vv + (1.0 - ADAM_B2) * (gv * gv)
    m_hat = mn / (1.0 - ADAM_B1 ** ADAM_STEP)
    v_hat = vn / (1.0 - ADAM_B2 ** ADAM_STEP)
    return -ADAM_LR * (m_hat / (jnp.sqrt(v_hat) + ADAM_EPS) + ADAM_WD * wv), mn, vn


def adamw_into(w, lead, g, m, v, carry, name):
    R, C = w.shape[-2:]
    lead = tuple(lead)
    tm = _row_tile(R, C)
    blk = pl.BlockSpec((None,) * len(lead) + (tm, C), lambda i: lead + (i, 0))
    n_carry = 0 if carry is None else 4

    def body(*refs):
        w_r, g_r, m_r, v_r = refs[:4]
        g_o, d_o, m_o, v_o = refs[4 + n_carry:]
        gv = g_r[...]
        d, mn, vn = _adamw_math(w_r[...], gv, m_r[...], v_r[...])
        g_o[...] = gv
        d_o[...] = d
        m_o[...] = mn
        v_o[...] = vn

    shp = jax.ShapeDtypeStruct(w.shape, F32)
    return _pcall(body, name=name, grid=(R // tm,),
                  in_specs=[blk, pl.BlockSpec((tm, C), lambda i: (i, 0)), blk, blk] + [pl.BlockSpec(memory_space=pl.ANY)] * n_carry,
                  out_specs=[blk] * 4, out_shape=[shp] * 4,
                  input_output_aliases={4 + k: k for k in range(n_carry)},
                  compiler_params=_params())(w, g, m, v, *(carry or ()))


def adamw(w, g, m, v, name):
    w2, g2, m2, v2 = _as2d(w), g.reshape(_as2d(w).shape), _as2d(m), _as2d(v)
    R, C = w2.shape
    tm = _row_tile(R, C)

    d, mn, vn = tilemap(_adamw_math, [(w2, 't', 0), (g2, 't', 0), (m2, 't', 0), (v2, 't', 0)],
                        [(F32, 't'), (F32, 't'), (F32, 't')], M=R, N=C, tm=tm, tn=C, name=name)
    return d.reshape(w.shape), mn.reshape(w.shape), vn.reshape(w.shape)


WEIGHT_NAMES = ['norm_g', 'final_norm_g', 'ffn_w_gate', 'ffn_w_up', 'ffn_w_down', 'gmlp_w_in', 'gmlp_ln_g',
                'gmlp_ln_b', 'gmlp_w_s', 'gmlp_b_s', 'gmlp_w_out', 's5_w_in', 's5_lam_re', 's5_lam_im',
                's5_log_dt', 's5_b_re', 's5_b_im', 's5_c_re', 's5_c_im', 's5_d', 's5_w_out', 'ple_w_gate',
                'ple_w_proj']
BIG = {'ffn_w_gate': 1, 'ffn_w_up': 1, 'ffn_w_down': 0, 'gmlp_w_in': 1, 'gmlp_w_out': 0, 's5_w_in': 0,
       's5_w_out': 1, 'ple_w_gate': 0, 'ple_w_proj': 1}


def _blocks(depth):
    out = []
    for i in range(depth):
        for k, half in enumerate("ab"):
            ffn = [(n, (i, k)) for n in ('ffn_w_gate', 'ffn_w_up', 'ffn_w_down')]
            if k == 1:
                out.append((f"ffn{i}b", ffn))
                out.append((f"ple{i}", [('ple_w_gate', (i,)), ('ple_w_proj', (i,))]))
            else:
                out.append((f"ffn{i}a", ffn))
                mix = 'gmlp' if i % 2 == 0 else 's5'
                out.append((f"{mix}{i}", [(f'{mix}_w_in', (i // 2,)), (f'{mix}_w_out', (i // 2,))]))
    return out


def kernel(x, p, norm_g, final_norm_g, ffn_w_gate, ffn_w_up, ffn_w_down, gmlp_w_in, gmlp_ln_g, gmlp_ln_b, gmlp_w_s, gmlp_b_s, gmlp_w_out, s5_w_in, s5_lam_re, s5_lam_im, s5_log_dt, s5_b_re, s5_b_im, s5_c_re, s5_c_im, s5_d, s5_w_out, ple_w_gate, ple_w_proj, loss_target, m_norm_g, m_final_norm_g, m_ffn_w_gate, m_ffn_w_up, m_ffn_w_down, m_gmlp_w_in, m_gmlp_ln_g, m_gmlp_ln_b, m_gmlp_w_s, m_gmlp_b_s, m_gmlp_w_out, m_s5_w_in, m_s5_lam_re, m_s5_lam_im, m_s5_log_dt, m_s5_b_re, m_s5_b_im, m_s5_c_re, m_s5_c_im, m_s5_d, m_s5_w_out, m_ple_w_gate, m_ple_w_proj, v_norm_g, v_final_norm_g, v_ffn_w_gate, v_ffn_w_up, v_ffn_w_down, v_gmlp_w_in, v_gmlp_ln_g, v_gmlp_ln_b, v_gmlp_w_s, v_gmlp_b_s, v_gmlp_w_out, v_s5_w_in, v_s5_lam_re, v_s5_lam_im, v_s5_log_dt, v_s5_b_re, v_s5_b_im, v_s5_c_re, v_s5_c_im, v_s5_d, v_s5_w_out, v_ple_w_gate, v_ple_w_proj):
    env = dict(locals())
    W = {n: env[n] for n in WEIGHT_NAMES}
    Mo = {n: env["m_" + n] for n in WEIGHT_NAMES}
    Vo = {n: env["v_" + n] for n in WEIGHT_NAMES}
    depth = norm_g.shape[0]
    L, D = x.shape[1], x.shape[2]
    s_idx = 2 * lax.axis_index("x") + lax.axis_index("y")

    sc = jnp.stack([s_idx, lax.axis_index("c")]).astype(jnp.int32)
    blocks = _blocks(depth)

    ng2 = norm_g.reshape(depth * 4, norm_g.shape[-1])
    ng_full, sd_full = gather_small("gather_small", [ng2, s5_d], sc)
    full = {}
    gathers = []
    token = sd_full
    for bname, mats in blocks:
        casts = [cast_into_full(W[n], lead, BIG[n], sc, f"cast_{bname}_{n}", token) for n, lead in mats]
        handle, token = gather_start(f"gather_{bname}", casts, [W[n].shape[-2:] for n, _ in mats],
                                     [BIG[n] for n, _ in mats], token)
        gathers.append(handle)
    all_started = token
    ng_full = ng_full.reshape(depth, 4, 1, D)

    landed = {}

    def prefetcher(bi):
        def prefetch(value):
            if bi >= len(blocks):
                return value
            landed[bi], token = gather_land(gathers[bi], value)
            return token
        return prefetch

    def fetch(bi, after):
        if bi not in landed:
            landed[bi], after = gather_land(gathers[bi], after)
        full.update(dict(zip(blocks[bi][1], gather_take(landed[bi], after))))
    gf = final_norm_g.reshape(1, D)

    G, P, H = s5_b_re.shape[1:]
    n_grp, chunk = gmlp_w_s.shape[1], gmlp_w_s.shape[2]
    lam_re, lam_im = s5_lam_re[0], s5_lam_im[0]
    log_dt = s5_log_dt.reshape(G, 1)
    b_re, b_im, c_re, c_im = s5_b_re[0], s5_b_im[0], s5_c_re[0], s5_c_im[0]
    w_s, b_s = gmlp_w_s[0], gmlp_b_s[0].reshape(n_grp, chunk, 1)
    xs = x.reshape(L, D)
    saved = []
    def ffn_w(i, k):
        return [full[(n, (i, k))] for n in ('ffn_w_gate', 'ffn_w_up', 'ffn_w_down')]

    for i in range(depth):
        sv = {}
        fetch(4 * i, all_started if i == 0 else xs)
        xs, sv['ffn_a'] = ffn_fwd(xs, ng_full[i, 0], *ffn_w(i, 0), f"{i}a", prefetcher(4 * i + 1))
        j = (i // 2,)
        fetch(4 * i + 1, xs)
        if i % 2 == 0:
            xs, sv['mix'] = gmlp_fwd(xs, ng_full[i, 1], full[('gmlp_w_in', j)], gmlp_ln_g, gmlp_ln_b, w_s, b_s,
                                     full[('gmlp_w_out', j)], prefetcher(4 * i + 2))
        else:
            xs, sv['mix'] = s5_fwd(xs, ng_full[i, 1], full[('s5_w_in', j)], lam_re, lam_im, log_dt, b_re, b_im,
                                   c_re, c_im, sd_full, full[('s5_w_out', j)], prefetcher(4 * i + 2))
        fetch(4 * i + 2, xs)
        xs, sv['ffn_b'] = ffn_fwd(xs, ng_full[i, 2], *ffn_w(i, 1), f"{i}b", prefetcher(4 * i + 3))
        fetch(4 * i + 3, xs)
        xs, sv['ple'] = ple_fwd(xs, ng_full[i, 3], p[i, 0], full[('ple_w_gate', (i,))], full[('ple_w_proj', (i,))], f"{i}",
                                prefetcher(4 * i + 4))
        saved.append(sv)

    sq, dx, dx_bf, dgf = loss_head(xs, gf, loss_target.reshape(L, D))
    loss_local = 0.5 * jnp.sum(sq) / D
    dng = [[None] * 4 for _ in range(depth)]
    small = {}
    gshard = {}
    state = {'fence': sc}
    ici_inflight = []
    swaps = []

    def hook(bi):
        def on_grads(gs):
            bname, mats = blocks[bi]
            handle, token = reduce_pair_start(f"reduce_{bname}", gs, [BIG[n] for n, _ in mats], state['fence'])
            state['pair'] = (handle, mats)
            return token
        return on_grads

    def reduce_block(after, keep=1):
        handle, mats = state.pop('pair')
        ici_handle, fence = reduce_ici_start(handle, sc, after)
        ici_inflight.append((ici_handle, mats))
        while len(ici_inflight) > keep:
            prev_handle, prev_mats = ici_inflight.pop(0)
            swap_handle, fence = reduce_finish(prev_handle, fence)
            swaps.append((swap_handle, prev_mats))
        state['fence'] = fence

    for i in reversed(range(depth)):
        sv = saved[i]
        dx, dx_bf, dng[i][3] = ple_bwd(sv['ple'], ng_full[i, 3], p[i, 0], full[('ple_w_gate', (i,))], dx, f"{i}",
                                       state['fence'], hook(4 * i + 3))
        reduce_block(dx_bf, keep=2)
        dx, dx_bf, dng[i][2] = ffn_bwd(sv['ffn_b'], ng_full[i, 2], *ffn_w(i, 1), dx, dx_bf, f"{i}b",
                                       state['fence'], hook(4 * i + 2))
        reduce_block(dx_bf)
        j = (i // 2,)
        if i % 2 == 0:
            dx, dx_bf, dng[i][1], dlg, dlb, dws, dbs = gmlp_bwd(
                sv['mix'], ng_full[i, 1], full[('gmlp_w_in', j)], gmlp_ln_g, gmlp_ln_b, w_s, b_s,
                full[('gmlp_w_out', j)], dx, dx_bf, state['fence'], hook(4 * i + 1))
            small.update(gmlp_ln_g=dlg, gmlp_ln_b=dlb, gmlp_w_s=dws, gmlp_b_s=dbs)
        else:
            dx, dx_bf, dng[i][1], dlr, dli, dldt, db_re, db_im, dc_re, dc_im, dd = s5_bwd(
                sv['mix'], ng_full[i, 1], full[('s5_w_in', j)], lam_re, lam_im, log_dt, b_re, sd_full,
                full[('s5_w_out', j)], dx, state['fence'], hook(4 * i + 1))
            small.update(s5_lam_re=dlr, s5_lam_im=dli, s5_log_dt=dldt, s5_b_re=db_re, s5_b_im=db_im,
                         s5_c_re=dc_re, s5_c_im=dc_im, s5_d=dd)
        reduce_block(dx_bf)
        dx, dx_bf, dng[i][0] = ffn_bwd(sv['ffn_a'], ng_full[i, 0], *ffn_w(i, 0), dx, dx_bf, f"{i}a",
                                       state['fence'], hook(4 * i))
        reduce_block(dx_bf)
    grad_x = dx.reshape(x.shape)
    small['norm_g'] = jnp.stack([jnp.stack(r) for r in dng])
    small['final_norm_g'] = dgf

    grads, deltas, new_m, new_v = {}, {}, {}, {}
    carry = {}
    fence = state['fence']

    def update_big(key):
        n, lead = key
        carry[n] = adamw_into(W[n], lead, gshard[key], Mo[n], Vo[n], carry.get(n), f"adamw_{n}_{'_'.join(map(str, lead))}")
        return carry[n][1]

    for swap_handle, mats in swaps:
        gshard.update(dict(zip(mats, reduce_swap_wait(swap_handle, fence))))
        for key in mats:
            fence = update_big(key)
    small_names = [n for n in WEIGHT_NAMES if n not in BIG]
    flat = jnp.concatenate([small[n].astype(F32).reshape(-1) for n in small_names] + [loss_local.reshape(1)])
    pad = (-flat.size) % (256 * 128)
    flat = jnp.pad(flat, (0, pad)).reshape(-1, 128)
    tot = allreduce_small(flat, fence).reshape(-1)
    off = 0
    for n in small_names:
        sz = small[n].size
        gsum = tot[off:off + sz]
        off += sz
        if n == 'norm_g':
            gsum = lax.dynamic_slice_in_dim(gsum.reshape(depth, 4, D), s_idx * W[n].shape[-1], W[n].shape[-1], axis=2)
        elif n == 's5_d':
            gsum = lax.dynamic_slice_in_dim(gsum.reshape(1, D), s_idx * W[n].shape[-1], W[n].shape[-1], axis=1)
        grads[n] = gsum.reshape(W[n].shape)
        deltas[n], new_m[n], new_v[n] = adamw(W[n], grads[n], Mo[n], Vo[n], f"adamw_{n}")
        fence = deltas[n]
    loss = tot[off]

    last_handle, last_mats = ici_inflight.pop()
    swap_handle, token = reduce_finish(last_handle, fence)
    gshard.update(dict(zip(last_mats, reduce_swap_wait(swap_handle, token))))
    for key in last_mats:
        update_big(key)
    for n in BIG:
        grads[n], deltas[n], new_m[n], new_v[n] = carry[n]
    return (loss, grad_x, *[grads[n] for n in WEIGHT_NAMES], *[deltas[n] for n in WEIGHT_NAMES],
            *[new_m[n] for n in WEIGHT_NAMES], *[new_v[n] for n in WEIGHT_NAMES])
```

```python
import functools
import math

import jax
import jax.numpy as jnp
from jax import lax
from jax.experimental import pallas as pl
from jax.experimental.pallas import tpu as pltpu

F32 = jnp.float32
BF16 = jnp.bfloat16
MESH_ID = pl.DeviceIdType.MESH

EPS = 1e-6
ADAM_LR = 0.001
ADAM_B1 = 0.9
ADAM_B2 = 0.999
ADAM_EPS = 1e-08
ADAM_WD = 0.01
ADAM_STEP = 10

N_SHARD = 4
V7X_VMEM_LIMIT = 52 * 2 ** 20
SSM_TILE_GROUPS = 16
SCAN_LANES = 1024
GELU_C = math.sqrt(2.0 / math.pi)


def _pcall(body, **kw):
    return pl.pallas_call(body, **kw)


def _params():
    return pltpu.CompilerParams(vmem_limit_bytes=V7X_VMEM_LIMIT)


def _pick(n, cands):
    for c in cands:
        if c <= n and n % c == 0:
            return c
    return n


def _sigmoid(x):
    return 1.0 / (1.0 + jnp.exp(-x))


def _gelu(x):
    return 0.5 * x * (1.0 + jnp.tanh(GELU_C * (x + 0.044715 * x * x * x)))


def _gelu_grad(x):
    t = jnp.tanh(GELU_C * (x + 0.044715 * x * x * x))
    return 0.5 * (1.0 + t) + 0.5 * x * (1.0 - t * t) * GELU_C * (1.0 + 3.0 * 0.044715 * x * x)


def tilemap(fn, ins, outs, *, M, N, tm, tn, name, after=None):
    n_in = len(ins)
    n_dep = 0 if after is None else 1
    grid = (N // tn, M // tm)
    in_specs = []
    for arr, kind, off in ins:
        if kind == 't':
            in_specs.append(pl.BlockSpec((tm, tn), lambda j, i, off=off: (i, j + off)))
        else:
            in_specs.append(pl.BlockSpec((1, tn), lambda j, i, off=off: (0, j + off)))
    out_specs, out_shape = [], []
    for dt, kind in outs:
        if kind == 't':
            out_specs.append(pl.BlockSpec((tm, tn), lambda j, i: (i, j)))
            out_shape.append(jax.ShapeDtypeStruct((M, N), dt))
        else:
            out_specs.append(pl.BlockSpec((1, tn), lambda j, i: (0, j)))
            out_shape.append(jax.ShapeDtypeStruct((1, N), F32))
    in_specs += [pl.BlockSpec(memory_space=pl.ANY)] * n_dep

    def body(*refs):
        vals = fn(*[r[...] for r in refs[:n_in]])
        for (dt, kind), ref, v in zip(outs, refs[n_in + n_dep:], vals):
            if kind == 't':
                ref[...] = v.astype(ref.dtype)
            else:
                @pl.when(pl.program_id(1) == 0)
                def _():
                    ref[...] = jnp.zeros_like(ref)
                ref[...] += v

    res = _pcall(body, name=name, grid=grid, in_specs=in_specs, out_specs=out_specs,
                 out_shape=out_shape, compiler_params=_params())(*[a for a, _, _ in ins], *([after] * n_dep))
    return res


def _as2d(a):
    if a.ndim >= 2 and a.shape[-1] % 128 == 0:
        return a.reshape(-1, a.shape[-1])
    if a.size % 128 == 0:
        return a.reshape(-1, 128)
    return a.reshape(-1, a.shape[-1])


def _row_tile(rows, cols, nbytes=4, budget=1 << 20):
    cands = [c for c in (2048, 1024, 512, 256, 128, 64, 32, 16, 8) if c * cols * nbytes <= budget]
    return _pick(rows, cands) if cands else _pick(rows, (8,))


ROW_TILES = (256, 128, 64, 32, 16)


def _sc_call(body, sc, args, *, grid, in_specs, out_specs, out_shape, name):
    gs = pltpu.PrefetchScalarGridSpec(num_scalar_prefetch=1, grid=grid, in_specs=in_specs, out_specs=out_specs)
    return _pcall(body, name=name, grid_spec=gs, out_shape=out_shape, compiler_params=_params())(sc, *args)


def cast_into_full(w, lead, ax, sc, name, after):
    R, C = w.shape[-2:]
    tm = _pick(R, ROW_TILES)
    nb = R // tm
    lead = tuple(lead)
    in_spec = pl.BlockSpec((None,) * len(lead) + (tm, C), lambda i, s: lead + (i, 0))
    if ax == 0:
        shape, out_map = (R * N_SHARD, C), (lambda i, s: (i + s[0] * nb, 0))
    else:
        shape, out_map = (R, C * N_SHARD), (lambda i, s: (i, s[0]))

    def body(s_ref, w_ref, after_ref, o_ref):
        o_ref[...] = w_ref[...].astype(BF16)

    return _sc_call(body, sc, [w, after], grid=(nb,), in_specs=[in_spec, pl.BlockSpec(memory_space=pl.ANY)],
                    out_specs=pl.BlockSpec((tm, C), out_map), out_shape=jax.ShapeDtypeStruct(shape, BF16), name=name)


def pairsum(g, a, ax, sc, name):
    hR, hC = a.shape
    tm = _pick(hR, ROW_TILES)
    nb = hR // tm
    g_map = (lambda i, s: (i + s[1] * nb, 0)) if ax == 1 else (lambda i, s: (i, s[1]))
    blk = (tm, hC)

    def body(s_ref, g_ref, a_ref, o_ref):
        o_ref[...] = (g_ref[...].astype(F32) + a_ref[...].astype(F32)).astype(BF16)

    return _sc_call(body, sc, [g, a], grid=(nb,),
                    in_specs=[pl.BlockSpec(blk, g_map), pl.BlockSpec(blk, lambda i, s: (i, 0))],
                    out_specs=pl.BlockSpec(blk, lambda i, s: (i, 0)),
                    out_shape=jax.ShapeDtypeStruct((hR, hC), BF16), name=name)


def shardsum(b, cbuf, ax, sc, name):
    hR, hC = b.shape
    _, pR, pC = cbuf.shape
    tm = _pick(pR, ROW_TILES)
    nb = pR // tm
    if ax == 1:
        b_map, o_map, shape = (lambda i, s: (i, s[0])), (lambda i, s: (i + s[1] * nb, 0)), (2 * pR, pC)
    else:
        b_map, o_map, shape = (lambda i, s: (i + s[0] * nb, 0)), (lambda i, s: (i, s[1])), (pR, 2 * pC)

    def body(s_ref, b_ref, c_ref, o_ref):
        acc = b_ref[...].astype(F32)
        for k in range(N_SHARD - 1):
            acc = acc + c_ref[k].astype(F32)
        o_ref[...] = acc

    return _sc_call(body, sc, [b, cbuf], grid=(nb,),
                    in_specs=[pl.BlockSpec((tm, pC), b_map), pl.BlockSpec((N_SHARD - 1, tm, pC), lambda i, s: (0, i, 0))],
                    out_specs=pl.BlockSpec((tm, pC), o_map), out_shape=jax.ShapeDtypeStruct(shape, F32), name=name)


_DIMS = {'nn': (((1,), (0,)), ((), ())), 'nt': (((1,), (1,)), ((), ())), 'tn': (((0,), (0,)), ((), ()))}


def matmul(pairs, mode, outs, *, epilogue=None, extras=(), tm=512, tn=512, name, after=None):
    a0, b0 = pairs[0]
    if mode == 'nn':
        (M, K), N = a0.shape, b0.shape[1]
    elif mode == 'nt':
        (M, K), N = a0.shape, b0.shape[0]
    else:
        (K, M), N = a0.shape, b0.shape[1]
    tm, tn = _pick(M, (tm, 256, 128)), _pick(N, (tn, 256, 128))
    n_p, n_e = len(pairs), len(extras)
    if mode == 'tn':
        a_spec = pl.BlockSpec((K, tm), lambda i, j: (0, i))
    else:
        a_spec = pl.BlockSpec((tm, K), lambda i, j: (i, 0))
    if mode == 'nt':
        b_spec = pl.BlockSpec((tn, K), lambda i, j: (j, 0))
    else:
        b_spec = pl.BlockSpec((K, tn), lambda i, j: (0, j))
    in_specs, args = [], []
    for a, b in pairs:
        in_specs += [a_spec, b_spec]
        args += [a, b]
    for arr, kind in extras:
        if kind == 't':
            in_specs.append(pl.BlockSpec((tm, tn), lambda i, j: (i, j)))
        else:
            in_specs.append(pl.BlockSpec((1, tn), lambda i, j: (0, j)))
        args.append(arr)
    n_dep = 0 if after is None else 1
    in_specs += [pl.BlockSpec(memory_space=pl.ANY)] * n_dep
    args += [after] * n_dep
    dims = _DIMS[mode]

    def body(*refs):
        accs = [lax.dot_general(refs[2 * p][...].astype(BF16), refs[2 * p + 1][...].astype(BF16), dims,
                                preferred_element_type=F32) for p in range(n_p)]
        ex = [r[...] for r in refs[2 * n_p:2 * n_p + n_e]]
        if epilogue is None:
            acc = accs[0]
            for other in accs[1:]:
                acc = acc + other
            res = (acc,)
        else:
            res = epilogue(accs, ex)
        for ref, v in zip(refs[2 * n_p + n_e + n_dep:], res):
            ref[...] = v.astype(ref.dtype)

    return _pcall(body, name=name, grid=(M // tm, N // tn), in_specs=in_specs,
                  out_specs=[pl.BlockSpec((tm, tn), lambda i, j: (i, j)) for _ in outs],
                  out_shape=[jax.ShapeDtypeStruct((M, N), dt) for dt in outs],
                  compiler_params=_params())(*args)


def comm_call(name, ins, out_shapes, plan, n_local, n_remote, aliases=None, after=None):
    ins = list(ins) + ([] if after is None else [after])
    n_in, n_out = len(ins), len(out_shapes)

    def body(*refs):
        in_refs, out_refs = refs[:n_in], refs[n_in:n_in + n_out]
        lsem, ssem, rsem = refs[n_in + n_out:]
        me = (lax.axis_index("x"), lax.axis_index("y"), lax.axis_index("c"))
        local, remote = plan(me, in_refs, out_refs)
        assert len(local) == n_local and len(remote) == n_remote
        lcs = [pltpu.make_async_copy(s, d, lsem.at[k]) for k, (s, d) in enumerate(local)]
        rcs = [pltpu.make_async_remote_copy(src_ref=s, dst_ref=d, send_sem=ssem.at[k], recv_sem=rsem.at[k],
                                            device_id=peer, device_id_type=MESH_ID)
               for k, (s, d, peer) in enumerate(remote)]
        for cp in rcs:
            cp.start()
        for cp in lcs:
            cp.start()
        for cp in rcs:
            cp.wait()
        for cp in lcs:
            cp.wait()

    any_spec = pl.BlockSpec(memory_space=pl.ANY)
    return _pcall(body, name=name, in_specs=[any_spec] * n_in, out_specs=[any_spec] * n_out,
                  out_shape=list(out_shapes),
                  scratch_shapes=[pltpu.SemaphoreType.DMA((max(n_local, 1),)),
                                  pltpu.SemaphoreType.DMA((max(n_remote, 1),)),
                                  pltpu.SemaphoreType.DMA((max(n_remote, 1),))],
                  input_output_aliases=aliases or {},
                  compiler_params=pltpu.CompilerParams(has_side_effects=True))(*ins)


_HBM_SPEC = pl.BlockSpec(memory_space=pltpu.HBM)
_SEM_SPEC = pl.BlockSpec(memory_space=pltpu.SEMAPHORE)
_DATAFLOW = pltpu.SideEffectType.DATAFLOW_SIDE_EFFECTING


def split_start(name, arrays, n_copies, plan, after):
    n = len(arrays)

    def body(*refs):
        ins, (ssem, rsem) = refs[:n], refs[n + 1:n + 3]
        token = refs[-1]
        me = (lax.axis_index("x"), lax.axis_index("y"), lax.axis_index("c"))
        for k, (src, dst, peer) in enumerate(plan(me, ins)):
            pltpu.make_async_remote_copy(src_ref=src, dst_ref=dst, send_sem=ssem.at[k], recv_sem=rsem.at[k],
                                         device_id=peer, device_id_type=MESH_ID).start()
        token[...] = jnp.zeros_like(token)

    sems = pltpu.SemaphoreType.DMA((n_copies,))
    res = _pcall(body, name=name,
                 out_shape=(sems, sems, *[pltpu.HBM(a.shape, a.dtype) for a in arrays], jax.ShapeDtypeStruct((8, 128), F32)),
                 in_specs=[_HBM_SPEC] * n + [pl.BlockSpec(memory_space=pl.ANY)],
                 out_specs=(_SEM_SPEC, _SEM_SPEC, *[_HBM_SPEC] * n, pl.BlockSpec(memory_space=pltpu.VMEM)),
                 input_output_aliases={m: 2 + m for m in range(n)},
                 compiler_params=pltpu.CompilerParams(has_side_effects=_DATAFLOW))(
        *[pltpu.with_memory_space_constraint(a, pltpu.HBM) for a in arrays], after)
    return res[0], res[1], list(res[2:2 + n]), res[-1]


def split_wait(name, arrays, ssem, rsem, plan, after):
    n = len(arrays)

    def body(*refs):
        ins, (ssem_r, rsem_r) = refs[:n], refs[n:n + 2]
        me = (lax.axis_index("x"), lax.axis_index("y"), lax.axis_index("c"))
        for k, (src, dst, peer) in enumerate(plan(me, ins)):
            cp = pltpu.make_async_remote_copy(src_ref=src, dst_ref=dst, send_sem=ssem_r.at[k], recv_sem=rsem_r.at[k],
                                              device_id=peer, device_id_type=MESH_ID)
            cp.wait_send()
            cp.wait_recv()

    res = _pcall(body, name=name, out_shape=[pltpu.HBM(a.shape, a.dtype) for a in arrays],
                 in_specs=[_HBM_SPEC] * n + [_SEM_SPEC, _SEM_SPEC, pl.BlockSpec(memory_space=pl.ANY)],
                 out_specs=[_HBM_SPEC] * n, input_output_aliases={m: m for m in range(n)},
                 compiler_params=pltpu.CompilerParams(has_side_effects=_DATAFLOW))(*arrays, ssem, rsem, after)
    return list(res)


def _shard_of(me):
    return 2 * me[0] + me[1]


def _plane_peers(me):
    x, y, c = me
    return [((1 - x, y, c), 2 * (1 - x) + y), ((x, 1 - y, c), 2 * x + 1 - y),
            ((1 - x, 1 - y, c), 2 * (1 - x) + 1 - y)]


def _mats(arr):
    out = [()]
    for n in arr.shape[:-2]:
        out = [o + (k,) for o in out for k in range(n)]
    return out


ROW_ALIGN = 16
LANE_ALIGN = 128


def _win(ref, lead, rows, cols):
    idx = tuple(lead)
    for spec, align in ((rows, ROW_ALIGN), (cols, LANE_ALIGN)):
        if spec is None:
            idx += (slice(None),)
        else:
            start, size = spec
            if not isinstance(start, int):
                start = pl.multiple_of(start, align)
            idx += (pl.ds(start, size),)
    return ref.at[idx]


def gather_small(name, shards, after):
    full_shapes = [jax.ShapeDtypeStruct((a.shape[0], a.shape[1] * N_SHARD), a.dtype) for a in shards]
    n = len(shards)

    def plan(me, in_refs, out_refs):
        s = _shard_of(me)
        local, remote = [], []
        for t, a in enumerate(shards):
            dst = _win(out_refs[t], (), None, (s * a.shape[1], a.shape[1]))
            local.append((in_refs[t], dst))
            for peer, _ in _plane_peers(me):
                remote.append((in_refs[t], dst, peer))
        return local, remote

    return comm_call(name, shards, full_shapes, plan, n, 3 * n, after=after)


def _half_shape(g, ax):
    R, C = g.shape
    return (R // 2, C) if ax == 1 else (R, C // 2)


def reduce_pair_start(name, grads, axes, after):
    n = len(grads)
    landing = [lax.empty(_half_shape(g, ax), BF16) for g, ax in zip(grads, axes)]

    def plan_a(me, refs):
        x, y, c = me
        copies = []
        for m in range(n):
            R, C = grads[m].shape
            if axes[m] == 1:
                rows, cols = ((1 - c) * (R // 2), R // 2), None
            else:
                rows, cols = None, ((1 - c) * (C // 2), C // 2)
            copies.append((_win(refs[m], (), rows, cols), refs[n + m], (x, y, 1 - c)))
        return copies

    ssem, rsem, thru, token = split_start(name + "_pair_start", list(grads) + landing, n, plan_a, after)
    return (name, axes, plan_a, ssem, rsem, thru), token


def reduce_ici_start(handle, sc, after):
    name, axes, plan_a, ssem, rsem, thru = handle
    n = len(axes)
    done = split_wait(name + "_pair_wait", thru, ssem, rsem, plan_a, after)
    grads, a_bufs = done[:n], done[n:]
    b_bufs = [pairsum(g, a, ax, sc, f"{name}_pairsum{m}") for m, (g, a, ax) in enumerate(zip(grads, a_bufs, axes))]

    def piece_shape(m):
        R, C = a_bufs[m].shape
        return (R, C // N_SHARD) if axes[m] == 1 else (R // N_SHARD, C)

    def piece_win(m, s):
        R, C = piece_shape(m)
        if axes[m] == 1:
            return None, (s * C, C)
        return (s * R, R), None

    landing = [lax.empty((N_SHARD - 1,) + piece_shape(m), BF16) for m in range(n)]

    def plan_c(me, refs):
        copies = []
        for m in range(n):
            for j, (peer, ps) in enumerate(_plane_peers(me)):
                rows, cols = piece_win(m, ps)
                copies.append((_win(refs[m], (), rows, cols), refs[n + m].at[j], peer))
        return copies

    ssem, rsem, thru, token = split_start(name + "_ici_start", b_bufs + landing, 3 * n, plan_c, after)
    return (name, axes, sc, plan_c, ssem, rsem, thru), token


def reduce_finish(handle, after):
    name, axes, sc, plan_c, ssem, rsem, thru = handle
    n = len(axes)
    done = split_wait(name + "_ici_wait", thru, ssem, rsem, plan_c, after)
    b_bufs, c_bufs = done[:n], done[n:]
    shards = [shardsum(b, cb, ax, sc, f"{name}_shardsum{m}") for m, (b, cb, ax) in enumerate(zip(b_bufs, c_bufs, axes))]

    def plan_e(me, refs):
        x, y, c = me
        copies = []
        for m in range(n):
            R, C = shards[m].shape
            if axes[m] == 1:
                rows, cols = (c * (R // 2), R // 2), None
            else:
                rows, cols = None, (c * (C // 2), C // 2)
            copies.append((_win(refs[m], (), rows, cols), _win(refs[m], (), rows, cols), (x, y, 1 - c)))
        return copies

    ssem, rsem, thru, token = split_start(name + "_swap_start", shards, n, plan_e, after)
    return (name, plan_e, ssem, rsem, thru), token


def reduce_swap_wait(handle, after):
    name, plan_e, ssem, rsem, thru = handle
    return split_wait(name + "_swap_wait", thru, ssem, rsem, plan_e, after)


def gather_start(name, fulls, shard_shapes, axes, after):
    n = len(fulls)

    def win(ref, m, s, half):
        R, C = shard_shapes[m]
        r0 = s * R if axes[m] == 0 else 0
        return _win(ref, (), (r0 + half * (R // 2), R // 2), None if axes[m] == 0 else (s * C, C))

    def plan_ici(me, refs):
        s, c = _shard_of(me), me[2]
        return [(win(refs[m], m, s, c), win(refs[m], m, s, c), peer) for m in range(n) for peer, _ in _plane_peers(me)]

    def plan_fwd(me, refs):
        x, y, c = me
        return [(win(refs[m], m, ps, c), win(refs[m], m, ps, c), (x, y, 1 - c)) for m in range(n) for _, ps in _plane_peers(me)]

    ssem, rsem, thru, token = split_start(name + "_start", fulls, 3 * n, plan_ici, after)
    return (name, plan_ici, plan_fwd, ssem, rsem, thru), token


def gather_land(handle, after):
    name, plan_ici, plan_fwd, ssem, rsem, thru = handle
    got = split_wait(name + "_wait", thru, ssem, rsem, plan_ici, after)
    ssem, rsem, thru, token = split_start(name + "_fwd_start", got, 3 * len(got), plan_fwd, after)
    return (name, plan_fwd, ssem, rsem, thru), token


def gather_take(handle, after):
    name, plan_fwd, ssem, rsem, thru = handle
    return split_wait(name + "_fwd_wait", thru, ssem, rsem, plan_fwd, after)


SMALL_FLIPS = ((0, 0, 1), (0, 1, 0), (1, 0, 0))


def small_swap_start(axis, cur, after):
    def plan(me, refs):
        peer = tuple(v + f * (1 - 2 * v) for v, f in zip(me, SMALL_FLIPS[axis]))
        return [(refs[0], refs[1], peer)]

    ssem, rsem, thru, token = split_start(f"small_swap{axis}_start", [cur, lax.empty(cur.shape, cur.dtype)], 1, plan, after)
    return (axis, plan, ssem, rsem, thru), token


def small_swap_finish(handle, after):
    axis, plan, ssem, rsem, thru = handle
    cur, got = split_wait(f"small_swap{axis}_wait", thru, ssem, rsem, plan, after)
    R, C = cur.shape
    (cur,) = tilemap(lambda a, b: (a + b,), [(cur, 't', 0), (got, 't', 0)], [(F32, 't')], M=R, N=C,
                     tm=_pick(R, (2048, 1024, 512, 256)), tn=C, name=f"small_add{axis}")
    return cur


def rms_fwd(x, g, name):
    M, D = x.shape
    tm = _pick(M, (256, 128))

    def fn(xv, gv):
        r = lax.rsqrt(jnp.mean(xv * xv, axis=-1, keepdims=True) + EPS)
        return (xv * r * gv,)

    (h,) = tilemap(fn, [(x, 't', 0), (g, 'r', 0)], [(BF16, 't')], M=M, N=D, tm=tm, tn=D, name=name)
    return h


def rms_bwd(x, g, dh, dres, name):
    M, D = x.shape
    tm = _pick(M, (256, 128))

    def fn(xv, gv, dhv, drv):
        r = lax.rsqrt(jnp.mean(xv * xv, axis=-1, keepdims=True) + EPS)
        xh = xv * r
        dxh = dhv * gv
        m = jnp.mean(dxh * xh, axis=-1, keepdims=True)
        dx = drv + r * (dxh - xh * m)
        return dx, dx, jnp.sum(dhv * xh, axis=0, keepdims=True)

    return tilemap(fn, [(x, 't', 0), (g, 'r', 0), (dh, 't', 0), (dres, 't', 0)],
                   [(F32, 't'), (BF16, 't'), (F32, 'a')], M=M, N=D, tm=tm, tn=D, name=name)


def ffn_fwd(x, g, wg, wu, wd, tag, prefetch):
    h = rms_fwd(x, g, f"ffn_norm_{tag}")

    def ep(accs, ex):
        a, b = accs
        return a, b, a * _sigmoid(a) * b

    a, b, s = matmul([(h, wg), (h, wu)], 'nn', [F32, F32, BF16], epilogue=ep, tm=1024, tn=512,
                     name=f"ffn_gateup_{tag}")
    (xo,) = matmul([(s, wd)], 'nn', [F32], epilogue=lambda accs, ex: (ex[0] + 0.5 * accs[0],),
                   extras=[(x, 't')], tm=512, tn=512, name=f"ffn_down_{tag}", after=prefetch(s))
    return xo, (x, h, a, b, s)


def ffn_bwd(saved, g, wg, wu, wd, dxo, dxo_bf, tag, after, on_grads):
    x, h, a, b, s = saved

    def ep(accs, ex):
        ds = 0.5 * accs[0]
        av, bv = ex
        sg = _sigmoid(av)
        return ds * bv * (sg * (1.0 + av * (1.0 - sg))), ds * (av * sg)

    da, db = matmul([(dxo_bf, wd)], 'nt', [BF16, BF16], epilogue=ep, extras=[(a, 't'), (b, 't')],
                    tm=1024, tn=512, name=f"ffn_dact_{tag}", after=after)
    (dwd,) = matmul([(s, dxo_bf)], 'tn', [BF16], epilogue=lambda accs, ex: (0.5 * accs[0],),
                    tm=512, tn=512, name=f"ffn_dwd_{tag}")
    dwg, dwu = matmul([(h, da), (h, db)], 'tn', [BF16, BF16], epilogue=lambda accs, ex: tuple(accs),
                      tm=1024, tn=512, name=f"ffn_dwgu_{tag}")
    started = on_grads([dwg, dwu, dwd])
    (dh,) = matmul([(da, wg), (db, wu)], 'nt', [F32], tm=512, tn=256, name=f"ffn_dh_{tag}", after=started)
    dx, dx_bf, dg = rms_bwd(x, g, dh, dxo, f"ffn_dnorm_{tag}")
    return dx, dx_bf, dg


def _tril_mask(n):
    return lax.broadcasted_iota(jnp.int32, (n, n), 0) >= lax.broadcasted_iota(jnp.int32, (n, n), 1)


GMLP_CHUNKS_PER_STEP = 4


def _gmlp_specs(L, half, n_grp, chunk):
    gd = half // n_grp
    cps = _pick(L // chunk, (GMLP_CHUNKS_PER_STEP, 2, 1))
    rows = cps * chunk
    specs = [pl.BlockSpec((rows, gd), lambda g, n: (n, g)),
             pl.BlockSpec((rows, gd), lambda g, n: (n, n_grp + g)),
             pl.BlockSpec((1, gd), lambda g, n: (0, g)),
             pl.BlockSpec((1, gd), lambda g, n: (0, g)),
             pl.BlockSpec((None, chunk, chunk), lambda g, n: (g, 0, 0)),
             pl.BlockSpec((None, chunk, 1), lambda g, n: (g, 0, 0))]
    return gd, cps, specs


def _gmlp_gate_values(zu, zv, lg, lb, ws, bs):
    u, v = _gelu(zu), _gelu(zv)
    mu = jnp.mean(v, axis=-1, keepdims=True)
    d = v - mu
    rstd = lax.rsqrt(jnp.mean(d * d, axis=-1, keepdims=True) + EPS)
    vhat = d * rstd
    vn = vhat * lg + lb
    w = jnp.where(_tril_mask(ws.shape[0]), ws, 0.0).astype(BF16)
    sv = jnp.dot(w, vn.astype(BF16), preferred_element_type=F32) + bs
    return u, vhat, rstd, vn, w, sv


def gmlp_gate_fwd(zpre, ln_g, ln_b, w_s, b_s):
    L, half = zpre.shape[0], zpre.shape[1] // 2
    n_grp, chunk = w_s.shape[0], w_s.shape[1]
    gd, cps, specs = _gmlp_specs(L, half, n_grp, chunk)

    def body(zu, zv, lg, lb, ws, bs, o):
        for k in range(cps):
            rows = pl.ds(k * chunk, chunk)
            u, _, _, _, _, sv = _gmlp_gate_values(zu[rows, :], zv[rows, :], lg[...], lb[...], ws[...], bs[...])
            o[rows, :] = (u * sv).astype(o.dtype)

    return _pcall(body, name="gmlp_gate", grid=(n_grp, L // (cps * chunk)), in_specs=specs,
                  out_specs=pl.BlockSpec((cps * chunk, gd), lambda g, n: (n, g)),
                  out_shape=jax.ShapeDtypeStruct((L, half), BF16), compiler_params=_params())(
        zpre, zpre, ln_g, ln_b, w_s, b_s)


def gmlp_gate_bwd(zpre, ln_g, ln_b, w_s, b_s, dgated):
    L, half = zpre.shape[0], zpre.shape[1] // 2
    n_grp, chunk = w_s.shape[0], w_s.shape[1]
    gd, cps, specs = _gmlp_specs(L, half, n_grp, chunk)
    tile = pl.BlockSpec((cps * chunk, gd), lambda g, n: (n, g))
    specs = specs + [tile]

    def body(zu, zv, lg, lb, ws, bs, dg, dzu, dzv, dws, dbs, dlg, dlb):
        @pl.when(pl.program_id(1) == 0)
        def _():
            dws[...] = jnp.zeros_like(dws)
            dbs[...] = jnp.zeros_like(dbs)
            dlg[...] = jnp.zeros_like(dlg)
            dlb[...] = jnp.zeros_like(dlb)

        lgv = lg[...]
        for k in range(cps):
            rows = pl.ds(k * chunk, chunk)
            zuv, zvv = zu[rows, :], zv[rows, :]
            u, vhat, rstd, vn, w, sv = _gmlp_gate_values(zuv, zvv, lgv, lb[...], ws[...], bs[...])
            dgv = dg[rows, :]
            du = dgv * sv
            dsv = dgv * u
            dsv_bf = dsv.astype(BF16)
            dw = lax.dot_general(dsv_bf, vn.astype(BF16), _DIMS['nt'], preferred_element_type=F32)
            dvn = lax.dot_general(w, dsv_bf, _DIMS['tn'], preferred_element_type=F32)
            dvhat = dvn * lgv
            dv = rstd * (dvhat - jnp.mean(dvhat, axis=-1, keepdims=True)
                         - vhat * jnp.mean(dvhat * vhat, axis=-1, keepdims=True))
            dzu[rows, :] = (du * _gelu_grad(zuv)).astype(dzu.dtype)
            dzv[rows, :] = (dv * _gelu_grad(zvv)).astype(dzv.dtype)
            dws[...] += jnp.where(_tril_mask(chunk), dw, 0.0)
            dbs[...] += jnp.sum(dsv, axis=1, keepdims=True)
            dlg[...] += jnp.sum(dvn * vhat, axis=0, keepdims=True)
            dlb[...] += jnp.sum(dvn, axis=0, keepdims=True)

    vec = pl.BlockSpec((1, gd), lambda g, n: (0, g))
    return _pcall(body, name="gmlp_gate_bwd", grid=(n_grp, L // (cps * chunk)), in_specs=specs,
                  out_specs=[tile, tile, pl.BlockSpec((None, chunk, chunk), lambda g, n: (g, 0, 0)),
                             pl.BlockSpec((None, chunk, 1), lambda g, n: (g, 0, 0)), vec, vec],
                  out_shape=[jax.ShapeDtypeStruct((L, half), BF16), jax.ShapeDtypeStruct((L, half), BF16),
                             jax.ShapeDtypeStruct((n_grp, chunk, chunk), F32),
                             jax.ShapeDtypeStruct((n_grp, chunk, 1), F32),
                             jax.ShapeDtypeStruct((1, half), F32), jax.ShapeDtypeStruct((1, half), F32)],
                  compiler_params=_params())(zpre, zpre, ln_g, ln_b, w_s, b_s, dgated)


def gmlp_fwd(x, g, w_in, ln_g, ln_b, w_s, b_s, w_out, prefetch):
    h = rms_fwd(x, g, "gmlp_norm")
    (zpre,) = matmul([(h, w_in)], 'nn', [F32], tm=1024, tn=512, name="gmlp_in")
    gated = gmlp_gate_fwd(zpre, ln_g, ln_b, w_s, b_s)
    (xo,) = matmul([(gated, w_out)], 'nn', [F32], epilogue=lambda accs, ex: (ex[0] + accs[0],),
                   extras=[(x, 't')], tm=512, tn=512, name="gmlp_out", after=prefetch(gated))
    return xo, (x, h, zpre, gated)


def gmlp_bwd(saved, g, w_in, ln_g, ln_b, w_s, b_s, w_out, dxo, dxo_bf, after, on_grads):
    x, h, zpre, gated = saved
    (dgated,) = matmul([(dxo_bf, w_out)], 'nt', [F32], tm=512, tn=512, name="gmlp_dgated", after=after)
    (dw_out,) = matmul([(gated, dxo_bf)], 'tn', [BF16], tm=512, tn=512, name="gmlp_dwout")
    dzu, dzv, dws, dbs, dlg, dlb = gmlp_gate_bwd(zpre, ln_g, ln_b, w_s, b_s, dgated)
    dz = jnp.concatenate([dzu, dzv], axis=1)
    (dw_in,) = matmul([(h, dz)], 'tn', [BF16], tm=1024, tn=512, name="gmlp_dwin")
    started = on_grads([dw_in, dw_out])
    (dh,) = matmul([(dz, w_in)], 'nt', [F32], tm=256, tn=256, name="gmlp_dh", after=started)
    dx, dx_bf, dg = rms_bwd(x, g, dh, dxo, "gmlp_dnorm")
    return dx, dx_bf, dg, dlg, dlb, dws, dbs


def _s5_disc(lr, li, ldt, br, bi):
    dt = jnp.exp(ldt)
    mag = jnp.exp(lr * dt)
    ang = li * dt
    ar = mag * jnp.cos(ang)
    ai = mag * jnp.sin(ang)
    den = lr * lr + li * li
    nr = ar - 1.0
    zr = (nr * lr + ai * li) / den
    zi = (ai * lr - nr * li) / den
    return ar, ai, zr[None] * br - zi[None] * bi, zr[None] * bi + zi[None] * br


def s5_disc_fwd(lr, li, ldt, br, bi):
    def body(lr_r, li_r, ldt_r, br_r, bi_r, ar_o, ai_o, bbr_o, bbi_o):
        res = _s5_disc(lr_r[...], li_r[...], ldt_r[...], br_r[...], bi_r[...])
        for o, v in zip((ar_o, ai_o, bbr_o, bbi_o), res):
            o[...] = v

    shp = lambda a: jax.ShapeDtypeStruct(a.shape, F32)
    return _pcall(body, name="s5_disc", out_shape=[shp(lr), shp(lr), shp(br), shp(br)],
                  compiler_params=_params())(lr, li, ldt, br, bi)


def s5_disc_bwd(lr, li, ldt, br, bi, dar, dai, dbbr, dbbi):
    def body(lr_r, li_r, ldt_r, br_r, bi_r, dar_r, dai_r, dbbr_r, dbbi_r, o1, o2, o3, o4, o5):
        _, vjp = jax.vjp(_s5_disc, lr_r[...], li_r[...], ldt_r[...], br_r[...], bi_r[...])
        res = vjp((dar_r[...], dai_r[...], dbbr_r[...], dbbi_r[...]))
        for o, v in zip((o1, o2, o3, o4, o5), res):
            o[...] = v

    shp = lambda a: jax.ShapeDtypeStruct(a.shape, F32)
    return _pcall(body, name="s5_disc_bwd", out_shape=[shp(lr), shp(lr), shp(ldt), shp(br), shp(br)],
                  compiler_params=_params())(lr, li, ldt, br, bi, dar, dai, dbbr, dbbi)


def blockdiag_matmul(pairs, outs, *, epilogue=None, extras=(), name):
    a0, b0 = pairs[0]
    M = a0.shape[0]
    T, wa, wo = b0.shape
    tm = _pick(M, (512, 256, 128))
    n_p, n_e = len(pairs), len(extras)
    in_specs, args = [], []
    for a, b in pairs:
        in_specs += [pl.BlockSpec((tm, wa), lambda k, i: (i, k)), pl.BlockSpec((None, wa, wo), lambda k, i: (k, 0, 0))]
        args += [a, b]
    for arr, kind in extras:
        in_specs.append(pl.BlockSpec((tm, wo), lambda k, i: (i, k)) if kind == 't'
                        else pl.BlockSpec((1, wo), lambda k, i: (0, k)))
        args.append(arr)

    def body(*refs):
        accs = [jnp.dot(refs[2 * p][...].astype(BF16), refs[2 * p + 1][...], preferred_element_type=F32)
                for p in range(n_p)]
        ex = [r[...] for r in refs[2 * n_p:2 * n_p + n_e]]
        res = tuple(accs) if epilogue is None else epilogue(accs, ex)
        for ref, v in zip(refs[2 * n_p + n_e:], res):
            ref[...] = v.astype(ref.dtype)

    return _pcall(body, name=name, grid=(T, M // tm), in_specs=in_specs,
                  out_specs=[pl.BlockSpec((tm, wo), lambda k, i: (i, k)) for _ in outs],
                  out_shape=[jax.ShapeDtypeStruct((M, T * wo), dt) for dt in outs],
                  compiler_params=_params())(*args)


def blockdiag_outer(pairs, name):
    M = pairs[0][0].shape[0]
    n_p = len(pairs)
    shapes = []
    in_specs, args = [], []
    tm = _pick(M, (512, 256, 128))
    T = None
    for a, b, wa, wb in pairs:
        T = a.shape[1] // wa
        in_specs += [pl.BlockSpec((tm, wa), lambda k, i: (i, k)), pl.BlockSpec((tm, wb), lambda k, i: (i, k))]
        args += [a, b]
        shapes.append((T, wa, wb))

    def body(*refs):
        @pl.when(pl.program_id(1) == 0)
        def _():
            for o in refs[2 * n_p:]:
                o[...] = jnp.zeros_like(o)
        for p in range(n_p):
            refs[2 * n_p + p][...] += lax.dot_general(refs[2 * p][...].astype(BF16), refs[2 * p + 1][...].astype(BF16),
                                                      _DIMS['tn'], preferred_element_type=F32)

    return _pcall(body, name=name, grid=(T, M // tm), in_specs=in_specs,
                  out_specs=[pl.BlockSpec((None, s[1], s[2]), lambda k, i: (k, 0, 0)) for s in shapes],
                  out_shape=[jax.ShapeDtypeStruct(s, F32) for s in shapes], compiler_params=_params())(*args)


def s5_scan(br, bi, ar, ai, reverse, want_prev, name):
    L, S = br.shape
    ln = _pick(S, (SCAN_LANES, 512, 256, 128))
    tb = _pick(L, (512, 256, 128))
    n_t = L // tb
    n_q = tb // 8

    def tmap(j, t):
        return ((n_t - 1 - t) if reverse else t, j)

    blk = pl.BlockSpec((tb, ln), tmap)
    vec = pl.BlockSpec((1, ln), lambda j, t: (0, j))

    def cmul(xr, xi, yr, yi):
        return xr * yr - xi * yi, xr * yi + xi * yr

    n_out = 4 if want_prev else 2

    def body(br_r, bi_r, ar_r, ai_r, *rest):
        outs, (cr_s, ci_s) = rest[:n_out], rest[n_out:]

        @pl.when(pl.program_id(1) == 0)
        def _():
            cr_s[...] = jnp.zeros_like(cr_s)
            ci_s[...] = jnp.zeros_like(ci_s)

        a1r, a1i = ar_r[...], ai_r[...]
        a2r, a2i = cmul(a1r, a1i, a1r, a1i)
        a4r, a4i = cmul(a2r, a2i, a2r, a2i)
        a8r, a8i = cmul(a4r, a4i, a4r, a4i)
        row = lax.broadcasted_iota(jnp.int32, (8, ln), 0)
        dist = (7 - row) if reverse else row
        pwr, pwi = jnp.broadcast_to(a1r, (8, ln)), jnp.broadcast_to(a1i, (8, ln))
        for bit, (er, ei) in ((1, (a1r, a1i)), (2, (a2r, a2i)), (4, (a4r, a4i))):
            nr, ni = cmul(pwr, pwi, er, ei)
            sel = (dist & bit) != 0
            pwr, pwi = jnp.where(sel, nr, pwr), jnp.where(sel, ni, pwi)
        last = 0 if reverse else 7
        steps = [(d, jnp.where(dist >= d, er, 0.0), jnp.where(dist >= d, ei, 0.0))
                 for d, (er, ei) in ((1, (a1r, a1i)), (2, (a2r, a2i)), (4, (a4r, a4i)))]

        def step(q, carry):
            cr, ci = carry
            qq = (n_q - 1 - q) if reverse else q
            rows = pl.ds(pl.multiple_of(qq * 8, 8), 8)
            xr, xi = br_r[rows, :], bi_r[rows, :]
            for d, er, ei in steps:
                sr = pltpu.roll(xr, (8 - d) if reverse else d, 0)
                si = pltpu.roll(xi, (8 - d) if reverse else d, 0)
                mr, mi = cmul(sr, si, er, ei)
                xr, xi = xr + mr, xi + mi
            lr, li = xr[last:last + 1, :], xi[last:last + 1, :]
            kr, ki = cmul(pwr, pwi, cr, ci)
            xr, xi = xr + kr, xi + ki
            outs[0][rows, :] = xr
            outs[1][rows, :] = xi
            if want_prev:
                outs[2][rows, :] = jnp.where(dist >= 1, pltpu.roll(xr, 7 if reverse else 1, 0), cr)
                outs[3][rows, :] = jnp.where(dist >= 1, pltpu.roll(xi, 7 if reverse else 1, 0), ci)
            nr, ni = cmul(a8r, a8i, cr, ci)
            return lr + nr, li + ni

        cr, ci = lax.fori_loop(0, n_q, step, (cr_s[...], ci_s[...]), unroll=2)
        cr_s[...] = cr
        ci_s[...] = ci

    shp = jax.ShapeDtypeStruct((L, S), F32)
    return _pcall(body, name=name, grid=(S // ln, n_t), in_specs=[blk, blk, vec, vec],
                  out_specs=[blk] * n_out, out_shape=[shp] * n_out,
                  scratch_shapes=[pltpu.VMEM((1, ln), F32), pltpu.VMEM((1, ln), F32)],
                  compiler_params=_params())(br, bi, ar, ai)


def _to_blockdiag(m, tile_groups):
    G, A, B = m.shape
    T = G // tile_groups
    eye = jnp.eye(tile_groups, dtype=m.dtype)
    t = m.reshape(T, tile_groups, A, 1, B) * eye[None, :, None, :, None]
    return t.reshape(T, tile_groups * A, tile_groups * B)


def _from_blockdiag(t, tile_groups):
    T, RA, RB = t.shape
    A, B = RA // tile_groups, RB // tile_groups
    d = jnp.diagonal(t.reshape(T, tile_groups, A, tile_groups, B), axis1=1, axis2=3)
    return jnp.moveaxis(d, 3, 1).reshape(T * tile_groups, A, B)


def s5_fwd(x, g, w_in, lam_re, lam_im, log_dt, b_re, b_im, c_re, c_im, d_skip, w_out, prefetch):
    G, P, H = b_re.shape
    tg = min(SSM_TILE_GROUPS, G)
    h = rms_fwd(x, g, "s5_norm")
    (u,) = matmul([(h, w_in)], 'nn', [F32], tm=512, tn=512, name="s5_in")
    br_t, bi_t = jnp.transpose(b_re, (2, 0, 1)), jnp.transpose(b_im, (2, 0, 1))
    ar, ai, bbr, bbi = s5_disc_fwd(lam_re, lam_im, log_dt, br_t, bi_t)
    bbr_g, bbi_g = jnp.transpose(bbr, (1, 0, 2)), jnp.transpose(bbi, (1, 0, 2))
    bd_br, bd_bi = _to_blockdiag(bbr_g.astype(BF16), tg), _to_blockdiag(bbi_g.astype(BF16), tg)
    bur, bui = blockdiag_matmul([(u, bd_br), (u, bd_bi)], [F32, F32], name="s5_bu")
    a_r, a_i = ar.reshape(1, G * P), ai.reshape(1, G * P)
    hr, hi, hpr, hpi = s5_scan(bur, bui, a_r, a_i, False, True, "s5_scan")
    c_pg_r = jnp.transpose(c_re, (0, 2, 1)).astype(BF16)
    c_pg_i = jnp.transpose(c_im, (0, 2, 1)).astype(BF16)
    bd_cr, bd_nci = _to_blockdiag(c_pg_r, tg), _to_blockdiag(-c_pg_i, tg)

    def ep(accs, ex):
        y = accs[0] + accs[1] + ex[1] * ex[0]
        return y, _gelu(y)

    y, act = blockdiag_matmul([(hr, bd_cr), (hi, bd_nci)], [F32, BF16], epilogue=ep,
                              extras=[(u, 't'), (d_skip, 'r')], name="s5_y")
    (o,) = matmul([(act, w_out)], 'nn', [F32], tm=512, tn=512, name="s5_out", after=prefetch(act))
    M, D = x.shape
    tm = _pick(M, (256, 128))
    (xo,) = tilemap(lambda xv, val, gt: (xv + val * _sigmoid(gt),), [(x, 't', 0), (o, 't', 0), (o, 't', 1)],
                    [(F32, 't')], M=M, N=D, tm=tm, tn=D, name="s5_glu")
    saved = (x, h, u, hr, hi, hpr, hpi, y, act, o, a_r, a_i, bd_br, bd_bi, bd_cr, bd_nci, br_t, bi_t)
    return xo, saved


def s5_bwd(saved, g, w_in, lam_re, lam_im, log_dt, b_re, d_skip, w_out, dxo, after, on_grads):
    x, h, u, hr, hi, hpr, hpi, y, act, o, a_r, a_i, bd_br, bd_bi, bd_cr, bd_nci, br_t, bi_t = saved
    G, P, H = b_re.shape
    tg = min(SSM_TILE_GROUPS, G)
    M, D = x.shape
    tm = _pick(M, (256, 128))

    def glu_bwd(dv, val, gt):
        sg = _sigmoid(gt)
        return dv * sg, dv * val * sg * (1.0 - sg)

    dval, dgate = tilemap(glu_bwd, [(dxo, 't', 0), (o, 't', 0), (o, 't', 1)], [(BF16, 't'), (BF16, 't')],
                          M=M, N=D, tm=tm, tn=D, name="s5_dglu", after=after)
    do = jnp.concatenate([dval, dgate], axis=1)
    (dw_out,) = matmul([(act, do)], 'tn', [BF16], tm=512, tn=512, name="s5_dwout")
    (dact,) = matmul([(do, w_out)], 'nt', [F32], tm=512, tn=512, name="s5_dact")
    dy, dd = tilemap(lambda da, yv, uv: (da * _gelu_grad(yv), jnp.sum(da * _gelu_grad(yv) * uv, axis=0, keepdims=True)),
                     [(dact, 't', 0), (y, 't', 0), (u, 't', 0)], [(F32, 't'), (F32, 'a')],
                     M=M, N=D, tm=tm, tn=D, name="s5_dy")
    bd_crT, bd_nciT = jnp.transpose(bd_cr, (0, 2, 1)), jnp.transpose(bd_nci, (0, 2, 1))
    dhr, dhi = blockdiag_matmul([(dy, bd_crT), (dy, bd_nciT)], [F32, F32], name="s5_dh")
    gr, gi = s5_scan(dhr, dhi, a_r, -a_i, True, False, "s5_scan_rev")
    S = G * P
    tms = _pick(M, (128,))

    def da_fn(grv, giv, hprv, hpiv):
        return (jnp.sum(grv * hprv + giv * hpiv, axis=0, keepdims=True),
                jnp.sum(giv * hprv - grv * hpiv, axis=0, keepdims=True))

    dar, dai = tilemap(da_fn, [(gr, 't', 0), (gi, 't', 0), (hpr, 't', 0), (hpi, 't', 0)], [(F32, 'a'), (F32, 'a')],
                       M=M, N=S, tm=tms, tn=_pick(S, (2048, 1024, 512)), name="s5_dabar")
    wa, wb = tg * H, tg * P
    xc_r, xc_i, xb_r, xb_i = blockdiag_outer([(dy, hr, wa, wb), (dy, hi, wa, wb), (u, gr, wa, wb), (u, gi, wa, wb)],
                                             "s5_dcb")
    dc_re = _from_blockdiag(xc_r, tg)
    dc_im = -_from_blockdiag(xc_i, tg)
    dbb_r = jnp.transpose(_from_blockdiag(xb_r, tg), (1, 0, 2))
    dbb_i = jnp.transpose(_from_blockdiag(xb_i, tg), (1, 0, 2))
    dlr, dli, dldt, dbr_t, dbi_t = s5_disc_bwd(lam_re, lam_im, log_dt, br_t, bi_t,
                                               dar.reshape(G, P), dai.reshape(G, P), dbb_r, dbb_i)
    db_re, db_im = jnp.transpose(dbr_t, (1, 2, 0)), jnp.transpose(dbi_t, (1, 2, 0))
    bd_brT, bd_biT = jnp.transpose(bd_br, (0, 2, 1)), jnp.transpose(bd_bi, (0, 2, 1))
    (du,) = blockdiag_matmul([(gr, bd_brT), (gi, bd_biT)], [BF16],
                             epilogue=lambda accs, ex: (accs[0] + accs[1] + ex[1] * ex[0],),
                             extras=[(dy, 't'), (d_skip, 'r')], name="s5_du")
    (dw_in,) = matmul([(h, du)], 'tn', [BF16], tm=512, tn=512, name="s5_dwin")
    started = on_grads([dw_in, dw_out])
    (dh,) = matmul([(du, w_in)], 'nt', [F32], tm=512, tn=512, name="s5_dhin", after=started)
    dx, dx_bf, dg = rms_bwd(x, g, dh, dxo, "s5_dnorm")
    return dx, dx_bf, dg, dlr, dli, dldt, db_re, db_im, dc_re, dc_im, dd


def ple_fwd(x, g, p_emb, w_gate, w_proj, tag, prefetch):
    h = rms_fwd(x, g, f"ple_norm_{tag}")
    (q,) = matmul([(p_emb, w_proj)], 'nn', [F32], tm=512, tn=512, name=f"ple_proj_{tag}")

    def ep(accs, ex):
        gt = _sigmoid(accs[0])
        return ex[0] + gt * ex[1], gt

    xo, gate = matmul([(h, w_gate)], 'nn', [F32, F32], epilogue=ep, extras=[(x, 't'), (q, 't')],
                      tm=512, tn=512, name=f"ple_gate_{tag}", after=prefetch(q))
    return xo, (x, h, q, gate)


def ple_bwd(saved, g, p_emb, w_gate, dxo, tag, after, on_grads):
    x, h, q, gate = saved
    M, D = x.shape
    tm = _pick(M, (256, 128))
    dq, dpre = tilemap(lambda dv, qv, gv: (dv * gv, dv * qv * gv * (1.0 - gv)),
                       [(dxo, 't', 0), (q, 't', 0), (gate, 't', 0)], [(BF16, 't'), (BF16, 't')],
                       M=M, N=D, tm=tm, tn=D, name=f"ple_dgate_{tag}", after=after)
    (dw_proj,) = matmul([(p_emb, dq)], 'tn', [BF16], tm=256, tn=512, name=f"ple_dwproj_{tag}")
    (dw_gate,) = matmul([(h, dpre)], 'tn', [BF16], tm=512, tn=512, name=f"ple_dwgate_{tag}")
    started = on_grads([dw_gate, dw_proj])
    (dh,) = matmul([(dpre, w_gate)], 'nt', [F32], tm=512, tn=512, name=f"ple_dh_{tag}", after=started)
    dx, dx_bf, dg = rms_bwd(x, g, dh, dxo, f"ple_dnorm_{tag}")
    return dx, dx_bf, dg


def loss_head(x, g, target):
    M, D = x.shape
    tm = _pick(M, (256, 128))

    def fn(xv, gv, tv):
        r = lax.rsqrt(jnp.mean(xv * xv, axis=-1, keepdims=True) + EPS)
        xh = xv * r
        e = xh * gv - tv
        dy = e * (1.0 / D)
        dxh = dy * gv
        m = jnp.mean(dxh * xh, axis=-1, keepdims=True)
        dx = r * (dxh - xh * m)
        return jnp.sum(e * e, axis=0, keepdims=True), dx, dx, jnp.sum(dy * xh, axis=0, keepdims=True)

    return tilemap(fn, [(x, 't', 0), (g, 'r', 0), (target, 't', 0)],
                   [(F32, 'a'), (F32, 't'), (BF16, 't'), (F32, 'a')], M=M, N=D, tm=tm, tn=D, name="loss_head")


def _adamw_math(wv, gv, mv, vv):
    mn = ADAM_B1 * mv + (1.0 - ADAM_B1) * gv
    vn = ADAM_B2 * vv + (1.0 - ADAM_B2) * (gv * gv)
    m_hat = mn / (1.0 - ADAM_B1 ** ADAM_STEP)
    v_hat = vn / (1.0 - ADAM_B2 ** ADAM_STEP)
    return -ADAM_LR * (m_hat / (jnp.sqrt(v_hat) + ADAM_EPS) + ADAM_WD * wv), mn, vn


def adamw_into(w, lead, g, m, v, carry, name):
    R, C = w.shape[-2:]
    lead = tuple(lead)
    tm = _row_tile(R, C)
    blk = pl.BlockSpec((None,) * len(lead) + (tm, C), lambda i: lead + (i, 0))
    n_carry = 0 if carry is None else 4

    def body(*refs):
        w_r, g_r, m_r, v_r = refs[:4]
        g_o, d_o, m_o, v_o = refs[4 + n_carry:]
        gv = g_r[...]
        d, mn, vn = _adamw_math(w_r[...], gv, m_r[...], v_r[...])
        g_o[...] = gv
        d_o[...] = d
        m_o[...] = mn
        v_o[...] = vn

    shp = jax.ShapeDtypeStruct(w.shape, F32)
    return _pcall(body, name=name, grid=(R // tm,),
                  in_specs=[blk, pl.BlockSpec((tm, C), lambda i: (i, 0)), blk, blk] + [pl.BlockSpec(memory_space=pl.ANY)] * n_carry,
                  out_specs=[blk] * 4, out_shape=[shp] * 4,
                  input_output_aliases={4 + k: k for k in range(n_carry)},
                  compiler_params=_params())(w, g, m, v, *(carry or ()))


def adamw(w, g, m, v, name):
    w2, g2, m2, v2 = _as2d(w), g.reshape(_as2d(w).shape), _as2d(m), _as2d(v)
    R, C = w2.shape
    tm = _row_tile(R, C)

    d, mn, vn = tilemap(_adamw_math, [(w2, 't', 0), (g2, 't', 0), (m2, 't', 0), (v2, 't', 0)],
                        [(F32, 't'), (F32, 't'), (F32, 't')], M=R, N=C, tm=tm, tn=C, name=name)
    return d.reshape(w.shape), mn.reshape(w.shape), vn.reshape(w.shape)


WEIGHT_NAMES = ['norm_g', 'final_norm_g', 'ffn_w_gate', 'ffn_w_up', 'ffn_w_down', 'gmlp_w_in', 'gmlp_ln_g',
                'gmlp_ln_b', 'gmlp_w_s', 'gmlp_b_s', 'gmlp_w_out', 's5_w_in', 's5_lam_re', 's5_lam_im',
                's5_log_dt', 's5_b_re', 's5_b_im', 's5_c_re', 's5_c_im', 's5_d', 's5_w_out', 'ple_w_gate',
                'ple_w_proj']
BIG = {'ffn_w_gate': 1, 'ffn_w_up': 1, 'ffn_w_down': 0, 'gmlp_w_in': 1, 'gmlp_w_out': 0, 's5_w_in': 0,
       's5_w_out': 1, 'ple_w_gate': 0, 'ple_w_proj': 1}


def _blocks(depth):
    out = []
    for i in range(depth):
        for k, half in enumerate("ab"):
            ffn = [(n, (i, k)) for n in ('ffn_w_gate', 'ffn_w_up', 'ffn_w_down')]
            if k == 1:
                out.append((f"ffn{i}b", ffn))
                out.append((f"ple{i}", [('ple_w_gate', (i,)), ('ple_w_proj', (i,))]))
            else:
                out.append((f"ffn{i}a", ffn))
                mix = 'gmlp' if i % 2 == 0 else 's5'
                out.append((f"{mix}{i}", [(f'{mix}_w_in', (i // 2,)), (f'{mix}_w_out', (i // 2,))]))
    return out


def kernel(x, p, norm_g, final_norm_g, ffn_w_gate, ffn_w_up, ffn_w_down, gmlp_w_in, gmlp_ln_g, gmlp_ln_b, gmlp_w_s, gmlp_b_s, gmlp_w_out, s5_w_in, s5_lam_re, s5_lam_im, s5_log_dt, s5_b_re, s5_b_im, s5_c_re, s5_c_im, s5_d, s5_w_out, ple_w_gate, ple_w_proj, loss_target, m_norm_g, m_final_norm_g, m_ffn_w_gate, m_ffn_w_up, m_ffn_w_down, m_gmlp_w_in, m_gmlp_ln_g, m_gmlp_ln_b, m_gmlp_w_s, m_gmlp_b_s, m_gmlp_w_out, m_s5_w_in, m_s5_lam_re, m_s5_lam_im, m_s5_log_dt, m_s5_b_re, m_s5_b_im, m_s5_c_re, m_s5_c_im, m_s5_d, m_s5_w_out, m_ple_w_gate, m_ple_w_proj, v_norm_g, v_final_norm_g, v_ffn_w_gate, v_ffn_w_up, v_ffn_w_down, v_gmlp_w_in, v_gmlp_ln_g, v_gmlp_ln_b, v_gmlp_w_s, v_gmlp_b_s, v_gmlp_w_out, v_s5_w_in, v_s5_lam_re, v_s5_lam_im, v_s5_log_dt, v_s5_b_re, v_s5_b_im, v_s5_c_re, v_s5_c_im, v_s5_d, v_s5_w_out, v_ple_w_gate, v_ple_w_proj):
    env = dict(locals())
    W = {n: env[n] for n in WEIGHT_NAMES}
    Mo = {n: env["m_" + n] for n in WEIGHT_NAMES}
    Vo = {n: env["v_" + n] for n in WEIGHT_NAMES}
    depth = norm_g.shape[0]
    L, D = x.shape[1], x.shape[2]
    s_idx = 2 * lax.axis_index("x") + lax.axis_index("y")

    sc = jnp.stack([s_idx, lax.axis_index("c")]).astype(jnp.int32)
    blocks = _blocks(depth)

    ng2 = norm_g.reshape(depth * 4, norm_g.shape[-1])
    ng_full, sd_full = gather_small("gather_small", [ng2, s5_d], sc)
    full = {}
    gathers = []
    token = sd_full
    for bname, mats in blocks:
        casts = [cast_into_full(W[n], lead, BIG[n], sc, f"cast_{bname}_{n}", token) for n, lead in mats]
        handle, token = gather_start(f"gather_{bname}", casts, [W[n].shape[-2:] for n, _ in mats],
                                     [BIG[n] for n, _ in mats], token)
        gathers.append(handle)
    all_started = token
    ng_full = ng_full.reshape(depth, 4, 1, D)

    landed = {}

    def prefetcher(bi):
        def prefetch(value):
            if bi >= len(blocks):
                return value
            landed[bi], token = gather_land(gathers[bi], value)
            return token
        return prefetch

    def fetch(bi, after):
        if bi not in landed:
            landed[bi], after = gather_land(gathers[bi], after)
        full.update(dict(zip(blocks[bi][1], gather_take(landed[bi], after))))
    gf = final_norm_g.reshape(1, D)

    G, P, H = s5_b_re.shape[1:]
    n_grp, chunk = gmlp_w_s.shape[1], gmlp_w_s.shape[2]
    lam_re, lam_im = s5_lam_re[0], s5_lam_im[0]
    log_dt = s5_log_dt.reshape(G, 1)
    b_re, b_im, c_re, c_im = s5_b_re[0], s5_b_im[0], s5_c_re[0], s5_c_im[0]
    w_s, b_s = gmlp_w_s[0], gmlp_b_s[0].reshape(n_grp, chunk, 1)
    xs = x.reshape(L, D)
    saved = []
    def ffn_w(i, k):
        return [full[(n, (i, k))] for n in ('ffn_w_gate', 'ffn_w_up', 'ffn_w_down')]

    for i in range(depth):
        sv = {}
        fetch(4 * i, all_started if i == 0 else xs)
        xs, sv['ffn_a'] = ffn_fwd(xs, ng_full[i, 0], *ffn_w(i, 0), f"{i}a", prefetcher(4 * i + 1))
        j = (i // 2,)
        fetch(4 * i + 1, xs)
        if i % 2 == 0:
            xs, sv['mix'] = gmlp_fwd(xs, ng_full[i, 1], full[('gmlp_w_in', j)], gmlp_ln_g, gmlp_ln_b, w_s, b_s,
                                     full[('gmlp_w_out', j)], prefetcher(4 * i + 2))
        else:
            xs, sv['mix'] = s5_fwd(xs, ng_full[i, 1], full[('s5_w_in', j)], lam_re, lam_im, log_dt, b_re, b_im,
                                   c_re, c_im, sd_full, full[('s5_w_out', j)], prefetcher(4 * i + 2))
        fetch(4 * i + 2, xs)
        xs, sv['ffn_b'] = ffn_fwd(xs, ng_full[i, 2], *ffn_w(i, 1), f"{i}b", prefetcher(4 * i + 3))
        fetch(4 * i + 3, xs)
        xs, sv['ple'] = ple_fwd(xs, ng_full[i, 3], p[i, 0], full[('ple_w_gate', (i,))], full[('ple_w_proj', (i,))], f"{i}",
                                prefetcher(4 * i + 4))
        saved.append(sv)

    sq, dx, dx_bf, dgf = loss_head(xs, gf, loss_target.reshape(L, D))
    loss_local = 0.5 * jnp.sum(sq) / D
    dng = [[None] * 4 for _ in range(depth)]
    small = {}
    gshard = {}
    state = {'fence': sc}
    ici_inflight = []
    swaps = []

    def hook(bi):
        def on_grads(gs):
            bname, mats = blocks[bi]
            handle, token = reduce_pair_start(f"reduce_{bname}", gs, [BIG[n] for n, _ in mats], state['fence'])
            state['pair'] = (handle, mats)
            return token
        return on_grads

    def reduce_block(after, keep=1):
        handle, mats = state.pop('pair')
        ici_handle, fence = reduce_ici_start(handle, sc, after)
        ici_inflight.append((ici_handle, mats))
        while len(ici_inflight) > keep:
            prev_handle, prev_mats = ici_inflight.pop(0)
            swap_handle, fence = reduce_finish(prev_handle, fence)
            swaps.append((swap_handle, prev_mats))
        state['fence'] = fence

    for i in reversed(range(depth)):
        sv = saved[i]
        dx, dx_bf, dng[i][3] = ple_bwd(sv['ple'], ng_full[i, 3], p[i, 0], full[('ple_w_gate', (i,))], dx, f"{i}",
                                       state['fence'], hook(4 * i + 3))
        reduce_block(dx_bf, keep=2)
        dx, dx_bf, dng[i][2] = ffn_bwd(sv['ffn_b'], ng_full[i, 2], *ffn_w(i, 1), dx, dx_bf, f"{i}b",
                                       state['fence'], hook(4 * i + 2))
        reduce_block(dx_bf)
        j = (i // 2,)
        if i % 2 == 0:
            dx, dx_bf, dng[i][1], dlg, dlb, dws, dbs = gmlp_bwd(
                sv['mix'], ng_full[i, 1], full[('gmlp_w_in', j)], gmlp_ln_g, gmlp_ln_b, w_s, b_s,
                full[('gmlp_w_out', j)], dx, dx_bf, state['fence'], hook(4 * i + 1))
            small.update(gmlp_ln_g=dlg, gmlp_ln_b=dlb, gmlp_w_s=dws, gmlp_b_s=dbs)
        else:
            dx, dx_bf, dng[i][1], dlr, dli, dldt, db_re, db_im, dc_re, dc_im, dd = s5_bwd(
                sv['mix'], ng_full[i, 1], full[('s5_w_in', j)], lam_re, lam_im, log_dt, b_re, sd_full,
                full[('s5_w_out', j)], dx, state['fence'], hook(4 * i + 1))
            small.update(s5_lam_re=dlr, s5_lam_im=dli, s5_log_dt=dldt, s5_b_re=db_re, s5_b_im=db_im,
                         s5_c_re=dc_re, s5_c_im=dc_im, s5_d=dd)
        reduce_block(dx_bf)
        dx, dx_bf, dng[i][0] = ffn_bwd(sv['ffn_a'], ng_full[i, 0], *ffn_w(i, 0), dx, dx_bf, f"{i}a",
                                       state['fence'], hook(4 * i))
        reduce_block(dx_bf)
    grad_x = dx.reshape(x.shape)
    small['norm_g'] = jnp.stack([jnp.stack(r) for r in dng])
    small['final_norm_g'] = dgf

    grads, deltas, new_m, new_v = {}, {}, {}, {}
    carry = {}
    fence = state['fence']

    def update_big(key):
        n, lead = key
        carry[n] = adamw_into(W[n], lead, gshard[key], Mo[n], Vo[n], carry.get(n), f"adamw_{n}_{'_'.join(map(str, lead))}")
        return carry[n][1]

    def update_blocks(done, fence):
        for swap_handle, mats in done:
            gshard.update(dict(zip(mats, reduce_swap_wait(swap_handle, fence))))
            for key in mats:
                fence = update_big(key)
        return fence

    small_names = [n for n in WEIGHT_NAMES if n not in BIG]
    flat = jnp.concatenate([small[n].astype(F32).reshape(-1) for n in small_names] + [loss_local.reshape(1)])
    pad = (-flat.size) % (256 * 128)
    flat = jnp.pad(flat, (0, pad)).reshape(-1, 128)
    handle, fence = small_swap_start(0, flat, fence)
    handle, fence = small_swap_start(1, small_swap_finish(handle, fence), fence)
    fence = update_blocks(swaps[:len(swaps) // 2], fence)
    handle, fence = small_swap_start(2, small_swap_finish(handle, fence), fence)
    fence = update_blocks(swaps[len(swaps) // 2:], fence)
    tot = small_swap_finish(handle, fence).reshape(-1)
    off = 0
    for n in small_names:
        sz = small[n].size
        gsum = tot[off:off + sz]
        off += sz
        if n == 'norm_g':
            gsum = lax.dynamic_slice_in_dim(gsum.reshape(depth, 4, D), s_idx * W[n].shape[-1], W[n].shape[-1], axis=2)
        elif n == 's5_d':
            gsum = lax.dynamic_slice_in_dim(gsum.reshape(1, D), s_idx * W[n].shape[-1], W[n].shape[-1], axis=1)
        grads[n] = gsum.reshape(W[n].shape)
        deltas[n], new_m[n], new_v[n] = adamw(W[n], grads[n], Mo[n], Vo[n], f"adamw_{n}")
        fence = deltas[n]
    loss = tot[off]

    last_handle, last_mats = ici_inflight.pop()
    swap_handle, token = reduce_finish(last_handle, fence)
    gshard.update(dict(zip(last_mats, reduce_swap_wait(swap_handle, token))))
    for key in last_mats:
        update_big(key)
    for n in BIG:
        grads[n], deltas[n], new_m[n], new_v[n] = carry[n]
    return (loss, grad_x, *[grads[n] for n in WEIGHT_NAMES], *[deltas[n] for n in WEIGHT_NAMES],
            *[new_m[n] for n in WEIGHT_NAMES], *[new_v[n] for n in WEIGHT_NAMES])
```

```python
import functools
import math

import jax
import jax.numpy as jnp
from jax import lax
from jax.experimental import pallas as pl
from jax.experimental.pallas import tpu as pltpu

F32 = jnp.float32
BF16 = jnp.bfloat16
MESH_ID = pl.DeviceIdType.MESH

EPS = 1e-6
ADAM_LR = 0.001
ADAM_B1 = 0.9
ADAM_B2 = 0.999
ADAM_EPS = 1e-08
ADAM_WD = 0.01
ADAM_STEP = 10

N_SHARD = 4
V7X_VMEM_LIMIT = 52 * 2 ** 20
SSM_TILE_GROUPS = 16
SCAN_LANES = 1024
GELU_C = math.sqrt(2.0 / math.pi)


def _pcall(body, **kw):
    return pl.pallas_call(body, **kw)


def _params():
    return pltpu.CompilerParams(vmem_limit_bytes=V7X_VMEM_LIMIT)


def _pick(n, cands):
    for c in cands:
        if c <= n and n % c == 0:
            return c
    return n


def _sigmoid(x):
    return 1.0 / (1.0 + jnp.exp(-x))


def _gelu(x):
    return 0.5 * x * (1.0 + jnp.tanh(GELU_C * (x + 0.044715 * x * x * x)))


def _gelu_grad(x):
    t = jnp.tanh(GELU_C * (x + 0.044715 * x * x * x))
    return 0.5 * (1.0 + t) + 0.5 * x * (1.0 - t * t) * GELU_C * (1.0 + 3.0 * 0.044715 * x * x)


def tilemap(fn, ins, outs, *, M, N, tm, tn, name, after=None):
    n_in = len(ins)
    n_dep = 0 if after is None else 1
    grid = (N // tn, M // tm)
    in_specs = []
    for arr, kind, off in ins:
        if kind == 't':
            in_specs.append(pl.BlockSpec((tm, tn), lambda j, i, off=off: (i, j + off)))
        else:
            in_specs.append(pl.BlockSpec((1, tn), lambda j, i, off=off: (0, j + off)))
    out_specs, out_shape = [], []
    for dt, kind in outs:
        if kind == 't':
            out_specs.append(pl.BlockSpec((tm, tn), lambda j, i: (i, j)))
            out_shape.append(jax.ShapeDtypeStruct((M, N), dt))
        else:
            out_specs.append(pl.BlockSpec((1, tn), lambda j, i: (0, j)))
            out_shape.append(jax.ShapeDtypeStruct((1, N), F32))
    in_specs += [pl.BlockSpec(memory_space=pl.ANY)] * n_dep

    def body(*refs):
        vals = fn(*[r[...] for r in refs[:n_in]])
        for (dt, kind), ref, v in zip(outs, refs[n_in + n_dep:], vals):
            if kind == 't':
                ref[...] = v.astype(ref.dtype)
            else:
                @pl.when(pl.program_id(1) == 0)
                def _():
                    ref[...] = jnp.zeros_like(ref)
                ref[...] += v

    res = _pcall(body, name=name, grid=grid, in_specs=in_specs, out_specs=out_specs,
                 out_shape=out_shape, compiler_params=_params())(*[a for a, _, _ in ins], *([after] * n_dep))
    return res


def _as2d(a):
    if a.ndim >= 2 and a.shape[-1] % 128 == 0:
        return a.reshape(-1, a.shape[-1])
    if a.size % 128 == 0:
        return a.reshape(-1, 128)
    return a.reshape(-1, a.shape[-1])


def _row_tile(rows, cols, nbytes=4, budget=1 << 20):
    cands = [c for c in (2048, 1024, 512, 256, 128, 64, 32, 16, 8) if c * cols * nbytes <= budget]
    return _pick(rows, cands) if cands else _pick(rows, (8,))


ROW_TILES = (256, 128, 64, 32, 16)


def _sc_call(body, sc, args, *, grid, in_specs, out_specs, out_shape, name):
    gs = pltpu.PrefetchScalarGridSpec(num_scalar_prefetch=1, grid=grid, in_specs=in_specs, out_specs=out_specs)
    return _pcall(body, name=name, grid_spec=gs, out_shape=out_shape, compiler_params=_params())(sc, *args)


def cast_into_full(w, lead, ax, sc, name, after):
    R, C = w.shape[-2:]
    tm = _pick(R, ROW_TILES)
    nb = R // tm
    lead = tuple(lead)
    in_spec = pl.BlockSpec((None,) * len(lead) + (tm, C), lambda i, s: lead + (i, 0))
    if ax == 0:
        shape, out_map = (R * N_SHARD, C), (lambda i, s: (i + s[0] * nb, 0))
    else:
        shape, out_map = (R, C * N_SHARD), (lambda i, s: (i, s[0]))

    def body(s_ref, w_ref, after_ref, o_ref):
        o_ref[...] = w_ref[...].astype(BF16)

    return _sc_call(body, sc, [w, after], grid=(nb,), in_specs=[in_spec, pl.BlockSpec(memory_space=pl.ANY)],
                    out_specs=pl.BlockSpec((tm, C), out_map), out_shape=jax.ShapeDtypeStruct(shape, BF16), name=name)


def pairsum(g, a, ax, sc, name):
    hR, hC = a.shape
    tm = _pick(hR, ROW_TILES)
    nb = hR // tm
    g_map = (lambda i, s: (i + s[1] * nb, 0)) if ax == 1 else (lambda i, s: (i, s[1]))
    blk = (tm, hC)

    def body(s_ref, g_ref, a_ref, o_ref):
        o_ref[...] = (g_ref[...].astype(F32) + a_ref[...].astype(F32)).astype(BF16)

    return _sc_call(body, sc, [g, a], grid=(nb,),
                    in_specs=[pl.BlockSpec(blk, g_map), pl.BlockSpec(blk, lambda i, s: (i, 0))],
                    out_specs=pl.BlockSpec(blk, lambda i, s: (i, 0)),
                    out_shape=jax.ShapeDtypeStruct((hR, hC), BF16), name=name)


def shardsum(b, cbuf, ax, sc, name):
    hR, hC = b.shape
    _, pR, pC = cbuf.shape
    tm = _pick(pR, ROW_TILES)
    nb = pR // tm
    if ax == 1:
        b_map, o_map, shape = (lambda i, s: (i, s[0])), (lambda i, s: (i + s[1] * nb, 0)), (2 * pR, pC)
    else:
        b_map, o_map, shape = (lambda i, s: (i + s[0] * nb, 0)), (lambda i, s: (i, s[1])), (pR, 2 * pC)

    def body(s_ref, b_ref, c_ref, o_ref):
        acc = b_ref[...].astype(F32)
        for k in range(N_SHARD - 1):
            acc = acc + c_ref[k].astype(F32)
        o_ref[...] = acc

    return _sc_call(body, sc, [b, cbuf], grid=(nb,),
                    in_specs=[pl.BlockSpec((tm, pC), b_map), pl.BlockSpec((N_SHARD - 1, tm, pC), lambda i, s: (0, i, 0))],
                    out_specs=pl.BlockSpec((tm, pC), o_map), out_shape=jax.ShapeDtypeStruct(shape, F32), name=name)


_DIMS = {'nn': (((1,), (0,)), ((), ())), 'nt': (((1,), (1,)), ((), ())), 'tn': (((0,), (0,)), ((), ()))}


def matmul(pairs, mode, outs, *, epilogue=None, extras=(), tm=512, tn=512, name, after=None):
    a0, b0 = pairs[0]
    if mode == 'nn':
        (M, K), N = a0.shape, b0.shape[1]
    elif mode == 'nt':
        (M, K), N = a0.shape, b0.shape[0]
    else:
        (K, M), N = a0.shape, b0.shape[1]
    tm, tn = _pick(M, (tm, 256, 128)), _pick(N, (tn, 256, 128))
    n_p, n_e = len(pairs), len(extras)
    if mode == 'tn':
        a_spec = pl.BlockSpec((K, tm), lambda i, j: (0, i))
    else:
        a_spec = pl.BlockSpec((tm, K), lambda i, j: (i, 0))
    if mode == 'nt':
        b_spec = pl.BlockSpec((tn, K), lambda i, j: (j, 0))
    else:
        b_spec = pl.BlockSpec((K, tn), lambda i, j: (0, j))
    in_specs, args = [], []
    for a, b in pairs:
        in_specs += [a_spec, b_spec]
        args += [a, b]
    for arr, kind in extras:
        if kind == 't':
            in_specs.append(pl.BlockSpec((tm, tn), lambda i, j: (i, j)))
        else:
            in_specs.append(pl.BlockSpec((1, tn), lambda i, j: (0, j)))
        args.append(arr)
    n_dep = 0 if after is None else 1
    in_specs += [pl.BlockSpec(memory_space=pl.ANY)] * n_dep
    args += [after] * n_dep
    dims = _DIMS[mode]

    def body(*refs):
        accs = [lax.dot_general(refs[2 * p][...].astype(BF16), refs[2 * p + 1][...].astype(BF16), dims,
                                preferred_element_type=F32) for p in range(n_p)]
        ex = [r[...] for r in refs[2 * n_p:2 * n_p + n_e]]
        if epilogue is None:
            acc = accs[0]
            for other in accs[1:]:
                acc = acc + other
            res = (acc,)
        else:
            res = epilogue(accs, ex)
        for ref, v in zip(refs[2 * n_p + n_e + n_dep:], res):
            ref[...] = v.astype(ref.dtype)

    return _pcall(body, name=name, grid=(M // tm, N // tn), in_specs=in_specs,
                  out_specs=[pl.BlockSpec((tm, tn), lambda i, j: (i, j)) for _ in outs],
                  out_shape=[jax.ShapeDtypeStruct((M, N), dt) for dt in outs],
                  compiler_params=_params())(*args)


def comm_call(name, ins, out_shapes, plan, n_local, n_remote, aliases=None, after=None):
    ins = list(ins) + ([] if after is None else [after])
    n_in, n_out = len(ins), len(out_shapes)

    def body(*refs):
        in_refs, out_refs = refs[:n_in], refs[n_in:n_in + n_out]
        lsem, ssem, rsem = refs[n_in + n_out:]
        me = (lax.axis_index("x"), lax.axis_index("y"), lax.axis_index("c"))
        local, remote = plan(me, in_refs, out_refs)
        assert len(local) == n_local and len(remote) == n_remote
        lcs = [pltpu.make_async_copy(s, d, lsem.at[k]) for k, (s, d) in enumerate(local)]
        rcs = [pltpu.make_async_remote_copy(src_ref=s, dst_ref=d, send_sem=ssem.at[k], recv_sem=rsem.at[k],
                                            device_id=peer, device_id_type=MESH_ID)
               for k, (s, d, peer) in enumerate(remote)]
        for cp in rcs:
            cp.start()
        for cp in lcs:
            cp.start()
        for cp in rcs:
            cp.wait()
        for cp in lcs:
            cp.wait()

    any_spec = pl.BlockSpec(memory_space=pl.ANY)
    return _pcall(body, name=name, in_specs=[any_spec] * n_in, out_specs=[any_spec] * n_out,
                  out_shape=list(out_shapes),
                  scratch_shapes=[pltpu.SemaphoreType.DMA((max(n_local, 1),)),
                                  pltpu.SemaphoreType.DMA((max(n_remote, 1),)),
                                  pltpu.SemaphoreType.DMA((max(n_remote, 1),))],
                  input_output_aliases=aliases or {},
                  compiler_params=pltpu.CompilerParams(has_side_effects=True))(*ins)


_HBM_SPEC = pl.BlockSpec(memory_space=pltpu.HBM)
_SEM_SPEC = pl.BlockSpec(memory_space=pltpu.SEMAPHORE)
_DATAFLOW = pltpu.SideEffectType.DATAFLOW_SIDE_EFFECTING


def split_start(name, arrays, n_copies, plan, after):
    n = len(arrays)

    def body(*refs):
        ins, (ssem, rsem) = refs[:n], refs[n + 1:n + 3]
        token = refs[-1]
        me = (lax.axis_index("x"), lax.axis_index("y"), lax.axis_index("c"))
        for k, (src, dst, peer) in enumerate(plan(me, ins)):
            pltpu.make_async_remote_copy(src_ref=src, dst_ref=dst, send_sem=ssem.at[k], recv_sem=rsem.at[k],
                                         device_id=peer, device_id_type=MESH_ID).start()
        token[...] = jnp.zeros_like(token)

    sems = pltpu.SemaphoreType.DMA((n_copies,))
    res = _pcall(body, name=name,
                 out_shape=(sems, sems, *[pltpu.HBM(a.shape, a.dtype) for a in arrays], jax.ShapeDtypeStruct((8, 128), F32)),
                 in_specs=[_HBM_SPEC] * n + [pl.BlockSpec(memory_space=pl.ANY)],
                 out_specs=(_SEM_SPEC, _SEM_SPEC, *[_HBM_SPEC] * n, pl.BlockSpec(memory_space=pltpu.VMEM)),
                 input_output_aliases={m: 2 + m for m in range(n)},
                 compiler_params=pltpu.CompilerParams(has_side_effects=_DATAFLOW))(
        *[pltpu.with_memory_space_constraint(a, pltpu.HBM) for a in arrays], after)
    return res[0], res[1], list(res[2:2 + n]), res[-1]


def split_wait(name, arrays, ssem, rsem, plan, after):
    n = len(arrays)

    def body(*refs):
        ins, (ssem_r, rsem_r) = refs[:n], refs[n:n + 2]
        me = (lax.axis_index("x"), lax.axis_index("y"), lax.axis_index("c"))
        for k, (src, dst, peer) in enumerate(plan(me, ins)):
            cp = pltpu.make_async_remote_copy(src_ref=src, dst_ref=dst, send_sem=ssem_r.at[k], recv_sem=rsem_r.at[k],
                                              device_id=peer, device_id_type=MESH_ID)
            cp.wait_send()
            cp.wait_recv()

    res = _pcall(body, name=name, out_shape=[pltpu.HBM(a.shape, a.dtype) for a in arrays],
                 in_specs=[_HBM_SPEC] * n + [_SEM_SPEC, _SEM_SPEC, pl.BlockSpec(memory_space=pl.ANY)],
                 out_specs=[_HBM_SPEC] * n, input_output_aliases={m: m for m in range(n)},
                 compiler_params=pltpu.CompilerParams(has_side_effects=_DATAFLOW))(*arrays, ssem, rsem, after)
    return list(res)


def _shard_of(me):
    return 2 * me[0] + me[1]


def _plane_peers(me):
    x, y, c = me
    return [((1 - x, y, c), 2 * (1 - x) + y), ((x, 1 - y, c), 2 * x + 1 - y),
            ((1 - x, 1 - y, c), 2 * (1 - x) + 1 - y)]


def _mats(arr):
    out = [()]
    for n in arr.shape[:-2]:
        out = [o + (k,) for o in out for k in range(n)]
    return out


ROW_ALIGN = 16
LANE_ALIGN = 128


def _win(ref, lead, rows, cols):
    idx = tuple(lead)
    for spec, align in ((rows, ROW_ALIGN), (cols, LANE_ALIGN)):
        if spec is None:
            idx += (slice(None),)
        else:
            start, size = spec
            if not isinstance(start, int):
                start = pl.multiple_of(start, align)
            idx += (pl.ds(start, size),)
    return ref.at[idx]


def gather_small(name, shards, after):
    full_shapes = [jax.ShapeDtypeStruct((a.shape[0], a.shape[1] * N_SHARD), a.dtype) for a in shards]
    n = len(shards)

    def plan(me, in_refs, out_refs):
        s = _shard_of(me)
        local, remote = [], []
        for t, a in enumerate(shards):
            dst = _win(out_refs[t], (), None, (s * a.shape[1], a.shape[1]))
            local.append((in_refs[t], dst))
            for peer, _ in _plane_peers(me):
                remote.append((in_refs[t], dst, peer))
        return local, remote

    return comm_call(name, shards, full_shapes, plan, n, 3 * n, after=after)


def _half_shape(g, ax):
    R, C = g.shape
    return (R // 2, C) if ax == 1 else (R, C // 2)


def reduce_pair_start(name, grads, axes, after):
    n = len(grads)
    landing = [lax.empty(_half_shape(g, ax), BF16) for g, ax in zip(grads, axes)]

    def plan_a(me, refs):
        x, y, c = me
        copies = []
        for m in range(n):
            R, C = grads[m].shape
            if axes[m] == 1:
                rows, cols = ((1 - c) * (R // 2), R // 2), None
            else:
                rows, cols = None, ((1 - c) * (C // 2), C // 2)
            copies.append((_win(refs[m], (), rows, cols), refs[n + m], (x, y, 1 - c)))
        return copies

    ssem, rsem, thru, token = split_start(name + "_pair_start", list(grads) + landing, n, plan_a, after)
    return (name, axes, plan_a, ssem, rsem, thru), token


def reduce_ici_start(handle, sc, after):
    name, axes, plan_a, ssem, rsem, thru = handle
    n = len(axes)
    done = split_wait(name + "_pair_wait", thru, ssem, rsem, plan_a, after)
    grads, a_bufs = done[:n], done[n:]
    b_bufs = [pairsum(g, a, ax, sc, f"{name}_pairsum{m}") for m, (g, a, ax) in enumerate(zip(grads, a_bufs, axes))]

    def piece_shape(m):
        R, C = a_bufs[m].shape
        return (R, C // N_SHARD) if axes[m] == 1 else (R // N_SHARD, C)

    def piece_win(m, s):
        R, C = piece_shape(m)
        if axes[m] == 1:
            return None, (s * C, C)
        return (s * R, R), None

    landing = [lax.empty((N_SHARD - 1,) + piece_shape(m), BF16) for m in range(n)]

    def plan_c(me, refs):
        copies = []
        for m in range(n):
            for j, (peer, ps) in enumerate(_plane_peers(me)):
                rows, cols = piece_win(m, ps)
                copies.append((_win(refs[m], (), rows, cols), refs[n + m].at[j], peer))
        return copies

    ssem, rsem, thru, token = split_start(name + "_ici_start", b_bufs + landing, 3 * n, plan_c, after)
    return (name, axes, sc, plan_c, ssem, rsem, thru), token


def reduce_finish(handle, after):
    name, axes, sc, plan_c, ssem, rsem, thru = handle
    n = len(axes)
    done = split_wait(name + "_ici_wait", thru, ssem, rsem, plan_c, after)
    b_bufs, c_bufs = done[:n], done[n:]
    shards = [shardsum(b, cb, ax, sc, f"{name}_shardsum{m}") for m, (b, cb, ax) in enumerate(zip(b_bufs, c_bufs, axes))]

    def plan_e(me, refs):
        x, y, c = me
        copies = []
        for m in range(n):
            R, C = shards[m].shape
            if axes[m] == 1:
                rows, cols = (c * (R // 2), R // 2), None
            else:
                rows, cols = None, (c * (C // 2), C // 2)
            copies.append((_win(refs[m], (), rows, cols), _win(refs[m], (), rows, cols), (x, y, 1 - c)))
        return copies

    ssem, rsem, thru, token = split_start(name + "_swap_start", shards, n, plan_e, after)
    return (name, plan_e, ssem, rsem, thru), token


def reduce_swap_wait(handle, after):
    name, plan_e, ssem, rsem, thru = handle
    return split_wait(name + "_swap_wait", thru, ssem, rsem, plan_e, after)


def gather_start(name, fulls, shard_shapes, axes, after):
    n = len(fulls)

    def win(ref, m, s, half):
        R, C = shard_shapes[m]
        r0 = s * R if axes[m] == 0 else 0
        return _win(ref, (), (r0 + half * (R // 2), R // 2), None if axes[m] == 0 else (s * C, C))

    def plan_ici(me, refs):
        s, c = _shard_of(me), me[2]
        return [(win(refs[m], m, s, c), win(refs[m], m, s, c), peer) for m in range(n) for peer, _ in _plane_peers(me)]

    def plan_fwd(me, refs):
        x, y, c = me
        return [(win(refs[m], m, ps, c), win(refs[m], m, ps, c), (x, y, 1 - c)) for m in range(n) for _, ps in _plane_peers(me)]

    ssem, rsem, thru, token = split_start(name + "_start", fulls, 3 * n, plan_ici, after)
    return (name, plan_ici, plan_fwd, ssem, rsem, thru), token


def gather_land(handle, after):
    name, plan_ici, plan_fwd, ssem, rsem, thru = handle
    got = split_wait(name + "_wait", thru, ssem, rsem, plan_ici, after)
    ssem, rsem, thru, token = split_start(name + "_fwd_start", got, 3 * len(got), plan_fwd, after)
    return (name, plan_fwd, ssem, rsem, thru), token


def gather_take(handle, after):
    name, plan_fwd, ssem, rsem, thru = handle
    return split_wait(name + "_fwd_wait", thru, ssem, rsem, plan_fwd, after)


SMALL_FLIPS = ((0, 0, 1), (0, 1, 0), (1, 0, 0))


def small_swap_start(axis, cur, after):
    def plan(me, refs):
        peer = tuple(v + f * (1 - 2 * v) for v, f in zip(me, SMALL_FLIPS[axis]))
        return [(refs[0], refs[1], peer)]

    ssem, rsem, thru, token = split_start(f"small_swap{axis}_start", [cur, lax.empty(cur.shape, cur.dtype)], 1, plan, after)
    return (axis, plan, ssem, rsem, thru), token


def small_swap_finish(handle, after):
    axis, plan, ssem, rsem, thru = handle
    cur, got = split_wait(f"small_swap{axis}_wait", thru, ssem, rsem, plan, after)
    R, C = cur.shape
    (cur,) = tilemap(lambda a, b: (a + b,), [(cur, 't', 0), (got, 't', 0)], [(F32, 't')], M=R, N=C,
                     tm=_pick(R, (2048, 1024, 512, 256)), tn=C, name=f"small_add{axis}")
    return cur


def rms_fwd(x, g, name):
    M, D = x.shape
    tm = _pick(M, (256, 128))

    def fn(xv, gv):
        r = lax.rsqrt(jnp.mean(xv * xv, axis=-1, keepdims=True) + EPS)
        return (xv * r * gv,)

    (h,) = tilemap(fn, [(x, 't', 0), (g, 'r', 0)], [(BF16, 't')], M=M, N=D, tm=tm, tn=D, name=name)
    return h


def rms_bwd(x, g, dh, dres, name):
    M, D = x.shape
    tm = _pick(M, (256, 128))

    def fn(xv, gv, dhv, drv):
        r = lax.rsqrt(jnp.mean(xv * xv, axis=-1, keepdims=True) + EPS)
        xh = xv * r
        dxh = dhv * gv
        m = jnp.mean(dxh * xh, axis=-1, keepdims=True)
        dx = drv + r * (dxh - xh * m)
        return dx, dx, jnp.sum(dhv * xh, axis=0, keepdims=True)

    return tilemap(fn, [(x, 't', 0), (g, 'r', 0), (dh, 't', 0), (dres, 't', 0)],
                   [(F32, 't'), (BF16, 't'), (F32, 'a')], M=M, N=D, tm=tm, tn=D, name=name)


def ffn_fwd(x, g, wg, wu, wd, tag, prefetch):
    h = rms_fwd(x, g, f"ffn_norm_{tag}")

    def ep(accs, ex):
        a, b = accs
        return a, b, a * _sigmoid(a) * b

    a, b, s = matmul([(h, wg), (h, wu)], 'nn', [F32, F32, BF16], epilogue=ep, tm=1024, tn=512,
                     name=f"ffn_gateup_{tag}")
    (xo,) = matmul([(s, wd)], 'nn', [F32], epilogue=lambda accs, ex: (ex[0] + 0.5 * accs[0],),
                   extras=[(x, 't')], tm=512, tn=512, name=f"ffn_down_{tag}", after=prefetch(s))
    return xo, (x, h, a, b, s)


def ffn_bwd(saved, g, wg, wu, wd, dxo, dxo_bf, tag, after, on_grads):
    x, h, a, b, s = saved

    def ep(accs, ex):
        ds = 0.5 * accs[0]
        av, bv = ex
        sg = _sigmoid(av)
        return ds * bv * (sg * (1.0 + av * (1.0 - sg))), ds * (av * sg)

    da, db = matmul([(dxo_bf, wd)], 'nt', [BF16, BF16], epilogue=ep, extras=[(a, 't'), (b, 't')],
                    tm=1024, tn=512, name=f"ffn_dact_{tag}", after=after)
    (dwd,) = matmul([(s, dxo_bf)], 'tn', [BF16], epilogue=lambda accs, ex: (0.5 * accs[0],),
                    tm=512, tn=512, name=f"ffn_dwd_{tag}")
    dwg, dwu = matmul([(h, da), (h, db)], 'tn', [BF16, BF16], epilogue=lambda accs, ex: tuple(accs),
                      tm=1024, tn=512, name=f"ffn_dwgu_{tag}")
    started = on_grads([dwg, dwu, dwd])
    (dh,) = matmul([(da, wg), (db, wu)], 'nt', [F32], tm=512, tn=256, name=f"ffn_dh_{tag}", after=started)
    dx, dx_bf, dg = rms_bwd(x, g, dh, dxo, f"ffn_dnorm_{tag}")
    return dx, dx_bf, dg


def _tril_mask(n):
    return lax.broadcasted_iota(jnp.int32, (n, n), 0) >= lax.broadcasted_iota(jnp.int32, (n, n), 1)


GMLP_CHUNKS_PER_STEP = 4


def _gmlp_specs(L, half, n_grp, chunk):
    gd = half // n_grp
    cps = _pick(L // chunk, (GMLP_CHUNKS_PER_STEP, 2, 1))
    rows = cps * chunk
    specs = [pl.BlockSpec((rows, gd), lambda g, n: (n, g)),
             pl.BlockSpec((rows, gd), lambda g, n: (n, n_grp + g)),
             pl.BlockSpec((1, gd), lambda g, n: (0, g)),
             pl.BlockSpec((1, gd), lambda g, n: (0, g)),
             pl.BlockSpec((None, chunk, chunk), lambda g, n: (g, 0, 0)),
             pl.BlockSpec((None, chunk, 1), lambda g, n: (g, 0, 0))]
    return gd, cps, specs


def _gmlp_gate_values(zu, zv, lg, lb, ws, bs):
    u, v = _gelu(zu), _gelu(zv)
    mu = jnp.mean(v, axis=-1, keepdims=True)
    d = v - mu
    rstd = lax.rsqrt(jnp.mean(d * d, axis=-1, keepdims=True) + EPS)
    vhat = d * rstd
    vn = vhat * lg + lb
    w = jnp.where(_tril_mask(ws.shape[0]), ws, 0.0).astype(BF16)
    sv = jnp.dot(w, vn.astype(BF16), preferred_element_type=F32) + bs
    return u, vhat, rstd, vn, w, sv


def gmlp_gate_fwd(zpre, ln_g, ln_b, w_s, b_s):
    L, half = zpre.shape[0], zpre.shape[1] // 2
    n_grp, chunk = w_s.shape[0], w_s.shape[1]
    gd, cps, specs = _gmlp_specs(L, half, n_grp, chunk)

    def body(zu, zv, lg, lb, ws, bs, o):
        for k in range(cps):
            rows = pl.ds(k * chunk, chunk)
            u, _, _, _, _, sv = _gmlp_gate_values(zu[rows, :], zv[rows, :], lg[...], lb[...], ws[...], bs[...])
            o[rows, :] = (u * sv).astype(o.dtype)

    return _pcall(body, name="gmlp_gate", grid=(n_grp, L // (cps * chunk)), in_specs=specs,
                  out_specs=pl.BlockSpec((cps * chunk, gd), lambda g, n: (n, g)),
                  out_shape=jax.ShapeDtypeStruct((L, half), BF16), compiler_params=_params())(
        zpre, zpre, ln_g, ln_b, w_s, b_s)


def gmlp_gate_bwd(zpre, ln_g, ln_b, w_s, b_s, dgated):
    L, half = zpre.shape[0], zpre.shape[1] // 2
    n_grp, chunk = w_s.shape[0], w_s.shape[1]
    gd, cps, specs = _gmlp_specs(L, half, n_grp, chunk)
    tile = pl.BlockSpec((cps * chunk, gd), lambda g, n: (n, g))
    specs = specs + [tile]

    def body(zu, zv, lg, lb, ws, bs, dg, dzu, dzv, dws, dbs, dlg, dlb):
        @pl.when(pl.program_id(1) == 0)
        def _():
            dws[...] = jnp.zeros_like(dws)
            dbs[...] = jnp.zeros_like(dbs)
            dlg[...] = jnp.zeros_like(dlg)
            dlb[...] = jnp.zeros_like(dlb)

        lgv = lg[...]
        for k in range(cps):
            rows = pl.ds(k * chunk, chunk)
            zuv, zvv = zu[rows, :], zv[rows, :]
            u, vhat, rstd, vn, w, sv = _gmlp_gate_values(zuv, zvv, lgv, lb[...], ws[...], bs[...])
            dgv = dg[rows, :]
            du = dgv * sv
            dsv = dgv * u
            dsv_bf = dsv.astype(BF16)
            dw = lax.dot_general(dsv_bf, vn.astype(BF16), _DIMS['nt'], preferred_element_type=F32)
            dvn = lax.dot_general(w, dsv_bf, _DIMS['tn'], preferred_element_type=F32)
            dvhat = dvn * lgv
            dv = rstd * (dvhat - jnp.mean(dvhat, axis=-1, keepdims=True)
                         - vhat * jnp.mean(dvhat * vhat, axis=-1, keepdims=True))
            dzu[rows, :] = (du * _gelu_grad(zuv)).astype(dzu.dtype)
            dzv[rows, :] = (dv * _gelu_grad(zvv)).astype(dzv.dtype)
            dws[...] += jnp.where(_tril_mask(chunk), dw, 0.0)
            dbs[...] += jnp.sum(dsv, axis=1, keepdims=True)
            dlg[...] += jnp.sum(dvn * vhat, axis=0, keepdims=True)
            dlb[...] += jnp.sum(dvn, axis=0, keepdims=True)

    vec = pl.BlockSpec((1, gd), lambda g, n: (0, g))
    return _pcall(body, name="gmlp_gate_bwd", grid=(n_grp, L // (cps * chunk)), in_specs=specs,
                  out_specs=[tile, tile, pl.BlockSpec((None, chunk, chunk), lambda g, n: (g, 0, 0)),
                             pl.BlockSpec((None, chunk, 1), lambda g, n: (g, 0, 0)), vec, vec],
                  out_shape=[jax.ShapeDtypeStruct((L, half), BF16), jax.ShapeDtypeStruct((L, half), BF16),
                             jax.ShapeDtypeStruct((n_grp, chunk, chunk), F32),
                             jax.ShapeDtypeStruct((n_grp, chunk, 1), F32),
                             jax.ShapeDtypeStruct((1, half), F32), jax.ShapeDtypeStruct((1, half), F32)],
                  compiler_params=_params())(zpre, zpre, ln_g, ln_b, w_s, b_s, dgated)


def gmlp_fwd(x, g, w_in, ln_g, ln_b, w_s, b_s, w_out, prefetch):
    h = rms_fwd(x, g, "gmlp_norm")
    (zpre,) = matmul([(h, w_in)], 'nn', [F32], tm=1024, tn=512, name="gmlp_in")
    gated = gmlp_gate_fwd(zpre, ln_g, ln_b, w_s, b_s)
    (xo,) = matmul([(gated, w_out)], 'nn', [F32], epilogue=lambda accs, ex: (ex[0] + accs[0],),
                   extras=[(x, 't')], tm=512, tn=512, name="gmlp_out", after=prefetch(gated))
    return xo, (x, h, zpre, gated)


def gmlp_bwd(saved, g, w_in, ln_g, ln_b, w_s, b_s, w_out, dxo, dxo_bf, after, on_grads):
    x, h, zpre, gated = saved
    (dgated,) = matmul([(dxo_bf, w_out)], 'nt', [F32], tm=512, tn=512, name="gmlp_dgated", after=after)
    (dw_out,) = matmul([(gated, dxo_bf)], 'tn', [BF16], tm=512, tn=512, name="gmlp_dwout")
    dzu, dzv, dws, dbs, dlg, dlb = gmlp_gate_bwd(zpre, ln_g, ln_b, w_s, b_s, dgated)
    dz = jnp.concatenate([dzu, dzv], axis=1)
    (dw_in,) = matmul([(h, dz)], 'tn', [BF16], tm=1024, tn=512, name="gmlp_dwin")
    started = on_grads([dw_in, dw_out])
    (dh,) = matmul([(dz, w_in)], 'nt', [F32], tm=256, tn=256, name="gmlp_dh", after=started)
    dx, dx_bf, dg = rms_bwd(x, g, dh, dxo, "gmlp_dnorm")
    return dx, dx_bf, dg, dlg, dlb, dws, dbs


def _s5_disc(lr, li, ldt, br, bi):
    dt = jnp.exp(ldt)
    mag = jnp.exp(lr * dt)
    ang = li * dt
    ar = mag * jnp.cos(ang)
    ai = mag * jnp.sin(ang)
    den = lr * lr + li * li
    nr = ar - 1.0
    zr = (nr * lr + ai * li) / den
    zi = (ai * lr - nr * li) / den
    return ar, ai, zr[None] * br - zi[None] * bi, zr[None] * bi + zi[None] * br


def s5_disc_fwd(lr, li, ldt, br, bi):
    def body(lr_r, li_r, ldt_r, br_r, bi_r, ar_o, ai_o, bbr_o, bbi_o):
        res = _s5_disc(lr_r[...], li_r[...], ldt_r[...], br_r[...], bi_r[...])
        for o, v in zip((ar_o, ai_o, bbr_o, bbi_o), res):
            o[...] = v

    shp = lambda a: jax.ShapeDtypeStruct(a.shape, F32)
    return _pcall(body, name="s5_disc", out_shape=[shp(lr), shp(lr), shp(br), shp(br)],
                  compiler_params=_params())(lr, li, ldt, br, bi)


def s5_disc_bwd(lr, li, ldt, br, bi, dar, dai, dbbr, dbbi):
    def body(lr_r, li_r, ldt_r, br_r, bi_r, dar_r, dai_r, dbbr_r, dbbi_r, o1, o2, o3, o4, o5):
        _, vjp = jax.vjp(_s5_disc, lr_r[...], li_r[...], ldt_r[...], br_r[...], bi_r[...])
        res = vjp((dar_r[...], dai_r[...], dbbr_r[...], dbbi_r[...]))
        for o, v in zip((o1, o2, o3, o4, o5), res):
            o[...] = v

    shp = lambda a: jax.ShapeDtypeStruct(a.shape, F32)
    return _pcall(body, name="s5_disc_bwd", out_shape=[shp(lr), shp(lr), shp(ldt), shp(br), shp(br)],
                  compiler_params=_params())(lr, li, ldt, br, bi, dar, dai, dbbr, dbbi)


def blockdiag_matmul(pairs, outs, *, epilogue=None, extras=(), name):
    a0, b0 = pairs[0]
    M = a0.shape[0]
    T, wa, wo = b0.shape
    tm = _pick(M, (512, 256, 128))
    n_p, n_e = len(pairs), len(extras)
    in_specs, args = [], []
    for a, b in pairs:
        in_specs += [pl.BlockSpec((tm, wa), lambda k, i: (i, k)), pl.BlockSpec((None, wa, wo), lambda k, i: (k, 0, 0))]
        args += [a, b]
    for arr, kind in extras:
        in_specs.append(pl.BlockSpec((tm, wo), lambda k, i: (i, k)) if kind == 't'
                        else pl.BlockSpec((1, wo), lambda k, i: (0, k)))
        args.append(arr)

    def body(*refs):
        accs = [jnp.dot(refs[2 * p][...].astype(BF16), refs[2 * p + 1][...], preferred_element_type=F32)
                for p in range(n_p)]
        ex = [r[...] for r in refs[2 * n_p:2 * n_p + n_e]]
        res = tuple(accs) if epilogue is None else epilogue(accs, ex)
        for ref, v in zip(refs[2 * n_p + n_e:], res):
            ref[...] = v.astype(ref.dtype)

    return _pcall(body, name=name, grid=(T, M // tm), in_specs=in_specs,
                  out_specs=[pl.BlockSpec((tm, wo), lambda k, i: (i, k)) for _ in outs],
                  out_shape=[jax.ShapeDtypeStruct((M, T * wo), dt) for dt in outs],
                  compiler_params=_params())(*args)


def blockdiag_outer(pairs, name):
    M = pairs[0][0].shape[0]
    n_p = len(pairs)
    shapes = []
    in_specs, args = [], []
    tm = _pick(M, (512, 256, 128))
    T = None
    for a, b, wa, wb in pairs:
        T = a.shape[1] // wa
        in_specs += [pl.BlockSpec((tm, wa), lambda k, i: (i, k)), pl.BlockSpec((tm, wb), lambda k, i: (i, k))]
        args += [a, b]
        shapes.append((T, wa, wb))

    def body(*refs):
        @pl.when(pl.program_id(1) == 0)
        def _():
            for o in refs[2 * n_p:]:
                o[...] = jnp.zeros_like(o)
        for p in range(n_p):
            refs[2 * n_p + p][...] += lax.dot_general(refs[2 * p][...].astype(BF16), refs[2 * p + 1][...].astype(BF16),
                                                      _DIMS['tn'], preferred_element_type=F32)

    return _pcall(body, name=name, grid=(T, M // tm), in_specs=in_specs,
                  out_specs=[pl.BlockSpec((None, s[1], s[2]), lambda k, i: (k, 0, 0)) for s in shapes],
                  out_shape=[jax.ShapeDtypeStruct(s, F32) for s in shapes], compiler_params=_params())(*args)


def s5_scan(br, bi, ar, ai, reverse, want_prev, name):
    L, S = br.shape
    ln = _pick(S, (SCAN_LANES, 512, 256, 128))
    tb = _pick(L, (512, 256, 128))
    n_t = L // tb
    n_q = tb // 8

    def tmap(j, t):
        return ((n_t - 1 - t) if reverse else t, j)

    blk = pl.BlockSpec((tb, ln), tmap)
    vec = pl.BlockSpec((1, ln), lambda j, t: (0, j))

    def cmul(xr, xi, yr, yi):
        return xr * yr - xi * yi, xr * yi + xi * yr

    n_out = 4 if want_prev else 2

    def body(br_r, bi_r, ar_r, ai_r, *rest):
        outs, (cr_s, ci_s) = rest[:n_out], rest[n_out:]

        @pl.when(pl.program_id(1) == 0)
        def _():
            cr_s[...] = jnp.zeros_like(cr_s)
            ci_s[...] = jnp.zeros_like(ci_s)

        a1r, a1i = ar_r[...], ai_r[...]
        a2r, a2i = cmul(a1r, a1i, a1r, a1i)
        a4r, a4i = cmul(a2r, a2i, a2r, a2i)
        a8r, a8i = cmul(a4r, a4i, a4r, a4i)
        row = lax.broadcasted_iota(jnp.int32, (8, ln), 0)
        dist = (7 - row) if reverse else row
        pwr, pwi = jnp.broadcast_to(a1r, (8, ln)), jnp.broadcast_to(a1i, (8, ln))
        for bit, (er, ei) in ((1, (a1r, a1i)), (2, (a2r, a2i)), (4, (a4r, a4i))):
            nr, ni = cmul(pwr, pwi, er, ei)
            sel = (dist & bit) != 0
            pwr, pwi = jnp.where(sel, nr, pwr), jnp.where(sel, ni, pwi)
        last = 0 if reverse else 7
        steps = [(d, jnp.where(dist >= d, er, 0.0), jnp.where(dist >= d, ei, 0.0))
                 for d, (er, ei) in ((1, (a1r, a1i)), (2, (a2r, a2i)), (4, (a4r, a4i)))]

        def step(q, carry):
            cr, ci = carry
            qq = (n_q - 1 - q) if reverse else q
            rows = pl.ds(pl.multiple_of(qq * 8, 8), 8)
            xr, xi = br_r[rows, :], bi_r[rows, :]
            for d, er, ei in steps:
                sr = pltpu.roll(xr, (8 - d) if reverse else d, 0)
                si = pltpu.roll(xi, (8 - d) if reverse else d, 0)
                mr, mi = cmul(sr, si, er, ei)
                xr, xi = xr + mr, xi + mi
            lr, li = xr[last:last + 1, :], xi[last:last + 1, :]
            kr, ki = cmul(pwr, pwi, cr, ci)
            xr, xi = xr + kr, xi + ki
            outs[0][rows, :] = xr
            outs[1][rows, :] = xi
            if want_prev:
                outs[2][rows, :] = jnp.where(dist >= 1, pltpu.roll(xr, 7 if reverse else 1, 0), cr)
                outs[3][rows, :] = jnp.where(dist >= 1, pltpu.roll(xi, 7 if reverse else 1, 0), ci)
            nr, ni = cmul(a8r, a8i, cr, ci)
            return lr + nr, li + ni

        cr, ci = lax.fori_loop(0, n_q, step, (cr_s[...], ci_s[...]), unroll=2)
        cr_s[...] = cr
        ci_s[...] = ci

    shp = jax.ShapeDtypeStruct((L, S), F32)
    return _pcall(body, name=name, grid=(S // ln, n_t), in_specs=[blk, blk, vec, vec],
                  out_specs=[blk] * n_out, out_shape=[shp] * n_out,
                  scratch_shapes=[pltpu.VMEM((1, ln), F32), pltpu.VMEM((1, ln), F32)],
                  compiler_params=_params())(br, bi, ar, ai)


def _to_blockdiag(m, tile_groups):
    G, A, B = m.shape
    T = G // tile_groups
    eye = jnp.eye(tile_groups, dtype=m.dtype)
    t = m.reshape(T, tile_groups, A, 1, B) * eye[None, :, None, :, None]
    return t.reshape(T, tile_groups * A, tile_groups * B)


def _from_blockdiag(t, tile_groups):
    T, RA, RB = t.shape
    A, B = RA // tile_groups, RB // tile_groups
    d = jnp.diagonal(t.reshape(T, tile_groups, A, tile_groups, B), axis1=1, axis2=3)
    return jnp.moveaxis(d, 3, 1).reshape(T * tile_groups, A, B)


def s5_fwd(x, g, w_in, lam_re, lam_im, log_dt, b_re, b_im, c_re, c_im, d_skip, w_out, prefetch):
    G, P, H = b_re.shape
    tg = min(SSM_TILE_GROUPS, G)
    h = rms_fwd(x, g, "s5_norm")
    (u,) = matmul([(h, w_in)], 'nn', [F32], tm=512, tn=512, name="s5_in")
    br_t, bi_t = jnp.transpose(b_re, (2, 0, 1)), jnp.transpose(b_im, (2, 0, 1))
    ar, ai, bbr, bbi = s5_disc_fwd(lam_re, lam_im, log_dt, br_t, bi_t)
    bbr_g, bbi_g = jnp.transpose(bbr, (1, 0, 2)), jnp.transpose(bbi, (1, 0, 2))
    bd_br, bd_bi = _to_blockdiag(bbr_g.astype(BF16), tg), _to_blockdiag(bbi_g.astype(BF16), tg)
    bur, bui = blockdiag_matmul([(u, bd_br), (u, bd_bi)], [F32, F32], name="s5_bu")
    a_r, a_i = ar.reshape(1, G * P), ai.reshape(1, G * P)
    hr, hi, hpr, hpi = s5_scan(bur, bui, a_r, a_i, False, True, "s5_scan")
    c_pg_r = jnp.transpose(c_re, (0, 2, 1)).astype(BF16)
    c_pg_i = jnp.transpose(c_im, (0, 2, 1)).astype(BF16)
    bd_cr, bd_nci = _to_blockdiag(c_pg_r, tg), _to_blockdiag(-c_pg_i, tg)

    def ep(accs, ex):
        y = accs[0] + accs[1] + ex[1] * ex[0]
        return y, _gelu(y)

    y, act = blockdiag_matmul([(hr, bd_cr), (hi, bd_nci)], [F32, BF16], epilogue=ep,
                              extras=[(u, 't'), (d_skip, 'r')], name="s5_y")
    (o,) = matmul([(act, w_out)], 'nn', [F32], tm=512, tn=512, name="s5_out", after=prefetch(act))
    M, D = x.shape
    tm = _pick(M, (256, 128))
    (xo,) = tilemap(lambda xv, val, gt: (xv + val * _sigmoid(gt),), [(x, 't', 0), (o, 't', 0), (o, 't', 1)],
                    [(F32, 't')], M=M, N=D, tm=tm, tn=D, name="s5_glu")
    saved = (x, h, u, hr, hi, hpr, hpi, y, act, o, a_r, a_i, bd_br, bd_bi, bd_cr, bd_nci, br_t, bi_t)
    return xo, saved


def s5_bwd(saved, g, w_in, lam_re, lam_im, log_dt, b_re, d_skip, w_out, dxo, after, on_grads):
    x, h, u, hr, hi, hpr, hpi, y, act, o, a_r, a_i, bd_br, bd_bi, bd_cr, bd_nci, br_t, bi_t = saved
    G, P, H = b_re.shape
    tg = min(SSM_TILE_GROUPS, G)
    M, D = x.shape
    tm = _pick(M, (256, 128))

    def glu_bwd(dv, val, gt):
        sg = _sigmoid(gt)
        return dv * sg, dv * val * sg * (1.0 - sg)

    dval, dgate = tilemap(glu_bwd, [(dxo, 't', 0), (o, 't', 0), (o, 't', 1)], [(BF16, 't'), (BF16, 't')],
                          M=M, N=D, tm=tm, tn=D, name="s5_dglu", after=after)
    do = jnp.concatenate([dval, dgate], axis=1)
    (dw_out,) = matmul([(act, do)], 'tn', [BF16], tm=512, tn=512, name="s5_dwout")
    (dact,) = matmul([(do, w_out)], 'nt', [F32], tm=512, tn=512, name="s5_dact")
    dy, dd = tilemap(lambda da, yv, uv: (da * _gelu_grad(yv), jnp.sum(da * _gelu_grad(yv) * uv, axis=0, keepdims=True)),
                     [(dact, 't', 0), (y, 't', 0), (u, 't', 0)], [(F32, 't'), (F32, 'a')],
                     M=M, N=D, tm=tm, tn=D, name="s5_dy")
    bd_crT, bd_nciT = jnp.transpose(bd_cr, (0, 2, 1)), jnp.transpose(bd_nci, (0, 2, 1))
    dhr, dhi = blockdiag_matmul([(dy, bd_crT), (dy, bd_nciT)], [F32, F32], name="s5_dh")
    gr, gi = s5_scan(dhr, dhi, a_r, -a_i, True, False, "s5_scan_rev")
    S = G * P
    tms = _pick(M, (128,))

    def da_fn(grv, giv, hprv, hpiv):
        return (jnp.sum(grv * hprv + giv * hpiv, axis=0, keepdims=True),
                jnp.sum(giv * hprv - grv * hpiv, axis=0, keepdims=True))

    dar, dai = tilemap(da_fn, [(gr, 't', 0), (gi, 't', 0), (hpr, 't', 0), (hpi, 't', 0)], [(F32, 'a'), (F32, 'a')],
                       M=M, N=S, tm=tms, tn=_pick(S, (2048, 1024, 512)), name="s5_dabar")
    wa, wb = tg * H, tg * P
    xc_r, xc_i, xb_r, xb_i = blockdiag_outer([(dy, hr, wa, wb), (dy, hi, wa, wb), (u, gr, wa, wb), (u, gi, wa, wb)],
                                             "s5_dcb")
    dc_re = _from_blockdiag(xc_r, tg)
    dc_im = -_from_blockdiag(xc_i, tg)
    dbb_r = jnp.transpose(_from_blockdiag(xb_r, tg), (1, 0, 2))
    dbb_i = jnp.transpose(_from_blockdiag(xb_i, tg), (1, 0, 2))
    dlr, dli, dldt, dbr_t, dbi_t = s5_disc_bwd(lam_re, lam_im, log_dt, br_t, bi_t,
                                               dar.reshape(G, P), dai.reshape(G, P), dbb_r, dbb_i)
    db_re, db_im = jnp.transpose(dbr_t, (1, 2, 0)), jnp.transpose(dbi_t, (1, 2, 0))
    bd_brT, bd_biT = jnp.transpose(bd_br, (0, 2, 1)), jnp.transpose(bd_bi, (0, 2, 1))
    (du,) = blockdiag_matmul([(gr, bd_brT), (gi, bd_biT)], [BF16],
                             epilogue=lambda accs, ex: (accs[0] + accs[1] + ex[1] * ex[0],),
                             extras=[(dy, 't'), (d_skip, 'r')], name="s5_du")
    (dw_in,) = matmul([(h, du)], 'tn', [BF16], tm=512, tn=512, name="s5_dwin")
    started = on_grads([dw_in, dw_out])
    (dh,) = matmul([(du, w_in)], 'nt', [F32], tm=512, tn=512, name="s5_dhin", after=started)
    dx, dx_bf, dg = rms_bwd(x, g, dh, dxo, "s5_dnorm")
    return dx, dx_bf, dg, dlr, dli, dldt, db_re, db_im, dc_re, dc_im, dd


def ple_fwd(x, g, p_emb, w_gate, w_proj, tag, prefetch):
    h = rms_fwd(x, g, f"ple_norm_{tag}")
    (q,) = matmul([(p_emb, w_proj)], 'nn', [F32], tm=512, tn=512, name=f"ple_proj_{tag}")

    def ep(accs, ex):
        gt = _sigmoid(accs[0])
        return ex[0] + gt * ex[1], gt

    xo, gate = matmul([(h, w_gate)], 'nn', [F32, F32], epilogue=ep, extras=[(x, 't'), (q, 't')],
                      tm=512, tn=512, name=f"ple_gate_{tag}", after=prefetch(q))
    return xo, (x, h, q, gate)


def ple_bwd(saved, g, p_emb, w_gate, dxo, tag, after, on_grads):
    x, h, q, gate = saved
    M, D = x.shape
    tm = _pick(M, (256, 128))
    dq, dpre = tilemap(lambda dv, qv, gv: (dv * gv, dv * qv * gv * (1.0 - gv)),
                       [(dxo, 't', 0), (q, 't', 0), (gate, 't', 0)], [(BF16, 't'), (BF16, 't')],
                       M=M, N=D, tm=tm, tn=D, name=f"ple_dgate_{tag}", after=after)
    (dw_proj,) = matmul([(p_emb, dq)], 'tn', [BF16], tm=256, tn=512, name=f"ple_dwproj_{tag}")
    (dw_gate,) = matmul([(h, dpre)], 'tn', [BF16], tm=512, tn=512, name=f"ple_dwgate_{tag}")
    started = on_grads([dw_gate, dw_proj])
    (dh,) = matmul([(dpre, w_gate)], 'nt', [F32], tm=512, tn=512, name=f"ple_dh_{tag}", after=started)
    dx, dx_bf, dg = rms_bwd(x, g, dh, dxo, f"ple_dnorm_{tag}")
    return dx, dx_bf, dg


def loss_head(x, g, target):
    M, D = x.shape
    tm = _pick(M, (256, 128))

    def fn(xv, gv, tv):
        r = lax.rsqrt(jnp.mean(xv * xv, axis=-1, keepdims=True) + EPS)
        xh = xv * r
        e = xh * gv - tv
        dy = e * (1.0 / D)
        dxh = dy * gv
        m = jnp.mean(dxh * xh, axis=-1, keepdims=True)
        dx = r * (dxh - xh * m)
        return jnp.sum(e * e, axis=0, keepdims=True), dx, dx, jnp.sum(dy * xh, axis=0, keepdims=True)

    return tilemap(fn, [(x, 't', 0), (g, 'r', 0), (target, 't', 0)],
                   [(F32, 'a'), (F32, 't'), (BF16, 't'), (F32, 'a')], M=M, N=D, tm=tm, tn=D, name="loss_head")


def _adamw_math(wv, gv, mv, vv):
    mn = ADAM_B1 * mv + (1.0 - ADAM_B1) * gv
    vn = ADAM_B2 * vv + (1.0 - ADAM_B2) * (gv * gv)
    m_hat = mn / (1.0 - ADAM_B1 ** ADAM_STEP)
    v_hat = vn / (1.0 - ADAM_B2 ** ADAM_STEP)
    return -ADAM_LR * (m_hat / (jnp.sqrt(v_hat) + ADAM_EPS) + ADAM_WD * wv), mn, vn


def adamw_into(w, lead, g, m, v, carry, name, after):
    R, C = w.shape[-2:]
    lead = tuple(lead)
    tm = _row_tile(R, C)
    blk = pl.BlockSpec((None,) * len(lead) + (tm, C), lambda i: lead + (i, 0))
    n_carry = 0 if carry is None else 4

    def body(*refs):
        w_r, g_r, m_r, v_r = refs[:4]
        g_o, d_o, m_o, v_o, token = refs[5 + n_carry:]
        gv = g_r[...]
        d, mn, vn = _adamw_math(w_r[...], gv, m_r[...], v_r[...])
        g_o[...] = gv
        d_o[...] = d
        m_o[...] = mn
        v_o[...] = vn
        token[...] = jnp.zeros_like(token)

    shp = jax.ShapeDtypeStruct(w.shape, F32)
    res = _pcall(body, name=name, grid=(R // tm,),
                 in_specs=[blk, pl.BlockSpec((tm, C), lambda i: (i, 0)), blk, blk]
                 + [pl.BlockSpec(memory_space=pl.ANY)] * (1 + n_carry),
                 out_specs=[blk] * 4 + [pl.BlockSpec((8, 128), lambda i: (0, 0))],
                 out_shape=[shp] * 4 + [jax.ShapeDtypeStruct((8, 128), F32)],
                 input_output_aliases={5 + k: k for k in range(n_carry)},
                 compiler_params=_params())(w, g, m, v, after, *(carry or ()))
    return tuple(res[:4]), res[4]


def adamw(w, g, m, v, name):
    w2, g2, m2, v2 = _as2d(w), g.reshape(_as2d(w).shape), _as2d(m), _as2d(v)
    R, C = w2.shape
    tm = _row_tile(R, C)

    d, mn, vn = tilemap(_adamw_math, [(w2, 't', 0), (g2, 't', 0), (m2, 't', 0), (v2, 't', 0)],
                        [(F32, 't'), (F32, 't'), (F32, 't')], M=R, N=C, tm=tm, tn=C, name=name)
    return d.reshape(w.shape), mn.reshape(w.shape), vn.reshape(w.shape)


WEIGHT_NAMES = ['norm_g', 'final_norm_g', 'ffn_w_gate', 'ffn_w_up', 'ffn_w_down', 'gmlp_w_in', 'gmlp_ln_g',
                'gmlp_ln_b', 'gmlp_w_s', 'gmlp_b_s', 'gmlp_w_out', 's5_w_in', 's5_lam_re', 's5_lam_im',
                's5_log_dt', 's5_b_re', 's5_b_im', 's5_c_re', 's5_c_im', 's5_d', 's5_w_out', 'ple_w_gate',
                'ple_w_proj']
BIG = {'ffn_w_gate': 1, 'ffn_w_up': 1, 'ffn_w_down': 0, 'gmlp_w_in': 1, 'gmlp_w_out': 0, 's5_w_in': 0,
       's5_w_out': 1, 'ple_w_gate': 0, 'ple_w_proj': 1}


def _blocks(depth):
    out = []
    for i in range(depth):
        for k, half in enumerate("ab"):
            ffn = [(n, (i, k)) for n in ('ffn_w_gate', 'ffn_w_up', 'ffn_w_down')]
            if k == 1:
                out.append((f"ffn{i}b", ffn))
                out.append((f"ple{i}", [('ple_w_gate', (i,)), ('ple_w_proj', (i,))]))
            else:
                out.append((f"ffn{i}a", ffn))
                mix = 'gmlp' if i % 2 == 0 else 's5'
                out.append((f"{mix}{i}", [(f'{mix}_w_in', (i // 2,)), (f'{mix}_w_out', (i // 2,))]))
    return out


def kernel(x, p, norm_g, final_norm_g, ffn_w_gate, ffn_w_up, ffn_w_down, gmlp_w_in, gmlp_ln_g, gmlp_ln_b, gmlp_w_s, gmlp_b_s, gmlp_w_out, s5_w_in, s5_lam_re, s5_lam_im, s5_log_dt, s5_b_re, s5_b_im, s5_c_re, s5_c_im, s5_d, s5_w_out, ple_w_gate, ple_w_proj, loss_target, m_norm_g, m_final_norm_g, m_ffn_w_gate, m_ffn_w_up, m_ffn_w_down, m_gmlp_w_in, m_gmlp_ln_g, m_gmlp_ln_b, m_gmlp_w_s, m_gmlp_b_s, m_gmlp_w_out, m_s5_w_in, m_s5_lam_re, m_s5_lam_im, m_s5_log_dt, m_s5_b_re, m_s5_b_im, m_s5_c_re, m_s5_c_im, m_s5_d, m_s5_w_out, m_ple_w_gate, m_ple_w_proj, v_norm_g, v_final_norm_g, v_ffn_w_gate, v_ffn_w_up, v_ffn_w_down, v_gmlp_w_in, v_gmlp_ln_g, v_gmlp_ln_b, v_gmlp_w_s, v_gmlp_b_s, v_gmlp_w_out, v_s5_w_in, v_s5_lam_re, v_s5_lam_im, v_s5_log_dt, v_s5_b_re, v_s5_b_im, v_s5_c_re, v_s5_c_im, v_s5_d, v_s5_w_out, v_ple_w_gate, v_ple_w_proj):
    env = dict(locals())
    W = {n: env[n] for n in WEIGHT_NAMES}
    Mo = {n: env["m_" + n] for n in WEIGHT_NAMES}
    Vo = {n: env["v_" + n] for n in WEIGHT_NAMES}
    depth = norm_g.shape[0]
    L, D = x.shape[1], x.shape[2]
    s_idx = 2 * lax.axis_index("x") + lax.axis_index("y")

    sc = jnp.stack([s_idx, lax.axis_index("c")]).astype(jnp.int32)
    blocks = _blocks(depth)

    ng2 = norm_g.reshape(depth * 4, norm_g.shape[-1])
    ng_full, sd_full = gather_small("gather_small", [ng2, s5_d], sc)
    full = {}
    gathers = []
    token = sd_full
    for bname, mats in blocks:
        casts = [cast_into_full(W[n], lead, BIG[n], sc, f"cast_{bname}_{n}", token) for n, lead in mats]
        handle, token = gather_start(f"gather_{bname}", casts, [W[n].shape[-2:] for n, _ in mats],
                                     [BIG[n] for n, _ in mats], token)
        gathers.append(handle)
    all_started = token
    ng_full = ng_full.reshape(depth, 4, 1, D)

    landed = {}

    def prefetcher(bi):
        def prefetch(value):
            if bi >= len(blocks):
                return value
            landed[bi], token = gather_land(gathers[bi], value)
            return token
        return prefetch

    def fetch(bi, after):
        if bi not in landed:
            landed[bi], after = gather_land(gathers[bi], after)
        full.update(dict(zip(blocks[bi][1], gather_take(landed[bi], after))))
    gf = final_norm_g.reshape(1, D)

    G, P, H = s5_b_re.shape[1:]
    n_grp, chunk = gmlp_w_s.shape[1], gmlp_w_s.shape[2]
    lam_re, lam_im = s5_lam_re[0], s5_lam_im[0]
    log_dt = s5_log_dt.reshape(G, 1)
    b_re, b_im, c_re, c_im = s5_b_re[0], s5_b_im[0], s5_c_re[0], s5_c_im[0]
    w_s, b_s = gmlp_w_s[0], gmlp_b_s[0].reshape(n_grp, chunk, 1)
    xs = x.reshape(L, D)
    saved = []
    def ffn_w(i, k):
        return [full[(n, (i, k))] for n in ('ffn_w_gate', 'ffn_w_up', 'ffn_w_down')]

    for i in range(depth):
        sv = {}
        fetch(4 * i, all_started if i == 0 else xs)
        xs, sv['ffn_a'] = ffn_fwd(xs, ng_full[i, 0], *ffn_w(i, 0), f"{i}a", prefetcher(4 * i + 1))
        j = (i // 2,)
        fetch(4 * i + 1, xs)
        if i % 2 == 0:
            xs, sv['mix'] = gmlp_fwd(xs, ng_full[i, 1], full[('gmlp_w_in', j)], gmlp_ln_g, gmlp_ln_b, w_s, b_s,
                                     full[('gmlp_w_out', j)], prefetcher(4 * i + 2))
        else:
            xs, sv['mix'] = s5_fwd(xs, ng_full[i, 1], full[('s5_w_in', j)], lam_re, lam_im, log_dt, b_re, b_im,
                                   c_re, c_im, sd_full, full[('s5_w_out', j)], prefetcher(4 * i + 2))
        fetch(4 * i + 2, xs)
        xs, sv['ffn_b'] = ffn_fwd(xs, ng_full[i, 2], *ffn_w(i, 1), f"{i}b", prefetcher(4 * i + 3))
        fetch(4 * i + 3, xs)
        xs, sv['ple'] = ple_fwd(xs, ng_full[i, 3], p[i, 0], full[('ple_w_gate', (i,))], full[('ple_w_proj', (i,))], f"{i}",
                                prefetcher(4 * i + 4))
        saved.append(sv)

    sq, dx, dx_bf, dgf = loss_head(xs, gf, loss_target.reshape(L, D))
    loss_local = 0.5 * jnp.sum(sq) / D
    dng = [[None] * 4 for _ in range(depth)]
    small = {}
    gshard = {}
    state = {'fence': sc}
    ici_inflight = []
    swaps = []

    def hook(bi):
        def on_grads(gs):
            bname, mats = blocks[bi]
            handle, token = reduce_pair_start(f"reduce_{bname}", gs, [BIG[n] for n, _ in mats], state['fence'])
            state['pair'] = (handle, mats)
            return token
        return on_grads

    def reduce_block(after, keep=1):
        handle, mats = state.pop('pair')
        ici_handle, fence = reduce_ici_start(handle, sc, after)
        ici_inflight.append((ici_handle, mats))
        while len(ici_inflight) > keep:
            prev_handle, prev_mats = ici_inflight.pop(0)
            swap_handle, fence = reduce_finish(prev_handle, fence)
            swaps.append((swap_handle, prev_mats))
        state['fence'] = fence

    for i in reversed(range(depth)):
        sv = saved[i]
        dx, dx_bf, dng[i][3] = ple_bwd(sv['ple'], ng_full[i, 3], p[i, 0], full[('ple_w_gate', (i,))], dx, f"{i}",
                                       state['fence'], hook(4 * i + 3))
        reduce_block(dx_bf, keep=2)
        dx, dx_bf, dng[i][2] = ffn_bwd(sv['ffn_b'], ng_full[i, 2], *ffn_w(i, 1), dx, dx_bf, f"{i}b",
                                       state['fence'], hook(4 * i + 2))
        reduce_block(dx_bf)
        j = (i // 2,)
        if i % 2 == 0:
            dx, dx_bf, dng[i][1], dlg, dlb, dws, dbs = gmlp_bwd(
                sv['mix'], ng_full[i, 1], full[('gmlp_w_in', j)], gmlp_ln_g, gmlp_ln_b, w_s, b_s,
                full[('gmlp_w_out', j)], dx, dx_bf, state['fence'], hook(4 * i + 1))
            small.update(gmlp_ln_g=dlg, gmlp_ln_b=dlb, gmlp_w_s=dws, gmlp_b_s=dbs)
        else:
            dx, dx_bf, dng[i][1], dlr, dli, dldt, db_re, db_im, dc_re, dc_im, dd = s5_bwd(
                sv['mix'], ng_full[i, 1], full[('s5_w_in', j)], lam_re, lam_im, log_dt, b_re, sd_full,
                full[('s5_w_out', j)], dx, state['fence'], hook(4 * i + 1))
            small.update(s5_lam_re=dlr, s5_lam_im=dli, s5_log_dt=dldt, s5_b_re=db_re, s5_b_im=db_im,
                         s5_c_re=dc_re, s5_c_im=dc_im, s5_d=dd)
        reduce_block(dx_bf)
        dx, dx_bf, dng[i][0] = ffn_bwd(sv['ffn_a'], ng_full[i, 0], *ffn_w(i, 0), dx, dx_bf, f"{i}a",
                                       state['fence'], hook(4 * i))
        reduce_block(dx_bf)
    grad_x = dx.reshape(x.shape)
    small['norm_g'] = jnp.stack([jnp.stack(r) for r in dng])
    small['final_norm_g'] = dgf

    grads, deltas, new_m, new_v = {}, {}, {}, {}
    carry = {}
    fence = state['fence']

    def update_big(key, fence):
        n, lead = key
        carry[n], fence = adamw_into(W[n], lead, gshard[key], Mo[n], Vo[n], carry.get(n),
                                     f"adamw_{n}_{'_'.join(map(str, lead))}", fence)
        return fence

    small_names = [n for n in WEIGHT_NAMES if n not in BIG]
    flat = jnp.concatenate([small[n].astype(F32).reshape(-1) for n in small_names] + [loss_local.reshape(1)])
    pad = (-flat.size) % (256 * 128)
    flat = jnp.pad(flat, (0, pad)).reshape(-1, 128)
    handle, fence = small_swap_start(0, flat, fence)
    handle, fence = small_swap_start(1, small_swap_finish(handle, fence), fence)
    for swap_handle, mats in swaps:
        gshard.update(dict(zip(mats, reduce_swap_wait(swap_handle, fence))))
        for key in mats:
            fence = update_big(key, fence)
    handle, fence = small_swap_start(2, small_swap_finish(handle, fence), fence)
    last_handle, last_mats = ici_inflight.pop()
    swap_handle, fence = reduce_finish(last_handle, fence)
    gshard.update(dict(zip(last_mats, reduce_swap_wait(swap_handle, fence))))
    for key in last_mats:
        fence = update_big(key, fence)
    tot = small_swap_finish(handle, fence).reshape(-1)
    off = 0
    for n in small_names:
        sz = small[n].size
        gsum = tot[off:off + sz]
        off += sz
        if n == 'norm_g':
            gsum = lax.dynamic_slice_in_dim(gsum.reshape(depth, 4, D), s_idx * W[n].shape[-1], W[n].shape[-1], axis=2)
        elif n == 's5_d':
            gsum = lax.dynamic_slice_in_dim(gsum.reshape(1, D), s_idx * W[n].shape[-1], W[n].shape[-1], axis=1)
        grads[n] = gsum.reshape(W[n].shape)
        deltas[n], new_m[n], new_v[n] = adamw(W[n], grads[n], Mo[n], Vo[n], f"adamw_{n}")
    loss = tot[off]
    for n in BIG:
        grads[n], deltas[n], new_m[n], new_v[n] = carry[n]
    return (loss, grad_x, *[grads[n] for n in WEIGHT_NAMES], *[deltas[n] for n in WEIGHT_NAMES],
            *[new_m[n] for n in WEIGHT_NAMES], *[new_v[n] for n in WEIGHT_NAMES])
```

```python
import functools
import math

import jax
import jax.numpy as jnp
from jax import lax
from jax.experimental import pallas as pl
from jax.experimental.pallas import tpu as pltpu

F32 = jnp.float32
BF16 = jnp.bfloat16
MESH_ID = pl.DeviceIdType.MESH

EPS = 1e-6
ADAM_LR = 0.001
ADAM_B1 = 0.9
ADAM_B2 = 0.999
ADAM_EPS = 1e-08
ADAM_WD = 0.01
ADAM_STEP = 10

N_SHARD = 4
V7X_VMEM_LIMIT = 52 * 2 ** 20
SSM_TILE_GROUPS = 16
SCAN_LANES = 1024
GELU_C = math.sqrt(2.0 / math.pi)


def _pcall(body, **kw):
    return pl.pallas_call(body, **kw)


def _params():
    return pltpu.CompilerParams(vmem_limit_bytes=V7X_VMEM_LIMIT)


def _pick(n, cands):
    for c in cands:
        if c <= n and n % c == 0:
            return c
    return n


def _sigmoid(x):
    return 1.0 / (1.0 + jnp.exp(-x))


def _gelu(x):
    return 0.5 * x * (1.0 + jnp.tanh(GELU_C * (x + 0.044715 * x * x * x)))


def _gelu_grad(x):
    t = jnp.tanh(GELU_C * (x + 0.044715 * x * x * x))
    return 0.5 * (1.0 + t) + 0.5 * x * (1.0 - t * t) * GELU_C * (1.0 + 3.0 * 0.044715 * x * x)


def tilemap(fn, ins, outs, *, M, N, tm, tn, name, after=None):
    n_in = len(ins)
    n_dep = 0 if after is None else 1
    grid = (N // tn, M // tm)
    in_specs = []
    for arr, kind, off in ins:
        if kind == 't':
            in_specs.append(pl.BlockSpec((tm, tn), lambda j, i, off=off: (i, j + off)))
        else:
            in_specs.append(pl.BlockSpec((1, tn), lambda j, i, off=off: (0, j + off)))
    out_specs, out_shape = [], []
    for dt, kind in outs:
        if kind == 't':
            out_specs.append(pl.BlockSpec((tm, tn), lambda j, i: (i, j)))
            out_shape.append(jax.ShapeDtypeStruct((M, N), dt))
        else:
            out_specs.append(pl.BlockSpec((1, tn), lambda j, i: (0, j)))
            out_shape.append(jax.ShapeDtypeStruct((1, N), F32))
    in_specs += [pl.BlockSpec(memory_space=pl.ANY)] * n_dep

    def body(*refs):
        vals = fn(*[r[...] for r in refs[:n_in]])
        for (dt, kind), ref, v in zip(outs, refs[n_in + n_dep:], vals):
            if kind == 't':
                ref[...] = v.astype(ref.dtype)
            else:
                @pl.when(pl.program_id(1) == 0)
                def _():
                    ref[...] = jnp.zeros_like(ref)
                ref[...] += v

    res = _pcall(body, name=name, grid=grid, in_specs=in_specs, out_specs=out_specs,
                 out_shape=out_shape, compiler_params=_params())(*[a for a, _, _ in ins], *([after] * n_dep))
    return res


def _as2d(a):
    if a.ndim >= 2 and a.shape[-1] % 128 == 0:
        return a.reshape(-1, a.shape[-1])
    if a.size % 128 == 0:
        return a.reshape(-1, 128)
    return a.reshape(-1, a.shape[-1])


def _row_tile(rows, cols, nbytes=4, budget=1 << 20):
    cands = [c for c in (2048, 1024, 512, 256, 128, 64, 32, 16, 8) if c * cols * nbytes <= budget]
    return _pick(rows, cands) if cands else _pick(rows, (8,))


ROW_TILES = (256, 128, 64, 32, 16)


def _sc_call(body, sc, args, *, grid, in_specs, out_specs, out_shape, name):
    gs = pltpu.PrefetchScalarGridSpec(num_scalar_prefetch=1, grid=grid, in_specs=in_specs, out_specs=out_specs)
    return _pcall(body, name=name, grid_spec=gs, out_shape=out_shape, compiler_params=_params())(sc, *args)


def cast_into_full(w, lead, ax, sc, name, after):
    R, C = w.shape[-2:]
    tm = _pick(R, ROW_TILES)
    nb = R // tm
    lead = tuple(lead)
    in_spec = pl.BlockSpec((None,) * len(lead) + (tm, C), lambda i, s: lead + (i, 0))
    if ax == 0:
        shape, out_map = (R * N_SHARD, C), (lambda i, s: (i + s[0] * nb, 0))
    else:
        shape, out_map = (R, C * N_SHARD), (lambda i, s: (i, s[0]))

    def body(s_ref, w_ref, after_ref, o_ref):
        o_ref[...] = w_ref[...].astype(BF16)

    return _sc_call(body, sc, [w, after], grid=(nb,), in_specs=[in_spec, pl.BlockSpec(memory_space=pl.ANY)],
                    out_specs=pl.BlockSpec((tm, C), out_map), out_shape=jax.ShapeDtypeStruct(shape, BF16), name=name)


def pairsum(g, a, ax, sc, name):
    hR, hC = a.shape
    tm = _pick(hR, ROW_TILES)
    nb = hR // tm
    g_map = (lambda i, s: (i + s[1] * nb, 0)) if ax == 1 else (lambda i, s: (i, s[1]))
    blk = (tm, hC)

    def body(s_ref, g_ref, a_ref, o_ref):
        o_ref[...] = (g_ref[...].astype(F32) + a_ref[...].astype(F32)).astype(BF16)

    return _sc_call(body, sc, [g, a], grid=(nb,),
                    in_specs=[pl.BlockSpec(blk, g_map), pl.BlockSpec(blk, lambda i, s: (i, 0))],
                    out_specs=pl.BlockSpec(blk, lambda i, s: (i, 0)),
                    out_shape=jax.ShapeDtypeStruct((hR, hC), BF16), name=name)


def shardsum(b, cbuf, ax, sc, name):
    hR, hC = b.shape
    _, pR, pC = cbuf.shape
    tm = _pick(pR, ROW_TILES)
    nb = pR // tm
    if ax == 1:
        b_map, o_map, shape = (lambda i, s: (i, s[0])), (lambda i, s: (i + s[1] * nb, 0)), (2 * pR, pC)
    else:
        b_map, o_map, shape = (lambda i, s: (i + s[0] * nb, 0)), (lambda i, s: (i, s[1])), (pR, 2 * pC)

    def body(s_ref, b_ref, c_ref, o_ref):
        acc = b_ref[...].astype(F32)
        for k in range(N_SHARD - 1):
            acc = acc + c_ref[k].astype(F32)
        o_ref[...] = acc

    return _sc_call(body, sc, [b, cbuf], grid=(nb,),
                    in_specs=[pl.BlockSpec((tm, pC), b_map), pl.BlockSpec((N_SHARD - 1, tm, pC), lambda i, s: (0, i, 0))],
                    out_specs=pl.BlockSpec((tm, pC), o_map), out_shape=jax.ShapeDtypeStruct(shape, F32), name=name)


_DIMS = {'nn': (((1,), (0,)), ((), ())), 'nt': (((1,), (1,)), ((), ())), 'tn': (((0,), (0,)), ((), ()))}


def matmul(pairs, mode, outs, *, epilogue=None, extras=(), tm=512, tn=512, name, after=None):
    a0, b0 = pairs[0]
    if mode == 'nn':
        (M, K), N = a0.shape, b0.shape[1]
    elif mode == 'nt':
        (M, K), N = a0.shape, b0.shape[0]
    else:
        (K, M), N = a0.shape, b0.shape[1]
    tm, tn = _pick(M, (tm, 256, 128)), _pick(N, (tn, 256, 128))
    n_p, n_e = len(pairs), len(extras)
    if mode == 'tn':
        a_spec = pl.BlockSpec((K, tm), lambda i, j: (0, i))
    else:
        a_spec = pl.BlockSpec((tm, K), lambda i, j: (i, 0))
    if mode == 'nt':
        b_spec = pl.BlockSpec((tn, K), lambda i, j: (j, 0))
    else:
        b_spec = pl.BlockSpec((K, tn), lambda i, j: (0, j))
    in_specs, args = [], []
    for a, b in pairs:
        in_specs += [a_spec, b_spec]
        args += [a, b]
    for arr, kind in extras:
        if kind == 't':
            in_specs.append(pl.BlockSpec((tm, tn), lambda i, j: (i, j)))
        else:
            in_specs.append(pl.BlockSpec((1, tn), lambda i, j: (0, j)))
        args.append(arr)
    n_dep = 0 if after is None else 1
    in_specs += [pl.BlockSpec(memory_space=pl.ANY)] * n_dep
    args += [after] * n_dep
    dims = _DIMS[mode]

    def body(*refs):
        accs = [lax.dot_general(refs[2 * p][...].astype(BF16), refs[2 * p + 1][...].astype(BF16), dims,
                                preferred_element_type=F32) for p in range(n_p)]
        ex = [r[...] for r in refs[2 * n_p:2 * n_p + n_e]]
        if epilogue is None:
            acc = accs[0]
            for other in accs[1:]:
                acc = acc + other
            res = (acc,)
        else:
            res = epilogue(accs, ex)
        for ref, v in zip(refs[2 * n_p + n_e + n_dep:], res):
            ref[...] = v.astype(ref.dtype)

    return _pcall(body, name=name, grid=(M // tm, N // tn), in_specs=in_specs,
                  out_specs=[pl.BlockSpec((tm, tn), lambda i, j: (i, j)) for _ in outs],
                  out_shape=[jax.ShapeDtypeStruct((M, N), dt) for dt in outs],
                  compiler_params=_params())(*args)


def comm_call(name, ins, out_shapes, plan, n_local, n_remote, aliases=None, after=None):
    ins = list(ins) + ([] if after is None else [after])
    n_in, n_out = len(ins), len(out_shapes)

    def body(*refs):
        in_refs, out_refs = refs[:n_in], refs[n_in:n_in + n_out]
        lsem, ssem, rsem = refs[n_in + n_out:]
        me = (lax.axis_index("x"), lax.axis_index("y"), lax.axis_index("c"))
        local, remote = plan(me, in_refs, out_refs)
        assert len(local) == n_local and len(remote) == n_remote
        lcs = [pltpu.make_async_copy(s, d, lsem.at[k]) for k, (s, d) in enumerate(local)]
        rcs = [pltpu.make_async_remote_copy(src_ref=s, dst_ref=d, send_sem=ssem.at[k], recv_sem=rsem.at[k],
                                            device_id=peer, device_id_type=MESH_ID)
               for k, (s, d, peer) in enumerate(remote)]
        for cp in rcs:
            cp.start()
        for cp in lcs:
            cp.start()
        for cp in rcs:
            cp.wait()
        for cp in lcs:
            cp.wait()

    any_spec = pl.BlockSpec(memory_space=pl.ANY)
    return _pcall(body, name=name, in_specs=[any_spec] * n_in, out_specs=[any_spec] * n_out,
                  out_shape=list(out_shapes),
                  scratch_shapes=[pltpu.SemaphoreType.DMA((max(n_local, 1),)),
                                  pltpu.SemaphoreType.DMA((max(n_remote, 1),)),
                                  pltpu.SemaphoreType.DMA((max(n_remote, 1),))],
                  input_output_aliases=aliases or {},
                  compiler_params=pltpu.CompilerParams(has_side_effects=True))(*ins)


_HBM_SPEC = pl.BlockSpec(memory_space=pltpu.HBM)
_SEM_SPEC = pl.BlockSpec(memory_space=pltpu.SEMAPHORE)
_DATAFLOW = pltpu.SideEffectType.DATAFLOW_SIDE_EFFECTING


def split_start(name, arrays, n_copies, plan, after):
    n = len(arrays)

    def body(*refs):
        ins, (ssem, rsem) = refs[:n], refs[n + 1:n + 3]
        token = refs[-1]
        me = (lax.axis_index("x"), lax.axis_index("y"), lax.axis_index("c"))
        for k, (src, dst, peer) in enumerate(plan(me, ins)):
            pltpu.make_async_remote_copy(src_ref=src, dst_ref=dst, send_sem=ssem.at[k], recv_sem=rsem.at[k],
                                         device_id=peer, device_id_type=MESH_ID).start()
        token[...] = jnp.zeros_like(token)

    sems = pltpu.SemaphoreType.DMA((n_copies,))
    res = _pcall(body, name=name,
                 out_shape=(sems, sems, *[pltpu.HBM(a.shape, a.dtype) for a in arrays], jax.ShapeDtypeStruct((8, 128), F32)),
                 in_specs=[_HBM_SPEC] * n + [pl.BlockSpec(memory_space=pl.ANY)],
                 out_specs=(_SEM_SPEC, _SEM_SPEC, *[_HBM_SPEC] * n, pl.BlockSpec(memory_space=pltpu.VMEM)),
                 input_output_aliases={m: 2 + m for m in range(n)},
                 compiler_params=pltpu.CompilerParams(has_side_effects=_DATAFLOW))(
        *[pltpu.with_memory_space_constraint(a, pltpu.HBM) for a in arrays], after)
    return res[0], res[1], list(res[2:2 + n]), res[-1]


def split_wait(name, arrays, ssem, rsem, plan, after):
    n = len(arrays)

    def body(*refs):
        ins, (ssem_r, rsem_r) = refs[:n], refs[n:n + 2]
        me = (lax.axis_index("x"), lax.axis_index("y"), lax.axis_index("c"))
        for k, (src, dst, peer) in enumerate(plan(me, ins)):
            cp = pltpu.make_async_remote_copy(src_ref=src, dst_ref=dst, send_sem=ssem_r.at[k], recv_sem=rsem_r.at[k],
                                              device_id=peer, device_id_type=MESH_ID)
            cp.wait_send()
            cp.wait_recv()

    res = _pcall(body, name=name, out_shape=[pltpu.HBM(a.shape, a.dtype) for a in arrays],
                 in_specs=[_HBM_SPEC] * n + [_SEM_SPEC, _SEM_SPEC, pl.BlockSpec(memory_space=pl.ANY)],
                 out_specs=[_HBM_SPEC] * n, input_output_aliases={m: m for m in range(n)},
                 compiler_params=pltpu.CompilerParams(has_side_effects=_DATAFLOW))(*arrays, ssem, rsem, after)
    return list(res)


def _shard_of(me):
    return 2 * me[0] + me[1]


def _plane_peers(me):
    x, y, c = me
    return [((1 - x, y, c), 2 * (1 - x) + y), ((x, 1 - y, c), 2 * x + 1 - y),
            ((1 - x, 1 - y, c), 2 * (1 - x) + 1 - y)]


def _mats(arr):
    out = [()]
    for n in arr.shape[:-2]:
        out = [o + (k,) for o in out for k in range(n)]
    return out


ROW_ALIGN = 16
LANE_ALIGN = 128


def _win(ref, lead, rows, cols):
    idx = tuple(lead)
    for spec, align in ((rows, ROW_ALIGN), (cols, LANE_ALIGN)):
        if spec is None:
            idx += (slice(None),)
        else:
            start, size = spec
            if not isinstance(start, int):
                start = pl.multiple_of(start, align)
            idx += (pl.ds(start, size),)
    return ref.at[idx]


def gather_small(name, shards, after):
    full_shapes = [jax.ShapeDtypeStruct((a.shape[0], a.shape[1] * N_SHARD), a.dtype) for a in shards]
    n = len(shards)

    def plan(me, in_refs, out_refs):
        s = _shard_of(me)
        local, remote = [], []
        for t, a in enumerate(shards):
            dst = _win(out_refs[t], (), None, (s * a.shape[1], a.shape[1]))
            local.append((in_refs[t], dst))
            for peer, _ in _plane_peers(me):
                remote.append((in_refs[t], dst, peer))
        return local, remote

    return comm_call(name, shards, full_shapes, plan, n, 3 * n, after=after)


def _half_shape(g, ax):
    R, C = g.shape
    return (R // 2, C) if ax == 1 else (R, C // 2)


def reduce_pair_start(name, grads, axes, after):
    n = len(grads)
    landing = [lax.empty(_half_shape(g, ax), BF16) for g, ax in zip(grads, axes)]

    def plan_a(me, refs):
        x, y, c = me
        copies = []
        for m in range(n):
            R, C = grads[m].shape
            if axes[m] == 1:
                rows, cols = ((1 - c) * (R // 2), R // 2), None
            else:
                rows, cols = None, ((1 - c) * (C // 2), C // 2)
            copies.append((_win(refs[m], (), rows, cols), refs[n + m], (x, y, 1 - c)))
        return copies

    ssem, rsem, thru, token = split_start(name + "_pair_start", list(grads) + landing, n, plan_a, after)
    return (name, axes, plan_a, ssem, rsem, thru), token


def reduce_ici_start(handle, sc, after):
    name, axes, plan_a, ssem, rsem, thru = handle
    n = len(axes)
    done = split_wait(name + "_pair_wait", thru, ssem, rsem, plan_a, after)
    grads, a_bufs = done[:n], done[n:]
    b_bufs = [pairsum(g, a, ax, sc, f"{name}_pairsum{m}") for m, (g, a, ax) in enumerate(zip(grads, a_bufs, axes))]

    def piece_shape(m):
        R, C = a_bufs[m].shape
        return (R, C // N_SHARD) if axes[m] == 1 else (R // N_SHARD, C)

    def piece_win(m, s):
        R, C = piece_shape(m)
        if axes[m] == 1:
            return None, (s * C, C)
        return (s * R, R), None

    landing = [lax.empty((N_SHARD - 1,) + piece_shape(m), BF16) for m in range(n)]

    def plan_c(me, refs):
        copies = []
        for m in range(n):
            for j, (peer, ps) in enumerate(_plane_peers(me)):
                rows, cols = piece_win(m, ps)
                copies.append((_win(refs[m], (), rows, cols), refs[n + m].at[j], peer))
        return copies

    ssem, rsem, thru, token = split_start(name + "_ici_start", b_bufs + landing, 3 * n, plan_c, after)
    return (name, axes, sc, plan_c, ssem, rsem, thru), token


def reduce_finish(handle, after):
    name, axes, sc, plan_c, ssem, rsem, thru = handle
    n = len(axes)
    done = split_wait(name + "_ici_wait", thru, ssem, rsem, plan_c, after)
    b_bufs, c_bufs = done[:n], done[n:]
    shards = [shardsum(b, cb, ax, sc, f"{name}_shardsum{m}") for m, (b, cb, ax) in enumerate(zip(b_bufs, c_bufs, axes))]

    def plan_e(me, refs):
        x, y, c = me
        copies = []
        for m in range(n):
            R, C = shards[m].shape
            if axes[m] == 1:
                rows, cols = (c * (R // 2), R // 2), None
            else:
                rows, cols = None, (c * (C // 2), C // 2)
            copies.append((_win(refs[m], (), rows, cols), _win(refs[m], (), rows, cols), (x, y, 1 - c)))
        return copies

    ssem, rsem, thru, token = split_start(name + "_swap_start", shards, n, plan_e, after)
    return (name, plan_e, ssem, rsem, thru), token


def reduce_swap_wait(handle, after):
    name, plan_e, ssem, rsem, thru = handle
    return split_wait(name + "_swap_wait", thru, ssem, rsem, plan_e, after)


def gather_start(name, fulls, shard_shapes, axes, after):
    n = len(fulls)

    def win(ref, m, s, half):
        R, C = shard_shapes[m]
        r0 = s * R if axes[m] == 0 else 0
        return _win(ref, (), (r0 + half * (R // 2), R // 2), None if axes[m] == 0 else (s * C, C))

    def plan_ici(me, refs):
        s, c = _shard_of(me), me[2]
        return [(win(refs[m], m, s, c), win(refs[m], m, s, c), peer) for m in range(n) for peer, _ in _plane_peers(me)]

    def plan_fwd(me, refs):
        x, y, c = me
        return [(win(refs[m], m, ps, c), win(refs[m], m, ps, c), (x, y, 1 - c)) for m in range(n) for _, ps in _plane_peers(me)]

    ssem, rsem, thru, token = split_start(name + "_start", fulls, 3 * n, plan_ici, after)
    return (name, plan_ici, plan_fwd, ssem, rsem, thru), token


def gather_land(handle, after):
    name, plan_ici, plan_fwd, ssem, rsem, thru = handle
    got = split_wait(name + "_wait", thru, ssem, rsem, plan_ici, after)
    ssem, rsem, thru, token = split_start(name + "_fwd_start", got, 3 * len(got), plan_fwd, after)
    return (name, plan_fwd, ssem, rsem, thru), token


def gather_take(handle, after):
    name, plan_fwd, ssem, rsem, thru = handle
    return split_wait(name + "_fwd_wait", thru, ssem, rsem, plan_fwd, after)


SMALL_FLIPS = ((0, 0, 1), (0, 1, 0), (1, 0, 0))


def small_swap_start(axis, cur, after):
    def plan(me, refs):
        peer = tuple(v + f * (1 - 2 * v) for v, f in zip(me, SMALL_FLIPS[axis]))
        return [(refs[0], refs[1], peer)]

    ssem, rsem, thru, token = split_start(f"small_swap{axis}_start", [cur, lax.empty(cur.shape, cur.dtype)], 1, plan, after)
    return (axis, plan, ssem, rsem, thru), token


def small_swap_finish(handle, after):
    axis, plan, ssem, rsem, thru = handle
    cur, got = split_wait(f"small_swap{axis}_wait", thru, ssem, rsem, plan, after)
    R, C = cur.shape
    (cur,) = tilemap(lambda a, b: (a + b,), [(cur, 't', 0), (got, 't', 0)], [(F32, 't')], M=R, N=C,
                     tm=_pick(R, (2048, 1024, 512, 256)), tn=C, name=f"small_add{axis}")
    return cur


def rms_fwd(x, g, name):
    M, D = x.shape
    tm = _pick(M, (256, 128))

    def fn(xv, gv):
        r = lax.rsqrt(jnp.mean(xv * xv, axis=-1, keepdims=True) + EPS)
        return (xv * r * gv,)

    (h,) = tilemap(fn, [(x, 't', 0), (g, 'r', 0)], [(BF16, 't')], M=M, N=D, tm=tm, tn=D, name=name)
    return h


def rms_bwd(x, g, dh, dres, name):
    M, D = x.shape
    tm = _pick(M, (256, 128))

    def fn(xv, gv, dhv, drv):
        r = lax.rsqrt(jnp.mean(xv * xv, axis=-1, keepdims=True) + EPS)
        xh = xv * r
        dxh = dhv * gv
        m = jnp.mean(dxh * xh, axis=-1, keepdims=True)
        dx = drv + r * (dxh - xh * m)
        return dx, dx, jnp.sum(dhv * xh, axis=0, keepdims=True)

    return tilemap(fn, [(x, 't', 0), (g, 'r', 0), (dh, 't', 0), (dres, 't', 0)],
                   [(F32, 't'), (BF16, 't'), (F32, 'a')], M=M, N=D, tm=tm, tn=D, name=name)


def ffn_fwd(x, g, wg, wu, wd, tag, prefetch):
    h = rms_fwd(x, g, f"ffn_norm_{tag}")

    def ep(accs, ex):
        a, b = accs
        return a, b, a * _sigmoid(a) * b

    a, b, s = matmul([(h, wg), (h, wu)], 'nn', [F32, F32, BF16], epilogue=ep, tm=1024, tn=512,
                     name=f"ffn_gateup_{tag}")
    (xo,) = matmul([(s, wd)], 'nn', [F32], epilogue=lambda accs, ex: (ex[0] + 0.5 * accs[0],),
                   extras=[(x, 't')], tm=512, tn=512, name=f"ffn_down_{tag}", after=prefetch(s))
    return xo, (x, h, a, b, s)


def ffn_bwd(saved, g, wg, wu, wd, dxo, dxo_bf, tag, after, on_grads):
    x, h, a, b, s = saved

    def ep(accs, ex):
        ds = 0.5 * accs[0]
        av, bv = ex
        sg = _sigmoid(av)
        return ds * bv * (sg * (1.0 + av * (1.0 - sg))), ds * (av * sg)

    da, db = matmul([(dxo_bf, wd)], 'nt', [BF16, BF16], epilogue=ep, extras=[(a, 't'), (b, 't')],
                    tm=1024, tn=512, name=f"ffn_dact_{tag}", after=after)
    (dwd,) = matmul([(s, dxo_bf)], 'tn', [BF16], epilogue=lambda accs, ex: (0.5 * accs[0],),
                    tm=512, tn=512, name=f"ffn_dwd_{tag}")
    dwg, dwu = matmul([(h, da), (h, db)], 'tn', [BF16, BF16], epilogue=lambda accs, ex: tuple(accs),
                      tm=1024, tn=512, name=f"ffn_dwgu_{tag}")
    started = on_grads([dwg, dwu, dwd])
    (dh,) = matmul([(da, wg), (db, wu)], 'nt', [F32], tm=512, tn=256, name=f"ffn_dh_{tag}", after=started)
    dx, dx_bf, dg = rms_bwd(x, g, dh, dxo, f"ffn_dnorm_{tag}")
    return dx, dx_bf, dg


def _tril_mask(n):
    return lax.broadcasted_iota(jnp.int32, (n, n), 0) >= lax.broadcasted_iota(jnp.int32, (n, n), 1)


GMLP_CHUNKS_PER_STEP = 4


def _gmlp_specs(L, half, n_grp, chunk):
    gd = half // n_grp
    cps = _pick(L // chunk, (GMLP_CHUNKS_PER_STEP, 2, 1))
    rows = cps * chunk
    specs = [pl.BlockSpec((rows, gd), lambda g, n: (n, g)),
             pl.BlockSpec((rows, gd), lambda g, n: (n, n_grp + g)),
             pl.BlockSpec((1, gd), lambda g, n: (0, g)),
             pl.BlockSpec((1, gd), lambda g, n: (0, g)),
             pl.BlockSpec((None, chunk, chunk), lambda g, n: (g, 0, 0)),
             pl.BlockSpec((None, chunk, 1), lambda g, n: (g, 0, 0))]
    return gd, cps, specs


def _gmlp_gate_values(zu, zv, lg, lb, ws, bs):
    u, v = _gelu(zu), _gelu(zv)
    mu = jnp.mean(v, axis=-1, keepdims=True)
    d = v - mu
    rstd = lax.rsqrt(jnp.mean(d * d, axis=-1, keepdims=True) + EPS)
    vhat = d * rstd
    vn = vhat * lg + lb
    w = jnp.where(_tril_mask(ws.shape[0]), ws, 0.0).astype(BF16)
    sv = jnp.dot(w, vn.astype(BF16), preferred_element_type=F32) + bs
    return u, vhat, rstd, vn, w, sv


def gmlp_gate_fwd(zpre, ln_g, ln_b, w_s, b_s):
    L, half = zpre.shape[0], zpre.shape[1] // 2
    n_grp, chunk = w_s.shape[0], w_s.shape[1]
    gd, cps, specs = _gmlp_specs(L, half, n_grp, chunk)

    def body(zu, zv, lg, lb, ws, bs, o):
        for k in range(cps):
            rows = pl.ds(k * chunk, chunk)
            u, _, _, _, _, sv = _gmlp_gate_values(zu[rows, :], zv[rows, :], lg[...], lb[...], ws[...], bs[...])
            o[rows, :] = (u * sv).astype(o.dtype)

    return _pcall(body, name="gmlp_gate", grid=(n_grp, L // (cps * chunk)), in_specs=specs,
                  out_specs=pl.BlockSpec((cps * chunk, gd), lambda g, n: (n, g)),
                  out_shape=jax.ShapeDtypeStruct((L, half), BF16), compiler_params=_params())(
        zpre, zpre, ln_g, ln_b, w_s, b_s)


def gmlp_gate_bwd(zpre, ln_g, ln_b, w_s, b_s, dgated):
    L, half = zpre.shape[0], zpre.shape[1] // 2
    n_grp, chunk = w_s.shape[0], w_s.shape[1]
    gd, cps, specs = _gmlp_specs(L, half, n_grp, chunk)
    tile = pl.BlockSpec((cps * chunk, gd), lambda g, n: (n, g))
    specs = specs + [tile]

    def body(zu, zv, lg, lb, ws, bs, dg, dzu, dzv, dws, dbs, dlg, dlb):
        @pl.when(pl.program_id(1) == 0)
        def _():
            dws[...] = jnp.zeros_like(dws)
            dbs[...] = jnp.zeros_like(dbs)
            dlg[...] = jnp.zeros_like(dlg)
            dlb[...] = jnp.zeros_like(dlb)

        lgv = lg[...]
        for k in range(cps):
            rows = pl.ds(k * chunk, chunk)
            zuv, zvv = zu[rows, :], zv[rows, :]
            u, vhat, rstd, vn, w, sv = _gmlp_gate_values(zuv, zvv, lgv, lb[...], ws[...], bs[...])
            dgv = dg[rows, :]
            du = dgv * sv
            dsv = dgv * u
            dsv_bf = dsv.astype(BF16)
            dw = lax.dot_general(dsv_bf, vn.astype(BF16), _DIMS['nt'], preferred_element_type=F32)
            dvn = lax.dot_general(w, dsv_bf, _DIMS['tn'], preferred_element_type=F32)
            dvhat = dvn * lgv
            dv = rstd * (dvhat - jnp.mean(dvhat, axis=-1, keepdims=True)
                         - vhat * jnp.mean(dvhat * vhat, axis=-1, keepdims=True))
            dzu[rows, :] = (du * _gelu_grad(zuv)).astype(dzu.dtype)
            dzv[rows, :] = (dv * _gelu_grad(zvv)).astype(dzv.dtype)
            dws[...] += jnp.where(_tril_mask(chunk), dw, 0.0)
            dbs[...] += jnp.sum(dsv, axis=1, keepdims=True)
            dlg[...] += jnp.sum(dvn * vhat, axis=0, keepdims=True)
            dlb[...] += jnp.sum(dvn, axis=0, keepdims=True)

    vec = pl.BlockSpec((1, gd), lambda g, n: (0, g))
    return _pcall(body, name="gmlp_gate_bwd", grid=(n_grp, L // (cps * chunk)), in_specs=specs,
                  out_specs=[tile, tile, pl.BlockSpec((None, chunk, chunk), lambda g, n: (g, 0, 0)),
                             pl.BlockSpec((None, chunk, 1), lambda g, n: (g, 0, 0)), vec, vec],
                  out_shape=[jax.ShapeDtypeStruct((L, half), BF16), jax.ShapeDtypeStruct((L, half), BF16),
                             jax.ShapeDtypeStruct((n_grp, chunk, chunk), F32),
                             jax.ShapeDtypeStruct((n_grp, chunk, 1), F32),
                             jax.ShapeDtypeStruct((1, half), F32), jax.ShapeDtypeStruct((1, half), F32)],
                  compiler_params=_params())(zpre, zpre, ln_g, ln_b, w_s, b_s, dgated)


def gmlp_fwd(x, g, w_in, ln_g, ln_b, w_s, b_s, w_out, prefetch):
    h = rms_fwd(x, g, "gmlp_norm")
    (zpre,) = matmul([(h, w_in)], 'nn', [F32], tm=1024, tn=512, name="gmlp_in")
    gated = gmlp_gate_fwd(zpre, ln_g, ln_b, w_s, b_s)
    (xo,) = matmul([(gated, w_out)], 'nn', [F32], epilogue=lambda accs, ex: (ex[0] + accs[0],),
                   extras=[(x, 't')], tm=512, tn=512, name="gmlp_out", after=prefetch(gated))
    return xo, (x, h, zpre, gated)


def gmlp_bwd(saved, g, w_in, ln_g, ln_b, w_s, b_s, w_out, dxo, dxo_bf, after, on_grads):
    x, h, zpre, gated = saved
    (dgated,) = matmul([(dxo_bf, w_out)], 'nt', [F32], tm=1024, tn=512, name="gmlp_dgated", after=after)
    (dw_out,) = matmul([(gated, dxo_bf)], 'tn', [BF16], tm=1024, tn=512, name="gmlp_dwout")
    dzu, dzv, dws, dbs, dlg, dlb = gmlp_gate_bwd(zpre, ln_g, ln_b, w_s, b_s, dgated)
    dz = jnp.concatenate([dzu, dzv], axis=1)
    (dw_in,) = matmul([(h, dz)], 'tn', [BF16], tm=1024, tn=512, name="gmlp_dwin")
    started = on_grads([dw_in, dw_out])
    (dh,) = matmul([(dz, w_in)], 'nt', [F32], tm=512, tn=256, name="gmlp_dh", after=started)
    dx, dx_bf, dg = rms_bwd(x, g, dh, dxo, "gmlp_dnorm")
    return dx, dx_bf, dg, dlg, dlb, dws, dbs


def _s5_disc(lr, li, ldt, br, bi):
    dt = jnp.exp(ldt)
    mag = jnp.exp(lr * dt)
    ang = li * dt
    ar = mag * jnp.cos(ang)
    ai = mag * jnp.sin(ang)
    den = lr * lr + li * li
    nr = ar - 1.0
    zr = (nr * lr + ai * li) / den
    zi = (ai * lr - nr * li) / den
    return ar, ai, zr[None] * br - zi[None] * bi, zr[None] * bi + zi[None] * br


def s5_disc_fwd(lr, li, ldt, br, bi):
    def body(lr_r, li_r, ldt_r, br_r, bi_r, ar_o, ai_o, bbr_o, bbi_o):
        res = _s5_disc(lr_r[...], li_r[...], ldt_r[...], br_r[...], bi_r[...])
        for o, v in zip((ar_o, ai_o, bbr_o, bbi_o), res):
            o[...] = v

    shp = lambda a: jax.ShapeDtypeStruct(a.shape, F32)
    return _pcall(body, name="s5_disc", out_shape=[shp(lr), shp(lr), shp(br), shp(br)],
                  compiler_params=_params())(lr, li, ldt, br, bi)


def s5_disc_bwd(lr, li, ldt, br, bi, dar, dai, dbbr, dbbi):
    def body(lr_r, li_r, ldt_r, br_r, bi_r, dar_r, dai_r, dbbr_r, dbbi_r, o1, o2, o3, o4, o5):
        _, vjp = jax.vjp(_s5_disc, lr_r[...], li_r[...], ldt_r[...], br_r[...], bi_r[...])
        res = vjp((dar_r[...], dai_r[...], dbbr_r[...], dbbi_r[...]))
        for o, v in zip((o1, o2, o3, o4, o5), res):
            o[...] = v

    shp = lambda a: jax.ShapeDtypeStruct(a.shape, F32)
    return _pcall(body, name="s5_disc_bwd", out_shape=[shp(lr), shp(lr), shp(ldt), shp(br), shp(br)],
                  compiler_params=_params())(lr, li, ldt, br, bi, dar, dai, dbbr, dbbi)


def blockdiag_matmul(pairs, outs, *, epilogue=None, extras=(), name):
    a0, b0 = pairs[0]
    M = a0.shape[0]
    T, wa, wo = b0.shape
    tm = _pick(M, (512, 256, 128))
    n_p, n_e = len(pairs), len(extras)
    in_specs, args = [], []
    for a, b in pairs:
        in_specs += [pl.BlockSpec((tm, wa), lambda k, i: (i, k)), pl.BlockSpec((None, wa, wo), lambda k, i: (k, 0, 0))]
        args += [a, b]
    for arr, kind in extras:
        in_specs.append(pl.BlockSpec((tm, wo), lambda k, i: (i, k)) if kind == 't'
                        else pl.BlockSpec((1, wo), lambda k, i: (0, k)))
        args.append(arr)

    def body(*refs):
        accs = [jnp.dot(refs[2 * p][...].astype(BF16), refs[2 * p + 1][...], preferred_element_type=F32)
                for p in range(n_p)]
        ex = [r[...] for r in refs[2 * n_p:2 * n_p + n_e]]
        res = tuple(accs) if epilogue is None else epilogue(accs, ex)
        for ref, v in zip(refs[2 * n_p + n_e:], res):
            ref[...] = v.astype(ref.dtype)

    return _pcall(body, name=name, grid=(T, M // tm), in_specs=in_specs,
                  out_specs=[pl.BlockSpec((tm, wo), lambda k, i: (i, k)) for _ in outs],
                  out_shape=[jax.ShapeDtypeStruct((M, T * wo), dt) for dt in outs],
                  compiler_params=_params())(*args)


def blockdiag_outer(pairs, name):
    M = pairs[0][0].shape[0]
    n_p = len(pairs)
    shapes = []
    in_specs, args = [], []
    tm = _pick(M, (512, 256, 128))
    T = None
    for a, b, wa, wb in pairs:
        T = a.shape[1] // wa
        in_specs += [pl.BlockSpec((tm, wa), lambda k, i: (i, k)), pl.BlockSpec((tm, wb), lambda k, i: (i, k))]
        args += [a, b]
        shapes.append((T, wa, wb))

    def body(*refs):
        @pl.when(pl.program_id(1) == 0)
        def _():
            for o in refs[2 * n_p:]:
                o[...] = jnp.zeros_like(o)
        for p in range(n_p):
            refs[2 * n_p + p][...] += lax.dot_general(refs[2 * p][...].astype(BF16), refs[2 * p + 1][...].astype(BF16),
                                                      _DIMS['tn'], preferred_element_type=F32)

    return _pcall(body, name=name, grid=(T, M // tm), in_specs=in_specs,
                  out_specs=[pl.BlockSpec((None, s[1], s[2]), lambda k, i: (k, 0, 0)) for s in shapes],
                  out_shape=[jax.ShapeDtypeStruct(s, F32) for s in shapes], compiler_params=_params())(*args)


def s5_scan(br, bi, ar, ai, reverse, want_prev, name):
    L, S = br.shape
    ln = _pick(S, (SCAN_LANES, 512, 256, 128))
    tb = _pick(L, (512, 256, 128))
    n_t = L // tb
    n_q = tb // 8

    def tmap(j, t):
        return ((n_t - 1 - t) if reverse else t, j)

    blk = pl.BlockSpec((tb, ln), tmap)
    vec = pl.BlockSpec((1, ln), lambda j, t: (0, j))

    def cmul(xr, xi, yr, yi):
        return xr * yr - xi * yi, xr * yi + xi * yr

    n_out = 4 if want_prev else 2

    def body(br_r, bi_r, ar_r, ai_r, *rest):
        outs, (cr_s, ci_s) = rest[:n_out], rest[n_out:]

        @pl.when(pl.program_id(1) == 0)
        def _():
            cr_s[...] = jnp.zeros_like(cr_s)
            ci_s[...] = jnp.zeros_like(ci_s)

        a1r, a1i = ar_r[...], ai_r[...]
        a2r, a2i = cmul(a1r, a1i, a1r, a1i)
        a4r, a4i = cmul(a2r, a2i, a2r, a2i)
        a8r, a8i = cmul(a4r, a4i, a4r, a4i)
        row = lax.broadcasted_iota(jnp.int32, (8, ln), 0)
        dist = (7 - row) if reverse else row
        pwr, pwi = jnp.broadcast_to(a1r, (8, ln)), jnp.broadcast_to(a1i, (8, ln))
        for bit, (er, ei) in ((1, (a1r, a1i)), (2, (a2r, a2i)), (4, (a4r, a4i))):
            nr, ni = cmul(pwr, pwi, er, ei)
            sel = (dist & bit) != 0
            pwr, pwi = jnp.where(sel, nr, pwr), jnp.where(sel, ni, pwi)
        last = 0 if reverse else 7
        steps = [(d, jnp.where(dist >= d, er, 0.0), jnp.where(dist >= d, ei, 0.0))
                 for d, (er, ei) in ((1, (a1r, a1i)), (2, (a2r, a2i)), (4, (a4r, a4i)))]

        def step(q, carry):
            cr, ci = carry
            qq = (n_q - 1 - q) if reverse else q
            rows = pl.ds(pl.multiple_of(qq * 8, 8), 8)
            xr, xi = br_r[rows, :], bi_r[rows, :]
            for d, er, ei in steps:
                sr = pltpu.roll(xr, (8 - d) if reverse else d, 0)
                si = pltpu.roll(xi, (8 - d) if reverse else d, 0)
                mr, mi = cmul(sr, si, er, ei)
                xr, xi = xr + mr, xi + mi
            lr, li = xr[last:last + 1, :], xi[last:last + 1, :]
            kr, ki = cmul(pwr, pwi, cr, ci)
            xr, xi = xr + kr, xi + ki
            outs[0][rows, :] = xr
            outs[1][rows, :] = xi
            if want_prev:
                outs[2][rows, :] = jnp.where(dist >= 1, pltpu.roll(xr, 7 if reverse else 1, 0), cr)
                outs[3][rows, :] = jnp.where(dist >= 1, pltpu.roll(xi, 7 if reverse else 1, 0), ci)
            nr, ni = cmul(a8r, a8i, cr, ci)
            return lr + nr, li + ni

        cr, ci = lax.fori_loop(0, n_q, step, (cr_s[...], ci_s[...]), unroll=2)
        cr_s[...] = cr
        ci_s[...] = ci

    shp = jax.ShapeDtypeStruct((L, S), F32)
    return _pcall(body, name=name, grid=(S // ln, n_t), in_specs=[blk, blk, vec, vec],
                  out_specs=[blk] * n_out, out_shape=[shp] * n_out,
                  scratch_shapes=[pltpu.VMEM((1, ln), F32), pltpu.VMEM((1, ln), F32)],
                  compiler_params=_params())(br, bi, ar, ai)


def _to_blockdiag(m, tile_groups):
    G, A, B = m.shape
    T = G // tile_groups
    eye = jnp.eye(tile_groups, dtype=m.dtype)
    t = m.reshape(T, tile_groups, A, 1, B) * eye[None, :, None, :, None]
    return t.reshape(T, tile_groups * A, tile_groups * B)


def _from_blockdiag(t, tile_groups):
    T, RA, RB = t.shape
    A, B = RA // tile_groups, RB // tile_groups
    d = jnp.diagonal(t.reshape(T, tile_groups, A, tile_groups, B), axis1=1, axis2=3)
    return jnp.moveaxis(d, 3, 1).reshape(T * tile_groups, A, B)


def s5_fwd(x, g, w_in, lam_re, lam_im, log_dt, b_re, b_im, c_re, c_im, d_skip, w_out, prefetch):
    G, P, H = b_re.shape
    tg = min(SSM_TILE_GROUPS, G)
    h = rms_fwd(x, g, "s5_norm")
    (u,) = matmul([(h, w_in)], 'nn', [F32], tm=512, tn=512, name="s5_in")
    br_t, bi_t = jnp.transpose(b_re, (2, 0, 1)), jnp.transpose(b_im, (2, 0, 1))
    ar, ai, bbr, bbi = s5_disc_fwd(lam_re, lam_im, log_dt, br_t, bi_t)
    bbr_g, bbi_g = jnp.transpose(bbr, (1, 0, 2)), jnp.transpose(bbi, (1, 0, 2))
    bd_br, bd_bi = _to_blockdiag(bbr_g.astype(BF16), tg), _to_blockdiag(bbi_g.astype(BF16), tg)
    bur, bui = blockdiag_matmul([(u, bd_br), (u, bd_bi)], [F32, F32], name="s5_bu")
    a_r, a_i = ar.reshape(1, G * P), ai.reshape(1, G * P)
    hr, hi, hpr, hpi = s5_scan(bur, bui, a_r, a_i, False, True, "s5_scan")
    c_pg_r = jnp.transpose(c_re, (0, 2, 1)).astype(BF16)
    c_pg_i = jnp.transpose(c_im, (0, 2, 1)).astype(BF16)
    bd_cr, bd_nci = _to_blockdiag(c_pg_r, tg), _to_blockdiag(-c_pg_i, tg)

    def ep(accs, ex):
        y = accs[0] + accs[1] + ex[1] * ex[0]
        return y, _gelu(y)

    y, act = blockdiag_matmul([(hr, bd_cr), (hi, bd_nci)], [F32, BF16], epilogue=ep,
                              extras=[(u, 't'), (d_skip, 'r')], name="s5_y")
    (o,) = matmul([(act, w_out)], 'nn', [F32], tm=512, tn=512, name="s5_out", after=prefetch(act))
    M, D = x.shape
    tm = _pick(M, (256, 128))
    (xo,) = tilemap(lambda xv, val, gt: (xv + val * _sigmoid(gt),), [(x, 't', 0), (o, 't', 0), (o, 't', 1)],
                    [(F32, 't')], M=M, N=D, tm=tm, tn=D, name="s5_glu")
    saved = (x, h, u, hr, hi, hpr, hpi, y, act, o, a_r, a_i, bd_br, bd_bi, bd_cr, bd_nci, br_t, bi_t)
    return xo, saved


def s5_bwd(saved, g, w_in, lam_re, lam_im, log_dt, b_re, d_skip, w_out, dxo, after, on_grads):
    x, h, u, hr, hi, hpr, hpi, y, act, o, a_r, a_i, bd_br, bd_bi, bd_cr, bd_nci, br_t, bi_t = saved
    G, P, H = b_re.shape
    tg = min(SSM_TILE_GROUPS, G)
    M, D = x.shape
    tm = _pick(M, (256, 128))

    def glu_bwd(dv, val, gt):
        sg = _sigmoid(gt)
        return dv * sg, dv * val * sg * (1.0 - sg)

    dval, dgate = tilemap(glu_bwd, [(dxo, 't', 0), (o, 't', 0), (o, 't', 1)], [(BF16, 't'), (BF16, 't')],
                          M=M, N=D, tm=tm, tn=D, name="s5_dglu", after=after)
    do = jnp.concatenate([dval, dgate], axis=1)
    (dw_out,) = matmul([(act, do)], 'tn', [BF16], tm=512, tn=512, name="s5_dwout")
    (dact,) = matmul([(do, w_out)], 'nt', [F32], tm=512, tn=512, name="s5_dact")
    dy, dd = tilemap(lambda da, yv, uv: (da * _gelu_grad(yv), jnp.sum(da * _gelu_grad(yv) * uv, axis=0, keepdims=True)),
                     [(dact, 't', 0), (y, 't', 0), (u, 't', 0)], [(F32, 't'), (F32, 'a')],
                     M=M, N=D, tm=tm, tn=D, name="s5_dy")
    bd_crT, bd_nciT = jnp.transpose(bd_cr, (0, 2, 1)), jnp.transpose(bd_nci, (0, 2, 1))
    dhr, dhi = blockdiag_matmul([(dy, bd_crT), (dy, bd_nciT)], [F32, F32], name="s5_dh")
    gr, gi = s5_scan(dhr, dhi, a_r, -a_i, True, False, "s5_scan_rev")
    S = G * P
    tms = _pick(M, (128,))

    def da_fn(grv, giv, hprv, hpiv):
        return (jnp.sum(grv * hprv + giv * hpiv, axis=0, keepdims=True),
                jnp.sum(giv * hprv - grv * hpiv, axis=0, keepdims=True))

    dar, dai = tilemap(da_fn, [(gr, 't', 0), (gi, 't', 0), (hpr, 't', 0), (hpi, 't', 0)], [(F32, 'a'), (F32, 'a')],
                       M=M, N=S, tm=tms, tn=_pick(S, (2048, 1024, 512)), name="s5_dabar")
    wa, wb = tg * H, tg * P
    xc_r, xc_i, xb_r, xb_i = blockdiag_outer([(dy, hr, wa, wb), (dy, hi, wa, wb), (u, gr, wa, wb), (u, gi, wa, wb)],
                                             "s5_dcb")
    dc_re = _from_blockdiag(xc_r, tg)
    dc_im = -_from_blockdiag(xc_i, tg)
    dbb_r = jnp.transpose(_from_blockdiag(xb_r, tg), (1, 0, 2))
    dbb_i = jnp.transpose(_from_blockdiag(xb_i, tg), (1, 0, 2))
    dlr, dli, dldt, dbr_t, dbi_t = s5_disc_bwd(lam_re, lam_im, log_dt, br_t, bi_t,
                                               dar.reshape(G, P), dai.reshape(G, P), dbb_r, dbb_i)
    db_re, db_im = jnp.transpose(dbr_t, (1, 2, 0)), jnp.transpose(dbi_t, (1, 2, 0))
    bd_brT, bd_biT = jnp.transpose(bd_br, (0, 2, 1)), jnp.transpose(bd_bi, (0, 2, 1))
    (du,) = blockdiag_matmul([(gr, bd_brT), (gi, bd_biT)], [BF16],
                             epilogue=lambda accs, ex: (accs[0] + accs[1] + ex[1] * ex[0],),
                             extras=[(dy, 't'), (d_skip, 'r')], name="s5_du")
    (dw_in,) = matmul([(h, du)], 'tn', [BF16], tm=512, tn=512, name="s5_dwin")
    started = on_grads([dw_in, dw_out])
    (dh,) = matmul([(du, w_in)], 'nt', [F32], tm=512, tn=512, name="s5_dhin", after=started)
    dx, dx_bf, dg = rms_bwd(x, g, dh, dxo, "s5_dnorm")
    return dx, dx_bf, dg, dlr, dli, dldt, db_re, db_im, dc_re, dc_im, dd


def ple_fwd(x, g, p_emb, w_gate, w_proj, tag, prefetch):
    h = rms_fwd(x, g, f"ple_norm_{tag}")
    (q,) = matmul([(p_emb, w_proj)], 'nn', [F32], tm=512, tn=512, name=f"ple_proj_{tag}")

    def ep(accs, ex):
        gt = _sigmoid(accs[0])
        return ex[0] + gt * ex[1], gt

    xo, gate = matmul([(h, w_gate)], 'nn', [F32, F32], epilogue=ep, extras=[(x, 't'), (q, 't')],
                      tm=512, tn=512, name=f"ple_gate_{tag}", after=prefetch(q))
    return xo, (x, h, q, gate)


def ple_bwd(saved, g, p_emb, w_gate, dxo, tag, after, on_grads):
    x, h, q, gate = saved
    M, D = x.shape
    tm = _pick(M, (256, 128))
    dq, dpre = tilemap(lambda dv, qv, gv: (dv * gv, dv * qv * gv * (1.0 - gv)),
                       [(dxo, 't', 0), (q, 't', 0), (gate, 't', 0)], [(BF16, 't'), (BF16, 't')],
                       M=M, N=D, tm=tm, tn=D, name=f"ple_dgate_{tag}", after=after)
    (dw_proj,) = matmul([(p_emb, dq)], 'tn', [BF16], tm=256, tn=512, name=f"ple_dwproj_{tag}")
    (dw_gate,) = matmul([(h, dpre)], 'tn', [BF16], tm=512, tn=512, name=f"ple_dwgate_{tag}")
    started = on_grads([dw_gate, dw_proj])
    (dh,) = matmul([(dpre, w_gate)], 'nt', [F32], tm=512, tn=512, name=f"ple_dh_{tag}", after=started)
    dx, dx_bf, dg = rms_bwd(x, g, dh, dxo, f"ple_dnorm_{tag}")
    return dx, dx_bf, dg


def loss_head(x, g, target):
    M, D = x.shape
    tm = _pick(M, (256, 128))

    def fn(xv, gv, tv):
        r = lax.rsqrt(jnp.mean(xv * xv, axis=-1, keepdims=True) + EPS)
        xh = xv * r
        e = xh * gv - tv
        dy = e * (1.0 / D)
        dxh = dy * gv
        m = jnp.mean(dxh * xh, axis=-1, keepdims=True)
        dx = r * (dxh - xh * m)
        return jnp.sum(e * e, axis=0, keepdims=True), dx, dx, jnp.sum(dy * xh, axis=0, keepdims=True)

    return tilemap(fn, [(x, 't', 0), (g, 'r', 0), (target, 't', 0)],
                   [(F32, 'a'), (F32, 't'), (BF16, 't'), (F32, 'a')], M=M, N=D, tm=tm, tn=D, name="loss_head")


def _adamw_math(wv, gv, mv, vv):
    mn = ADAM_B1 * mv + (1.0 - ADAM_B1) * gv
    vn = ADAM_B2 * vv + (1.0 - ADAM_B2) * (gv * gv)
    m_hat = mn / (1.0 - ADAM_B1 ** ADAM_STEP)
    v_hat = vn / (1.0 - ADAM_B2 ** ADAM_STEP)
    return -ADAM_LR * (m_hat / (jnp.sqrt(v_hat) + ADAM_EPS) + ADAM_WD * wv), mn, vn


def adamw_into(w, lead, g, m, v, carry, name, after):
    R, C = w.shape[-2:]
    lead = tuple(lead)
    tm = _row_tile(R, C)
    blk = pl.BlockSpec((None,) * len(lead) + (tm, C), lambda i: lead + (i, 0))
    n_carry = 0 if carry is None else 4

    def body(*refs):
        w_r, g_r, m_r, v_r = refs[:4]
        g_o, d_o, m_o, v_o, token = refs[5 + n_carry:]
        gv = g_r[...]
        d, mn, vn = _adamw_math(w_r[...], gv, m_r[...], v_r[...])
        g_o[...] = gv
        d_o[...] = d
        m_o[...] = mn
        v_o[...] = vn
        token[...] = jnp.zeros_like(token)

    shp = jax.ShapeDtypeStruct(w.shape, F32)
    res = _pcall(body, name=name, grid=(R // tm,),
                 in_specs=[blk, pl.BlockSpec((tm, C), lambda i: (i, 0)), blk, blk]
                 + [pl.BlockSpec(memory_space=pl.ANY)] * (1 + n_carry),
                 out_specs=[blk] * 4 + [pl.BlockSpec((8, 128), lambda i: (0, 0))],
                 out_shape=[shp] * 4 + [jax.ShapeDtypeStruct((8, 128), F32)],
                 input_output_aliases={5 + k: k for k in range(n_carry)},
                 compiler_params=_params())(w, g, m, v, after, *(carry or ()))
    return tuple(res[:4]), res[4]


def adamw(w, g, m, v, name):
    w2, g2, m2, v2 = _as2d(w), g.reshape(_as2d(w).shape), _as2d(m), _as2d(v)
    R, C = w2.shape
    tm = _row_tile(R, C)

    d, mn, vn = tilemap(_adamw_math, [(w2, 't', 0), (g2, 't', 0), (m2, 't', 0), (v2, 't', 0)],
                        [(F32, 't'), (F32, 't'), (F32, 't')], M=R, N=C, tm=tm, tn=C, name=name)
    return d.reshape(w.shape), mn.reshape(w.shape), vn.reshape(w.shape)


WEIGHT_NAMES = ['norm_g', 'final_norm_g', 'ffn_w_gate', 'ffn_w_up', 'ffn_w_down', 'gmlp_w_in', 'gmlp_ln_g',
                'gmlp_ln_b', 'gmlp_w_s', 'gmlp_b_s', 'gmlp_w_out', 's5_w_in', 's5_lam_re', 's5_lam_im',
                's5_log_dt', 's5_b_re', 's5_b_im', 's5_c_re', 's5_c_im', 's5_d', 's5_w_out', 'ple_w_gate',
                'ple_w_proj']
BIG = {'ffn_w_gate': 1, 'ffn_w_up': 1, 'ffn_w_down': 0, 'gmlp_w_in': 1, 'gmlp_w_out': 0, 's5_w_in': 0,
       's5_w_out': 1, 'ple_w_gate': 0, 'ple_w_proj': 1}


def _blocks(depth):
    out = []
    for i in range(depth):
        for k, half in enumerate("ab"):
            ffn = [(n, (i, k)) for n in ('ffn_w_gate', 'ffn_w_up', 'ffn_w_down')]
            if k == 1:
                out.append((f"ffn{i}b", ffn))
                out.append((f"ple{i}", [('ple_w_gate', (i,)), ('ple_w_proj', (i,))]))
            else:
                out.append((f"ffn{i}a", ffn))
                mix = 'gmlp' if i % 2 == 0 else 's5'
                out.append((f"{mix}{i}", [(f'{mix}_w_in', (i // 2,)), (f'{mix}_w_out', (i // 2,))]))
    return out


def kernel(x, p, norm_g, final_norm_g, ffn_w_gate, ffn_w_up, ffn_w_down, gmlp_w_in, gmlp_ln_g, gmlp_ln_b, gmlp_w_s, gmlp_b_s, gmlp_w_out, s5_w_in, s5_lam_re, s5_lam_im, s5_log_dt, s5_b_re, s5_b_im, s5_c_re, s5_c_im, s5_d, s5_w_out, ple_w_gate, ple_w_proj, loss_target, m_norm_g, m_final_norm_g, m_ffn_w_gate, m_ffn_w_up, m_ffn_w_down, m_gmlp_w_in, m_gmlp_ln_g, m_gmlp_ln_b, m_gmlp_w_s, m_gmlp_b_s, m_gmlp_w_out, m_s5_w_in, m_s5_lam_re, m_s5_lam_im, m_s5_log_dt, m_s5_b_re, m_s5_b_im, m_s5_c_re, m_s5_c_im, m_s5_d, m_s5_w_out, m_ple_w_gate, m_ple_w_proj, v_norm_g, v_final_norm_g, v_ffn_w_gate, v_ffn_w_up, v_ffn_w_down, v_gmlp_w_in, v_gmlp_ln_g, v_gmlp_ln_b, v_gmlp_w_s, v_gmlp_b_s, v_gmlp_w_out, v_s5_w_in, v_s5_lam_re, v_s5_lam_im, v_s5_log_dt, v_s5_b_re, v_s5_b_im, v_s5_c_re, v_s5_c_im, v_s5_d, v_s5_w_out, v_ple_w_gate, v_ple_w_proj):
    env = dict(locals())
    W = {n: env[n] for n in WEIGHT_NAMES}
    Mo = {n: env["m_" + n] for n in WEIGHT_NAMES}
    Vo = {n: env["v_" + n] for n in WEIGHT_NAMES}
    depth = norm_g.shape[0]
    L, D = x.shape[1], x.shape[2]
    s_idx = 2 * lax.axis_index("x") + lax.axis_index("y")

    sc = jnp.stack([s_idx, lax.axis_index("c")]).astype(jnp.int32)
    blocks = _blocks(depth)

    ng2 = norm_g.reshape(depth * 4, norm_g.shape[-1])
    ng_full, sd_full = gather_small("gather_small", [ng2, s5_d], sc)
    full = {}
    gathers = []
    token = sd_full
    for bname, mats in blocks:
        casts = [cast_into_full(W[n], lead, BIG[n], sc, f"cast_{bname}_{n}", token) for n, lead in mats]
        handle, token = gather_start(f"gather_{bname}", casts, [W[n].shape[-2:] for n, _ in mats],
                                     [BIG[n] for n, _ in mats], token)
        gathers.append(handle)
    all_started = token
    ng_full = ng_full.reshape(depth, 4, 1, D)

    landed = {}

    def prefetcher(bi):
        def prefetch(value):
            if bi >= len(blocks):
                return value
            landed[bi], token = gather_land(gathers[bi], value)
            return token
        return prefetch

    def fetch(bi, after):
        if bi not in landed:
            landed[bi], after = gather_land(gathers[bi], after)
        full.update(dict(zip(blocks[bi][1], gather_take(landed[bi], after))))
    gf = final_norm_g.reshape(1, D)

    G, P, H = s5_b_re.shape[1:]
    n_grp, chunk = gmlp_w_s.shape[1], gmlp_w_s.shape[2]
    lam_re, lam_im = s5_lam_re[0], s5_lam_im[0]
    log_dt = s5_log_dt.reshape(G, 1)
    b_re, b_im, c_re, c_im = s5_b_re[0], s5_b_im[0], s5_c_re[0], s5_c_im[0]
    w_s, b_s = gmlp_w_s[0], gmlp_b_s[0].reshape(n_grp, chunk, 1)
    xs = x.reshape(L, D)
    saved = []
    def ffn_w(i, k):
        return [full[(n, (i, k))] for n in ('ffn_w_gate', 'ffn_w_up', 'ffn_w_down')]

    for i in range(depth):
        sv = {}
        fetch(4 * i, all_started if i == 0 else xs)
        xs, sv['ffn_a'] = ffn_fwd(xs, ng_full[i, 0], *ffn_w(i, 0), f"{i}a", prefetcher(4 * i + 1))
        j = (i // 2,)
        fetch(4 * i + 1, xs)
        if i % 2 == 0:
            xs, sv['mix'] = gmlp_fwd(xs, ng_full[i, 1], full[('gmlp_w_in', j)], gmlp_ln_g, gmlp_ln_b, w_s, b_s,
                                     full[('gmlp_w_out', j)], prefetcher(4 * i + 2))
        else:
            xs, sv['mix'] = s5_fwd(xs, ng_full[i, 1], full[('s5_w_in', j)], lam_re, lam_im, log_dt, b_re, b_im,
                                   c_re, c_im, sd_full, full[('s5_w_out', j)], prefetcher(4 * i + 2))
        fetch(4 * i + 2, xs)
        xs, sv['ffn_b'] = ffn_fwd(xs, ng_full[i, 2], *ffn_w(i, 1), f"{i}b", prefetcher(4 * i + 3))
        fetch(4 * i + 3, xs)
        xs, sv['ple'] = ple_fwd(xs, ng_full[i, 3], p[i, 0], full[('ple_w_gate', (i,))], full[('ple_w_proj', (i,))], f"{i}",
                                prefetcher(4 * i + 4))
        saved.append(sv)

    sq, dx, dx_bf, dgf = loss_head(xs, gf, loss_target.reshape(L, D))
    loss_local = 0.5 * jnp.sum(sq) / D
    dng = [[None] * 4 for _ in range(depth)]
    small = {}
    gshard = {}
    state = {'fence': sc}
    ici_inflight = []
    swaps = []

    def hook(bi):
        def on_grads(gs):
            bname, mats = blocks[bi]
            handle, token = reduce_pair_start(f"reduce_{bname}", gs, [BIG[n] for n, _ in mats], state['fence'])
            state['pair'] = (handle, mats)
            return token
        return on_grads

    def reduce_block(after, keep=1):
        handle, mats = state.pop('pair')
        ici_handle, fence = reduce_ici_start(handle, sc, after)
        ici_inflight.append((ici_handle, mats))
        while len(ici_inflight) > keep:
            prev_handle, prev_mats = ici_inflight.pop(0)
            swap_handle, fence = reduce_finish(prev_handle, fence)
            swaps.append((swap_handle, prev_mats))
        state['fence'] = fence

    for i in reversed(range(depth)):
        sv = saved[i]
        dx, dx_bf, dng[i][3] = ple_bwd(sv['ple'], ng_full[i, 3], p[i, 0], full[('ple_w_gate', (i,))], dx, f"{i}",
                                       state['fence'], hook(4 * i + 3))
        reduce_block(dx_bf, keep=2)
        dx, dx_bf, dng[i][2] = ffn_bwd(sv['ffn_b'], ng_full[i, 2], *ffn_w(i, 1), dx, dx_bf, f"{i}b",
                                       state['fence'], hook(4 * i + 2))
        reduce_block(dx_bf)
        j = (i // 2,)
        if i % 2 == 0:
            dx, dx_bf, dng[i][1], dlg, dlb, dws, dbs = gmlp_bwd(
                sv['mix'], ng_full[i, 1], full[('gmlp_w_in', j)], gmlp_ln_g, gmlp_ln_b, w_s, b_s,
                full[('gmlp_w_out', j)], dx, dx_bf, state['fence'], hook(4 * i + 1))
            small.update(gmlp_ln_g=dlg, gmlp_ln_b=dlb, gmlp_w_s=dws, gmlp_b_s=dbs)
        else:
            dx, dx_bf, dng[i][1], dlr, dli, dldt, db_re, db_im, dc_re, dc_im, dd = s5_bwd(
                sv['mix'], ng_full[i, 1], full[('s5_w_in', j)], lam_re, lam_im, log_dt, b_re, sd_full,
                full[('s5_w_out', j)], dx, state['fence'], hook(4 * i + 1))
            small.update(s5_lam_re=dlr, s5_lam_im=dli, s5_log_dt=dldt, s5_b_re=db_re, s5_b_im=db_im,
                         s5_c_re=dc_re, s5_c_im=dc_im, s5_d=dd)
        reduce_block(dx_bf)
        dx, dx_bf, dng[i][0] = ffn_bwd(sv['ffn_a'], ng_full[i, 0], *ffn_w(i, 0), dx, dx_bf, f"{i}a",
                                       state['fence'], hook(4 * i))
        reduce_block(dx_bf)
    grad_x = dx.reshape(x.shape)
    small['norm_g'] = jnp.stack([jnp.stack(r) for r in dng])
    small['final_norm_g'] = dgf

    grads, deltas, new_m, new_v = {}, {}, {}, {}
    carry = {}
    fence = state['fence']

    def update_big(key, fence):
        n, lead = key
        carry[n], fence = adamw_into(W[n], lead, gshard[key], Mo[n], Vo[n], carry.get(n),
                                     f"adamw_{n}_{'_'.join(map(str, lead))}", fence)
        return fence

    small_names = [n for n in WEIGHT_NAMES if n not in BIG]
    flat = jnp.concatenate([small[n].astype(F32).reshape(-1) for n in small_names] + [loss_local.reshape(1)])
    pad = (-flat.size) % (256 * 128)
    flat = jnp.pad(flat, (0, pad)).reshape(-1, 128)
    handle, fence = small_swap_start(0, flat, fence)
    handle, fence = small_swap_start(1, small_swap_finish(handle, fence), fence)
    for swap_handle, mats in swaps:
        gshard.update(dict(zip(mats, reduce_swap_wait(swap_handle, fence))))
        for key in mats:
            fence = update_big(key, fence)
    handle, fence = small_swap_start(2, small_swap_finish(handle, fence), fence)
    last_handle, last_mats = ici_inflight.pop()
    swap_handle, fence = reduce_finish(last_handle, fence)
    gshard.update(dict(zip(last_mats, reduce_swap_wait(swap_handle, fence))))
    for key in last_mats:
        fence = update_big(key, fence)
    tot = small_swap_finish(handle, fence).reshape(-1)
    off = 0
    for n in small_names:
        sz = small[n].size
        gsum = tot[off:off + sz]
        off += sz
        if n == 'norm_g':
            gsum = lax.dynamic_slice_in_dim(gsum.reshape(depth, 4, D), s_idx * W[n].shape[-1], W[n].shape[-1], axis=2)
        elif n == 's5_d':
            gsum = lax.dynamic_slice_in_dim(gsum.reshape(1, D), s_idx * W[n].shape[-1], W[n].shape[-1], axis=1)
        grads[n] = gsum.reshape(W[n].shape)
        deltas[n], new_m[n], new_v[n] = adamw(W[n], grads[n], Mo[n], Vo[n], f"adamw_{n}")
    loss = tot[off]
    for n in BIG:
        grads[n], deltas[n], new_m[n], new_v[n] = carry[n]
    return (loss, grad_x, *[grads[n] for n in WEIGHT_NAMES], *[deltas[n] for n in WEIGHT_NAMES],
            *[new_m[n] for n in WEIGHT_NAMES], *[new_v[n] for n in WEIGHT_NAMES])
```

```python
import functools
import math

import jax
import jax.numpy as jnp
from jax import lax
from jax.experimental import pallas as pl
from jax.experimental.pallas import tpu as pltpu

F32 = jnp.float32
BF16 = jnp.bfloat16
MESH_ID = pl.DeviceIdType.MESH

EPS = 1e-6
ADAM_LR = 0.001
ADAM_B1 = 0.9
ADAM_B2 = 0.999
ADAM_EPS = 1e-08
ADAM_WD = 0.01
ADAM_STEP = 10

N_SHARD = 4
V7X_VMEM_LIMIT = 52 * 2 ** 20
SSM_TILE_GROUPS = 16
SCAN_LANES = 1024
GELU_C = math.sqrt(2.0 / math.pi)


def _pcall(body, **kw):
    return pl.pallas_call(body, **kw)


def _params():
    return pltpu.CompilerParams(vmem_limit_bytes=V7X_VMEM_LIMIT)


def _pick(n, cands):
    for c in cands:
        if c <= n and n % c == 0:
            return c
    return n


def _sigmoid(x):
    return 1.0 / (1.0 + jnp.exp(-x))


def _gelu(x):
    return 0.5 * x * (1.0 + jnp.tanh(GELU_C * (x + 0.044715 * x * x * x)))


def _gelu_grad(x):
    t = jnp.tanh(GELU_C * (x + 0.044715 * x * x * x))
    return 0.5 * (1.0 + t) + 0.5 * x * (1.0 - t * t) * GELU_C * (1.0 + 3.0 * 0.044715 * x * x)


def tilemap(fn, ins, outs, *, M, N, tm, tn, name, after=None):
    n_in = len(ins)
    n_dep = 0 if after is None else 1
    grid = (N // tn, M // tm)
    in_specs = []
    for arr, kind, off in ins:
        if kind == 't':
            in_specs.append(pl.BlockSpec((tm, tn), lambda j, i, off=off: (i, j + off)))
        else:
            in_specs.append(pl.BlockSpec((1, tn), lambda j, i, off=off: (0, j + off)))
    out_specs, out_shape = [], []
    for dt, kind in outs:
        if kind == 't':
            out_specs.append(pl.BlockSpec((tm, tn), lambda j, i: (i, j)))
            out_shape.append(jax.ShapeDtypeStruct((M, N), dt))
        else:
            out_specs.append(pl.BlockSpec((1, tn), lambda j, i: (0, j)))
            out_shape.append(jax.ShapeDtypeStruct((1, N), F32))
    in_specs += [pl.BlockSpec(memory_space=pl.ANY)] * n_dep

    def body(*refs):
        vals = fn(*[r[...] for r in refs[:n_in]])
        for (dt, kind), ref, v in zip(outs, refs[n_in + n_dep:], vals):
            if kind == 't':
                ref[...] = v.astype(ref.dtype)
            else:
                @pl.when(pl.program_id(1) == 0)
                def _():
                    ref[...] = jnp.zeros_like(ref)
                ref[...] += v

    res = _pcall(body, name=name, grid=grid, in_specs=in_specs, out_specs=out_specs,
                 out_shape=out_shape, compiler_params=_params())(*[a for a, _, _ in ins], *([after] * n_dep))
    return res


def _as2d(a):
    if a.ndim >= 2 and a.shape[-1] % 128 == 0:
        return a.reshape(-1, a.shape[-1])
    if a.size % 128 == 0:
        return a.reshape(-1, 128)
    return a.reshape(-1, a.shape[-1])


def _row_tile(rows, cols, nbytes=4, budget=1 << 20):
    cands = [c for c in (2048, 1024, 512, 256, 128, 64, 32, 16, 8) if c * cols * nbytes <= budget]
    return _pick(rows, cands) if cands else _pick(rows, (8,))


ROW_TILES = (256, 128, 64, 32, 16)


def _sc_call(body, sc, args, *, grid, in_specs, out_specs, out_shape, name):
    gs = pltpu.PrefetchScalarGridSpec(num_scalar_prefetch=1, grid=grid, in_specs=in_specs, out_specs=out_specs)
    return _pcall(body, name=name, grid_spec=gs, out_shape=out_shape, compiler_params=_params())(sc, *args)


def cast_into_full(w, lead, ax, sc, name, after):
    R, C = w.shape[-2:]
    tm = _pick(R, ROW_TILES)
    nb = R // tm
    lead = tuple(lead)
    in_spec = pl.BlockSpec((None,) * len(lead) + (tm, C), lambda i, s: lead + (i, 0))
    if ax == 0:
        shape, out_map = (R * N_SHARD, C), (lambda i, s: (i + s[0] * nb, 0))
    else:
        shape, out_map = (R, C * N_SHARD), (lambda i, s: (i, s[0]))

    def body(s_ref, w_ref, after_ref, o_ref):
        o_ref[...] = w_ref[...].astype(BF16)

    return _sc_call(body, sc, [w, after], grid=(nb,), in_specs=[in_spec, pl.BlockSpec(memory_space=pl.ANY)],
                    out_specs=pl.BlockSpec((tm, C), out_map), out_shape=jax.ShapeDtypeStruct(shape, BF16), name=name)


def pairsum(g, a, ax, sc, name):
    hR, hC = a.shape
    tm = _pick(hR, ROW_TILES)
    nb = hR // tm
    g_map = (lambda i, s: (i + s[1] * nb, 0)) if ax == 1 else (lambda i, s: (i, s[1]))
    blk = (tm, hC)

    def body(s_ref, g_ref, a_ref, o_ref):
        o_ref[...] = (g_ref[...].astype(F32) + a_ref[...].astype(F32)).astype(BF16)

    return _sc_call(body, sc, [g, a], grid=(nb,),
                    in_specs=[pl.BlockSpec(blk, g_map), pl.BlockSpec(blk, lambda i, s: (i, 0))],
                    out_specs=pl.BlockSpec(blk, lambda i, s: (i, 0)),
                    out_shape=jax.ShapeDtypeStruct((hR, hC), BF16), name=name)


def shardsum(b, cbuf, ax, sc, name):
    hR, hC = b.shape
    _, pR, pC = cbuf.shape
    tm = _pick(pR, ROW_TILES)
    nb = pR // tm
    if ax == 1:
        b_map, o_map, shape = (lambda i, s: (i, s[0])), (lambda i, s: (i + s[1] * nb, 0)), (2 * pR, pC)
    else:
        b_map, o_map, shape = (lambda i, s: (i + s[0] * nb, 0)), (lambda i, s: (i, s[1])), (pR, 2 * pC)

    def body(s_ref, b_ref, c_ref, o_ref):
        acc = b_ref[...].astype(F32)
        for k in range(N_SHARD - 1):
            acc = acc + c_ref[k].astype(F32)
        o_ref[...] = acc

    return _sc_call(body, sc, [b, cbuf], grid=(nb,),
                    in_specs=[pl.BlockSpec((tm, pC), b_map), pl.BlockSpec((N_SHARD - 1, tm, pC), lambda i, s: (0, i, 0))],
                    out_specs=pl.BlockSpec((tm, pC), o_map), out_shape=jax.ShapeDtypeStruct(shape, F32), name=name)


_DIMS = {'nn': (((1,), (0,)), ((), ())), 'nt': (((1,), (1,)), ((), ())), 'tn': (((0,), (0,)), ((), ()))}


def matmul(pairs, mode, outs, *, epilogue=None, extras=(), tm=512, tn=512, name, after=None):
    a0, b0 = pairs[0]
    if mode == 'nn':
        (M, K), N = a0.shape, b0.shape[1]
    elif mode == 'nt':
        (M, K), N = a0.shape, b0.shape[0]
    else:
        (K, M), N = a0.shape, b0.shape[1]
    tm, tn = _pick(M, (tm, 256, 128)), _pick(N, (tn, 256, 128))
    n_p, n_e = len(pairs), len(extras)
    if mode == 'tn':
        a_spec = pl.BlockSpec((K, tm), lambda i, j: (0, i))
    else:
        a_spec = pl.BlockSpec((tm, K), lambda i, j: (i, 0))
    if mode == 'nt':
        b_spec = pl.BlockSpec((tn, K), lambda i, j: (j, 0))
    else:
        b_spec = pl.BlockSpec((K, tn), lambda i, j: (0, j))
    in_specs, args = [], []
    for a, b in pairs:
        in_specs += [a_spec, b_spec]
        args += [a, b]
    for arr, kind in extras:
        if kind == 't':
            in_specs.append(pl.BlockSpec((tm, tn), lambda i, j: (i, j)))
        else:
            in_specs.append(pl.BlockSpec((1, tn), lambda i, j: (0, j)))
        args.append(arr)
    n_dep = 0 if after is None else 1
    in_specs += [pl.BlockSpec(memory_space=pl.ANY)] * n_dep
    args += [after] * n_dep
    dims = _DIMS[mode]

    def body(*refs):
        accs = [lax.dot_general(refs[2 * p][...].astype(BF16), refs[2 * p + 1][...].astype(BF16), dims,
                                preferred_element_type=F32) for p in range(n_p)]
        ex = [r[...] for r in refs[2 * n_p:2 * n_p + n_e]]
        if epilogue is None:
            acc = accs[0]
            for other in accs[1:]:
                acc = acc + other
            res = (acc,)
        else:
            res = epilogue(accs, ex)
        for ref, v in zip(refs[2 * n_p + n_e + n_dep:], res):
            ref[...] = v.astype(ref.dtype)

    return _pcall(body, name=name, grid=(M // tm, N // tn), in_specs=in_specs,
                  out_specs=[pl.BlockSpec((tm, tn), lambda i, j: (i, j)) for _ in outs],
                  out_shape=[jax.ShapeDtypeStruct((M, N), dt) for dt in outs],
                  compiler_params=_params())(*args)


def comm_call(name, ins, out_shapes, plan, n_local, n_remote, aliases=None, after=None):
    ins = list(ins) + ([] if after is None else [after])
    n_in, n_out = len(ins), len(out_shapes)

    def body(*refs):
        in_refs, out_refs = refs[:n_in], refs[n_in:n_in + n_out]
        lsem, ssem, rsem = refs[n_in + n_out:]
        me = (lax.axis_index("x"), lax.axis_index("y"), lax.axis_index("c"))
        local, remote = plan(me, in_refs, out_refs)
        assert len(local) == n_local and len(remote) == n_remote
        lcs = [pltpu.make_async_copy(s, d, lsem.at[k]) for k, (s, d) in enumerate(local)]
        rcs = [pltpu.make_async_remote_copy(src_ref=s, dst_ref=d, send_sem=ssem.at[k], recv_sem=rsem.at[k],
                                            device_id=peer, device_id_type=MESH_ID)
               for k, (s, d, peer) in enumerate(remote)]
        for cp in rcs:
            cp.start()
        for cp in lcs:
            cp.start()
        for cp in rcs:
            cp.wait()
        for cp in lcs:
            cp.wait()

    any_spec = pl.BlockSpec(memory_space=pl.ANY)
    return _pcall(body, name=name, in_specs=[any_spec] * n_in, out_specs=[any_spec] * n_out,
                  out_shape=list(out_shapes),
                  scratch_shapes=[pltpu.SemaphoreType.DMA((max(n_local, 1),)),
                                  pltpu.SemaphoreType.DMA((max(n_remote, 1),)),
                                  pltpu.SemaphoreType.DMA((max(n_remote, 1),))],
                  input_output_aliases=aliases or {},
                  compiler_params=pltpu.CompilerParams(has_side_effects=True))(*ins)


_HBM_SPEC = pl.BlockSpec(memory_space=pltpu.HBM)
_SEM_SPEC = pl.BlockSpec(memory_space=pltpu.SEMAPHORE)
_DATAFLOW = pltpu.SideEffectType.DATAFLOW_SIDE_EFFECTING


def split_start(name, arrays, n_copies, plan, after):
    n = len(arrays)

    def body(*refs):
        ins, (ssem, rsem) = refs[:n], refs[n + 1:n + 3]
        token = refs[-1]
        me = (lax.axis_index("x"), lax.axis_index("y"), lax.axis_index("c"))
        for k, (src, dst, peer) in enumerate(plan(me, ins)):
            pltpu.make_async_remote_copy(src_ref=src, dst_ref=dst, send_sem=ssem.at[k], recv_sem=rsem.at[k],
                                         device_id=peer, device_id_type=MESH_ID).start()
        token[...] = jnp.zeros_like(token)

    sems = pltpu.SemaphoreType.DMA((n_copies,))
    res = _pcall(body, name=name,
                 out_shape=(sems, sems, *[pltpu.HBM(a.shape, a.dtype) for a in arrays], jax.ShapeDtypeStruct((8, 128), F32)),
                 in_specs=[_HBM_SPEC] * n + [pl.BlockSpec(memory_space=pl.ANY)],
                 out_specs=(_SEM_SPEC, _SEM_SPEC, *[_HBM_SPEC] * n, pl.BlockSpec(memory_space=pltpu.VMEM)),
                 input_output_aliases={m: 2 + m for m in range(n)},
                 compiler_params=pltpu.CompilerParams(has_side_effects=_DATAFLOW))(
        *[pltpu.with_memory_space_constraint(a, pltpu.HBM) for a in arrays], after)
    return res[0], res[1], list(res[2:2 + n]), res[-1]


def split_wait(name, arrays, ssem, rsem, plan, after):
    n = len(arrays)

    def body(*refs):
        ins, (ssem_r, rsem_r) = refs[:n], refs[n:n + 2]
        me = (lax.axis_index("x"), lax.axis_index("y"), lax.axis_index("c"))
        for k, (src, dst, peer) in enumerate(plan(me, ins)):
            cp = pltpu.make_async_remote_copy(src_ref=src, dst_ref=dst, send_sem=ssem_r.at[k], recv_sem=rsem_r.at[k],
                                              device_id=peer, device_id_type=MESH_ID)
            cp.wait_send()
            cp.wait_recv()

    res = _pcall(body, name=name, out_shape=[pltpu.HBM(a.shape, a.dtype) for a in arrays],
                 in_specs=[_HBM_SPEC] * n + [_SEM_SPEC, _SEM_SPEC, pl.BlockSpec(memory_space=pl.ANY)],
                 out_specs=[_HBM_SPEC] * n, input_output_aliases={m: m for m in range(n)},
                 compiler_params=pltpu.CompilerParams(has_side_effects=_DATAFLOW))(*arrays, ssem, rsem, after)
    return list(res)


def _shard_of(me):
    return 2 * me[0] + me[1]


def _plane_peers(me):
    x, y, c = me
    return [((1 - x, y, c), 2 * (1 - x) + y), ((x, 1 - y, c), 2 * x + 1 - y),
            ((1 - x, 1 - y, c), 2 * (1 - x) + 1 - y)]


def _mats(arr):
    out = [()]
    for n in arr.shape[:-2]:
        out = [o + (k,) for o in out for k in range(n)]
    return out


ROW_ALIGN = 16
LANE_ALIGN = 128


def _win(ref, lead, rows, cols):
    idx = tuple(lead)
    for spec, align in ((rows, ROW_ALIGN), (cols, LANE_ALIGN)):
        if spec is None:
            idx += (slice(None),)
        else:
            start, size = spec
            if not isinstance(start, int):
                start = pl.multiple_of(start, align)
            idx += (pl.ds(start, size),)
    return ref.at[idx]


def gather_small(name, shards, after):
    full_shapes = [jax.ShapeDtypeStruct((a.shape[0], a.shape[1] * N_SHARD), a.dtype) for a in shards]
    n = len(shards)

    def plan(me, in_refs, out_refs):
        s = _shard_of(me)
        local, remote = [], []
        for t, a in enumerate(shards):
            dst = _win(out_refs[t], (), None, (s * a.shape[1], a.shape[1]))
            local.append((in_refs[t], dst))
            for peer, _ in _plane_peers(me):
                remote.append((in_refs[t], dst, peer))
        return local, remote

    return comm_call(name, shards, full_shapes, plan, n, 3 * n, after=after)


def _half_shape(g, ax):
    R, C = g.shape
    return (R // 2, C) if ax == 1 else (R, C // 2)


def reduce_pair_start(name, grads, axes, after):
    n = len(grads)
    landing = [lax.empty(_half_shape(g, ax), BF16) for g, ax in zip(grads, axes)]

    def plan_a(me, refs):
        x, y, c = me
        copies = []
        for m in range(n):
            R, C = grads[m].shape
            if axes[m] == 1:
                rows, cols = ((1 - c) * (R // 2), R // 2), None
            else:
                rows, cols = None, ((1 - c) * (C // 2), C // 2)
            copies.append((_win(refs[m], (), rows, cols), refs[n + m], (x, y, 1 - c)))
        return copies

    ssem, rsem, thru, token = split_start(name + "_pair_start", list(grads) + landing, n, plan_a, after)
    return (name, axes, plan_a, ssem, rsem, thru), token


def reduce_ici_start(handle, sc, after):
    name, axes, plan_a, ssem, rsem, thru = handle
    n = len(axes)
    done = split_wait(name + "_pair_wait", thru, ssem, rsem, plan_a, after)
    grads, a_bufs = done[:n], done[n:]
    b_bufs = [pairsum(g, a, ax, sc, f"{name}_pairsum{m}") for m, (g, a, ax) in enumerate(zip(grads, a_bufs, axes))]

    def piece_shape(m):
        R, C = a_bufs[m].shape
        return (R, C // N_SHARD) if axes[m] == 1 else (R // N_SHARD, C)

    def piece_win(m, s):
        R, C = piece_shape(m)
        if axes[m] == 1:
            return None, (s * C, C)
        return (s * R, R), None

    landing = [lax.empty((N_SHARD - 1,) + piece_shape(m), BF16) for m in range(n)]

    def plan_c(me, refs):
        copies = []
        for m in range(n):
            for j, (peer, ps) in enumerate(_plane_peers(me)):
                rows, cols = piece_win(m, ps)
                copies.append((_win(refs[m], (), rows, cols), refs[n + m].at[j], peer))
        return copies

    ssem, rsem, thru, token = split_start(name + "_ici_start", b_bufs + landing, 3 * n, plan_c, after)
    return (name, axes, sc, plan_c, ssem, rsem, thru), token


def reduce_finish(handle, after):
    name, axes, sc, plan_c, ssem, rsem, thru = handle
    n = len(axes)
    done = split_wait(name + "_ici_wait", thru, ssem, rsem, plan_c, after)
    b_bufs, c_bufs = done[:n], done[n:]
    shards = [shardsum(b, cb, ax, sc, f"{name}_shardsum{m}") for m, (b, cb, ax) in enumerate(zip(b_bufs, c_bufs, axes))]

    def plan_e(me, refs):
        x, y, c = me
        copies = []
        for m in range(n):
            R, C = shards[m].shape
            if axes[m] == 1:
                rows, cols = (c * (R // 2), R // 2), None
            else:
                rows, cols = None, (c * (C // 2), C // 2)
            copies.append((_win(refs[m], (), rows, cols), _win(refs[m], (), rows, cols), (x, y, 1 - c)))
        return copies

    ssem, rsem, thru, token = split_start(name + "_swap_start", shards, n, plan_e, after)
    return (name, plan_e, ssem, rsem, thru), token


def reduce_swap_wait(handle, after):
    name, plan_e, ssem, rsem, thru = handle
    return split_wait(name + "_swap_wait", thru, ssem, rsem, plan_e, after)


def gather_start(name, fulls, shard_shapes, axes, after):
    n = len(fulls)

    def win(ref, m, s, half):
        R, C = shard_shapes[m]
        r0 = s * R if axes[m] == 0 else 0
        return _win(ref, (), (r0 + half * (R // 2), R // 2), None if axes[m] == 0 else (s * C, C))

    def plan_ici(me, refs):
        s, c = _shard_of(me), me[2]
        return [(win(refs[m], m, s, c), win(refs[m], m, s, c), peer) for m in range(n) for peer, _ in _plane_peers(me)]

    def plan_fwd(me, refs):
        x, y, c = me
        return [(win(refs[m], m, ps, c), win(refs[m], m, ps, c), (x, y, 1 - c)) for m in range(n) for _, ps in _plane_peers(me)]

    ssem, rsem, thru, token = split_start(name + "_start", fulls, 3 * n, plan_ici, after)
    return (name, plan_ici, plan_fwd, ssem, rsem, thru), token


def gather_land(handle, after):
    name, plan_ici, plan_fwd, ssem, rsem, thru = handle
    got = split_wait(name + "_wait", thru, ssem, rsem, plan_ici, after)
    ssem, rsem, thru, token = split_start(name + "_fwd_start", got, 3 * len(got), plan_fwd, after)
    return (name, plan_fwd, ssem, rsem, thru), token


def gather_take(handle, after):
    name, plan_fwd, ssem, rsem, thru = handle
    return split_wait(name + "_fwd_wait", thru, ssem, rsem, plan_fwd, after)


SMALL_FLIPS = ((0, 0, 1), (0, 1, 0), (1, 0, 0))


def small_swap_start(axis, cur, after):
    def plan(me, refs):
        peer = tuple(v + f * (1 - 2 * v) for v, f in zip(me, SMALL_FLIPS[axis]))
        return [(refs[0], refs[1], peer)]

    ssem, rsem, thru, token = split_start(f"small_swap{axis}_start", [cur, lax.empty(cur.shape, cur.dtype)], 1, plan, after)
    return (axis, plan, ssem, rsem, thru), token


def small_swap_finish(handle, after):
    axis, plan, ssem, rsem, thru = handle
    cur, got = split_wait(f"small_swap{axis}_wait", thru, ssem, rsem, plan, after)
    R, C = cur.shape
    (cur,) = tilemap(lambda a, b: (a + b,), [(cur, 't', 0), (got, 't', 0)], [(F32, 't')], M=R, N=C,
                     tm=_pick(R, (2048, 1024, 512, 256)), tn=C, name=f"small_add{axis}")
    return cur


def rms_fwd(x, g, name):
    M, D = x.shape
    tm = _pick(M, (256, 128))

    def fn(xv, gv):
        r = lax.rsqrt(jnp.mean(xv * xv, axis=-1, keepdims=True) + EPS)
        return (xv * r * gv,)

    (h,) = tilemap(fn, [(x, 't', 0), (g, 'r', 0)], [(BF16, 't')], M=M, N=D, tm=tm, tn=D, name=name)
    return h


def rms_bwd(x, g, dh, dres, name):
    M, D = x.shape
    tm = _pick(M, (256, 128))

    def fn(xv, gv, dhv, drv):
        r = lax.rsqrt(jnp.mean(xv * xv, axis=-1, keepdims=True) + EPS)
        xh = xv * r
        dxh = dhv * gv
        m = jnp.mean(dxh * xh, axis=-1, keepdims=True)
        dx = drv + r * (dxh - xh * m)
        return dx, dx, jnp.sum(dhv * xh, axis=0, keepdims=True)

    return tilemap(fn, [(x, 't', 0), (g, 'r', 0), (dh, 't', 0), (dres, 't', 0)],
                   [(F32, 't'), (BF16, 't'), (F32, 'a')], M=M, N=D, tm=tm, tn=D, name=name)


def ffn_fwd(x, g, wg, wu, wd, tag, prefetch):
    h = rms_fwd(x, g, f"ffn_norm_{tag}")

    def ep(accs, ex):
        a, b = accs
        return a, b, a * _sigmoid(a) * b

    a, b, s = matmul([(h, wg), (h, wu)], 'nn', [BF16, BF16, BF16], epilogue=ep, tm=1024, tn=512,
                     name=f"ffn_gateup_{tag}")
    (xo,) = matmul([(s, wd)], 'nn', [F32], epilogue=lambda accs, ex: (ex[0] + 0.5 * accs[0],),
                   extras=[(x, 't')], tm=512, tn=512, name=f"ffn_down_{tag}", after=prefetch(s))
    return xo, (x, h, a, b, s)


def ffn_bwd(saved, g, wg, wu, wd, dxo, dxo_bf, tag, after, on_grads):
    x, h, a, b, s = saved

    def ep(accs, ex):
        ds = 0.5 * accs[0]
        av, bv = ex[0].astype(F32), ex[1].astype(F32)
        sg = _sigmoid(av)
        return ds * bv * (sg * (1.0 + av * (1.0 - sg))), ds * (av * sg)

    da, db = matmul([(dxo_bf, wd)], 'nt', [BF16, BF16], epilogue=ep, extras=[(a, 't'), (b, 't')],
                    tm=1024, tn=512, name=f"ffn_dact_{tag}", after=after)
    (dwd,) = matmul([(s, dxo_bf)], 'tn', [BF16], epilogue=lambda accs, ex: (0.5 * accs[0],),
                    tm=512, tn=512, name=f"ffn_dwd_{tag}")
    dwg, dwu = matmul([(h, da), (h, db)], 'tn', [BF16, BF16], epilogue=lambda accs, ex: tuple(accs),
                      tm=1024, tn=512, name=f"ffn_dwgu_{tag}")
    started = on_grads([dwg, dwu, dwd])
    (dh,) = matmul([(da, wg), (db, wu)], 'nt', [F32], tm=512, tn=256, name=f"ffn_dh_{tag}", after=started)
    dx, dx_bf, dg = rms_bwd(x, g, dh, dxo, f"ffn_dnorm_{tag}")
    return dx, dx_bf, dg


def _tril_mask(n):
    return lax.broadcasted_iota(jnp.int32, (n, n), 0) >= lax.broadcasted_iota(jnp.int32, (n, n), 1)


GMLP_CHUNKS_PER_STEP = 4


def _gmlp_specs(L, half, n_grp, chunk):
    gd = half // n_grp
    cps = _pick(L // chunk, (GMLP_CHUNKS_PER_STEP, 2, 1))
    rows = cps * chunk
    specs = [pl.BlockSpec((rows, gd), lambda g, n: (n, g)),
             pl.BlockSpec((rows, gd), lambda g, n: (n, n_grp + g)),
             pl.BlockSpec((1, gd), lambda g, n: (0, g)),
             pl.BlockSpec((1, gd), lambda g, n: (0, g)),
             pl.BlockSpec((None, chunk, chunk), lambda g, n: (g, 0, 0)),
             pl.BlockSpec((None, chunk, 1), lambda g, n: (g, 0, 0))]
    return gd, cps, specs


def _gmlp_gate_values(zu, zv, lg, lb, ws, bs):
    u, v = _gelu(zu), _gelu(zv)
    mu = jnp.mean(v, axis=-1, keepdims=True)
    d = v - mu
    rstd = lax.rsqrt(jnp.mean(d * d, axis=-1, keepdims=True) + EPS)
    vhat = d * rstd
    vn = vhat * lg + lb
    w = jnp.where(_tril_mask(ws.shape[0]), ws, 0.0).astype(BF16)
    sv = jnp.dot(w, vn.astype(BF16), preferred_element_type=F32) + bs
    return u, vhat, rstd, vn, w, sv


def gmlp_gate_fwd(zpre, ln_g, ln_b, w_s, b_s):
    L, half = zpre.shape[0], zpre.shape[1] // 2
    n_grp, chunk = w_s.shape[0], w_s.shape[1]
    gd, cps, specs = _gmlp_specs(L, half, n_grp, chunk)

    def body(zu, zv, lg, lb, ws, bs, o):
        for k in range(cps):
            rows = pl.ds(k * chunk, chunk)
            u, _, _, _, _, sv = _gmlp_gate_values(zu[rows, :], zv[rows, :], lg[...], lb[...], ws[...], bs[...])
            o[rows, :] = (u * sv).astype(o.dtype)

    return _pcall(body, name="gmlp_gate", grid=(n_grp, L // (cps * chunk)), in_specs=specs,
                  out_specs=pl.BlockSpec((cps * chunk, gd), lambda g, n: (n, g)),
                  out_shape=jax.ShapeDtypeStruct((L, half), BF16), compiler_params=_params())(
        zpre, zpre, ln_g, ln_b, w_s, b_s)


def gmlp_gate_bwd(zpre, ln_g, ln_b, w_s, b_s, dgated):
    L, half = zpre.shape[0], zpre.shape[1] // 2
    n_grp, chunk = w_s.shape[0], w_s.shape[1]
    gd, cps, specs = _gmlp_specs(L, half, n_grp, chunk)
    tile = pl.BlockSpec((cps * chunk, gd), lambda g, n: (n, g))
    specs = specs + [tile]

    def body(zu, zv, lg, lb, ws, bs, dg, dzu, dzv, dws, dbs, dlg, dlb):
        @pl.when(pl.program_id(1) == 0)
        def _():
            dws[...] = jnp.zeros_like(dws)
            dbs[...] = jnp.zeros_like(dbs)
            dlg[...] = jnp.zeros_like(dlg)
            dlb[...] = jnp.zeros_like(dlb)

        lgv = lg[...]
        for k in range(cps):
            rows = pl.ds(k * chunk, chunk)
            zuv, zvv = zu[rows, :], zv[rows, :]
            u, vhat, rstd, vn, w, sv = _gmlp_gate_values(zuv, zvv, lgv, lb[...], ws[...], bs[...])
            dgv = dg[rows, :]
            du = dgv * sv
            dsv = dgv * u
            dsv_bf = dsv.astype(BF16)
            dw = lax.dot_general(dsv_bf, vn.astype(BF16), _DIMS['nt'], preferred_element_type=F32)
            dvn = lax.dot_general(w, dsv_bf, _DIMS['tn'], preferred_element_type=F32)
            dvhat = dvn * lgv
            dv = rstd * (dvhat - jnp.mean(dvhat, axis=-1, keepdims=True)
                         - vhat * jnp.mean(dvhat * vhat, axis=-1, keepdims=True))
            dzu[rows, :] = (du * _gelu_grad(zuv)).astype(dzu.dtype)
            dzv[rows, :] = (dv * _gelu_grad(zvv)).astype(dzv.dtype)
            dws[...] += jnp.where(_tril_mask(chunk), dw, 0.0)
            dbs[...] += jnp.sum(dsv, axis=1, keepdims=True)
            dlg[...] += jnp.sum(dvn * vhat, axis=0, keepdims=True)
            dlb[...] += jnp.sum(dvn, axis=0, keepdims=True)

    vec = pl.BlockSpec((1, gd), lambda g, n: (0, g))
    return _pcall(body, name="gmlp_gate_bwd", grid=(n_grp, L // (cps * chunk)), in_specs=specs,
                  out_specs=[tile, tile, pl.BlockSpec((None, chunk, chunk), lambda g, n: (g, 0, 0)),
                             pl.BlockSpec((None, chunk, 1), lambda g, n: (g, 0, 0)), vec, vec],
                  out_shape=[jax.ShapeDtypeStruct((L, half), BF16), jax.ShapeDtypeStruct((L, half), BF16),
                             jax.ShapeDtypeStruct((n_grp, chunk, chunk), F32),
                             jax.ShapeDtypeStruct((n_grp, chunk, 1), F32),
                             jax.ShapeDtypeStruct((1, half), F32), jax.ShapeDtypeStruct((1, half), F32)],
                  compiler_params=_params())(zpre, zpre, ln_g, ln_b, w_s, b_s, dgated)


def gmlp_fwd(x, g, w_in, ln_g, ln_b, w_s, b_s, w_out, prefetch):
    h = rms_fwd(x, g, "gmlp_norm")
    (zpre,) = matmul([(h, w_in)], 'nn', [F32], tm=1024, tn=512, name="gmlp_in")
    gated = gmlp_gate_fwd(zpre, ln_g, ln_b, w_s, b_s)
    (xo,) = matmul([(gated, w_out)], 'nn', [F32], epilogue=lambda accs, ex: (ex[0] + accs[0],),
                   extras=[(x, 't')], tm=512, tn=512, name="gmlp_out", after=prefetch(gated))
    return xo, (x, h, zpre, gated)


def gmlp_bwd(saved, g, w_in, ln_g, ln_b, w_s, b_s, w_out, dxo, dxo_bf, after, on_grads):
    x, h, zpre, gated = saved
    (dgated,) = matmul([(dxo_bf, w_out)], 'nt', [F32], tm=1024, tn=512, name="gmlp_dgated", after=after)
    (dw_out,) = matmul([(gated, dxo_bf)], 'tn', [BF16], tm=1024, tn=512, name="gmlp_dwout")
    dzu, dzv, dws, dbs, dlg, dlb = gmlp_gate_bwd(zpre, ln_g, ln_b, w_s, b_s, dgated)
    dz = jnp.concatenate([dzu, dzv], axis=1)
    (dw_in,) = matmul([(h, dz)], 'tn', [BF16], tm=1024, tn=512, name="gmlp_dwin")
    started = on_grads([dw_in, dw_out])
    (dh,) = matmul([(dz, w_in)], 'nt', [F32], tm=512, tn=256, name="gmlp_dh", after=started)
    dx, dx_bf, dg = rms_bwd(x, g, dh, dxo, "gmlp_dnorm")
    return dx, dx_bf, dg, dlg, dlb, dws, dbs


def _s5_disc(lr, li, ldt, br, bi):
    dt = jnp.exp(ldt)
    mag = jnp.exp(lr * dt)
    ang = li * dt
    ar = mag * jnp.cos(ang)
    ai = mag * jnp.sin(ang)
    den = lr * lr + li * li
    nr = ar - 1.0
    zr = (nr * lr + ai * li) / den
    zi = (ai * lr - nr * li) / den
    return ar, ai, zr[None] * br - zi[None] * bi, zr[None] * bi + zi[None] * br


def s5_disc_fwd(lr, li, ldt, br, bi):
    def body(lr_r, li_r, ldt_r, br_r, bi_r, ar_o, ai_o, bbr_o, bbi_o):
        res = _s5_disc(lr_r[...], li_r[...], ldt_r[...], br_r[...], bi_r[...])
        for o, v in zip((ar_o, ai_o, bbr_o, bbi_o), res):
            o[...] = v

    shp = lambda a: jax.ShapeDtypeStruct(a.shape, F32)
    return _pcall(body, name="s5_disc", out_shape=[shp(lr), shp(lr), shp(br), shp(br)],
                  compiler_params=_params())(lr, li, ldt, br, bi)


def s5_disc_bwd(lr, li, ldt, br, bi, dar, dai, dbbr, dbbi):
    def body(lr_r, li_r, ldt_r, br_r, bi_r, dar_r, dai_r, dbbr_r, dbbi_r, o1, o2, o3, o4, o5):
        _, vjp = jax.vjp(_s5_disc, lr_r[...], li_r[...], ldt_r[...], br_r[...], bi_r[...])
        res = vjp((dar_r[...], dai_r[...], dbbr_r[...], dbbi_r[...]))
        for o, v in zip((o1, o2, o3, o4, o5), res):
            o[...] = v

    shp = lambda a: jax.ShapeDtypeStruct(a.shape, F32)
    return _pcall(body, name="s5_disc_bwd", out_shape=[shp(lr), shp(lr), shp(ldt), shp(br), shp(br)],
                  compiler_params=_params())(lr, li, ldt, br, bi, dar, dai, dbbr, dbbi)


def blockdiag_matmul(pairs, outs, *, epilogue=None, extras=(), name):
    a0, b0 = pairs[0]
    M = a0.shape[0]
    T, wa, wo = b0.shape
    tm = _pick(M, (512, 256, 128))
    n_p, n_e = len(pairs), len(extras)
    in_specs, args = [], []
    for a, b in pairs:
        in_specs += [pl.BlockSpec((tm, wa), lambda k, i: (i, k)), pl.BlockSpec((None, wa, wo), lambda k, i: (k, 0, 0))]
        args += [a, b]
    for arr, kind in extras:
        in_specs.append(pl.BlockSpec((tm, wo), lambda k, i: (i, k)) if kind == 't'
                        else pl.BlockSpec((1, wo), lambda k, i: (0, k)))
        args.append(arr)

    def body(*refs):
        accs = [jnp.dot(refs[2 * p][...].astype(BF16), refs[2 * p + 1][...], preferred_element_type=F32)
                for p in range(n_p)]
        ex = [r[...] for r in refs[2 * n_p:2 * n_p + n_e]]
        res = tuple(accs) if epilogue is None else epilogue(accs, ex)
        for ref, v in zip(refs[2 * n_p + n_e:], res):
            ref[...] = v.astype(ref.dtype)

    return _pcall(body, name=name, grid=(T, M // tm), in_specs=in_specs,
                  out_specs=[pl.BlockSpec((tm, wo), lambda k, i: (i, k)) for _ in outs],
                  out_shape=[jax.ShapeDtypeStruct((M, T * wo), dt) for dt in outs],
                  compiler_params=_params())(*args)


def blockdiag_outer(pairs, name):
    M = pairs[0][0].shape[0]
    n_p = len(pairs)
    shapes = []
    in_specs, args = [], []
    tm = _pick(M, (512, 256, 128))
    T = None
    for a, b, wa, wb in pairs:
        T = a.shape[1] // wa
        in_specs += [pl.BlockSpec((tm, wa), lambda k, i: (i, k)), pl.BlockSpec((tm, wb), lambda k, i: (i, k))]
        args += [a, b]
        shapes.append((T, wa, wb))

    def body(*refs):
        @pl.when(pl.program_id(1) == 0)
        def _():
            for o in refs[2 * n_p:]:
                o[...] = jnp.zeros_like(o)
        for p in range(n_p):
            refs[2 * n_p + p][...] += lax.dot_general(refs[2 * p][...].astype(BF16), refs[2 * p + 1][...].astype(BF16),
                                                      _DIMS['tn'], preferred_element_type=F32)

    return _pcall(body, name=name, grid=(T, M // tm), in_specs=in_specs,
                  out_specs=[pl.BlockSpec((None, s[1], s[2]), lambda k, i: (k, 0, 0)) for s in shapes],
                  out_shape=[jax.ShapeDtypeStruct(s, F32) for s in shapes], compiler_params=_params())(*args)


def s5_scan(br, bi, ar, ai, reverse, want_prev, name):
    L, S = br.shape
    ln = _pick(S, (SCAN_LANES, 512, 256, 128))
    tb = _pick(L, (512, 256, 128))
    n_t = L // tb
    n_q = tb // 8

    def tmap(j, t):
        return ((n_t - 1 - t) if reverse else t, j)

    blk = pl.BlockSpec((tb, ln), tmap)
    vec = pl.BlockSpec((1, ln), lambda j, t: (0, j))

    def cmul(xr, xi, yr, yi):
        return xr * yr - xi * yi, xr * yi + xi * yr

    n_out = 4 if want_prev else 2

    def body(br_r, bi_r, ar_r, ai_r, *rest):
        outs, (cr_s, ci_s) = rest[:n_out], rest[n_out:]

        @pl.when(pl.program_id(1) == 0)
        def _():
            cr_s[...] = jnp.zeros_like(cr_s)
            ci_s[...] = jnp.zeros_like(ci_s)

        a1r, a1i = ar_r[...], ai_r[...]
        a2r, a2i = cmul(a1r, a1i, a1r, a1i)
        a4r, a4i = cmul(a2r, a2i, a2r, a2i)
        a8r, a8i = cmul(a4r, a4i, a4r, a4i)
        row = lax.broadcasted_iota(jnp.int32, (8, ln), 0)
        dist = (7 - row) if reverse else row
        pwr, pwi = jnp.broadcast_to(a1r, (8, ln)), jnp.broadcast_to(a1i, (8, ln))
        for bit, (er, ei) in ((1, (a1r, a1i)), (2, (a2r, a2i)), (4, (a4r, a4i))):
            nr, ni = cmul(pwr, pwi, er, ei)
            sel = (dist & bit) != 0
            pwr, pwi = jnp.where(sel, nr, pwr), jnp.where(sel, ni, pwi)
        last = 0 if reverse else 7
        steps = [(d, jnp.where(dist >= d, er, 0.0), jnp.where(dist >= d, ei, 0.0))
                 for d, (er, ei) in ((1, (a1r, a1i)), (2, (a2r, a2i)), (4, (a4r, a4i)))]

        def step(q, carry):
            cr, ci = carry
            qq = (n_q - 1 - q) if reverse else q
            rows = pl.ds(pl.multiple_of(qq * 8, 8), 8)
            xr, xi = br_r[rows, :], bi_r[rows, :]
            for d, er, ei in steps:
                sr = pltpu.roll(xr, (8 - d) if reverse else d, 0)
                si = pltpu.roll(xi, (8 - d) if reverse else d, 0)
                mr, mi = cmul(sr, si, er, ei)
                xr, xi = xr + mr, xi + mi
            lr, li = xr[last:last + 1, :], xi[last:last + 1, :]
            kr, ki = cmul(pwr, pwi, cr, ci)
            xr, xi = xr + kr, xi + ki
            outs[0][rows, :] = xr
            outs[1][rows, :] = xi
            if want_prev:
                outs[2][rows, :] = jnp.where(dist >= 1, pltpu.roll(xr, 7 if reverse else 1, 0), cr)
                outs[3][rows, :] = jnp.where(dist >= 1, pltpu.roll(xi, 7 if reverse else 1, 0), ci)
            nr, ni = cmul(a8r, a8i, cr, ci)
            return lr + nr, li + ni

        cr, ci = lax.fori_loop(0, n_q, step, (cr_s[...], ci_s[...]), unroll=2)
        cr_s[...] = cr
        ci_s[...] = ci

    shp = jax.ShapeDtypeStruct((L, S), F32)
    return _pcall(body, name=name, grid=(S // ln, n_t), in_specs=[blk, blk, vec, vec],
                  out_specs=[blk] * n_out, out_shape=[shp] * n_out,
                  scratch_shapes=[pltpu.VMEM((1, ln), F32), pltpu.VMEM((1, ln), F32)],
                  compiler_params=_params())(br, bi, ar, ai)


def _to_blockdiag(m, tile_groups):
    G, A, B = m.shape
    T = G // tile_groups
    eye = jnp.eye(tile_groups, dtype=m.dtype)
    t = m.reshape(T, tile_groups, A, 1, B) * eye[None, :, None, :, None]
    return t.reshape(T, tile_groups * A, tile_groups * B)


def _from_blockdiag(t, tile_groups):
    T, RA, RB = t.shape
    A, B = RA // tile_groups, RB // tile_groups
    d = jnp.diagonal(t.reshape(T, tile_groups, A, tile_groups, B), axis1=1, axis2=3)
    return jnp.moveaxis(d, 3, 1).reshape(T * tile_groups, A, B)


def s5_fwd(x, g, w_in, lam_re, lam_im, log_dt, b_re, b_im, c_re, c_im, d_skip, w_out, prefetch):
    G, P, H = b_re.shape
    tg = min(SSM_TILE_GROUPS, G)
    h = rms_fwd(x, g, "s5_norm")
    (u,) = matmul([(h, w_in)], 'nn', [F32], tm=512, tn=512, name="s5_in")
    br_t, bi_t = jnp.transpose(b_re, (2, 0, 1)), jnp.transpose(b_im, (2, 0, 1))
    ar, ai, bbr, bbi = s5_disc_fwd(lam_re, lam_im, log_dt, br_t, bi_t)
    bbr_g, bbi_g = jnp.transpose(bbr, (1, 0, 2)), jnp.transpose(bbi, (1, 0, 2))
    bd_br, bd_bi = _to_blockdiag(bbr_g.astype(BF16), tg), _to_blockdiag(bbi_g.astype(BF16), tg)
    bur, bui = blockdiag_matmul([(u, bd_br), (u, bd_bi)], [F32, F32], name="s5_bu")
    a_r, a_i = ar.reshape(1, G * P), ai.reshape(1, G * P)
    hr, hi, hpr, hpi = s5_scan(bur, bui, a_r, a_i, False, True, "s5_scan")
    c_pg_r = jnp.transpose(c_re, (0, 2, 1)).astype(BF16)
    c_pg_i = jnp.transpose(c_im, (0, 2, 1)).astype(BF16)
    bd_cr, bd_nci = _to_blockdiag(c_pg_r, tg), _to_blockdiag(-c_pg_i, tg)

    def ep(accs, ex):
        y = accs[0] + accs[1] + ex[1] * ex[0]
        return y, _gelu(y)

    y, act = blockdiag_matmul([(hr, bd_cr), (hi, bd_nci)], [F32, BF16], epilogue=ep,
                              extras=[(u, 't'), (d_skip, 'r')], name="s5_y")
    (o,) = matmul([(act, w_out)], 'nn', [F32], tm=512, tn=512, name="s5_out", after=prefetch(act))
    M, D = x.shape
    tm = _pick(M, (256, 128))
    (xo,) = tilemap(lambda xv, val, gt: (xv + val * _sigmoid(gt),), [(x, 't', 0), (o, 't', 0), (o, 't', 1)],
                    [(F32, 't')], M=M, N=D, tm=tm, tn=D, name="s5_glu")
    saved = (x, h, u, hr, hi, hpr, hpi, y, act, o, a_r, a_i, bd_br, bd_bi, bd_cr, bd_nci, br_t, bi_t)
    return xo, saved


def s5_bwd(saved, g, w_in, lam_re, lam_im, log_dt, b_re, d_skip, w_out, dxo, after, on_grads):
    x, h, u, hr, hi, hpr, hpi, y, act, o, a_r, a_i, bd_br, bd_bi, bd_cr, bd_nci, br_t, bi_t = saved
    G, P, H = b_re.shape
    tg = min(SSM_TILE_GROUPS, G)
    M, D = x.shape
    tm = _pick(M, (256, 128))

    def glu_bwd(dv, val, gt):
        sg = _sigmoid(gt)
        return dv * sg, dv * val * sg * (1.0 - sg)

    dval, dgate = tilemap(glu_bwd, [(dxo, 't', 0), (o, 't', 0), (o, 't', 1)], [(BF16, 't'), (BF16, 't')],
                          M=M, N=D, tm=tm, tn=D, name="s5_dglu", after=after)
    do = jnp.concatenate([dval, dgate], axis=1)
    (dw_out,) = matmul([(act, do)], 'tn', [BF16], tm=512, tn=512, name="s5_dwout")
    (dact,) = matmul([(do, w_out)], 'nt', [F32], tm=512, tn=512, name="s5_dact")
    dy, dd = tilemap(lambda da, yv, uv: (da * _gelu_grad(yv), jnp.sum(da * _gelu_grad(yv) * uv, axis=0, keepdims=True)),
                     [(dact, 't', 0), (y, 't', 0), (u, 't', 0)], [(F32, 't'), (F32, 'a')],
                     M=M, N=D, tm=tm, tn=D, name="s5_dy")
    bd_crT, bd_nciT = jnp.transpose(bd_cr, (0, 2, 1)), jnp.transpose(bd_nci, (0, 2, 1))
    dhr, dhi = blockdiag_matmul([(dy, bd_crT), (dy, bd_nciT)], [F32, F32], name="s5_dh")
    gr, gi = s5_scan(dhr, dhi, a_r, -a_i, True, False, "s5_scan_rev")
    S = G * P
    tms = _pick(M, (128,))

    def da_fn(grv, giv, hprv, hpiv):
        return (jnp.sum(grv * hprv + giv * hpiv, axis=0, keepdims=True),
                jnp.sum(giv * hprv - grv * hpiv, axis=0, keepdims=True))

    dar, dai = tilemap(da_fn, [(gr, 't', 0), (gi, 't', 0), (hpr, 't', 0), (hpi, 't', 0)], [(F32, 'a'), (F32, 'a')],
                       M=M, N=S, tm=tms, tn=_pick(S, (2048, 1024, 512)), name="s5_dabar")
    wa, wb = tg * H, tg * P
    xc_r, xc_i, xb_r, xb_i = blockdiag_outer([(dy, hr, wa, wb), (dy, hi, wa, wb), (u, gr, wa, wb), (u, gi, wa, wb)],
                                             "s5_dcb")
    dc_re = _from_blockdiag(xc_r, tg)
    dc_im = -_from_blockdiag(xc_i, tg)
    dbb_r = jnp.transpose(_from_blockdiag(xb_r, tg), (1, 0, 2))
    dbb_i = jnp.transpose(_from_blockdiag(xb_i, tg), (1, 0, 2))
    dlr, dli, dldt, dbr_t, dbi_t = s5_disc_bwd(lam_re, lam_im, log_dt, br_t, bi_t,
                                               dar.reshape(G, P), dai.reshape(G, P), dbb_r, dbb_i)
    db_re, db_im = jnp.transpose(dbr_t, (1, 2, 0)), jnp.transpose(dbi_t, (1, 2, 0))
    bd_brT, bd_biT = jnp.transpose(bd_br, (0, 2, 1)), jnp.transpose(bd_bi, (0, 2, 1))
    (du,) = blockdiag_matmul([(gr, bd_brT), (gi, bd_biT)], [BF16],
                             epilogue=lambda accs, ex: (accs[0] + accs[1] + ex[1] * ex[0],),
                             extras=[(dy, 't'), (d_skip, 'r')], name="s5_du")
    (dw_in,) = matmul([(h, du)], 'tn', [BF16], tm=512, tn=512, name="s5_dwin")
    started = on_grads([dw_in, dw_out])
    (dh,) = matmul([(du, w_in)], 'nt', [F32], tm=512, tn=512, name="s5_dhin", after=started)
    dx, dx_bf, dg = rms_bwd(x, g, dh, dxo, "s5_dnorm")
    return dx, dx_bf, dg, dlr, dli, dldt, db_re, db_im, dc_re, dc_im, dd


def ple_fwd(x, g, p_emb, w_gate, w_proj, tag, prefetch):
    h = rms_fwd(x, g, f"ple_norm_{tag}")
    (q,) = matmul([(p_emb, w_proj)], 'nn', [F32], tm=512, tn=512, name=f"ple_proj_{tag}")

    def ep(accs, ex):
        gt = _sigmoid(accs[0])
        return ex[0] + gt * ex[1], gt

    xo, gate = matmul([(h, w_gate)], 'nn', [F32, F32], epilogue=ep, extras=[(x, 't'), (q, 't')],
                      tm=512, tn=512, name=f"ple_gate_{tag}", after=prefetch(q))
    return xo, (x, h, q, gate)


def ple_bwd(saved, g, p_emb, w_gate, dxo, tag, after, on_grads):
    x, h, q, gate = saved
    M, D = x.shape
    tm = _pick(M, (256, 128))
    dq, dpre = tilemap(lambda dv, qv, gv: (dv * gv, dv * qv * gv * (1.0 - gv)),
                       [(dxo, 't', 0), (q, 't', 0), (gate, 't', 0)], [(BF16, 't'), (BF16, 't')],
                       M=M, N=D, tm=tm, tn=D, name=f"ple_dgate_{tag}", after=after)
    (dw_proj,) = matmul([(p_emb, dq)], 'tn', [BF16], tm=256, tn=512, name=f"ple_dwproj_{tag}")
    (dw_gate,) = matmul([(h, dpre)], 'tn', [BF16], tm=512, tn=512, name=f"ple_dwgate_{tag}")
    started = on_grads([dw_gate, dw_proj])
    (dh,) = matmul([(dpre, w_gate)], 'nt', [F32], tm=512, tn=512, name=f"ple_dh_{tag}", after=started)
    dx, dx_bf, dg = rms_bwd(x, g, dh, dxo, f"ple_dnorm_{tag}")
    return dx, dx_bf, dg


def loss_head(x, g, target):
    M, D = x.shape
    tm = _pick(M, (256, 128))

    def fn(xv, gv, tv):
        r = lax.rsqrt(jnp.mean(xv * xv, axis=-1, keepdims=True) + EPS)
        xh = xv * r
        e = xh * gv - tv
        dy = e * (1.0 / D)
        dxh = dy * gv
        m = jnp.mean(dxh * xh, axis=-1, keepdims=True)
        dx = r * (dxh - xh * m)
        return jnp.sum(e * e, axis=0, keepdims=True), dx, dx, jnp.sum(dy * xh, axis=0, keepdims=True)

    return tilemap(fn, [(x, 't', 0), (g, 'r', 0), (target, 't', 0)],
                   [(F32, 'a'), (F32, 't'), (BF16, 't'), (F32, 'a')], M=M, N=D, tm=tm, tn=D, name="loss_head")


def _adamw_math(wv, gv, mv, vv):
    mn = ADAM_B1 * mv + (1.0 - ADAM_B1) * gv
    vn = ADAM_B2 * vv + (1.0 - ADAM_B2) * (gv * gv)
    m_hat = mn / (1.0 - ADAM_B1 ** ADAM_STEP)
    v_hat = vn / (1.0 - ADAM_B2 ** ADAM_STEP)
    return -ADAM_LR * (m_hat / (jnp.sqrt(v_hat) + ADAM_EPS) + ADAM_WD * wv), mn, vn


def adamw_into(w, lead, g, m, v, carry, name, after):
    R, C = w.shape[-2:]
    lead = tuple(lead)
    tm = _row_tile(R, C)
    blk = pl.BlockSpec((None,) * len(lead) + (tm, C), lambda i: lead + (i, 0))
    n_carry = 0 if carry is None else 4

    def body(*refs):
        w_r, g_r, m_r, v_r = refs[:4]
        g_o, d_o, m_o, v_o, token = refs[5 + n_carry:]
        gv = g_r[...]
        d, mn, vn = _adamw_math(w_r[...], gv, m_r[...], v_r[...])
        g_o[...] = gv
        d_o[...] = d
        m_o[...] = mn
        v_o[...] = vn
        token[...] = jnp.zeros_like(token)

    shp = jax.ShapeDtypeStruct(w.shape, F32)
    res = _pcall(body, name=name, grid=(R // tm,),
                 in_specs=[blk, pl.BlockSpec((tm, C), lambda i: (i, 0)), blk, blk]
                 + [pl.BlockSpec(memory_space=pl.ANY)] * (1 + n_carry),
                 out_specs=[blk] * 4 + [pl.BlockSpec((8, 128), lambda i: (0, 0))],
                 out_shape=[shp] * 4 + [jax.ShapeDtypeStruct((8, 128), F32)],
                 input_output_aliases={5 + k: k for k in range(n_carry)},
                 compiler_params=_params())(w, g, m, v, after, *(carry or ()))
    return tuple(res[:4]), res[4]


def adamw(w, g, m, v, name):
    w2, g2, m2, v2 = _as2d(w), g.reshape(_as2d(w).shape), _as2d(m), _as2d(v)
    R, C = w2.shape
    tm = _row_tile(R, C)

    d, mn, vn = tilemap(_adamw_math, [(w2, 't', 0), (g2, 't', 0), (m2, 't', 0), (v2, 't', 0)],
                        [(F32, 't'), (F32, 't'), (F32, 't')], M=R, N=C, tm=tm, tn=C, name=name)
    return d.reshape(w.shape), mn.reshape(w.shape), vn.reshape(w.shape)


WEIGHT_NAMES = ['norm_g', 'final_norm_g', 'ffn_w_gate', 'ffn_w_up', 'ffn_w_down', 'gmlp_w_in', 'gmlp_ln_g',
                'gmlp_ln_b', 'gmlp_w_s', 'gmlp_b_s', 'gmlp_w_out', 's5_w_in', 's5_lam_re', 's5_lam_im',
                's5_log_dt', 's5_b_re', 's5_b_im', 's5_c_re', 's5_c_im', 's5_d', 's5_w_out', 'ple_w_gate',
                'ple_w_proj']
BIG = {'ffn_w_gate': 1, 'ffn_w_up': 1, 'ffn_w_down': 0, 'gmlp_w_in': 1, 'gmlp_w_out': 0, 's5_w_in': 0,
       's5_w_out': 1, 'ple_w_gate': 0, 'ple_w_proj': 1}


def _blocks(depth):
    out = []
    for i in range(depth):
        for k, half in enumerate("ab"):
            ffn = [(n, (i, k)) for n in ('ffn_w_gate', 'ffn_w_up', 'ffn_w_down')]
            if k == 1:
                out.append((f"ffn{i}b", ffn))
                out.append((f"ple{i}", [('ple_w_gate', (i,)), ('ple_w_proj', (i,))]))
            else:
                out.append((f"ffn{i}a", ffn))
                mix = 'gmlp' if i % 2 == 0 else 's5'
                out.append((f"{mix}{i}", [(f'{mix}_w_in', (i // 2,)), (f'{mix}_w_out', (i // 2,))]))
    return out


def kernel(x, p, norm_g, final_norm_g, ffn_w_gate, ffn_w_up, ffn_w_down, gmlp_w_in, gmlp_ln_g, gmlp_ln_b, gmlp_w_s, gmlp_b_s, gmlp_w_out, s5_w_in, s5_lam_re, s5_lam_im, s5_log_dt, s5_b_re, s5_b_im, s5_c_re, s5_c_im, s5_d, s5_w_out, ple_w_gate, ple_w_proj, loss_target, m_norm_g, m_final_norm_g, m_ffn_w_gate, m_ffn_w_up, m_ffn_w_down, m_gmlp_w_in, m_gmlp_ln_g, m_gmlp_ln_b, m_gmlp_w_s, m_gmlp_b_s, m_gmlp_w_out, m_s5_w_in, m_s5_lam_re, m_s5_lam_im, m_s5_log_dt, m_s5_b_re, m_s5_b_im, m_s5_c_re, m_s5_c_im, m_s5_d, m_s5_w_out, m_ple_w_gate, m_ple_w_proj, v_norm_g, v_final_norm_g, v_ffn_w_gate, v_ffn_w_up, v_ffn_w_down, v_gmlp_w_in, v_gmlp_ln_g, v_gmlp_ln_b, v_gmlp_w_s, v_gmlp_b_s, v_gmlp_w_out, v_s5_w_in, v_s5_lam_re, v_s5_lam_im, v_s5_log_dt, v_s5_b_re, v_s5_b_im, v_s5_c_re, v_s5_c_im, v_s5_d, v_s5_w_out, v_ple_w_gate, v_ple_w_proj):
    env = dict(locals())
    W = {n: env[n] for n in WEIGHT_NAMES}
    Mo = {n: env["m_" + n] for n in WEIGHT_NAMES}
    Vo = {n: env["v_" + n] for n in WEIGHT_NAMES}
    depth = norm_g.shape[0]
    L, D = x.shape[1], x.shape[2]
    s_idx = 2 * lax.axis_index("x") + lax.axis_index("y")

    sc = jnp.stack([s_idx, lax.axis_index("c")]).astype(jnp.int32)
    blocks = _blocks(depth)

    ng2 = norm_g.reshape(depth * 4, norm_g.shape[-1])
    ng_full, sd_full = gather_small("gather_small", [ng2, s5_d], sc)
    full = {}
    gathers = []
    token = sd_full
    for bname, mats in blocks:
        casts = [cast_into_full(W[n], lead, BIG[n], sc, f"cast_{bname}_{n}", token) for n, lead in mats]
        handle, token = gather_start(f"gather_{bname}", casts, [W[n].shape[-2:] for n, _ in mats],
                                     [BIG[n] for n, _ in mats], token)
        gathers.append(handle)
    all_started = token
    ng_full = ng_full.reshape(depth, 4, 1, D)

    landed = {}

    def prefetcher(bi):
        def prefetch(value):
            if bi >= len(blocks):
                return value
            landed[bi], token = gather_land(gathers[bi], value)
            return token
        return prefetch

    def fetch(bi, after):
        if bi not in landed:
            landed[bi], after = gather_land(gathers[bi], after)
        full.update(dict(zip(blocks[bi][1], gather_take(landed[bi], after))))
    gf = final_norm_g.reshape(1, D)

    G, P, H = s5_b_re.shape[1:]
    n_grp, chunk = gmlp_w_s.shape[1], gmlp_w_s.shape[2]
    lam_re, lam_im = s5_lam_re[0], s5_lam_im[0]
    log_dt = s5_log_dt.reshape(G, 1)
    b_re, b_im, c_re, c_im = s5_b_re[0], s5_b_im[0], s5_c_re[0], s5_c_im[0]
    w_s, b_s = gmlp_w_s[0], gmlp_b_s[0].reshape(n_grp, chunk, 1)
    xs = x.reshape(L, D)
    saved = []
    def ffn_w(i, k):
        return [full[(n, (i, k))] for n in ('ffn_w_gate', 'ffn_w_up', 'ffn_w_down')]

    for i in range(depth):
        sv = {}
        fetch(4 * i, all_started if i == 0 else xs)
        xs, sv['ffn_a'] = ffn_fwd(xs, ng_full[i, 0], *ffn_w(i, 0), f"{i}a", prefetcher(4 * i + 1))
        j = (i // 2,)
        fetch(4 * i + 1, xs)
        if i % 2 == 0:
            xs, sv['mix'] = gmlp_fwd(xs, ng_full[i, 1], full[('gmlp_w_in', j)], gmlp_ln_g, gmlp_ln_b, w_s, b_s,
                                     full[('gmlp_w_out', j)], prefetcher(4 * i + 2))
        else:
            xs, sv['mix'] = s5_fwd(xs, ng_full[i, 1], full[('s5_w_in', j)], lam_re, lam_im, log_dt, b_re, b_im,
                                   c_re, c_im, sd_full, full[('s5_w_out', j)], prefetcher(4 * i + 2))
        fetch(4 * i + 2, xs)
        xs, sv['ffn_b'] = ffn_fwd(xs, ng_full[i, 2], *ffn_w(i, 1), f"{i}b", prefetcher(4 * i + 3))
        fetch(4 * i + 3, xs)
        xs, sv['ple'] = ple_fwd(xs, ng_full[i, 3], p[i, 0], full[('ple_w_gate', (i,))], full[('ple_w_proj', (i,))], f"{i}",
                                prefetcher(4 * i + 4))
        saved.append(sv)

    sq, dx, dx_bf, dgf = loss_head(xs, gf, loss_target.reshape(L, D))
    loss_local = 0.5 * jnp.sum(sq) / D
    dng = [[None] * 4 for _ in range(depth)]
    small = {}
    gshard = {}
    state = {'fence': sc}
    ici_inflight = []
    swaps = []

    def hook(bi):
        def on_grads(gs):
            bname, mats = blocks[bi]
            handle, token = reduce_pair_start(f"reduce_{bname}", gs, [BIG[n] for n, _ in mats], state['fence'])
            state['pair'] = (handle, mats)
            return token
        return on_grads

    def reduce_block(after, keep=1):
        handle, mats = state.pop('pair')
        ici_handle, fence = reduce_ici_start(handle, sc, after)
        ici_inflight.append((ici_handle, mats))
        while len(ici_inflight) > keep:
            prev_handle, prev_mats = ici_inflight.pop(0)
            swap_handle, fence = reduce_finish(prev_handle, fence)
            swaps.append((swap_handle, prev_mats))
        state['fence'] = fence

    for i in reversed(range(depth)):
        sv = saved[i]
        dx, dx_bf, dng[i][3] = ple_bwd(sv['ple'], ng_full[i, 3], p[i, 0], full[('ple_w_gate', (i,))], dx, f"{i}",
                                       state['fence'], hook(4 * i + 3))
        reduce_block(dx_bf, keep=2)
        dx, dx_bf, dng[i][2] = ffn_bwd(sv['ffn_b'], ng_full[i, 2], *ffn_w(i, 1), dx, dx_bf, f"{i}b",
                                       state['fence'], hook(4 * i + 2))
        reduce_block(dx_bf)
        j = (i // 2,)
        if i % 2 == 0:
            dx, dx_bf, dng[i][1], dlg, dlb, dws, dbs = gmlp_bwd(
                sv['mix'], ng_full[i, 1], full[('gmlp_w_in', j)], gmlp_ln_g, gmlp_ln_b, w_s, b_s,
                full[('gmlp_w_out', j)], dx, dx_bf, state['fence'], hook(4 * i + 1))
            small.update(gmlp_ln_g=dlg, gmlp_ln_b=dlb, gmlp_w_s=dws, gmlp_b_s=dbs)
        else:
            dx, dx_bf, dng[i][1], dlr, dli, dldt, db_re, db_im, dc_re, dc_im, dd = s5_bwd(
                sv['mix'], ng_full[i, 1], full[('s5_w_in', j)], lam_re, lam_im, log_dt, b_re, sd_full,
                full[('s5_w_out', j)], dx, state['fence'], hook(4 * i + 1))
            small.update(s5_lam_re=dlr, s5_lam_im=dli, s5_log_dt=dldt, s5_b_re=db_re, s5_b_im=db_im,
                         s5_c_re=dc_re, s5_c_im=dc_im, s5_d=dd)
        reduce_block(dx_bf)
        dx, dx_bf, dng[i][0] = ffn_bwd(sv['ffn_a'], ng_full[i, 0], *ffn_w(i, 0), dx, dx_bf, f"{i}a",
                                       state['fence'], hook(4 * i))
        reduce_block(dx_bf)
    grad_x = dx.reshape(x.shape)
    small['norm_g'] = jnp.stack([jnp.stack(r) for r in dng])
    small['final_norm_g'] = dgf

    grads, deltas, new_m, new_v = {}, {}, {}, {}
    carry = {}
    fence = state['fence']

    def update_big(key, fence):
        n, lead = key
        carry[n], fence = adamw_into(W[n], lead, gshard[key], Mo[n], Vo[n], carry.get(n),
                                     f"adamw_{n}_{'_'.join(map(str, lead))}", fence)
        return fence

    small_names = [n for n in WEIGHT_NAMES if n not in BIG]
    flat = jnp.concatenate([small[n].astype(F32).reshape(-1) for n in small_names] + [loss_local.reshape(1)])
    pad = (-flat.size) % (256 * 128)
    flat = jnp.pad(flat, (0, pad)).reshape(-1, 128)
    handle, fence = small_swap_start(0, flat, fence)
    handle, fence = small_swap_start(1, small_swap_finish(handle, fence), fence)
    for swap_handle, mats in swaps:
        gshard.update(dict(zip(mats, reduce_swap_wait(swap_handle, fence))))
        for key in mats:
            fence = update_big(key, fence)
    handle, fence = small_swap_start(2, small_swap_finish(handle, fence), fence)
    last_handle, last_mats = ici_inflight.pop()
    swap_handle, fence = reduce_finish(last_handle, fence)
    gshard.update(dict(zip(last_mats, reduce_swap_wait(swap_handle, fence))))
    for key in last_mats:
        fence = update_big(key, fence)
    tot = small_swap_finish(handle, fence).reshape(-1)
    off = 0
    for n in small_names:
        sz = small[n].size
        gsum = tot[off:off + sz]
        off += sz
        if n == 'norm_g':
            gsum = lax.dynamic_slice_in_dim(gsum.reshape(depth, 4, D), s_idx * W[n].shape[-1], W[n].shape[-1], axis=2)
        elif n == 's5_d':
            gsum = lax.dynamic_slice_in_dim(gsum.reshape(1, D), s_idx * W[n].shape[-1], W[n].shape[-1], axis=1)
        grads[n] = gsum.reshape(W[n].shape)
        deltas[n], new_m[n], new_v[n] = adamw(W[n], grads[n], Mo[n], Vo[n], f"adamw_{n}")
    loss = tot[off]
    for n in BIG:
        grads[n], deltas[n], new_m[n], new_v[n] = carry[n]
    return (loss, grad_x, *[grads[n] for n in WEIGHT_NAMES], *[deltas[n] for n in WEIGHT_NAMES],
            *[new_m[n] for n in WEIGHT_NAMES], *[new_v[n] for n in WEIGHT_NAMES])
```
